```python
import math
import jax, jax.numpy as jnp
from jax import lax
import numpy as np

D_MODEL = 1024
BATCH = 8
SEQ = 8192
DEPTH = 2

MIX_W = D_MODEL
GROUP_W = MIX_W // 4
HEAD_DIM = 64
EPS = 1e-6
NEG_INF = -1e30
ATTN_Q_HEADS = GROUP_W // HEAD_DIM
ATTN_KV_HEADS = 2
WINDOW = 128
SSM_GROUP = 16
SSM_GROUPS = GROUP_W // SSM_GROUP
SSM_STATE = 64
SSM_DT_MIN = 1e-3
SSM_DT_MAX = 1e-1
MLSTM_HEADS = GROUP_W // HEAD_DIM
MLSTM_CHUNK = 128
MLSTM_CONV = 4
SGU_GROUPS = 4
SGU_GROUP_W = GROUP_W // SGU_GROUPS
SGU_CHUNK = 128
IN_SPLITS = (GROUP_W, ATTN_KV_HEADS * HEAD_DIM, ATTN_KV_HEADS * HEAD_DIM,
             GROUP_W,
             GROUP_W, GROUP_W, GROUP_W, MLSTM_HEADS, MLSTM_HEADS, GROUP_W,
             GROUP_W, GROUP_W)
IN_W = 2312
N_EXPERT_GROUPS = 4
EXPERTS_PER_GROUP = 8
N_EXPERTS = N_EXPERT_GROUPS * EXPERTS_PER_GROUP
TOP_K = 2
D_EXPERT = D_MODEL // 2
MOE_BLOCK = 128

kernel_name = 'hymba_style_hybrid_hier_moe'


def rmsnorm(x, g):
    xf = x.astype(jnp.float32)
    y = xf * lax.rsqrt(jnp.mean(xf * xf, axis=-1, keepdims=True) + EPS)
    return (y * g.astype(jnp.float32)).astype(x.dtype)


def layernorm(x, g, b):
    xf = x.astype(jnp.float32)
    mu = jnp.mean(xf, axis=-1, keepdims=True)
    var = jnp.mean(jnp.square(xf - mu), axis=-1, keepdims=True)
    y = (xf - mu) * lax.rsqrt(var + EPS)
    return (y * g.astype(jnp.float32) + b.astype(jnp.float32)).astype(x.dtype)


def causal_depthwise_conv(x, w):
    k = w.shape[0]
    return lax.conv_general_dilated(x, w.astype(x.dtype)[:, None, :], window_strides=(1,),
                                    padding=[(k - 1, 0)], dimension_numbers=('NWC', 'WIO', 'NWC'),
                                    feature_group_count=x.shape[-1])


def sliding_window_sink_attention(q, k, v, sinks):
    bsz, seq, hq, dh = q.shape
    hkv = k.shape[2]
    rep = hq // hkv
    nb = seq // WINDOW
    qb = q.reshape(bsz, nb, WINDOW, hkv, rep, dh)

    def with_prev(t):
        t = t.reshape(bsz, nb, WINDOW, hkv, dh)
        prev = jnp.pad(t[:, :-1], ((0, 0), (1, 0), (0, 0), (0, 0), (0, 0)))
        return jnp.concatenate([prev, t], axis=2)

    kk = with_prev(k)
    vv = with_prev(v)
    scores = jnp.einsum('bnqgrd,bnkgd->bngrqk', qb, kk).astype(jnp.float32) * (dh ** -0.5)
    qi = jnp.arange(WINDOW)[:, None] + WINDOW
    kj = jnp.arange(2 * WINDOW)[None, :]
    band = (kj <= qi) & (kj > qi - WINDOW)
    valid = band[None] & ((jnp.arange(nb)[:, None, None] > 0) | (kj >= WINDOW)[None])
    scores = jnp.where(valid[None, :, None, None], scores, NEG_INF)
    sink = jnp.broadcast_to(sinks.astype(jnp.float32).reshape(hkv, rep)[None, None, :, :, None, None],
                            scores.shape[:-1] + (1,))
    p = jax.nn.softmax(jnp.concatenate([scores, sink], axis=-1), axis=-1)[..., :-1]
    o = jnp.einsum('bngrqk,bnkgd->bnqgrd', p.astype(v.dtype), vv)
    return o.reshape(bsz, seq, hq * dh)


def _ssm_combine(e1, e2):
    a1, b1 = e1
    a2, b2 = e2
    return a2 * a1, a2 * b1 + b2


def s5_mixer(u, a_re, a_im, b_re, b_im, c_re, c_im, d_skip, log_dt, glu_w, glu_b):
    bsz, seq, _ = u.shape
    f32 = jnp.float32
    uf = u.astype(f32).reshape(bsz, seq, SSM_GROUPS, SSM_GROUP)
    a = lax.complex(a_re.astype(f32), a_im.astype(f32))
    dt = jnp.exp(log_dt.astype(f32))[:, None]
    a_bar = jnp.exp(a * dt)
    b_mat = lax.complex(b_re.astype(f32), b_im.astype(f32))
    b_bar = ((a_bar - 1.0) / a)[..., None] * b_mat
    bu = jnp.einsum('bsgh,gph->bsgp', uf.astype(jnp.complex64), b_bar)
    a_seq = jnp.broadcast_to(a_bar, bu.shape)
    _, states = lax.associative_scan(_ssm_combine, (a_seq, bu), axis=1)
    c_mat = lax.complex(c_re.astype(f32), c_im.astype(f32))
    y = jnp.einsum('bsgp,ghp->bsgh', states, c_mat).real + d_skip.astype(f32).reshape(SSM_GROUPS, SSM_GROUP) * uf
    y = jax.nn.gelu(y.reshape(bsz, seq, GROUP_W))
    y = y * jax.nn.sigmoid(y @ glu_w.astype(f32) + glu_b.astype(f32))
    return y.astype(u.dtype)


def mlstm_chunkwise(q, k, v, i_pre, f_pre):
    bsz, seq, nh, dh = q.shape
    L = MLSTM_CHUNK
    nc = seq // L
    f32 = jnp.float32
    q = q.astype(f32)
    k = k.astype(f32) * (dh ** -0.5)
    v = v.astype(f32)
    log_f = jax.nn.log_sigmoid(f_pre.astype(f32))
    log_i = i_pre.astype(f32)

    def to_chunks(t):
        return jnp.moveaxis(t.reshape((bsz, nc, L) + t.shape[2:]), 1, 0)

    causal = jnp.tril(jnp.ones((L, L), dtype=bool))

    def step(carry, inp):
        c_st, n_st, m_st = carry
        qc, kc, vc, ic, fc = inp
        b = jnp.cumsum(fc, axis=1).transpose(0, 2, 1)
        ih = ic.transpose(0, 2, 1)
        log_d = jnp.where(causal, b[..., :, None] - b[..., None, :] + ih[..., None, :], -jnp.inf)
        log_inter = b + m_st[..., None]
        m_t = jnp.maximum(log_inter, jnp.max(log_d, axis=-1))
        w_intra = jnp.exp(log_d - m_t[..., None]) * jnp.einsum('blhd,bshd->bhls', qc, kc)
        w_inter = jnp.exp(log_inter - m_t)
        num = (jnp.einsum('bhls,bshd->blhd', w_intra, vc)
               + jnp.einsum('bhl,bhvk,blhk->blhv', w_inter, c_st, qc))
        den = jnp.sum(w_intra, axis=-1) + w_inter * jnp.einsum('bhk,blhk->bhl', n_st, qc)
        h = num / jnp.maximum(jnp.abs(den), jnp.exp(-m_t)).transpose(0, 2, 1)[..., None]
        m_new = m_t[..., -1]
        w_s = jnp.exp(b[..., -1:] - b + ih - m_new[..., None])
        decay = jnp.exp(b[..., -1] + m_st - m_new)
        c_new = decay[..., None, None] * c_st + jnp.einsum('bhs,bshv,bshk->bhvk', w_s, vc, kc)
        n_new = decay[..., None] * n_st + jnp.einsum('bhs,bshk->bhk', w_s, kc)
        return (c_new, n_new, m_new), h

    init = (jnp.zeros((bsz, nh, dh, dh), f32), jnp.zeros((bsz, nh, dh), f32), jnp.zeros((bsz, nh), f32))
    xs = (to_chunks(q), to_chunks(k), to_chunks(v), to_chunks(log_i), to_chunks(log_f))
    _, hs = lax.scan(step, init, xs)
    return jnp.moveaxis(hs, 0, 1).reshape(bsz, seq, nh, dh)


def mlstm_mixer(cq, ck, cv, ci, cf, co, conv_w, gate_b, norm_g):
    bsz, seq, _ = cq.shape
    qk = jax.nn.silu(causal_depthwise_conv(jnp.concatenate([cq, ck], axis=-1), conv_w))
    q, k = jnp.split(qk, 2, axis=-1)
    shp = (bsz, seq, MLSTM_HEADS, HEAD_DIM)
    gb = gate_b.astype(jnp.float32)
    h = mlstm_chunkwise(q.reshape(shp), k.reshape(shp), cv.reshape(shp),
                        ci.astype(jnp.float32) + gb[:MLSTM_HEADS], cf.astype(jnp.float32) + gb[MLSTM_HEADS:])
    h = h * lax.rsqrt(jnp.mean(h * h, axis=-1, keepdims=True) + EPS)
    h = h * norm_g.astype(jnp.float32).reshape(MLSTM_HEADS, HEAD_DIM)
    out = jax.nn.sigmoid(co.astype(jnp.float32)) * h.reshape(bsz, seq, GROUP_W)
    return out.astype(cq.dtype)


def sgu_mixer(gu, gv, ln_g, ln_b, w_s, b_s):
    bsz, seq, _ = gu.shape
    nc = seq // SGU_CHUNK
    u = jax.nn.gelu(gu)
    v = layernorm(jax.nn.gelu(gv), ln_g, ln_b)
    vb = v.reshape(bsz, nc, SGU_CHUNK, SGU_GROUPS, SGU_GROUP_W)
    w = w_s.astype(v.dtype) * jnp.tril(jnp.ones((SGU_CHUNK, SGU_CHUNK), v.dtype))
    mixed = jnp.einsum('gts,bnsgc->bntgc', w, vb) + b_s.astype(v.dtype).T[None, None, :, :, None]
    return u * mixed.reshape(bsz, seq, GROUP_W)


def hier_moe(h, wg, bg, we, be, w_gate, w_up, w_down):
    bsz, seq, d = h.shape
    n = bsz * seq
    xf = h.reshape(n, d)
    g_logits = (xf @ wg + bg).astype(jnp.float32)
    g_prob = jax.nn.softmax(g_logits, axis=-1)
    g_sel = jnp.argmax(g_logits, axis=-1)
    p_g = jnp.take_along_axis(g_prob, g_sel[:, None], axis=1)[:, 0]
    e_logits = (xf @ we + be).astype(jnp.float32).reshape(n, N_EXPERT_GROUPS, EXPERTS_PER_GROUP)
    e_in = jnp.take_along_axis(e_logits, g_sel[:, None, None], axis=1)[:, 0]
    top_v, top_i = lax.top_k(e_in, TOP_K)
    w_pair = jax.nn.softmax(top_v, axis=-1) * p_g[:, None]
    expert_id = (g_sel[:, None] * EXPERTS_PER_GROUP + top_i).reshape(-1).astype(jnp.int32)
    tok = jnp.repeat(jnp.arange(n, dtype=jnp.int32), TOP_K)
    wts = w_pair.reshape(-1)
    m = n * TOP_K
    order = jnp.argsort(expert_id)
    se, st, sw = expert_id[order], tok[order], wts[order]
    counts = jnp.bincount(expert_id, length=N_EXPERTS)
    padded = ((counts + MOE_BLOCK - 1) // MOE_BLOCK) * MOE_BLOCK
    pad_end = jnp.cumsum(padded)
    pad_start = pad_end - padded
    start = jnp.cumsum(counts) - counts
    dest = pad_start[se] + jnp.arange(m, dtype=jnp.int32) - start[se]
    p_tot = ((m + MOE_BLOCK - 1) // MOE_BLOCK) * MOE_BLOCK + N_EXPERTS * MOE_BLOCK
    nblk = p_tot // MOE_BLOCK
    pad_tok = jnp.full((p_tot,), n, jnp.int32).at[dest].set(st)
    pad_w = jnp.zeros((p_tot,), jnp.float32).at[dest].set(sw)
    blk_e = jnp.minimum(jnp.searchsorted(pad_end, jnp.arange(nblk, dtype=jnp.int32) * MOE_BLOCK, side='right'),
                        N_EXPERTS - 1)
    x_ext = jnp.concatenate([xf, jnp.zeros((1, d), xf.dtype)], axis=0)

    def run_block(args):
        tb, wb, e = args
        xb = x_ext[tb]
        y = (jax.nn.silu(xb @ w_gate[e]) * (xb @ w_up[e])) @ w_down[e]
        return y * wb[:, None].astype(y.dtype)

    yb = lax.map(run_block, (pad_tok.reshape(nblk, MOE_BLOCK), pad_w.reshape(nblk, MOE_BLOCK), blk_e))
    out = jnp.zeros((n + 1, d), yb.dtype).at[pad_tok].add(yb.reshape(p_tot, d))[:n]
    return out.reshape(bsz, seq, d).astype(h.dtype)


def setup_inputs(seed: int = 0) -> dict:
    key = jax.random.key(seed)
    ks = jax.random.split(key, 40)
    f32 = jnp.float32

    def nrm(k, shape, scale):
        return jax.random.normal(k, shape, f32) * scale

    P, G, H = SSM_STATE, SSM_GROUPS, SSM_GROUP
    return {
        'x': nrm(ks[0], (BATCH, SEQ, D_MODEL), 1.0),
        'norm_mix_g': 1.0 + nrm(ks[1], (DEPTH, D_MODEL), 0.02),
        'w_in': nrm(ks[2], (DEPTH, D_MODEL, IN_W), D_MODEL ** -0.5),
        'attn_sinks': nrm(ks[3], (DEPTH, ATTN_Q_HEADS), 0.5),
        'ssm_a_re': -0.5 + nrm(ks[4], (DEPTH, G, P), 0.01),
        'ssm_a_im': math.pi * jnp.arange(P, dtype=f32) + nrm(ks[5], (DEPTH, G, P), 0.01),
        'ssm_b_re': nrm(ks[6], (DEPTH, G, P, H), (2 * H) ** -0.5),
        'ssm_b_im': nrm(ks[7], (DEPTH, G, P, H), (2 * H) ** -0.5),
        'ssm_c_re': nrm(ks[8], (DEPTH, G, H, P), P ** -0.5),
        'ssm_c_im': nrm(ks[9], (DEPTH, G, H, P), P ** -0.5),
        'ssm_d': nrm(ks[10], (DEPTH, GROUP_W), 1.0),
        'ssm_log_dt': jax.random.uniform(ks[11], (DEPTH, G), f32, math.log(SSM_DT_MIN), math.log(SSM_DT_MAX)),
        'ssm_glu_w': nrm(ks[12], (DEPTH, GROUP_W, GROUP_W), GROUP_W ** -0.5),
        'ssm_glu_b': nrm(ks[13], (DEPTH, GROUP_W), 0.01),
        'mlstm_conv_w': nrm(ks[14], (DEPTH, MLSTM_CONV, 2 * GROUP_W), MLSTM_CONV ** -0.5),
        'mlstm_gate_b': jnp.concatenate([nrm(ks[15], (DEPTH, MLSTM_HEADS), 0.1),
                                         jnp.linspace(3.0, 6.0, MLSTM_HEADS, dtype=f32)[None, :]
                                         + nrm(ks[16], (DEPTH, MLSTM_HEADS), 0.1)], axis=-1),
        'mlstm_norm_g': 1.0 + nrm(ks[17], (DEPTH, GROUP_W), 0.02),
        'sgu_ln_g': 1.0 + nrm(ks[18], (DEPTH, GROUP_W), 0.02),
        'sgu_ln_b': nrm(ks[19], (DEPTH, GROUP_W), 0.02),
        'sgu_w': nrm(ks[20], (DEPTH, SGU_GROUPS, SGU_CHUNK, SGU_CHUNK), SGU_CHUNK ** -0.5),
        'sgu_b': 1.0 + nrm(ks[21], (DEPTH, SGU_GROUPS, SGU_CHUNK), 0.02),
        'w_out': nrm(ks[22], (DEPTH, MIX_W, D_MODEL), MIX_W ** -0.5),
        'norm_ffn_g': 1.0 + nrm(ks[23], (DEPTH, D_MODEL), 0.02),
        'router_group_w': nrm(ks[24], (DEPTH, D_MODEL, N_EXPERT_GROUPS), D_MODEL ** -0.5),
        'router_group_b': nrm(ks[25], (DEPTH, N_EXPERT_GROUPS), 0.01),
        'router_expert_w': nrm(ks[26], (DEPTH, D_MODEL, N_EXPERTS), D_MODEL ** -0.5),
        'router_expert_b': nrm(ks[27], (DEPTH, N_EXPERTS), 0.01),
        'expert_w_gate': nrm(ks[28], (DEPTH, N_EXPERTS, D_MODEL, D_EXPERT), D_MODEL ** -0.5),
        'expert_w_up': nrm(ks[29], (DEPTH, N_EXPERTS, D_MODEL, D_EXPERT), D_MODEL ** -0.5),
        'expert_w_down': nrm(ks[30], (DEPTH, N_EXPERTS, D_EXPERT, D_MODEL), D_EXPERT ** -0.5),
        'norm_final_g': 1.0 + nrm(ks[31], (D_MODEL,), 0.02),
    }


def reference(x, norm_mix_g, w_in, attn_sinks, ssm_a_re, ssm_a_im, ssm_b_re, ssm_b_im, ssm_c_re, ssm_c_im,
              ssm_d, ssm_log_dt, ssm_glu_w, ssm_glu_b, mlstm_conv_w, mlstm_gate_b, mlstm_norm_g,
              sgu_ln_g, sgu_ln_b, sgu_w, sgu_b, w_out, norm_ffn_g, router_group_w, router_group_b,
              router_expert_w, router_expert_b, expert_w_gate, expert_w_up, expert_w_down, norm_final_g):
    bsz, seq, _ = x.shape
    split_idx = np.cumsum(IN_SPLITS)[:-1].tolist()
    for l in range(DEPTH):
        h = rmsnorm(x, norm_mix_g[l])
        z = h @ w_in[l]
        aq, ak, av, su, cq, ck, cv, ci, cf, co, gu, gv = jnp.split(z, split_idx, axis=-1)
        y_a = sliding_window_sink_attention(aq.reshape(bsz, seq, ATTN_Q_HEADS, HEAD_DIM),
                                            ak.reshape(bsz, seq, ATTN_KV_HEADS, HEAD_DIM),
                                            av.reshape(bsz, seq, ATTN_KV_HEADS, HEAD_DIM), attn_sinks[l])
        y_b = s5_mixer(su, ssm_a_re[l], ssm_a_im[l], ssm_b_re[l], ssm_b_im[l], ssm_c_re[l], ssm_c_im[l],
                       ssm_d[l], ssm_log_dt[l], ssm_glu_w[l], ssm_glu_b[l])
        y_c = mlstm_mixer(cq, ck, cv, ci, cf, co, mlstm_conv_w[l], mlstm_gate_b[l], mlstm_norm_g[l])
        y_d = sgu_mixer(gu, gv, sgu_ln_g[l], sgu_ln_b[l], sgu_w[l], sgu_b[l])
        mix = jnp.concatenate([y_a.astype(x.dtype), y_b.astype(x.dtype), y_c.astype(x.dtype), y_d.astype(x.dtype)], axis=-1)
        x = x + mix @ w_out[l]
        h = rmsnorm(x, norm_ffn_g[l])
        x = x + hier_moe(h, router_group_w[l], router_group_b[l], router_expert_w[l], router_expert_b[l],
                         expert_w_gate[l], expert_w_up[l], expert_w_down[l])
    return rmsnorm(x, norm_final_g)
```

```python
import functools
import math

import jax
import jax.numpy as jnp
from jax import lax
from jax.experimental import pallas as pl
from jax.experimental.pallas import tpu as pltpu

F32 = jnp.float32
BF16 = jnp.bfloat16
I32 = jnp.int32

D_MODEL = 1024
GROUP_W = 256
HEAD_DIM = 64
EPS = 1e-6
NEG = -1e30
WINDOW = 128
KV_HEADS = 2
SSM_GROUP = 16
SSM_GROUPS = 16
SSM_STATE = 64
SSM_CHUNK = 16
MLSTM_HEADS = 4
CHUNK = 128
CONV_K = 4
SGU_GROUPS = 4
N_GROUPS = 4
EPG = 8
N_EXPERTS = 32
D_EXPERT = 512
PAIRS = EPG * (EPG - 1) // 2
N_BUCKETS = N_GROUPS * PAIRS
BUCKET_PAD = 128
ROW_W = D_MODEL + 128

TOKEN_TILE = 512
SEQ_TILE = 512
FFN_BLOCK = 256
VMEM_LIMIT = 56 * 1024 * 1024


def _cparams(*sem):
    return pltpu.CompilerParams(dimension_semantics=sem, vmem_limit_bytes=VMEM_LIMIT)


def _rms(x, g):
    return x * lax.rsqrt(jnp.mean(x * x, axis=-1, keepdims=True) + EPS) * g


def _gelu(x):
    return 0.5 * x * (1.0 + jnp.tanh(math.sqrt(2.0 / math.pi) * (x + 0.044715 * (x * x * x))))


def _sigmoid(x):
    return 1.0 / (1.0 + jnp.exp(-x))


def _log_sigmoid(x):
    return jnp.minimum(x, 0.0) - jnp.log(1.0 + jnp.exp(-jnp.abs(x)))


def _dot(a, b):
    return jnp.dot(a, b, preferred_element_type=F32)


def _dot_nt(a, b):
    return lax.dot_general(a, b, (((1,), (1,)), ((), ())), preferred_element_type=F32)


def _dot_tn(a, b):
    return lax.dot_general(a, b, (((0,), (0,)), ((), ())), preferred_element_type=F32)


def _split_bf16(x):
    hi = x.astype(BF16)
    lo = (x - hi.astype(F32)).astype(BF16)
    return hi, lo


def _in_proj_kernel(x_ref, g_ref, wa_ref, wb_ref, wc_ref, wd_ref, wg_ref,
                    za_ref, zb_ref, zc_ref, zd_ref, zg_ref):
    hb = _rms(x_ref[...], g_ref[...]).astype(BF16)
    za_ref[...] = _dot(hb, wa_ref[...]).astype(BF16)
    zb_ref[...] = _dot(hb, wb_ref[...]).astype(BF16)
    zc_ref[...] = _dot(hb, wc_ref[...]).astype(BF16)
    zd_ref[...] = _dot(hb, wd_ref[...]).astype(BF16)
    zg_ref[...] = _dot(hb, wg_ref[...])


def _in_proj(x2, g, wa, wb, wc, wd, wg):
    n = x2.shape[0]
    tm = TOKEN_TILE
    row = lambda w: pl.BlockSpec((tm, w), lambda i: (i, 0))
    full = lambda a: pl.BlockSpec(a.shape, lambda i: (0,) * a.ndim)
    widths = (wa.shape[1], wb.shape[1], wc.shape[1], wd.shape[1])
    return pl.pallas_call(
        _in_proj_kernel,
        grid=(n // tm,),
        in_specs=[row(D_MODEL), full(g), full(wa), full(wb), full(wc), full(wd), full(wg)],
        out_specs=[row(w) for w in widths] + [row(128)],
        out_shape=[jax.ShapeDtypeStruct((n, w), BF16) for w in widths]
        + [jax.ShapeDtypeStruct((n, 128), F32)],
        compiler_params=_cparams("parallel"),
        name="in_proj",
    )(x2, g, wa, wb, wc, wd, wg)


def _attn_kernel(sink_ref, cur_ref, prev_ref, o_ref):
    first = pl.program_id(1) == 0
    nblk = cur_ref.shape[0] // WINDOW
    row = lax.broadcasted_iota(I32, (2 * WINDOW, 2 * WINDOW), 0)
    col = lax.broadcasted_iota(I32, (2 * WINDOW, 2 * WINDOW), 1)
    qi = jnp.where(row >= WINDOW, row - WINDOW, row)
    band = (col <= qi + WINDOW) & (col > qi)
    row1 = lax.broadcasted_iota(I32, (2 * WINDOW, 1), 0)
    for j in range(nblk):
        cur = cur_ref[j * WINDOW:(j + 1) * WINDOW, :]
        if j == 0:
            prev = prev_ref[...]
            mask = band & ((col >= WINDOW) | jnp.logical_not(first))
        else:
            prev = cur_ref[(j - 1) * WINDOW:j * WINDOW, :]
            mask = band
        outs = []
        for g in range(KV_HEADS):
            q2 = jnp.concatenate([cur[:, (2 * g) * HEAD_DIM:(2 * g + 1) * HEAD_DIM],
                                  cur[:, (2 * g + 1) * HEAD_DIM:(2 * g + 2) * HEAD_DIM]], axis=0)
            ko = GROUP_W + g * HEAD_DIM
            vo = GROUP_W + KV_HEADS * HEAD_DIM + g * HEAD_DIM
            kk = jnp.concatenate([prev[:, ko:ko + HEAD_DIM], cur[:, ko:ko + HEAD_DIM]], axis=0)
            vv = jnp.concatenate([prev[:, vo:vo + HEAD_DIM], cur[:, vo:vo + HEAD_DIM]], axis=0)
            s = _dot_nt(q2, kk) * (HEAD_DIM ** -0.5)
            s = jnp.where(mask, s, NEG)
            sink = jnp.where(row1 < WINDOW, sink_ref[2 * g], sink_ref[2 * g + 1])
            m = jnp.maximum(jnp.max(s, axis=-1, keepdims=True), sink)
            p = jnp.exp(s - m)
            l = jnp.sum(p, axis=-1, keepdims=True) + jnp.exp(sink - m)
            o = _dot(p.astype(BF16), vv) / l
            outs.append(o[:WINDOW])
            outs.append(o[WINDOW:])
        o_ref[j * WINDOW:(j + 1) * WINDOW, :] = jnp.concatenate(outs, axis=-1).astype(o_ref.dtype)


def _attention(za3, sinks):
    b, s, w = za3.shape
    ts = SEQ_TILE
    per = ts // WINDOW
    return pl.pallas_call(
        _attn_kernel,
        grid=(b, s // ts),
        in_specs=[pl.BlockSpec(memory_space=pltpu.SMEM),
                  pl.BlockSpec((None, ts, w), lambda bi, i: (bi, i, 0)),
                  pl.BlockSpec((None, WINDOW, w), lambda bi, i: (bi, jnp.maximum(i * per - 1, 0), 0))],
        out_specs=pl.BlockSpec((None, ts, GROUP_W), lambda bi, i: (bi, i, 0)),
        out_shape=jax.ShapeDtypeStruct((b, s, GROUP_W), BF16),
        compiler_params=_cparams("parallel", "parallel"),
        name="attn",
    )(sinks, za3, za3)


def _ssm_kernel(u_ref, bre_ref, bim_ref, t_ref, cre_ref, cim_ref, are_ref, aim_ref, y_ref,
                vre_ref, vim_ref, sre_ref, sim_ref, *, batch):
    u = u_ref[...]
    vre_ref[...] = _dot(u, bre_ref[...])
    vim_ref[...] = _dot(u, bim_ref[...])
    nchunk = u.shape[0] // batch
    are = are_ref[...]
    aim = aim_ref[...]

    def step(c, carry):
        sre, sim = carry
        r0 = pl.multiple_of(c * batch, batch)
        sre_ref[pl.ds(r0, batch), :] = sre
        sim_ref[pl.ds(r0, batch), :] = sim
        vre = vre_ref[pl.ds(r0, batch), :]
        vim = vim_ref[pl.ds(r0, batch), :]
        return (are * sre - aim * sim + vre, are * sim + aim * sre + vim)

    zero = jnp.zeros((batch, SSM_STATE), F32)
    lax.fori_loop(0, nchunk, step, (zero, zero))
    y = _dot(u, t_ref[...])
    y = y + _dot(sre_ref[...].astype(BF16), cre_ref[...])
    y = y + _dot(sim_ref[...].astype(BF16), cim_ref[...])
    y_ref[...] = y.astype(y_ref.dtype)


def _ssm_matrices(a_re, a_im, b_re, b_im, c_re, c_im, d_skip, log_dt):
    L = SSM_CHUNK
    a = lax.complex(a_re.astype(F32), a_im.astype(F32))
    dt = jnp.exp(log_dt.astype(F32))[:, None]
    adt = a * dt
    a_bar = jnp.exp(adt)
    b_bar = ((a_bar - 1.0) / a)[..., None] * lax.complex(b_re.astype(F32), b_im.astype(F32))
    c_mat = lax.complex(c_re.astype(F32), c_im.astype(F32))
    lag = jnp.arange(L + 1, dtype=F32)
    pw = jnp.exp(adt[None] * lag[:, None, None])
    kern = jnp.einsum('gop,dgp,gpi->dgoi', c_mat, pw[:L], b_bar).real
    sig = jnp.arange(L)[:, None]
    tau = jnp.arange(L)[None, :]
    d = tau - sig
    kt = kern[jnp.clip(d, 0, L - 1)]
    kt = jnp.where((d >= 0)[:, :, None, None, None], kt, 0.0)
    eye = jnp.eye(SSM_GROUP, dtype=F32)
    dsk = d_skip.astype(F32).reshape(SSM_GROUPS, SSM_GROUP)
    kt = kt + (d == 0)[:, :, None, None, None] * (dsk[:, :, None] * eye)[None, None]
    t_mat = kt.transpose(2, 0, 4, 1, 3).reshape(SSM_GROUPS, L * SSM_GROUP, L * SSM_GROUP)
    bm = pw[:L][::-1][:, :, :, None] * b_bar[None]
    bm = bm.transpose(1, 0, 3, 2).reshape(SSM_GROUPS, L * SSM_GROUP, SSM_STATE)
    cm = c_mat[None] * pw[1:][:, :, None, :]
    cm = cm.transpose(1, 3, 0, 2).reshape(SSM_GROUPS, SSM_STATE, L * SSM_GROUP)
    a_chunk = pw[L]
    return (bm.real.astype(BF16), bm.imag.astype(BF16), t_mat.astype(BF16),
            cm.real.astype(BF16), (-cm.imag).astype(BF16),
            a_chunk.real[:, None, :], a_chunk.imag[:, None, :])


def _ssm(zb3, mats):
    b, s, _ = zb3.shape
    L = SSM_CHUNK
    nc = s // L
    rows = nc * b
    u = zb3.reshape(b, nc, L, SSM_GROUPS, SSM_GROUP).transpose(3, 1, 0, 2, 4).reshape(SSM_GROUPS, rows, L * SSM_GROUP)
    bre, bim, tm, cre, cim, are, aim = mats
    g3 = lambda r, c: pl.BlockSpec((None, r, c), lambda g: (g, 0, 0))
    y = pl.pallas_call(
        functools.partial(_ssm_kernel, batch=b),
        grid=(SSM_GROUPS,),
        in_specs=[g3(rows, L * SSM_GROUP), g3(L * SSM_GROUP, SSM_STATE), g3(L * SSM_GROUP, SSM_STATE),
                  g3(L * SSM_GROUP, L * SSM_GROUP), g3(SSM_STATE, L * SSM_GROUP), g3(SSM_STATE, L * SSM_GROUP),
                  g3(1, SSM_STATE), g3(1, SSM_STATE)],
        out_specs=g3(rows, L * SSM_GROUP),
        out_shape=jax.ShapeDtypeStruct((SSM_GROUPS, rows, L * SSM_GROUP), BF16),
        scratch_shapes=[pltpu.VMEM((rows, SSM_STATE), F32) for _ in range(4)],
        compiler_params=_cparams("parallel"),
        name="ssm",
    )(u, bre, bim, tm, cre, cim, are, aim)
    return y.reshape(SSM_GROUPS, nc, b, L, SSM_GROUP).transpose(2, 1, 3, 0, 4).reshape(b, s, GROUP_W)


def _mlstm_kernel(zc_ref, zg_ref, convw_ref, gb_ref, ng_ref, o_ref,
                  ct_ref, n_ref, m_ref, tail_ref, qk_ref):
    ts = zc_ref.shape[0]
    hd = HEAD_DIM

    @pl.when(pl.program_id(1) == 0)
    def _():
        ct_ref[...] = jnp.zeros_like(ct_ref)
        n_ref[...] = jnp.zeros_like(n_ref)
        m_ref[...] = jnp.zeros_like(m_ref)
        tail_ref[...] = jnp.zeros_like(tail_ref)

    x = zc_ref[:, 0:2 * GROUP_W].astype(F32)
    xe = jnp.concatenate([tail_ref[...], x], axis=0)
    tail_ref[...] = x[ts - 8:, :]
    cw = convw_ref[...]
    acc = x * cw[CONV_K - 1:CONV_K, :]
    for sft in range(1, CONV_K):
        acc = acc + xe[8 - sft:8 - sft + ts, :] * cw[CONV_K - 1 - sft:CONV_K - sft, :]
    qk_ref[...] = acc * _sigmoid(acc)

    r_i = lax.broadcasted_iota(I32, (CHUNK, CHUNK), 0)
    c_i = lax.broadcasted_iota(I32, (CHUNK, CHUNK), 1)
    causal = c_i <= r_i
    tril = causal.astype(BF16)
    lane = lax.broadcasted_iota(I32, (CHUNK, 128), 1)

    def chunk(c, carry):
        r0 = pl.multiple_of(c * CHUNK, CHUNK)
        gates = zg_ref[pl.ds(r0, CHUNK), :] + gb_ref[...]
        g2 = jnp.where(lane < MLSTM_HEADS, gates, _log_sigmoid(gates))
        ghi, glo = _split_bf16(g2)
        cum = _dot(tril, ghi) + _dot(tril, glo)
        g2t = g2.T
        cumt = cum.T
        outs = []
        for h in range(MLSTM_HEADS):
            q = qk_ref[pl.ds(r0, CHUNK), h * hd:(h + 1) * hd]
            k = qk_ref[pl.ds(r0, CHUNK), GROUP_W + h * hd:GROUP_W + (h + 1) * hd] * (hd ** -0.5)
            v = zc_ref[pl.ds(r0, CHUNK), 2 * GROUP_W + h * hd:2 * GROUP_W + (h + 1) * hd]
            og = zc_ref[pl.ds(r0, CHUNK), 3 * GROUP_W + h * hd:3 * GROUP_W + (h + 1) * hd].astype(F32)
            fh = MLSTM_HEADS + h
            b_col = cum[:, fh:fh + 1]
            i_col = g2[:, h:h + 1]
            b_row = cumt[fh:fh + 1, :]
            i_row = g2t[h:h + 1, :]
            b_last = cum[CHUNK - 1:CHUNK, fh:fh + 1]
            m_prev = m_ref[h:h + 1, 0:1]
            n_prev = n_ref[h:h + 1, :]
            ct_prev = ct_ref[h]
            qb = q.astype(BF16)
            log_d = jnp.where(causal, b_col - b_row + i_row, NEG)
            log_inter = b_col + m_prev
            m_t = jnp.maximum(log_inter, jnp.max(log_d, axis=-1, keepdims=True))
            w = jnp.exp(log_d - m_t) * _dot_nt(qb, k.astype(BF16))
            w_inter = jnp.exp(log_inter - m_t)
            num = _dot(w.astype(BF16), v) + w_inter * _dot(qb, ct_prev.astype(BF16))
            den = jnp.sum(w, axis=-1, keepdims=True) + w_inter * jnp.sum(q * n_prev, axis=-1, keepdims=True)
            hh = num / jnp.maximum(jnp.abs(den), jnp.exp(-m_t))
            m_new = m_t[CHUNK - 1:CHUNK, :]
            w_s = jnp.exp(b_last - b_col + i_col - m_new)
            decay = jnp.exp(b_last + m_prev - m_new)
            kw = k * w_s
            ct_ref[h] = decay * ct_prev + _dot_tn(kw.astype(BF16), v)
            n_ref[h:h + 1, :] = decay * n_prev + jnp.sum(kw, axis=0, keepdims=True)
            m_ref[h:h + 1, :] = jnp.broadcast_to(m_new, (1, 128))
            hn = hh * lax.rsqrt(jnp.mean(hh * hh, axis=-1, keepdims=True) + EPS) * ng_ref[:, h * hd:(h + 1) * hd]
            outs.append(_sigmoid(og) * hn)
        o_ref[pl.ds(r0, CHUNK), :] = jnp.concatenate(outs, axis=-1).astype(o_ref.dtype)
        return carry

    lax.fori_loop(0, ts // CHUNK, chunk, 0)


def _mlstm(zc3, zg3, conv_w, gate_b_pad, norm_g):
    b, s, w = zc3.shape
    ts = SEQ_TILE
    full = lambda a: pl.BlockSpec(a.shape, lambda bi, i: (0,) * a.ndim)
    return pl.pallas_call(
        _mlstm_kernel,
        grid=(b, s // ts),
        in_specs=[pl.BlockSpec((None, ts, w), lambda bi, i: (bi, i, 0)),
                  pl.BlockSpec((None, ts, 128), lambda bi, i: (bi, i, 0)),
                  full(conv_w), full(gate_b_pad), full(norm_g)],
        out_specs=pl.BlockSpec((None, ts, GROUP_W), lambda bi, i: (bi, i, 0)),
        out_shape=jax.ShapeDtypeStruct((b, s, GROUP_W), BF16),
        scratch_shapes=[pltpu.VMEM((MLSTM_HEADS, HEAD_DIM, HEAD_DIM), F32),
                        pltpu.VMEM((8, HEAD_DIM), F32),
                        pltpu.VMEM((8, 128), F32),
                        pltpu.VMEM((8, 2 * GROUP_W), F32),
                        pltpu.VMEM((ts, 2 * GROUP_W), F32)],
        compiler_params=_cparams("parallel", "arbitrary"),
        name="mlstm",
    )(zc3, zg3, conv_w, gate_b_pad, norm_g)


def _sgu_kernel(zd_ref, lng_ref, lnb_ref, w_ref, bias_ref, o_ref):
    tm = zd_ref.shape[0]
    gw = GROUP_W // SGU_GROUPS
    u = _gelu(zd_ref[:, 0:GROUP_W].astype(F32))
    v = _gelu(zd_ref[:, GROUP_W:2 * GROUP_W].astype(F32))
    mu = jnp.mean(v, axis=-1, keepdims=True)
    vc = v - mu
    var = jnp.mean(vc * vc, axis=-1, keepdims=True)
    vn = (vc * lax.rsqrt(var + EPS) * lng_ref[...] + lnb_ref[...]).astype(BF16)
    for c in range(tm // CHUNK):
        rows = slice(c * CHUNK, (c + 1) * CHUNK)
        mixed = jnp.concatenate([_dot(w_ref[g], vn[rows, g * gw:(g + 1) * gw]) for g in range(SGU_GROUPS)], axis=-1)
        o_ref[rows, :] = (u[rows, :] * (mixed + bias_ref[...])).astype(o_ref.dtype)


def _sgu(zd, ln_g, ln_b, w_tril, bias):
    n = zd.shape[0]
    tm = TOKEN_TILE
    full = lambda a: pl.BlockSpec(a.shape, lambda i: (0,) * a.ndim)
    return pl.pallas_call(
        _sgu_kernel,
        grid=(n // tm,),
        in_specs=[pl.BlockSpec((tm, 2 * GROUP_W), lambda i: (i, 0)), full(ln_g), full(ln_b), full(w_tril), full(bias)],
        out_specs=pl.BlockSpec((tm, GROUP_W), lambda i: (i, 0)),
        out_shape=jax.ShapeDtypeStruct((n, GROUP_W), BF16),
        compiler_params=_cparams("parallel"),
        name="sgu",
    )(zd, ln_g, ln_b, w_tril, bias)


def _out_proj_kernel(x_ref, ya_ref, yb_ref, yc_ref, yd_ref, gluw_ref, glub_ref, wo_ref, ng_ref,
                     rwh_ref, rwl_ref, rb_ref,
                     x1_ref, hx_ref, meta_ref, cnt_ref, carry_ref):
    tm = x_ref.shape[0]

    @pl.when(pl.program_id(0) == 0)
    def _():
        carry_ref[...] = jnp.zeros_like(carry_ref)

    yb = _gelu(yb_ref[...].astype(F32))
    yb = yb * _sigmoid(_dot(yb.astype(BF16), gluw_ref[...]) + glub_ref[...])
    mix = _dot(ya_ref[...], wo_ref[0:GROUP_W, :])
    mix = mix + _dot(yb.astype(BF16), wo_ref[GROUP_W:2 * GROUP_W, :])
    mix = mix + _dot(yc_ref[...], wo_ref[2 * GROUP_W:3 * GROUP_W, :])
    mix = mix + _dot(yd_ref[...], wo_ref[3 * GROUP_W:4 * GROUP_W, :])
    x1 = x_ref[...] + mix
    x1_ref[...] = x1
    h = _rms(x1, ng_ref[...])
    hx_ref[:, 0:D_MODEL] = h

    hh, hl = _split_bf16(h)
    wh = rwh_ref[...]
    wl = rwl_ref[...]
    logits = _dot_nt(wh, hh) + _dot_nt(wh, hl) + _dot_nt(wl, hh) + rb_ref[...]
    gl = [logits[j:j + 1, :] for j in range(N_GROUPS)]
    gmax = functools.reduce(jnp.maximum, gl)
    gsel = jnp.full((1, tm), N_GROUPS - 1, I32)
    for j in range(N_GROUPS - 2, -1, -1):
        gsel = jnp.where(gl[j] == gmax, j, gsel)
    p_g = 1.0 / functools.reduce(jnp.add, [jnp.exp(v - gmax) for v in gl])
    e_in = []
    for i in range(EPG):
        v = logits[N_GROUPS + i:N_GROUPS + i + 1, :]
        for j in range(1, N_GROUPS):
            r = N_GROUPS + j * EPG + i
            v = jnp.where(gsel == j, logits[r:r + 1, :], v)
        e_in.append(v)
    v1 = functools.reduce(jnp.maximum, e_in)
    i1 = jnp.full((1, tm), EPG - 1, I32)
    for i in range(EPG - 2, -1, -1):
        i1 = jnp.where(e_in[i] == v1, i, i1)
    rest = [jnp.where(i1 == i, NEG, e_in[i]) for i in range(EPG)]
    v2 = functools.reduce(jnp.maximum, rest)
    i2 = jnp.full((1, tm), EPG - 1, I32)
    for i in range(EPG - 2, -1, -1):
        i2 = jnp.where((rest[i] == v2) & (i1 != i), i, i2)
    e2 = jnp.exp(v2 - v1)
    w1 = p_g / (1.0 + e2)
    w2 = p_g * e2 / (1.0 + e2)
    lo = jnp.minimum(i1, i2)
    hi = jnp.maximum(i1, i2)
    w_lo = jnp.where(i1 < i2, w1, w2)
    w_hi = jnp.where(i1 < i2, w2, w1)
    bucket = gsel * PAIRS + ((lo * (2 * EPG - 1 - lo)) >> 1) + (hi - lo - 1)

    kid = lax.broadcasted_iota(I32, (BUCKET_PAD, tm), 0)
    onehot = (kid == bucket).astype(F32)
    s_i = lax.broadcasted_iota(I32, (tm, tm), 0)
    t_i = lax.broadcasted_iota(I32, (tm, tm), 1)
    prefix = _dot(onehot.astype(BF16), (s_i <= t_i).astype(BF16))
    carry = carry_ref[...]
    rank = jnp.sum(onehot * (prefix - 1.0 + carry[:, 0:1]), axis=0, keepdims=True)
    carry = carry + prefix[:, tm - 1:tm]
    carry_ref[...] = carry
    cnt_ref[...] = carry
    meta_ref[...] = jnp.concatenate([bucket, rank.astype(I32), jnp.zeros((6, tm), I32)], axis=0)
    wrows = jnp.concatenate([w_lo, w_hi, jnp.zeros((126, tm), F32)], axis=0)
    for c in range(tm // 128):
        hx_ref[c * 128:(c + 1) * 128, D_MODEL:ROW_W] = wrows[:, c * 128:(c + 1) * 128].T


def _out_proj(x2, ya, yb, yc, yd, glu_w, glu_b, w_out, norm_g, rw_hi, rw_lo, rb):
    n = x2.shape[0]
    tm = TOKEN_TILE
    row = lambda w: pl.BlockSpec((tm, w), lambda i: (i, 0))
    full = lambda a: pl.BlockSpec(a.shape, lambda i: (0,) * a.ndim)
    return pl.pallas_call(
        _out_proj_kernel,
        grid=(n // tm,),
        in_specs=[row(D_MODEL), row(GROUP_W), row(GROUP_W), row(GROUP_W), row(GROUP_W),
                  full(glu_w), full(glu_b), full(w_out), full(norm_g), full(rw_hi), full(rw_lo), full(rb)],
        out_specs=[row(D_MODEL), row(ROW_W), pl.BlockSpec((8, tm), lambda i: (0, i)),
                   pl.BlockSpec((BUCKET_PAD, 128), lambda i: (0, 0))],
        out_shape=[jax.ShapeDtypeStruct((n, D_MODEL), F32), jax.ShapeDtypeStruct((n, ROW_W), F32),
                   jax.ShapeDtypeStruct((8, n), I32), jax.ShapeDtypeStruct((BUCKET_PAD, 128), F32)],
        scratch_shapes=[pltpu.VMEM((BUCKET_PAD, 128), F32)],
        compiler_params=_cparams("arbitrary"),
        name="out_proj",
    )(x2, ya, yb, yc, yd, glu_w, glu_b, w_out, norm_g, rw_hi, rw_lo, rb)


def _row_copy(src_ref, src_row, dst_ref, dst_row, sem):
    return pltpu.make_async_copy(src_ref.at[pl.ds(src_row, 1)], dst_ref.at[pl.ds(dst_row, 1)], sem)


def _dispatch_kernel(dest_ref, hx_ref, xs_in_ref, xs_ref, sem):
    del xs_in_ref
    tm = hx_ref.shape[0]

    def start(t, c):
        _row_copy(hx_ref, t, xs_ref, dest_ref[0, 0, t], sem).start()
        return c

    def wait(t, c):
        _row_copy(hx_ref, t, xs_ref, dest_ref[0, 0, t], sem).wait()
        return c

    lax.fori_loop(0, tm, start, 0)
    lax.fori_loop(0, tm, wait, 0)


def _dispatch(dest3, hx, p_tot):
    n = hx.shape[0]
    tm = TOKEN_TILE
    xs0 = jnp.zeros((p_tot, ROW_W), F32)
    return pl.pallas_call(
        _dispatch_kernel,
        grid=(n // tm,),
        in_specs=[pl.BlockSpec((1, 1, tm), lambda i: (i, 0, 0), memory_space=pltpu.SMEM),
                  pl.BlockSpec((tm, ROW_W), lambda i: (i, 0)),
                  pl.BlockSpec(memory_space=pl.ANY)],
        out_specs=pl.BlockSpec(memory_space=pl.ANY),
        out_shape=jax.ShapeDtypeStruct((p_tot, ROW_W), F32),
        input_output_aliases={2: 0},
        scratch_shapes=[pltpu.SemaphoreType.DMA],
        compiler_params=_cparams("arbitrary"),
        name="dispatch",
    )(dest3, hx, xs0)


def _ffn_kernel(ea_ref, eb_ref, nact_ref, xs_ref, wgu_a_ref, wd_a_ref, wgu_b_ref, wd_b_ref, ys_ref):
    del ea_ref, eb_ref
    active = pl.program_id(0) < nact_ref[0]

    @pl.when(jnp.logical_not(active))
    def _():
        ys_ref[...] = jnp.zeros_like(ys_ref)

    @pl.when(active)
    def _():
        xb = xs_ref[:, 0:D_MODEL].astype(BF16)

        def expert(wgu_ref, wd_ref):
            gu = _dot(xb, wgu_ref[...])
            g = gu[:, 0:D_EXPERT]
            act = (g * _sigmoid(g) * gu[:, D_EXPERT:2 * D_EXPERT]).astype(BF16)
            return _dot(act, wd_ref[...])

        ya = expert(wgu_a_ref, wd_a_ref)
        yb = expert(wgu_b_ref, wd_b_ref)
        ys_ref[...] = ya * xs_ref[:, D_MODEL:D_MODEL + 1] + yb * xs_ref[:, D_MODEL + 1:D_MODEL + 2]


def _ffn(ea, eb, nact, xs, wgu, wd):
    p_tot = xs.shape[0]
    bm = FFN_BLOCK
    nblk = p_tot // bm
    wspec = lambda e_idx, shape: pl.BlockSpec((None,) + shape, lambda j, ea, eb, na: ((ea, eb)[e_idx][j], 0, 0))
    return pl.pallas_call(
        _ffn_kernel,
        grid_spec=pltpu.PrefetchScalarGridSpec(
            num_scalar_prefetch=3,
            grid=(nblk,),
            in_specs=[pl.BlockSpec((bm, ROW_W), lambda j, ea, eb, na: (j, 0)),
                      wspec(0, (D_MODEL, 2 * D_EXPERT)), wspec(0, (D_EXPERT, D_MODEL)),
                      wspec(1, (D_MODEL, 2 * D_EXPERT)), wspec(1, (D_EXPERT, D_MODEL))],
            out_specs=pl.BlockSpec((bm, D_MODEL), lambda j, ea, eb, na: (j, 0)),
        ),
        out_shape=jax.ShapeDtypeStruct((p_tot, D_MODEL), F32),
        compiler_params=_cparams("arbitrary"),
        name="ffn",
    )(ea, eb, nact, xs, wgu, wd, wgu, wd)


def _combine_kernel(dest_ref, x1_ref, ys_ref, ng_ref, o_ref, buf_ref, sem, *, final_norm):
    tm = x1_ref.shape[0]

    def start(t, c):
        _row_copy(ys_ref, dest_ref[0, 0, t], buf_ref, t, sem).start()
        return c

    def wait(t, c):
        _row_copy(ys_ref, dest_ref[0, 0, t], buf_ref, t, sem).wait()
        return c

    lax.fori_loop(0, tm, start, 0)
    lax.fori_loop(0, tm, wait, 0)
    x2 = x1_ref[...] + buf_ref[...]
    o_ref[...] = _rms(x2, ng_ref[...]) if final_norm else x2


def _combine(dest3, x1, ys, norm_g, final_norm):
    n = x1.shape[0]
    tm = TOKEN_TILE
    return pl.pallas_call(
        functools.partial(_combine_kernel, final_norm=final_norm),
        grid=(n // tm,),
        in_specs=[pl.BlockSpec((1, 1, tm), lambda i: (i, 0, 0), memory_space=pltpu.SMEM),
                  pl.BlockSpec((tm, D_MODEL), lambda i: (i, 0)),
                  pl.BlockSpec(memory_space=pl.ANY),
                  pl.BlockSpec((1, D_MODEL), lambda i: (0, 0))],
        out_specs=pl.BlockSpec((tm, D_MODEL), lambda i: (i, 0)),
        out_shape=jax.ShapeDtypeStruct((n, D_MODEL), F32),
        scratch_shapes=[pltpu.VMEM((tm, D_MODEL), F32), pltpu.SemaphoreType.DMA],
        compiler_params=_cparams("arbitrary"),
        name="combine",
    )(dest3, x1, ys, norm_g)


def _routing_tables(meta, counts, n):
    bm = FFN_BLOCK
    nblk = n // bm + N_BUCKETS
    bucket = meta[0]
    rank = meta[1]
    cnt = counts[:N_BUCKETS, 0].astype(I32)
    padded = ((cnt + bm - 1) // bm) * bm
    pad_end = jnp.cumsum(padded)
    pad_start = pad_end - padded
    dest = pad_start[bucket] + rank
    nact = (pad_end[-1] // bm).astype(I32)
    j = jnp.arange(nblk, dtype=I32)
    blk = jnp.minimum(j, jnp.maximum(nact - 1, 0))
    bkt = jnp.minimum(jnp.searchsorted(pad_end, blk * bm, side='right'), N_BUCKETS - 1).astype(I32)
    grp = bkt // PAIRS
    pidx = bkt % PAIRS
    lo_tab, hi_tab = [], []
    for a in range(EPG):
        for b in range(a + 1, EPG):
            lo_tab.append(a)
            hi_tab.append(b)
    ea = grp * EPG + jnp.asarray(lo_tab, I32)[pidx]
    eb = grp * EPG + jnp.asarray(hi_tab, I32)[pidx]
    return dest.astype(I32), ea.astype(I32), eb.astype(I32), nact.reshape(1), nblk * bm


def _layer(x2, b, s, p, final_g):
    n = b * s
    za, zb, zc, zd, zg = _in_proj(x2, p['norm_mix_g'], p['wa'], p['wb'], p['wc'], p['wd'], p['wg'])
    ya = _attention(za.reshape(b, s, -1), p['sinks']).reshape(n, GROUP_W)
    yb = _ssm(zb.reshape(b, s, GROUP_W), p['ssm']).reshape(n, GROUP_W)
    yc = _mlstm(zc.reshape(b, s, -1), zg.reshape(b, s, 128), p['conv_w'], p['gate_b'], p['mlstm_norm_g']).reshape(n, GROUP_W)
    yd = _sgu(zd, p['sgu_ln_g'], p['sgu_ln_b'], p['sgu_w'], p['sgu_bias'])
    x1, hx, meta, counts = _out_proj(x2, ya, yb, yc, yd, p['glu_w'], p['glu_b'], p['w_out'], p['norm_ffn_g'],
                                     p['rw_hi'], p['rw_lo'], p['rb'])
    dest, ea, eb, nact, p_tot = _routing_tables(meta, counts, n)
    dest3 = dest.reshape(n // TOKEN_TILE, 1, TOKEN_TILE)
    xs = _dispatch(dest3, hx, p_tot)
    ys = _ffn(ea, eb, nact, xs, p['wgu'], p['wdn'])
    if final_g is None:
        return _combine(dest3, x1, ys, p['norm_ffn_g'], False)
    return _combine(dest3, x1, ys, final_g, True)


def _prep_layer(l, norm_mix_g, w_in, attn_sinks, ssm_a_re, ssm_a_im, ssm_b_re, ssm_b_im, ssm_c_re, ssm_c_im,
                ssm_d, ssm_log_dt, ssm_glu_w, ssm_glu_b, mlstm_conv_w, mlstm_gate_b, mlstm_norm_g,
                sgu_ln_g, sgu_ln_b, sgu_w, sgu_b, w_out, norm_ffn_g, router_group_w, router_group_b,
                router_expert_w, router_expert_b, expert_w_gate, expert_w_up, expert_w_down):
    w = w_in[l]
    o_su = 2 * GROUP_W
    o_c = o_su + GROUP_W
    o_ci = o_c + 3 * GROUP_W
    o_co = o_ci + 2 * MLSTM_HEADS
    o_d = o_co + GROUP_W
    wc = jnp.concatenate([w[:, o_c:o_ci], w[:, o_co:o_d]], axis=1)
    wg = jnp.pad(w[:, o_ci:o_co], ((0, 0), (0, 128 - 2 * MLSTM_HEADS)))
    rw = jnp.concatenate([router_group_w[l], router_expert_w[l]], axis=1).T.astype(F32)
    rw = jnp.pad(rw, ((0, 4), (0, 0)))
    rw_hi = rw.astype(BF16)
    rw_lo = (rw - rw_hi.astype(F32)).astype(BF16)
    rb = jnp.pad(jnp.concatenate([router_group_b[l], router_expert_b[l]]).astype(F32), (0, 4))[:, None]
    tril = jnp.tril(jnp.ones((CHUNK, CHUNK), F32))
    gw = GROUP_W // SGU_GROUPS
    return dict(
        norm_mix_g=norm_mix_g[l][None, :].astype(F32),
        wa=w[:, 0:o_su].astype(BF16), wb=w[:, o_su:o_c].astype(BF16), wc=wc.astype(BF16),
        wd=w[:, o_d:].astype(BF16), wg=wg.astype(BF16),
        sinks=attn_sinks[l].astype(F32),
        ssm=_ssm_matrices(ssm_a_re[l], ssm_a_im[l], ssm_b_re[l], ssm_b_im[l], ssm_c_re[l], ssm_c_im[l],
                          ssm_d[l], ssm_log_dt[l]),
        glu_w=ssm_glu_w[l].astype(BF16), glu_b=ssm_glu_b[l][None, :].astype(F32),
        conv_w=mlstm_conv_w[l].astype(F32),
        gate_b=jnp.pad(mlstm_gate_b[l].astype(F32), (0, 128 - 2 * MLSTM_HEADS))[None, :],
        mlstm_norm_g=mlstm_norm_g[l][None, :].astype(F32),
        sgu_ln_g=sgu_ln_g[l][None, :].astype(F32), sgu_ln_b=sgu_ln_b[l][None, :].astype(F32),
        sgu_w=(sgu_w[l].astype(F32) * tril).astype(BF16),
        sgu_bias=jnp.repeat(sgu_b[l].astype(F32).T, gw, axis=1),
        w_out=w_out[l].astype(BF16), norm_ffn_g=norm_ffn_g[l][None, :].astype(F32),
        rw_hi=rw_hi, rw_lo=rw_lo, rb=rb,
        wgu=jnp.concatenate([expert_w_gate[l], expert_w_up[l]], axis=-1).astype(BF16),
        wdn=expert_w_down[l].astype(BF16),
    )


def kernel(x, norm_mix_g, w_in, attn_sinks, ssm_a_re, ssm_a_im, ssm_b_re, ssm_b_im, ssm_c_re, ssm_c_im, ssm_d, ssm_log_dt, ssm_glu_w, ssm_glu_b, mlstm_conv_w, mlstm_gate_b, mlstm_norm_g, sgu_ln_g, sgu_ln_b, sgu_w, sgu_b, w_out, norm_ffn_g, router_group_w, router_group_b, router_expert_w, router_expert_b, expert_w_gate, expert_w_up, expert_w_down, norm_final_g):
    b, s, d = x.shape
    depth = w_in.shape[0]
    x2 = x.reshape(b * s, d).astype(F32)
    for l in range(depth):
        p = _prep_layer(l, norm_mix_g, w_in, attn_sinks, ssm_a_re, ssm_a_im, ssm_b_re, ssm_b_im, ssm_c_re,
                        ssm_c_im, ssm_d, ssm_log_dt, ssm_glu_w, ssm_glu_b, mlstm_conv_w, mlstm_gate_b,
                        mlstm_norm_g, sgu_ln_g, sgu_ln_b, sgu_w, sgu_b, w_out, norm_ffn_g, router_group_w,
                        router_group_b, router_expert_w, router_expert_b, expert_w_gate, expert_w_up,
                        expert_w_down)
        final_g = norm_final_g[None, :].astype(F32) if l == depth - 1 else None
        x2 = _layer(x2, b, s, p, final_g)
    return x2.reshape(b, s, d).astype(x.dtype)
```

```python
import functools
import math

import jax
import jax.numpy as jnp
from jax import lax
from jax.experimental import pallas as pl
from jax.experimental.pallas import tpu as pltpu

F32 = jnp.float32
BF16 = jnp.bfloat16
I32 = jnp.int32

D_MODEL = 1024
GROUP_W = 256
HEAD_DIM = 64
EPS = 1e-6
NEG = -1e30
WINDOW = 128
KV_HEADS = 2
SSM_GROUP = 16
SSM_GROUPS = 16
SSM_STATE = 64
SSM_CHUNK = 16
MLSTM_HEADS = 4
CHUNK = 128
CONV_K = 4
SGU_GROUPS = 4
N_GROUPS = 4
EPG = 8
N_EXPERTS = 32
D_EXPERT = 512
PAIRS = EPG * (EPG - 1) // 2
N_BUCKETS = N_GROUPS * PAIRS
BUCKET_PAD = 128
ROW_W = D_MODEL + 128

TOKEN_TILE = 512
SEQ_TILE = 512
SSM_TILE = 4096
FFN_BLOCK = 256
ROW_DMA_UNROLL = 8
VMEM_LIMIT = 56 * 1024 * 1024


def _cparams(*sem):
    return pltpu.CompilerParams(dimension_semantics=sem, vmem_limit_bytes=VMEM_LIMIT)


def _rms(x, g):
    return x * lax.rsqrt(jnp.mean(x * x, axis=-1, keepdims=True) + EPS) * g


def _gelu(x):
    return 0.5 * x * (1.0 + jnp.tanh(math.sqrt(2.0 / math.pi) * (x + 0.044715 * (x * x * x))))


def _sigmoid(x):
    return 1.0 / (1.0 + jnp.exp(-x))


def _log_sigmoid(x):
    return jnp.minimum(x, 0.0) - jnp.log(1.0 + jnp.exp(-jnp.abs(x)))


def _dot(a, b):
    return jnp.dot(a, b, preferred_element_type=F32)


def _dot_nt(a, b):
    return lax.dot_general(a, b, (((1,), (1,)), ((), ())), preferred_element_type=F32)


def _dot_tn(a, b):
    return lax.dot_general(a, b, (((0,), (0,)), ((), ())), preferred_element_type=F32)


def _split_bf16(x):
    hi = x.astype(BF16)
    lo = (x - hi.astype(F32)).astype(BF16)
    return hi, lo


def _in_proj_kernel(x_ref, g_ref, wa_ref, wb_ref, wc_ref, wd_ref, wg_ref,
                    za_ref, zb_ref, zc_ref, zd_ref, zg_ref):
    hb = _rms(x_ref[...], g_ref[...]).astype(BF16)
    za_ref[...] = _dot(hb, wa_ref[...]).astype(BF16)
    zb_ref[...] = _dot(hb, wb_ref[...]).astype(BF16)
    zc_ref[...] = _dot(hb, wc_ref[...]).astype(BF16)
    zd_ref[...] = _dot(hb, wd_ref[...]).astype(BF16)
    zg_ref[...] = _dot(hb, wg_ref[...])


def _in_proj(x2, g, wa, wb, wc, wd, wg):
    n = x2.shape[0]
    tm = TOKEN_TILE
    row = lambda w: pl.BlockSpec((tm, w), lambda i: (i, 0))
    full = lambda a: pl.BlockSpec(a.shape, lambda i: (0,) * a.ndim)
    widths = (wa.shape[1], wb.shape[1], wc.shape[1], wd.shape[1])
    return pl.pallas_call(
        _in_proj_kernel,
        grid=(n // tm,),
        in_specs=[row(D_MODEL), full(g), full(wa), full(wb), full(wc), full(wd), full(wg)],
        out_specs=[row(w) for w in widths] + [row(128)],
        out_shape=[jax.ShapeDtypeStruct((n, w), BF16) for w in widths]
        + [jax.ShapeDtypeStruct((n, 128), F32)],
        compiler_params=_cparams("parallel"),
        name="in_proj",
    )(x2, g, wa, wb, wc, wd, wg)


def _attn_kernel(sink_ref, cur_ref, prev_ref, o_ref):
    first = pl.program_id(1) == 0
    nblk = cur_ref.shape[0] // WINDOW
    row = lax.broadcasted_iota(I32, (2 * WINDOW, 2 * WINDOW), 0)
    col = lax.broadcasted_iota(I32, (2 * WINDOW, 2 * WINDOW), 1)
    qi = jnp.where(row >= WINDOW, row - WINDOW, row)
    band = (col <= qi + WINDOW) & (col > qi)
    row1 = lax.broadcasted_iota(I32, (2 * WINDOW, 1), 0)
    for j in range(nblk):
        cur = cur_ref[j * WINDOW:(j + 1) * WINDOW, :]
        if j == 0:
            prev = prev_ref[...]
            mask = band & ((col >= WINDOW) | jnp.logical_not(first))
        else:
            prev = cur_ref[(j - 1) * WINDOW:j * WINDOW, :]
            mask = band
        outs = []
        for g in range(KV_HEADS):
            q2 = jnp.concatenate([cur[:, (2 * g) * HEAD_DIM:(2 * g + 1) * HEAD_DIM],
                                  cur[:, (2 * g + 1) * HEAD_DIM:(2 * g + 2) * HEAD_DIM]], axis=0)
            ko = GROUP_W + g * HEAD_DIM
            vo = GROUP_W + KV_HEADS * HEAD_DIM + g * HEAD_DIM
            kk = jnp.concatenate([prev[:, ko:ko + HEAD_DIM], cur[:, ko:ko + HEAD_DIM]], axis=0)
            vv = jnp.concatenate([prev[:, vo:vo + HEAD_DIM], cur[:, vo:vo + HEAD_DIM]], axis=0)
            s = _dot_nt(q2, kk) * (HEAD_DIM ** -0.5)
            s = jnp.where(mask, s, NEG)
            sink = jnp.where(row1 < WINDOW, sink_ref[2 * g], sink_ref[2 * g + 1])
            m = jnp.maximum(jnp.max(s, axis=-1, keepdims=True), sink)
            p = jnp.exp(s - m)
            l = jnp.sum(p, axis=-1, keepdims=True) + jnp.exp(sink - m)
            o = _dot(p.astype(BF16), vv) / l
            outs.append(o[:WINDOW])
            outs.append(o[WINDOW:])
        o_ref[j * WINDOW:(j + 1) * WINDOW, :] = jnp.concatenate(outs, axis=-1).astype(o_ref.dtype)


def _attention(za3, sinks):
    b, s, w = za3.shape
    ts = SEQ_TILE
    per = ts // WINDOW
    return pl.pallas_call(
        _attn_kernel,
        grid=(b, s // ts),
        in_specs=[pl.BlockSpec(memory_space=pltpu.SMEM),
                  pl.BlockSpec((None, ts, w), lambda bi, i: (bi, i, 0)),
                  pl.BlockSpec((None, WINDOW, w), lambda bi, i: (bi, jnp.maximum(i * per - 1, 0), 0))],
        out_specs=pl.BlockSpec((None, ts, GROUP_W), lambda bi, i: (bi, i, 0)),
        out_shape=jax.ShapeDtypeStruct((b, s, GROUP_W), BF16),
        compiler_params=_cparams("parallel", "parallel"),
        name="attn",
    )(sinks, za3, za3)


def _ssm_kernel(zb_ref, bre_ref, bim_ref, t_ref, cre_ref, cim_ref, are_ref, aim_ref, o_ref,
                x_ref, xs_ref, u_ref, vre_ref, vim_ref, sre_ref, sim_ref, y_ref, st_ref):
    ts = zb_ref.shape[0]
    L, G, H = SSM_CHUNK, SSM_GROUPS, SSM_GROUP
    nch = ts // L
    half_g = 128 // H
    n_half = G // half_g

    @pl.when(pl.program_id(1) == 0)
    def _():
        st_ref[...] = jnp.zeros_like(st_ref)

    for hf in range(n_half):
        x_ref[hf] = zb_ref[:, hf * 128:(hf + 1) * 128].astype(F32)
    for sg in range(L):
        for hf in range(n_half):
            xs_ref[sg * n_half + hf] = x_ref[hf, pl.ds(sg, nch, stride=L), :]
    for g in range(G):
        hf = g // half_g
        lo = (g % half_g) * H
        ug = jnp.concatenate([xs_ref[sg * n_half + hf, :, lo:lo + H] for sg in range(L)], axis=-1).astype(BF16)
        u_ref[g] = ug
        vre_ref[pl.ds(g, nch, stride=G), :] = _dot(ug, bre_ref[g])
        vim_ref[pl.ds(g, nch, stride=G), :] = _dot(ug, bim_ref[g])

    are = are_ref[...]
    aim = aim_ref[...]

    def step(c, carry):
        sre, sim = carry
        r0 = pl.multiple_of(c * G, G)
        sre_ref[pl.ds(r0, G), :] = sre
        sim_ref[pl.ds(r0, G), :] = sim
        vre = vre_ref[pl.ds(r0, G), :]
        vim = vim_ref[pl.ds(r0, G), :]
        return (are * sre - aim * sim + vre, are * sim + aim * sre + vim)

    sre, sim = lax.fori_loop(0, nch, step, (st_ref[0:G, :], st_ref[G:2 * G, :]), unroll=4)
    st_ref[0:G, :] = sre
    st_ref[G:2 * G, :] = sim

    for g in range(G):
        y = _dot(u_ref[g], t_ref[g])
        y = y + _dot(sre_ref[pl.ds(g, nch, stride=G), :].astype(BF16), cre_ref[g])
        y = y + _dot(sim_ref[pl.ds(g, nch, stride=G), :].astype(BF16), cim_ref[g])
        y_ref[g] = y
    for tau in range(L):
        for hf in range(n_half):
            z = jnp.concatenate([y_ref[hf * half_g + gl, :, tau * H:(tau + 1) * H] for gl in range(half_g)], axis=-1)
            x_ref[hf, pl.ds(tau, nch, stride=L), :] = z
    for hf in range(n_half):
        o_ref[:, hf * 128:(hf + 1) * 128] = x_ref[hf].astype(o_ref.dtype)


def _ssm_matrices(a_re, a_im, b_re, b_im, c_re, c_im, d_skip, log_dt):
    L = SSM_CHUNK
    a = lax.complex(a_re.astype(F32), a_im.astype(F32))
    dt = jnp.exp(log_dt.astype(F32))[:, None]
    adt = a * dt
    a_bar = jnp.exp(adt)
    b_bar = ((a_bar - 1.0) / a)[..., None] * lax.complex(b_re.astype(F32), b_im.astype(F32))
    c_mat = lax.complex(c_re.astype(F32), c_im.astype(F32))
    lag = jnp.arange(L + 1, dtype=F32)
    pw = jnp.exp(adt[None] * lag[:, None, None])
    kern = jnp.einsum('gop,dgp,gpi->dgoi', c_mat, pw[:L], b_bar).real
    sig = jnp.arange(L)[:, None]
    tau = jnp.arange(L)[None, :]
    d = tau - sig
    kt = kern[jnp.clip(d, 0, L - 1)]
    kt = jnp.where((d >= 0)[:, :, None, None, None], kt, 0.0)
    eye = jnp.eye(SSM_GROUP, dtype=F32)
    dsk = d_skip.astype(F32).reshape(SSM_GROUPS, SSM_GROUP)
    kt = kt + (d == 0)[:, :, None, None, None] * (dsk[:, :, None] * eye)[None, None]
    t_mat = kt.transpose(2, 0, 4, 1, 3).reshape(SSM_GROUPS, L * SSM_GROUP, L * SSM_GROUP)
    bm = pw[:L][::-1][:, :, :, None] * b_bar[None]
    bm = bm.transpose(1, 0, 3, 2).reshape(SSM_GROUPS, L * SSM_GROUP, SSM_STATE)
    cm = c_mat[None] * pw[1:][:, :, None, :]
    cm = cm.transpose(1, 3, 0, 2).reshape(SSM_GROUPS, SSM_STATE, L * SSM_GROUP)
    a_chunk = pw[L]
    pad = 128 - SSM_STATE
    pc = lambda m: jnp.pad(m, ((0, 0), (0, 0), (0, pad))).astype(BF16)
    pr = lambda m: jnp.pad(m, ((0, 0), (0, pad), (0, 0))).astype(BF16)
    pa = lambda m: jnp.pad(m, ((0, 0), (0, pad)))
    return (pc(bm.real), pc(bm.imag), t_mat.astype(BF16), pr(cm.real), pr(-cm.imag),
            pa(a_chunk.real), pa(a_chunk.imag))


def _ssm(zb3, mats):
    b, s, w = zb3.shape
    L, G = SSM_CHUNK, SSM_GROUPS
    ts = min(SSM_TILE, s)
    nch = ts // L
    full = lambda a: pl.BlockSpec(a.shape, lambda bi, i: (0,) * a.ndim)
    return pl.pallas_call(
        _ssm_kernel,
        grid=(b, s // ts),
        in_specs=[pl.BlockSpec((None, ts, w), lambda bi, i: (bi, i, 0))] + [full(m) for m in mats],
        out_specs=pl.BlockSpec((None, ts, w), lambda bi, i: (bi, i, 0)),
        out_shape=jax.ShapeDtypeStruct((b, s, w), BF16),
        scratch_shapes=[pltpu.VMEM((w // 128, ts, 128), F32),
                        pltpu.VMEM((L * (w // 128), nch, 128), F32),
                        pltpu.VMEM((G, nch, L * SSM_GROUP), BF16),
                        pltpu.VMEM((nch * G, 128), F32), pltpu.VMEM((nch * G, 128), F32),
                        pltpu.VMEM((nch * G, 128), F32), pltpu.VMEM((nch * G, 128), F32),
                        pltpu.VMEM((G, nch, L * SSM_GROUP), F32),
                        pltpu.VMEM((2 * G, 128), F32)],
        compiler_params=_cparams("parallel", "arbitrary"),
        name="ssm",
    )(zb3, *mats)


def _mlstm_kernel(zc_ref, zg_ref, convw_ref, gb_ref, ng_ref, o_ref,
                  ct_ref, n_ref, m_ref, tail_ref, qk_ref):
    ts = zc_ref.shape[0]
    hd = HEAD_DIM

    @pl.when(pl.program_id(1) == 0)
    def _():
        ct_ref[...] = jnp.zeros_like(ct_ref)
        n_ref[...] = jnp.zeros_like(n_ref)
        m_ref[...] = jnp.zeros_like(m_ref)
        tail_ref[...] = jnp.zeros_like(tail_ref)

    x = zc_ref[:, 0:2 * GROUP_W].astype(F32)
    xe = jnp.concatenate([tail_ref[...], x], axis=0)
    tail_ref[...] = x[ts - 8:, :]
    cw = convw_ref[...]
    acc = x * cw[CONV_K - 1:CONV_K, :]
    for sft in range(1, CONV_K):
        acc = acc + xe[8 - sft:8 - sft + ts, :] * cw[CONV_K - 1 - sft:CONV_K - sft, :]
    qk_ref[...] = acc * _sigmoid(acc)

    r_i = lax.broadcasted_iota(I32, (CHUNK, CHUNK), 0)
    c_i = lax.broadcasted_iota(I32, (CHUNK, CHUNK), 1)
    causal = c_i <= r_i
    tril = causal.astype(BF16)
    lane = lax.broadcasted_iota(I32, (CHUNK, 128), 1)

    def chunk(c, carry):
        r0 = pl.multiple_of(c * CHUNK, CHUNK)
        gates = zg_ref[pl.ds(r0, CHUNK), :] + gb_ref[...]
        g2 = jnp.where(lane < MLSTM_HEADS, gates, _log_sigmoid(gates))
        ghi, glo = _split_bf16(g2)
        cum = _dot(tril, ghi) + _dot(tril, glo)
        g2t = g2.T
        cumt = cum.T
        outs = []
        for h in range(MLSTM_HEADS):
            q = qk_ref[pl.ds(r0, CHUNK), h * hd:(h + 1) * hd]
            k = qk_ref[pl.ds(r0, CHUNK), GROUP_W + h * hd:GROUP_W + (h + 1) * hd] * (hd ** -0.5)
            v = zc_ref[pl.ds(r0, CHUNK), 2 * GROUP_W + h * hd:2 * GROUP_W + (h + 1) * hd]
            og = zc_ref[pl.ds(r0, CHUNK), 3 * GROUP_W + h * hd:3 * GROUP_W + (h + 1) * hd].astype(F32)
            fh = MLSTM_HEADS + h
            b_col = cum[:, fh:fh + 1]
            i_col = g2[:, h:h + 1]
            b_row = cumt[fh:fh + 1, :]
            i_row = g2t[h:h + 1, :]
            b_last = cum[CHUNK - 1:CHUNK, fh:fh + 1]
            m_prev = m_ref[h:h + 1, 0:1]
            n_prev = n_ref[h:h + 1, :]
            ct_prev = ct_ref[h]
            qb = q.astype(BF16)
            log_d = jnp.where(causal, b_col - b_row + i_row, NEG)
            log_inter = b_col + m_prev
            m_t = jnp.maximum(log_inter, jnp.max(log_d, axis=-1, keepdims=True))
            w = jnp.exp(log_d - m_t) * _dot_nt(qb, k.astype(BF16))
            w_inter = jnp.exp(log_inter - m_t)
            num = _dot(w.astype(BF16), v) + w_inter * _dot(qb, ct_prev.astype(BF16))
            den = jnp.sum(w, axis=-1, keepdims=True) + w_inter * jnp.sum(q * n_prev, axis=-1, keepdims=True)
            hh = num / jnp.maximum(jnp.abs(den), jnp.exp(-m_t))
            m_new = m_t[CHUNK - 1:CHUNK, :]
            w_s = jnp.exp(b_last - b_col + i_col - m_new)
            decay = jnp.exp(b_last + m_prev - m_new)
            kw = k * w_s
            ct_ref[h] = decay * ct_prev + _dot_tn(kw.astype(BF16), v)
            n_ref[h:h + 1, :] = decay * n_prev + jnp.sum(kw, axis=0, keepdims=True)
            m_ref[h:h + 1, :] = jnp.broadcast_to(m_new, (1, 128))
            hn = hh * lax.rsqrt(jnp.mean(hh * hh, axis=-1, keepdims=True) + EPS) * ng_ref[:, h * hd:(h + 1) * hd]
            outs.append(_sigmoid(og) * hn)
        o_ref[pl.ds(r0, CHUNK), :] = jnp.concatenate(outs, axis=-1).astype(o_ref.dtype)
        return carry

    lax.fori_loop(0, ts // CHUNK, chunk, 0)


def _mlstm(zc3, zg3, conv_w, gate_b_pad, norm_g):
    b, s, w = zc3.shape
    ts = SEQ_TILE
    full = lambda a: pl.BlockSpec(a.shape, lambda bi, i: (0,) * a.ndim)
    return pl.pallas_call(
        _mlstm_kernel,
        grid=(b, s // ts),
        in_specs=[pl.BlockSpec((None, ts, w), lambda bi, i: (bi, i, 0)),
                  pl.BlockSpec((None, ts, 128), lambda bi, i: (bi, i, 0)),
                  full(conv_w), full(gate_b_pad), full(norm_g)],
        out_specs=pl.BlockSpec((None, ts, GROUP_W), lambda bi, i: (bi, i, 0)),
        out_shape=jax.ShapeDtypeStruct((b, s, GROUP_W), BF16),
        scratch_shapes=[pltpu.VMEM((MLSTM_HEADS, HEAD_DIM, HEAD_DIM), F32),
                        pltpu.VMEM((8, HEAD_DIM), F32),
                        pltpu.VMEM((8, 128), F32),
                        pltpu.VMEM((8, 2 * GROUP_W), F32),
                        pltpu.VMEM((ts, 2 * GROUP_W), F32)],
        compiler_params=_cparams("parallel", "arbitrary"),
        name="mlstm",
    )(zc3, zg3, conv_w, gate_b_pad, norm_g)


def _sgu_kernel(zd_ref, lng_ref, lnb_ref, w_ref, bias_ref, o_ref):
    tm = zd_ref.shape[0]
    gw = GROUP_W // SGU_GROUPS
    u = _gelu(zd_ref[:, 0:GROUP_W].astype(F32))
    v = _gelu(zd_ref[:, GROUP_W:2 * GROUP_W].astype(F32))
    mu = jnp.mean(v, axis=-1, keepdims=True)
    vc = v - mu
    var = jnp.mean(vc * vc, axis=-1, keepdims=True)
    vn = (vc * lax.rsqrt(var + EPS) * lng_ref[...] + lnb_ref[...]).astype(BF16)
    for c in range(tm // CHUNK):
        rows = slice(c * CHUNK, (c + 1) * CHUNK)
        mixed = jnp.concatenate([_dot(w_ref[g], vn[rows, g * gw:(g + 1) * gw]) for g in range(SGU_GROUPS)], axis=-1)
        o_ref[rows, :] = (u[rows, :] * (mixed + bias_ref[...])).astype(o_ref.dtype)


def _sgu(zd, ln_g, ln_b, w_tril, bias):
    n = zd.shape[0]
    tm = TOKEN_TILE
    full = lambda a: pl.BlockSpec(a.shape, lambda i: (0,) * a.ndim)
    return pl.pallas_call(
        _sgu_kernel,
        grid=(n // tm,),
        in_specs=[pl.BlockSpec((tm, 2 * GROUP_W), lambda i: (i, 0)), full(ln_g), full(ln_b), full(w_tril), full(bias)],
        out_specs=pl.BlockSpec((tm, GROUP_W), lambda i: (i, 0)),
        out_shape=jax.ShapeDtypeStruct((n, GROUP_W), BF16),
        compiler_params=_cparams("parallel"),
        name="sgu",
    )(zd, ln_g, ln_b, w_tril, bias)


def _out_proj_kernel(x_ref, ya_ref, yb_ref, yc_ref, yd_ref, gluw_ref, glub_ref, wo_ref, ng_ref,
                     rwh_ref, rwl_ref, rb_ref,
                     x1_ref, hx_ref, meta_ref, cnt_ref, carry_ref):
    tm = x_ref.shape[0]

    @pl.when(pl.program_id(0) == 0)
    def _():
        carry_ref[...] = jnp.zeros_like(carry_ref)

    yb = _gelu(yb_ref[...].astype(F32))
    yb = yb * _sigmoid(_dot(yb.astype(BF16), gluw_ref[...]) + glub_ref[...])
    mix = _dot(ya_ref[...], wo_ref[0:GROUP_W, :])
    mix = mix + _dot(yb.astype(BF16), wo_ref[GROUP_W:2 * GROUP_W, :])
    mix = mix + _dot(yc_ref[...], wo_ref[2 * GROUP_W:3 * GROUP_W, :])
    mix = mix + _dot(yd_ref[...], wo_ref[3 * GROUP_W:4 * GROUP_W, :])
    x1 = x_ref[...] + mix
    x1_ref[...] = x1
    h = _rms(x1, ng_ref[...])
    hx_ref[:, 0:D_MODEL] = h

    hh, hl = _split_bf16(h)
    wh = rwh_ref[...]
    wl = rwl_ref[...]
    logits = _dot_nt(wh, hh) + _dot_nt(wh, hl) + _dot_nt(wl, hh) + rb_ref[...]
    gl = [logits[j:j + 1, :] for j in range(N_GROUPS)]
    gmax = functools.reduce(jnp.maximum, gl)
    gsel = jnp.full((1, tm), N_GROUPS - 1, I32)
    for j in range(N_GROUPS - 2, -1, -1):
        gsel = jnp.where(gl[j] == gmax, j, gsel)
    p_g = 1.0 / functools.reduce(jnp.add, [jnp.exp(v - gmax) for v in gl])
    e_in = []
    for i in range(EPG):
        v = logits[N_GROUPS + i:N_GROUPS + i + 1, :]
        for j in range(1, N_GROUPS):
            r = N_GROUPS + j * EPG + i
            v = jnp.where(gsel == j, logits[r:r + 1, :], v)
        e_in.append(v)
    v1 = functools.reduce(jnp.maximum, e_in)
    i1 = jnp.full((1, tm), EPG - 1, I32)
    for i in range(EPG - 2, -1, -1):
        i1 = jnp.where(e_in[i] == v1, i, i1)
    rest = [jnp.where(i1 == i, NEG, e_in[i]) for i in range(EPG)]
    v2 = functools.reduce(jnp.maximum, rest)
    i2 = jnp.full((1, tm), EPG - 1, I32)
    for i in range(EPG - 2, -1, -1):
        i2 = jnp.where((rest[i] == v2) & (i1 != i), i, i2)
    e2 = jnp.exp(v2 - v1)
    w1 = p_g / (1.0 + e2)
    w2 = p_g * e2 / (1.0 + e2)
    lo = jnp.minimum(i1, i2)
    hi = jnp.maximum(i1, i2)
    w_lo = jnp.where(i1 < i2, w1, w2)
    w_hi = jnp.where(i1 < i2, w2, w1)
    bucket = gsel * PAIRS + ((lo * (2 * EPG - 1 - lo)) >> 1) + (hi - lo - 1)

    kid = lax.broadcasted_iota(I32, (BUCKET_PAD, tm), 0)
    onehot = (kid == bucket).astype(F32)
    s_i = lax.broadcasted_iota(I32, (tm, tm), 0)
    t_i = lax.broadcasted_iota(I32, (tm, tm), 1)
    prefix = _dot(onehot.astype(BF16), (s_i <= t_i).astype(BF16))
    carry = carry_ref[...]
    rank = jnp.sum(onehot * (prefix - 1.0 + carry[:, 0:1]), axis=0, keepdims=True)
    carry = carry + prefix[:, tm - 1:tm]
    carry_ref[...] = carry
    cnt_ref[...] = carry
    meta_ref[...] = jnp.concatenate([bucket, rank.astype(I32), jnp.zeros((6, tm), I32)], axis=0)
    wrows = jnp.concatenate([w_lo, w_hi, jnp.zeros((126, tm), F32)], axis=0)
    for c in range(tm // 128):
        hx_ref[c * 128:(c + 1) * 128, D_MODEL:ROW_W] = wrows[:, c * 128:(c + 1) * 128].T


def _out_proj(x2, ya, yb, yc, yd, glu_w, glu_b, w_out, norm_g, rw_hi, rw_lo, rb):
    n = x2.shape[0]
    tm = TOKEN_TILE
    row = lambda w: pl.BlockSpec((tm, w), lambda i: (i, 0))
    full = lambda a: pl.BlockSpec(a.shape, lambda i: (0,) * a.ndim)
    return pl.pallas_call(
        _out_proj_kernel,
        grid=(n // tm,),
        in_specs=[row(D_MODEL), row(GROUP_W), row(GROUP_W), row(GROUP_W), row(GROUP_W),
                  full(glu_w), full(glu_b), full(w_out), full(norm_g), full(rw_hi), full(rw_lo), full(rb)],
        out_specs=[row(D_MODEL), row(ROW_W), pl.BlockSpec((8, tm), lambda i: (0, i)),
                   pl.BlockSpec((BUCKET_PAD, 128), lambda i: (0, 0))],
        out_shape=[jax.ShapeDtypeStruct((n, D_MODEL), F32), jax.ShapeDtypeStruct((n, ROW_W), F32),
                   jax.ShapeDtypeStruct((8, n), I32), jax.ShapeDtypeStruct((BUCKET_PAD, 128), F32)],
        scratch_shapes=[pltpu.VMEM((BUCKET_PAD, 128), F32)],
        compiler_params=_cparams("arbitrary"),
        name="out_proj",
    )(x2, ya, yb, yc, yd, glu_w, glu_b, w_out, norm_g, rw_hi, rw_lo, rb)


def _row_copy(src_ref, src_row, dst_ref, dst_row, sem):
    return pltpu.make_async_copy(src_ref.at[pl.ds(src_row, 1)], dst_ref.at[pl.ds(dst_row, 1)], sem)


def _dispatch_kernel(dest_ref, hx_ref, xs_in_ref, xs_ref, sem):
    del xs_in_ref
    tm = hx_ref.shape[0]

    def start(t, c):
        _row_copy(hx_ref, t, xs_ref, dest_ref[0, 0, t], sem).start()
        return c

    lax.fori_loop(0, tm, start, 0, unroll=ROW_DMA_UNROLL)
    pltpu.make_async_copy(hx_ref, xs_ref.at[pl.ds(0, tm)], sem).wait()


def _dispatch(dest3, hx, p_tot):
    n = hx.shape[0]
    tm = TOKEN_TILE
    xs0 = jnp.zeros((p_tot, ROW_W), F32)
    return pl.pallas_call(
        _dispatch_kernel,
        grid=(n // tm,),
        in_specs=[pl.BlockSpec((1, 1, tm), lambda i: (i, 0, 0), memory_space=pltpu.SMEM),
                  pl.BlockSpec((tm, ROW_W), lambda i: (i, 0)),
                  pl.BlockSpec(memory_space=pl.ANY)],
        out_specs=pl.BlockSpec(memory_space=pl.ANY),
        out_shape=jax.ShapeDtypeStruct((p_tot, ROW_W), F32),
        input_output_aliases={2: 0},
        scratch_shapes=[pltpu.SemaphoreType.DMA],
        compiler_params=_cparams("arbitrary"),
        name="dispatch",
    )(dest3, hx, xs0)


def _ffn_kernel(ea_ref, eb_ref, nact_ref, xs_ref, wg_a_ref, wu_a_ref, wd_a_ref, wg_b_ref, wu_b_ref, wd_b_ref,
                ys_ref, wgu_a, wdn_a, wgu_b, wdn_b):
    j = pl.program_id(0)
    active = j < nact_ref[0]
    jp = jnp.maximum(j - 1, 0)

    @pl.when((j == 0) | (ea_ref[j] != ea_ref[jp]))
    def _():
        wgu_a[:, 0:D_EXPERT] = wg_a_ref[...].astype(BF16)
        wgu_a[:, D_EXPERT:2 * D_EXPERT] = wu_a_ref[...].astype(BF16)
        wdn_a[...] = wd_a_ref[...].astype(BF16)

    @pl.when((j == 0) | (eb_ref[j] != eb_ref[jp]))
    def _():
        wgu_b[:, 0:D_EXPERT] = wg_b_ref[...].astype(BF16)
        wgu_b[:, D_EXPERT:2 * D_EXPERT] = wu_b_ref[...].astype(BF16)
        wdn_b[...] = wd_b_ref[...].astype(BF16)

    @pl.when(jnp.logical_not(active))
    def _():
        ys_ref[...] = jnp.zeros_like(ys_ref)

    @pl.when(active)
    def _():
        xb = xs_ref[:, 0:D_MODEL].astype(BF16)

        def expert(wgu, wdn):
            gu = _dot(xb, wgu[...])
            g = gu[:, 0:D_EXPERT]
            act = (g * _sigmoid(g) * gu[:, D_EXPERT:2 * D_EXPERT]).astype(BF16)
            return _dot(act, wdn[...])

        ya = expert(wgu_a, wdn_a)
        yb = expert(wgu_b, wdn_b)
        ys_ref[...] = ya * xs_ref[:, D_MODEL:D_MODEL + 1] + yb * xs_ref[:, D_MODEL + 1:D_MODEL + 2]


def _ffn(ea, eb, nact, xs, layer, w_gate, w_up, w_down):
    p_tot = xs.shape[0]
    bm = FFN_BLOCK
    nblk = p_tot // bm
    wspec = lambda e_idx, shape: pl.BlockSpec((None, None) + shape,
                                              lambda j, ea, eb, na: (layer, (ea, eb)[e_idx][j], 0, 0))
    up_shape = (D_MODEL, D_EXPERT)
    dn_shape = (D_EXPERT, D_MODEL)
    return pl.pallas_call(
        _ffn_kernel,
        grid_spec=pltpu.PrefetchScalarGridSpec(
            num_scalar_prefetch=3,
            grid=(nblk,),
            in_specs=[pl.BlockSpec((bm, ROW_W), lambda j, ea, eb, na: (j, 0)),
                      wspec(0, up_shape), wspec(0, up_shape), wspec(0, dn_shape),
                      wspec(1, up_shape), wspec(1, up_shape), wspec(1, dn_shape)],
            out_specs=pl.BlockSpec((bm, D_MODEL), lambda j, ea, eb, na: (j, 0)),
            scratch_shapes=[pltpu.VMEM((D_MODEL, 2 * D_EXPERT), BF16), pltpu.VMEM(dn_shape, BF16),
                            pltpu.VMEM((D_MODEL, 2 * D_EXPERT), BF16), pltpu.VMEM(dn_shape, BF16)],
        ),
        out_shape=jax.ShapeDtypeStruct((p_tot, D_MODEL), F32),
        compiler_params=_cparams("arbitrary"),
        name="ffn",
    )(ea, eb, nact, xs, w_gate, w_up, w_down, w_gate, w_up, w_down)


def _combine_kernel(dest_ref, x1_ref, ys_ref, ng_ref, o_ref, buf_ref, sem, *, final_norm):
    tm = x1_ref.shape[0]

    def start(t, c):
        _row_copy(ys_ref, dest_ref[0, 0, t], buf_ref, t, sem).start()
        return c

    lax.fori_loop(0, tm, start, 0, unroll=ROW_DMA_UNROLL)
    pltpu.make_async_copy(ys_ref.at[pl.ds(0, tm)], buf_ref, sem).wait()
    x2 = x1_ref[...] + buf_ref[...]
    o_ref[...] = _rms(x2, ng_ref[...]) if final_norm else x2


def _combine(dest3, x1, ys, norm_g, final_norm):
    n = x1.shape[0]
    tm = TOKEN_TILE
    return pl.pallas_call(
        functools.partial(_combine_kernel, final_norm=final_norm),
        grid=(n // tm,),
        in_specs=[pl.BlockSpec((1, 1, tm), lambda i: (i, 0, 0), memory_space=pltpu.SMEM),
                  pl.BlockSpec((tm, D_MODEL), lambda i: (i, 0)),
                  pl.BlockSpec(memory_space=pl.ANY),
                  pl.BlockSpec((1, D_MODEL), lambda i: (0, 0))],
        out_specs=pl.BlockSpec((tm, D_MODEL), lambda i: (i, 0)),
        out_shape=jax.ShapeDtypeStruct((n, D_MODEL), F32),
        scratch_shapes=[pltpu.VMEM((tm, D_MODEL), F32), pltpu.SemaphoreType.DMA],
        compiler_params=_cparams("arbitrary"),
        name="combine",
    )(dest3, x1, ys, norm_g)


def _routing_tables(meta, counts, n):
    bm = FFN_BLOCK
    nblk = n // bm + N_BUCKETS
    bucket = meta[0]
    rank = meta[1]
    cnt = counts[:N_BUCKETS, 0].astype(I32)
    padded = ((cnt + bm - 1) // bm) * bm
    pad_end = jnp.cumsum(padded)
    pad_start = pad_end - padded
    onehot = bucket[:, None] == jnp.arange(N_BUCKETS, dtype=I32)[None, :]
    dest = rank + jnp.sum(jnp.where(onehot, pad_start[None, :], 0), axis=1)
    nact = (pad_end[-1] // bm).astype(I32)
    j = jnp.arange(nblk, dtype=I32)
    blk = jnp.minimum(j, jnp.maximum(nact - 1, 0))
    bkt = jnp.minimum(jnp.searchsorted(pad_end, blk * bm, side='right'), N_BUCKETS - 1).astype(I32)
    grp = bkt // PAIRS
    pidx = bkt % PAIRS
    lo_tab, hi_tab = [], []
    for a in range(EPG):
        for b in range(a + 1, EPG):
            lo_tab.append(a)
            hi_tab.append(b)
    ea = grp * EPG + jnp.asarray(lo_tab, I32)[pidx]
    eb = grp * EPG + jnp.asarray(hi_tab, I32)[pidx]
    return dest.astype(I32), ea.astype(I32), eb.astype(I32), nact.reshape(1), nblk * bm


def _layer(x2, b, s, p, final_g):
    n = b * s
    za, zb, zc, zd, zg = _in_proj(x2, p['norm_mix_g'], p['wa'], p['wb'], p['wc'], p['wd'], p['wg'])
    ya = _attention(za.reshape(b, s, -1), p['sinks']).reshape(n, GROUP_W)
    yb = _ssm(zb.reshape(b, s, GROUP_W), p['ssm']).reshape(n, GROUP_W)
    yc = _mlstm(zc.reshape(b, s, -1), zg.reshape(b, s, 128), p['conv_w'], p['gate_b'], p['mlstm_norm_g']).reshape(n, GROUP_W)
    yd = _sgu(zd, p['sgu_ln_g'], p['sgu_ln_b'], p['sgu_w'], p['sgu_bias'])
    x1, hx, meta, counts = _out_proj(x2, ya, yb, yc, yd, p['glu_w'], p['glu_b'], p['w_out'], p['norm_ffn_g'],
                                     p['rw_hi'], p['rw_lo'], p['rb'])
    dest, ea, eb, nact, p_tot = _routing_tables(meta, counts, n)
    dest3 = dest.reshape(n // TOKEN_TILE, 1, TOKEN_TILE)
    xs = _dispatch(dest3, hx, p_tot)
    ys = _ffn(ea, eb, nact, xs, p['layer'], p['w_gate'], p['w_up'], p['w_down'])
    if final_g is None:
        return _combine(dest3, x1, ys, p['norm_ffn_g'], False)
    return _combine(dest3, x1, ys, final_g, True)


def _prep_layer(l, norm_mix_g, w_in, attn_sinks, ssm_a_re, ssm_a_im, ssm_b_re, ssm_b_im, ssm_c_re, ssm_c_im,
                ssm_d, ssm_log_dt, ssm_glu_w, ssm_glu_b, mlstm_conv_w, mlstm_gate_b, mlstm_norm_g,
                sgu_ln_g, sgu_ln_b, sgu_w, sgu_b, w_out, norm_ffn_g, router_group_w, router_group_b,
                router_expert_w, router_expert_b, expert_w_gate, expert_w_up, expert_w_down):
    w = w_in[l]
    o_su = 2 * GROUP_W
    o_c = o_su + GROUP_W
    o_ci = o_c + 3 * GROUP_W
    o_co = o_ci + 2 * MLSTM_HEADS
    o_d = o_co + GROUP_W
    wc = jnp.concatenate([w[:, o_c:o_ci], w[:, o_co:o_d]], axis=1)
    wg = jnp.pad(w[:, o_ci:o_co], ((0, 0), (0, 128 - 2 * MLSTM_HEADS)))
    rw = jnp.concatenate([router_group_w[l], router_expert_w[l]], axis=1).T.astype(F32)
    rw = jnp.pad(rw, ((0, 4), (0, 0)))
    rw_hi = rw.astype(BF16)
    rw_lo = (rw - rw_hi.astype(F32)).astype(BF16)
    rb = jnp.pad(jnp.concatenate([router_group_b[l], router_expert_b[l]]).astype(F32), (0, 4))[:, None]
    tril = jnp.tril(jnp.ones((CHUNK, CHUNK), F32))
    gw = GROUP_W // SGU_GROUPS
    return dict(
        norm_mix_g=norm_mix_g[l][None, :].astype(F32),
        wa=w[:, 0:o_su].astype(BF16), wb=w[:, o_su:o_c].astype(BF16), wc=wc.astype(BF16),
        wd=w[:, o_d:].astype(BF16), wg=wg.astype(BF16),
        sinks=attn_sinks[l].astype(F32),
        ssm=_ssm_matrices(ssm_a_re[l], ssm_a_im[l], ssm_b_re[l], ssm_b_im[l], ssm_c_re[l], ssm_c_im[l],
                          ssm_d[l], ssm_log_dt[l]),
        glu_w=ssm_glu_w[l].astype(BF16), glu_b=ssm_glu_b[l][None, :].astype(F32),
        conv_w=mlstm_conv_w[l].astype(F32),
        gate_b=jnp.pad(mlstm_gate_b[l].astype(F32), (0, 128 - 2 * MLSTM_HEADS))[None, :],
        mlstm_norm_g=mlstm_norm_g[l][None, :].astype(F32),
        sgu_ln_g=sgu_ln_g[l][None, :].astype(F32), sgu_ln_b=sgu_ln_b[l][None, :].astype(F32),
        sgu_w=(sgu_w[l].astype(F32) * tril).astype(BF16),
        sgu_bias=jnp.repeat(sgu_b[l].astype(F32).T, gw, axis=1),
        w_out=w_out[l].astype(BF16), norm_ffn_g=norm_ffn_g[l][None, :].astype(F32),
        rw_hi=rw_hi, rw_lo=rw_lo, rb=rb,
        layer=l, w_gate=expert_w_gate, w_up=expert_w_up, w_down=expert_w_down,
    )


def kernel(x, norm_mix_g, w_in, attn_sinks, ssm_a_re, ssm_a_im, ssm_b_re, ssm_b_im, ssm_c_re, ssm_c_im, ssm_d, ssm_log_dt, ssm_glu_w, ssm_glu_b, mlstm_conv_w, mlstm_gate_b, mlstm_norm_g, sgu_ln_g, sgu_ln_b, sgu_w, sgu_b, w_out, norm_ffn_g, router_group_w, router_group_b, router_expert_w, router_expert_b, expert_w_gate, expert_w_up, expert_w_down, norm_final_g):
    b, s, d = x.shape
    depth = w_in.shape[0]
    x2 = x.reshape(b * s, d).astype(F32)
    for l in range(depth):
        p = _prep_layer(l, norm_mix_g, w_in, attn_sinks, ssm_a_re, ssm_a_im, ssm_b_re, ssm_b_im, ssm_c_re,
                        ssm_c_im, ssm_d, ssm_log_dt, ssm_glu_w, ssm_glu_b, mlstm_conv_w, mlstm_gate_b,
                        mlstm_norm_g, sgu_ln_g, sgu_ln_b, sgu_w, sgu_b, w_out, norm_ffn_g, router_group_w,
                        router_group_b, router_expert_w, router_expert_b, expert_w_gate, expert_w_up,
                        expert_w_down)
        final_g = norm_final_g[None, :].astype(F32) if l == depth - 1 else None
        x2 = _layer(x2, b, s, p, final_g)
    return x2.reshape(b, s, d).astype(x.dtype)
```

```python
import functools
import math

import jax
import jax.numpy as jnp
from jax import lax
from jax.experimental import pallas as pl
from jax.experimental.pallas import tpu as pltpu

F32 = jnp.float32
BF16 = jnp.bfloat16
I32 = jnp.int32

D_MODEL = 1024
GROUP_W = 256
HEAD_DIM = 64
EPS = 1e-6
NEG = -1e30
WINDOW = 128
KV_HEADS = 2
SSM_GROUP = 16
SSM_GROUPS = 16
SSM_STATE = 64
SSM_CHUNK = 16
MLSTM_HEADS = 4
CHUNK = 128
CONV_K = 4
SGU_GROUPS = 4
N_GROUPS = 4
EPG = 8
N_EXPERTS = 32
D_EXPERT = 512
PAIRS = EPG * (EPG - 1) // 2
N_BUCKETS = N_GROUPS * PAIRS
BUCKET_PAD = 128
ROW_W = D_MODEL + 128

TOKEN_TILE = 512
SEQ_TILE = 512
SSM_TILE = 4096
FFN_BLOCK = 256
ROW_DMA_UNROLL = 8
VMEM_LIMIT = 56 * 1024 * 1024


def _cparams(*sem):
    return pltpu.CompilerParams(dimension_semantics=sem, vmem_limit_bytes=VMEM_LIMIT)


def _rms(x, g):
    return x * lax.rsqrt(jnp.mean(x * x, axis=-1, keepdims=True) + EPS) * g


def _gelu(x):
    return 0.5 * x * (1.0 + jnp.tanh(math.sqrt(2.0 / math.pi) * (x + 0.044715 * (x * x * x))))


def _sigmoid(x):
    return 1.0 / (1.0 + jnp.exp(-x))


def _log_sigmoid(x):
    return jnp.minimum(x, 0.0) - jnp.log(1.0 + jnp.exp(-jnp.abs(x)))


def _dot(a, b):
    return jnp.dot(a, b, preferred_element_type=F32)


def _dot_nt(a, b):
    return lax.dot_general(a, b, (((1,), (1,)), ((), ())), preferred_element_type=F32)


def _dot_tn(a, b):
    return lax.dot_general(a, b, (((0,), (0,)), ((), ())), preferred_element_type=F32)


def _split_bf16(x):
    hi = x.astype(BF16)
    lo = (x - hi.astype(F32)).astype(BF16)
    return hi, lo


def _in_proj_kernel(x_ref, g_ref, wa_ref, wb_ref, wc_ref, wd_ref, wgt_ref,
                    za_ref, zb_ref, zc_ref, zd_ref, zgt_ref):
    hb = _rms(x_ref[...], g_ref[...]).astype(BF16)
    za_ref[...] = _dot(hb, wa_ref[...]).astype(BF16)
    zb_ref[...] = _dot(hb, wb_ref[...]).astype(BF16)
    zc_ref[...] = _dot(hb, wc_ref[...]).astype(BF16)
    zd_ref[...] = _dot(hb, wd_ref[...]).astype(BF16)
    zgt_ref[...] = _dot_nt(wgt_ref[...], hb)


def _in_proj(x2, g, wa, wb, wc, wd, wgt):
    n = x2.shape[0]
    tm = TOKEN_TILE
    row = lambda w: pl.BlockSpec((tm, w), lambda i: (i, 0))
    full = lambda a: pl.BlockSpec(a.shape, lambda i: (0,) * a.ndim)
    widths = (wa.shape[1], wb.shape[1], wc.shape[1], wd.shape[1])
    return pl.pallas_call(
        _in_proj_kernel,
        grid=(n // tm,),
        in_specs=[row(D_MODEL), full(g), full(wa), full(wb), full(wc), full(wd), full(wgt)],
        out_specs=[row(w) for w in widths] + [pl.BlockSpec((wgt.shape[0], tm), lambda i: (0, i))],
        out_shape=[jax.ShapeDtypeStruct((n, w), BF16) for w in widths]
        + [jax.ShapeDtypeStruct((wgt.shape[0], n), F32)],
        compiler_params=_cparams("parallel"),
        name="in_proj",
    )(x2, g, wa, wb, wc, wd, wgt)


def _attn_kernel(sink_ref, cur_ref, prev_ref, o_ref):
    first = pl.program_id(1) == 0
    nblk = cur_ref.shape[0] // WINDOW
    row = lax.broadcasted_iota(I32, (2 * WINDOW, 2 * WINDOW), 0)
    col = lax.broadcasted_iota(I32, (2 * WINDOW, 2 * WINDOW), 1)
    qi = jnp.where(row >= WINDOW, row - WINDOW, row)
    band = (col <= qi + WINDOW) & (col > qi)
    row1 = lax.broadcasted_iota(I32, (2 * WINDOW, 1), 0)
    for j in range(nblk):
        cur = cur_ref[j * WINDOW:(j + 1) * WINDOW, :]
        if j == 0:
            prev = prev_ref[...]
            mask = band & ((col >= WINDOW) | jnp.logical_not(first))
        else:
            prev = cur_ref[(j - 1) * WINDOW:j * WINDOW, :]
            mask = band
        outs = []
        for g in range(KV_HEADS):
            q2 = jnp.concatenate([cur[:, (2 * g) * HEAD_DIM:(2 * g + 1) * HEAD_DIM],
                                  cur[:, (2 * g + 1) * HEAD_DIM:(2 * g + 2) * HEAD_DIM]], axis=0)
            ko = GROUP_W + g * HEAD_DIM
            vo = GROUP_W + KV_HEADS * HEAD_DIM + g * HEAD_DIM
            kk = jnp.concatenate([prev[:, ko:ko + HEAD_DIM], cur[:, ko:ko + HEAD_DIM]], axis=0)
            vv = jnp.concatenate([prev[:, vo:vo + HEAD_DIM], cur[:, vo:vo + HEAD_DIM]], axis=0)
            s = _dot_nt(q2, kk) * (HEAD_DIM ** -0.5)
            s = jnp.where(mask, s, NEG)
            sink = jnp.where(row1 < WINDOW, sink_ref[2 * g], sink_ref[2 * g + 1])
            m = jnp.maximum(jnp.max(s, axis=-1, keepdims=True), sink)
            p = jnp.exp(s - m)
            l = jnp.sum(p, axis=-1, keepdims=True) + jnp.exp(sink - m)
            o = _dot(p.astype(BF16), vv) / l
            outs.append(o[:WINDOW])
            outs.append(o[WINDOW:])
        o_ref[j * WINDOW:(j + 1) * WINDOW, :] = jnp.concatenate(outs, axis=-1).astype(o_ref.dtype)


def _attention(za3, sinks):
    b, s, w = za3.shape
    ts = SEQ_TILE
    per = ts // WINDOW
    return pl.pallas_call(
        _attn_kernel,
        grid=(b, s // ts),
        in_specs=[pl.BlockSpec(memory_space=pltpu.SMEM),
                  pl.BlockSpec((None, ts, w), lambda bi, i: (bi, i, 0)),
                  pl.BlockSpec((None, WINDOW, w), lambda bi, i: (bi, jnp.maximum(i * per - 1, 0), 0))],
        out_specs=pl.BlockSpec((None, ts, GROUP_W), lambda bi, i: (bi, i, 0)),
        out_shape=jax.ShapeDtypeStruct((b, s, GROUP_W), BF16),
        compiler_params=_cparams("parallel", "parallel"),
        name="attn",
    )(sinks, za3, za3)


def _ssm_kernel(zb_ref, bre_ref, bim_ref, t_ref, cre_ref, cim_ref, are_ref, aim_ref, o_ref,
                x_ref, xs_ref, u_ref, vre_ref, vim_ref, sre_ref, sim_ref, y_ref, st_ref):
    ts = zb_ref.shape[0]
    L, G, H = SSM_CHUNK, SSM_GROUPS, SSM_GROUP
    nch = ts // L
    half_g = 128 // H
    n_half = G // half_g

    @pl.when(pl.program_id(1) == 0)
    def _():
        st_ref[...] = jnp.zeros_like(st_ref)

    for hf in range(n_half):
        x_ref[hf] = zb_ref[:, hf * 128:(hf + 1) * 128].astype(F32)
    for sg in range(L):
        for hf in range(n_half):
            xs_ref[sg * n_half + hf] = x_ref[hf, pl.ds(sg, nch, stride=L), :]
    for g in range(G):
        hf = g // half_g
        lo = (g % half_g) * H
        ug = jnp.concatenate([xs_ref[sg * n_half + hf, :, lo:lo + H] for sg in range(L)], axis=-1).astype(BF16)
        u_ref[g] = ug
        vre_ref[pl.ds(g, nch, stride=G), :] = _dot(ug, bre_ref[g])
        vim_ref[pl.ds(g, nch, stride=G), :] = _dot(ug, bim_ref[g])

    are = are_ref[...]
    aim = aim_ref[...]

    def step(c, carry):
        sre, sim = carry
        r0 = pl.multiple_of(c * G, G)
        sre_ref[pl.ds(r0, G), :] = sre
        sim_ref[pl.ds(r0, G), :] = sim
        vre = vre_ref[pl.ds(r0, G), :]
        vim = vim_ref[pl.ds(r0, G), :]
        return (are * sre - aim * sim + vre, are * sim + aim * sre + vim)

    sre, sim = lax.fori_loop(0, nch, step, (st_ref[0:G, :], st_ref[G:2 * G, :]), unroll=4)
    st_ref[0:G, :] = sre
    st_ref[G:2 * G, :] = sim

    for g in range(G):
        y = _dot(u_ref[g], t_ref[g])
        y = y + _dot(sre_ref[pl.ds(g, nch, stride=G), :].astype(BF16), cre_ref[g])
        y = y + _dot(sim_ref[pl.ds(g, nch, stride=G), :].astype(BF16), cim_ref[g])
        y_ref[g] = y
    for tau in range(L):
        for hf in range(n_half):
            z = jnp.concatenate([y_ref[hf * half_g + gl, :, tau * H:(tau + 1) * H] for gl in range(half_g)], axis=-1)
            x_ref[hf, pl.ds(tau, nch, stride=L), :] = z
    for hf in range(n_half):
        o_ref[:, hf * 128:(hf + 1) * 128] = x_ref[hf].astype(o_ref.dtype)


def _ssm_matrices(a_re, a_im, b_re, b_im, c_re, c_im, d_skip, log_dt):
    L = SSM_CHUNK
    a = lax.complex(a_re.astype(F32), a_im.astype(F32))
    dt = jnp.exp(log_dt.astype(F32))[:, None]
    adt = a * dt
    a_bar = jnp.exp(adt)
    b_bar = ((a_bar - 1.0) / a)[..., None] * lax.complex(b_re.astype(F32), b_im.astype(F32))
    c_mat = lax.complex(c_re.astype(F32), c_im.astype(F32))
    lag = jnp.arange(L + 1, dtype=F32)
    pw = jnp.exp(adt[None] * lag[:, None, None])
    kern = jnp.einsum('gop,dgp,gpi->dgoi', c_mat, pw[:L], b_bar).real
    sig = jnp.arange(L)[:, None]
    tau = jnp.arange(L)[None, :]
    d = tau - sig
    kt = kern[jnp.clip(d, 0, L - 1)]
    kt = jnp.where((d >= 0)[:, :, None, None, None], kt, 0.0)
    eye = jnp.eye(SSM_GROUP, dtype=F32)
    dsk = d_skip.astype(F32).reshape(SSM_GROUPS, SSM_GROUP)
    kt = kt + (d == 0)[:, :, None, None, None] * (dsk[:, :, None] * eye)[None, None]
    t_mat = kt.transpose(2, 0, 4, 1, 3).reshape(SSM_GROUPS, L * SSM_GROUP, L * SSM_GROUP)
    bm = pw[:L][::-1][:, :, :, None] * b_bar[None]
    bm = bm.transpose(1, 0, 3, 2).reshape(SSM_GROUPS, L * SSM_GROUP, SSM_STATE)
    cm = c_mat[None] * pw[1:][:, :, None, :]
    cm = cm.transpose(1, 3, 0, 2).reshape(SSM_GROUPS, SSM_STATE, L * SSM_GROUP)
    a_chunk = pw[L]
    pad = 128 - SSM_STATE
    pc = lambda m: jnp.pad(m, ((0, 0), (0, 0), (0, pad))).astype(BF16)
    pr = lambda m: jnp.pad(m, ((0, 0), (0, pad), (0, 0))).astype(BF16)
    pa = lambda m: jnp.pad(m, ((0, 0), (0, pad)))
    return (pc(bm.real), pc(bm.imag), t_mat.astype(BF16), pr(cm.real), pr(-cm.imag),
            pa(a_chunk.real), pa(a_chunk.imag))


def _ssm(zb3, mats):
    b, s, w = zb3.shape
    L, G = SSM_CHUNK, SSM_GROUPS
    ts = min(SSM_TILE, s)
    nch = ts // L
    full = lambda a: pl.BlockSpec(a.shape, lambda bi, i: (0,) * a.ndim)
    return pl.pallas_call(
        _ssm_kernel,
        grid=(b, s // ts),
        in_specs=[pl.BlockSpec((None, ts, w), lambda bi, i: (bi, i, 0))] + [full(m) for m in mats],
        out_specs=pl.BlockSpec((None, ts, w), lambda bi, i: (bi, i, 0)),
        out_shape=jax.ShapeDtypeStruct((b, s, w), BF16),
        scratch_shapes=[pltpu.VMEM((w // 128, ts, 128), F32),
                        pltpu.VMEM((L * (w // 128), nch, 128), F32),
                        pltpu.VMEM((G, nch, L * SSM_GROUP), BF16),
                        pltpu.VMEM((nch * G, 128), F32), pltpu.VMEM((nch * G, 128), F32),
                        pltpu.VMEM((nch * G, 128), F32), pltpu.VMEM((nch * G, 128), F32),
                        pltpu.VMEM((G, nch, L * SSM_GROUP), F32),
                        pltpu.VMEM((2 * G, 128), F32)],
        compiler_params=_cparams("parallel", "arbitrary"),
        name="ssm",
    )(zb3, *mats)


def _mlstm_kernel(zc_ref, zgt_ref, convw_ref, gb_ref, ng_ref, triu_ref, bones_ref, o_ref,
                  st_ref, m_ref, tail_ref, qk_ref,
                  rt_ref, bl_ref, rm_ref, cr128_ref, cr64_ref, b64_ref, r64_ref, s_ref, hh_ref):
    ts = zc_ref.shape[0]
    hd, nh, w = HEAD_DIM, MLSTM_HEADS, GROUP_W
    nchunk = ts // CHUNK
    nr = nchunk * 8
    lane1 = lax.broadcasted_iota(I32, (1, 128), 1)
    half = [(lane1 // hd) == (h % 2) for h in range(nh)]
    grp = [slice((h * hd) // 128 * 128, (h * hd) // 128 * 128 + 128) for h in range(nh)]

    @pl.when(pl.program_id(1) == 0)
    def _():
        st_ref[...] = jnp.zeros_like(st_ref)
        m_ref[...] = jnp.zeros_like(m_ref)
        tail_ref[...] = jnp.zeros_like(tail_ref)

    r_i = lax.broadcasted_iota(I32, (CHUNK, CHUNK), 0)
    c_i = lax.broadcasted_iota(I32, (CHUNK, CHUNK), 1)
    causal = c_i <= r_i
    k64 = lax.broadcasted_iota(I32, (8, w), 0)
    j64 = lax.broadcasted_iota(I32, (8, w), 1)
    sel64 = (j64 // hd == k64 % nh).astype(BF16)
    k128 = lax.broadcasted_iota(I32, (8, nh * 128), 0)
    j128 = lax.broadcasted_iota(I32, (8, nh * 128), 1)
    sel128 = (j128 // 128 == k128 % nh).astype(BF16)

    def lanes(t):
        lo = jnp.where(lane1 < hd, t[0:1, :], t[1:2, :])
        hi = jnp.where(lane1 < hd, t[2:3, :], t[3:4, :])
        return jnp.concatenate([lo, hi], axis=1)

    rowm = lax.broadcasted_iota(I32, (nr, 128), 0) % 8
    lanem = lax.broadcasted_iota(I32, (nr, 128), 1)
    graw = jnp.concatenate([zgt_ref[:, c * CHUNK:(c + 1) * CHUNK] + gb_ref[...] for c in range(nchunk)], axis=0)
    g2 = jnp.where(rowm < nh, graw, _log_sigmoid(graw))
    ghi, glo = _split_bf16(g2)
    cum = _dot(ghi, triu_ref[...]) + _dot(glo, triu_ref[...])
    b_t = pltpu.roll(cum, nr - 4, axis=0)
    r_t = g2 - b_t
    cr = r_t
    sh = 1
    while sh < CHUNK:
        cr = jnp.maximum(cr, jnp.where(lanem >= sh, pltpu.roll(cr, sh, axis=1), NEG))
        sh *= 2
    rt_ref[...] = r_t
    bl_ref[...] = jnp.broadcast_to(b_t[:, CHUNK - 1:CHUNK], (nr, 128))
    rm_ref[...] = jnp.broadcast_to(cr[:, CHUNK - 1:CHUNK], (nr, 128))

    x = zc_ref[:, 0:2 * w].astype(F32)
    xe = jnp.concatenate([tail_ref[...], x], axis=0)
    tail_ref[...] = x[ts - 8:, :]
    cw = convw_ref[...]
    acc = x * cw[CONV_K - 1:CONV_K, :]
    for sft in range(1, CONV_K):
        acc = acc + xe[8 - sft:8 - sft + ts, :] * cw[CONV_K - 1 - sft:CONV_K - sft, :]
    lane_qk = lax.broadcasted_iota(I32, (1, 2 * w), 1)
    qk_ref[...] = acc * _sigmoid(acc) * jnp.where(lane_qk < w, 1.0, hd ** -0.5)

    def hi_lo_rows(v):
        hi = v.astype(BF16).astype(F32)
        return jnp.where(rowm < nh, hi, pltpu.roll(v - hi, 4, axis=0))

    a_cr = hi_lo_rows(cr)
    a_b = hi_lo_rows(b_t)
    a_r = hi_lo_rows(r_t)
    for c in range(nchunk):
        rows = slice(c * CHUNK, (c + 1) * CHUNK)
        t8 = slice(c * 8, (c + 1) * 8)
        a_cr_c = a_cr[t8].astype(BF16)
        cr128_ref[rows, :] = _dot_tn(a_cr_c, sel128)
        cr64_ref[rows, :] = _dot_tn(a_cr_c, sel64)
        b64_ref[rows, :] = _dot_tn(a_b[t8].astype(BF16), sel64)
        r64_ref[rows, :] = _dot_tn(a_r[t8].astype(BF16), sel64)
        for h in range(nh):
            qg = qk_ref[rows, grp[h]].astype(BF16)
            kg = qk_ref[rows, w + grp[h].start:w + grp[h].stop].astype(BF16)
            s_ref[rows, h * 128:(h + 1) * 128] = _dot_nt(qg, jnp.where(half[h], kg, jnp.zeros_like(kg)))

    lane_w = lax.broadcasted_iota(I32, (1, w), 1)
    hmask = [(lane_w // hd) == h for h in range(nh)]
    ones_blk = [m.astype(BF16) * jnp.ones((CHUNK, 1), BF16) for m in hmask]
    ones_cols = jnp.ones((CHUNK, w), BF16)
    zblk = jnp.zeros((hd, 128), F32)
    ngrp = w // 128
    m_prev = m_ref[...]
    cblk = [st_ref[h * hd:(h + 1) * hd, grp[h]] for h in range(nh)]
    nblk = [st_ref[h * hd:(h + 1) * hd, w + grp[h].start:w + grp[h].stop] for h in range(nh)]
    for c in range(nchunk):
        rows = slice(c * CHUNK, (c + 1) * CHUNK)
        t8 = slice(c * 8, (c + 1) * 8)
        r_c = rt_ref[t8, :]
        g_last = jnp.maximum(rm_ref[t8, :], m_prev)
        decay = jnp.exp(m_prev - g_last)
        mprev_l = lanes(m_prev)
        glast_l = lanes(g_last)
        gb64 = jnp.maximum(cr64_ref[rows, :], mprev_l)
        w_inter = jnp.exp(mprev_l - gb64)
        e_negm = jnp.exp(-(b64_ref[rows, :] + gb64))
        v_all = zc_ref[rows, 2 * w:3 * w]
        wcat = []
        vblocks = []
        srows = []
        for h in range(nh):
            gb128 = jnp.maximum(cr128_ref[rows, h * 128:(h + 1) * 128], m_prev[h:h + 1, :])
            d = jnp.exp(jnp.where(causal, r_c[h:h + 1, :] - gb128, NEG))
            wcat.append((d * s_ref[rows, h * 128:(h + 1) * 128]).astype(BF16))
            vblocks.append(jnp.concatenate([jnp.where(hmask[h], v_all, jnp.zeros_like(v_all)), ones_blk[h]], axis=1))
            g = (h * hd) // 128
            srows.append(jnp.concatenate([cblk[h] if j == g else zblk for j in range(ngrp)]
                                         + [nblk[h] if j == g else zblk for j in range(ngrp)], axis=1))
        s_bf = jnp.concatenate(srows, axis=0).astype(BF16)
        out_aug = _dot(jnp.concatenate(wcat, axis=1), jnp.concatenate(vblocks, axis=0))
        out_aug = out_aug + _dot((qk_ref[rows, 0:w] * w_inter).astype(BF16), s_bf)
        hh_ref[rows, :] = out_aug[:, 0:w] / jnp.maximum(jnp.abs(out_aug[:, w:2 * w]), e_negm)
        kw = (qk_ref[rows, w:2 * w] * jnp.exp(r64_ref[rows, :] - glast_l)).astype(BF16)
        upd = _dot_tn(kw, jnp.concatenate([v_all, ones_cols], axis=1))
        for h in range(nh):
            rs = slice(h * hd, (h + 1) * hd)
            cblk[h] = cblk[h] * decay[h:h + 1, :] + jnp.where(half[h], upd[rs, grp[h]], 0.0)
            nblk[h] = (nblk[h] * decay[h:h + 1, :]
                       + jnp.where(half[h], upd[rs, w + grp[h].start:w + grp[h].stop], 0.0))
        m_prev = bl_ref[t8, :] + g_last
    m_ref[...] = m_prev
    for h in range(nh):
        st_ref[h * hd:(h + 1) * hd, grp[h]] = cblk[h]
        st_ref[h * hd:(h + 1) * hd, w + grp[h].start:w + grp[h].stop] = nblk[h]

    hh = hh_ref[...]
    ms = _dot((hh * hh).astype(BF16), bones_ref[...])
    og = zc_ref[:, 3 * w:4 * w].astype(F32)
    o_ref[...] = (_sigmoid(og) * hh * lax.rsqrt(ms + EPS) * ng_ref[...]).astype(o_ref.dtype)


def _mlstm(zc3, zgt, conv_w, gate_b_rows, norm_g):
    b, s, w = zc3.shape
    ts = SEQ_TILE
    nt = s // ts
    gw = GROUP_W
    t = jnp.arange(CHUNK)
    triu = (t[:, None] <= t[None, :]).astype(BF16)
    hid = jnp.arange(gw) // HEAD_DIM
    bones = jnp.where(hid[:, None] == hid[None, :], 1.0 / HEAD_DIM, 0.0).astype(BF16)
    full = lambda a: pl.BlockSpec(a.shape, lambda bi, i: (0,) * a.ndim)
    return pl.pallas_call(
        _mlstm_kernel,
        grid=(b, nt),
        in_specs=[pl.BlockSpec((None, ts, w), lambda bi, i: (bi, i, 0)),
                  pl.BlockSpec((8, ts), lambda bi, i: (0, bi * nt + i)),
                  full(conv_w), full(gate_b_rows), full(norm_g), full(triu), full(bones)],
        out_specs=pl.BlockSpec((None, ts, gw), lambda bi, i: (bi, i, 0)),
        out_shape=jax.ShapeDtypeStruct((b, s, gw), BF16),
        scratch_shapes=[pltpu.VMEM((gw, 2 * gw), F32),
                        pltpu.VMEM((8, 128), F32),
                        pltpu.VMEM((8, 2 * gw), F32),
                        pltpu.VMEM((ts, 2 * gw), F32),
                        pltpu.VMEM((ts // CHUNK * 8, 128), F32),
                        pltpu.VMEM((ts // CHUNK * 8, 128), F32),
                        pltpu.VMEM((ts // CHUNK * 8, 128), F32),
                        pltpu.VMEM((ts, MLSTM_HEADS * 128), F32),
                        pltpu.VMEM((ts, gw), F32),
                        pltpu.VMEM((ts, gw), F32),
                        pltpu.VMEM((ts, gw), F32),
                        pltpu.VMEM((ts, MLSTM_HEADS * 128), F32),
                        pltpu.VMEM((ts, gw), F32)],
        compiler_params=_cparams("parallel", "arbitrary"),
        name="mlstm",
    )(zc3, zgt, conv_w, gate_b_rows, norm_g, triu, bones)


def _sgu_kernel(zd_ref, lng_ref, lnb_ref, w_ref, bias_ref, o_ref):
    tm = zd_ref.shape[0]
    gw = GROUP_W // SGU_GROUPS
    u = _gelu(zd_ref[:, 0:GROUP_W].astype(F32))
    v = _gelu(zd_ref[:, GROUP_W:2 * GROUP_W].astype(F32))
    mu = jnp.mean(v, axis=-1, keepdims=True)
    vc = v - mu
    var = jnp.mean(vc * vc, axis=-1, keepdims=True)
    vn = (vc * lax.rsqrt(var + EPS) * lng_ref[...] + lnb_ref[...]).astype(BF16)
    for c in range(tm // CHUNK):
        rows = slice(c * CHUNK, (c + 1) * CHUNK)
        mixed = jnp.concatenate([_dot(w_ref[g], vn[rows, g * gw:(g + 1) * gw]) for g in range(SGU_GROUPS)], axis=-1)
        o_ref[rows, :] = (u[rows, :] * (mixed + bias_ref[...])).astype(o_ref.dtype)


def _sgu(zd, ln_g, ln_b, w_tril, bias):
    n = zd.shape[0]
    tm = TOKEN_TILE
    full = lambda a: pl.BlockSpec(a.shape, lambda i: (0,) * a.ndim)
    return pl.pallas_call(
        _sgu_kernel,
        grid=(n // tm,),
        in_specs=[pl.BlockSpec((tm, 2 * GROUP_W), lambda i: (i, 0)), full(ln_g), full(ln_b), full(w_tril), full(bias)],
        out_specs=pl.BlockSpec((tm, GROUP_W), lambda i: (i, 0)),
        out_shape=jax.ShapeDtypeStruct((n, GROUP_W), BF16),
        compiler_params=_cparams("parallel"),
        name="sgu",
    )(zd, ln_g, ln_b, w_tril, bias)


def _out_proj_kernel(x_ref, ya_ref, yb_ref, yc_ref, yd_ref, gluw_ref, glub_ref, wo_ref, ng_ref,
                     rwh_ref, rwl_ref, rb_ref,
                     x1_ref, hx_ref, meta_ref, cnt_ref, carry_ref):
    tm = x_ref.shape[0]

    @pl.when(pl.program_id(0) == 0)
    def _():
        carry_ref[...] = jnp.zeros_like(carry_ref)

    yb = _gelu(yb_ref[...].astype(F32))
    yb = yb * _sigmoid(_dot(yb.astype(BF16), gluw_ref[...]) + glub_ref[...])
    mix = _dot(ya_ref[...], wo_ref[0:GROUP_W, :])
    mix = mix + _dot(yb.astype(BF16), wo_ref[GROUP_W:2 * GROUP_W, :])
    mix = mix + _dot(yc_ref[...], wo_ref[2 * GROUP_W:3 * GROUP_W, :])
    mix = mix + _dot(yd_ref[...], wo_ref[3 * GROUP_W:4 * GROUP_W, :])
    x1 = x_ref[...] + mix
    x1_ref[...] = x1
    h = _rms(x1, ng_ref[...])
    hx_ref[:, 0:D_MODEL] = h

    hh, hl = _split_bf16(h)
    wh = rwh_ref[...]
    wl = rwl_ref[...]
    logits = _dot_nt(wh, hh) + _dot_nt(wh, hl) + _dot_nt(wl, hh) + rb_ref[...]
    gl = [logits[j:j + 1, :] for j in range(N_GROUPS)]
    gmax = functools.reduce(jnp.maximum, gl)
    gsel = jnp.full((1, tm), N_GROUPS - 1, I32)
    for j in range(N_GROUPS - 2, -1, -1):
        gsel = jnp.where(gl[j] == gmax, j, gsel)
    p_g = 1.0 / functools.reduce(jnp.add, [jnp.exp(v - gmax) for v in gl])
    e_in = []
    for i in range(EPG):
        v = logits[N_GROUPS + i:N_GROUPS + i + 1, :]
        for j in range(1, N_GROUPS):
            r = N_GROUPS + j * EPG + i
            v = jnp.where(gsel == j, logits[r:r + 1, :], v)
        e_in.append(v)
    v1 = functools.reduce(jnp.maximum, e_in)
    i1 = jnp.full((1, tm), EPG - 1, I32)
    for i in range(EPG - 2, -1, -1):
        i1 = jnp.where(e_in[i] == v1, i, i1)
    rest = [jnp.where(i1 == i, NEG, e_in[i]) for i in range(EPG)]
    v2 = functools.reduce(jnp.maximum, rest)
    i2 = jnp.full((1, tm), EPG - 1, I32)
    for i in range(EPG - 2, -1, -1):
        i2 = jnp.where((rest[i] == v2) & (i1 != i), i, i2)
    e2 = jnp.exp(v2 - v1)
    w1 = p_g / (1.0 + e2)
    w2 = p_g * e2 / (1.0 + e2)
    lo = jnp.minimum(i1, i2)
    hi = jnp.maximum(i1, i2)
    w_lo = jnp.where(i1 < i2, w1, w2)
    w_hi = jnp.where(i1 < i2, w2, w1)
    bucket = gsel * PAIRS + ((lo * (2 * EPG - 1 - lo)) >> 1) + (hi - lo - 1)

    kid = lax.broadcasted_iota(I32, (BUCKET_PAD, tm), 0)
    onehot = (kid == bucket).astype(F32)
    s_i = lax.broadcasted_iota(I32, (tm, tm), 0)
    t_i = lax.broadcasted_iota(I32, (tm, tm), 1)
    prefix = _dot(onehot.astype(BF16), (s_i <= t_i).astype(BF16))
    carry = carry_ref[...]
    rank = jnp.sum(onehot * (prefix - 1.0 + carry[:, 0:1]), axis=0, keepdims=True)
    carry = carry + prefix[:, tm - 1:tm]
    carry_ref[...] = carry
    cnt_ref[...] = carry
    meta_ref[...] = jnp.concatenate([bucket, rank.astype(I32), jnp.zeros((6, tm), I32)], axis=0)
    wrows = jnp.concatenate([w_lo, w_hi, jnp.zeros((126, tm), F32)], axis=0)
    for c in range(tm // 128):
        hx_ref[c * 128:(c + 1) * 128, D_MODEL:ROW_W] = wrows[:, c * 128:(c + 1) * 128].T


def _out_proj(x2, ya, yb, yc, yd, glu_w, glu_b, w_out, norm_g, rw_hi, rw_lo, rb):
    n = x2.shape[0]
    tm = TOKEN_TILE
    row = lambda w: pl.BlockSpec((tm, w), lambda i: (i, 0))
    full = lambda a: pl.BlockSpec(a.shape, lambda i: (0,) * a.ndim)
    return pl.pallas_call(
        _out_proj_kernel,
        grid=(n // tm,),
        in_specs=[row(D_MODEL), row(GROUP_W), row(GROUP_W), row(GROUP_W), row(GROUP_W),
                  full(glu_w), full(glu_b), full(w_out), full(norm_g), full(rw_hi), full(rw_lo), full(rb)],
        out_specs=[row(D_MODEL), row(ROW_W), pl.BlockSpec((8, tm), lambda i: (0, i)),
                   pl.BlockSpec((BUCKET_PAD, 128), lambda i: (0, 0))],
        out_shape=[jax.ShapeDtypeStruct((n, D_MODEL), F32), jax.ShapeDtypeStruct((n, ROW_W), F32),
                   jax.ShapeDtypeStruct((8, n), I32), jax.ShapeDtypeStruct((BUCKET_PAD, 128), F32)],
        scratch_shapes=[pltpu.VMEM((BUCKET_PAD, 128), F32)],
        compiler_params=_cparams("arbitrary"),
        name="out_proj",
    )(x2, ya, yb, yc, yd, glu_w, glu_b, w_out, norm_g, rw_hi, rw_lo, rb)


def _row_copy(src_ref, src_row, dst_ref, dst_row, sem):
    return pltpu.make_async_copy(src_ref.at[pl.ds(src_row, 1)], dst_ref.at[pl.ds(dst_row, 1)], sem)


def _dispatch_kernel(dest_ref, hx_ref, xs_in_ref, xs_ref, sem):
    del xs_in_ref
    tm = hx_ref.shape[0]

    def start(t, c):
        _row_copy(hx_ref, t, xs_ref, dest_ref[0, 0, t], sem).start()
        return c

    lax.fori_loop(0, tm, start, 0, unroll=ROW_DMA_UNROLL)
    pltpu.make_async_copy(hx_ref, xs_ref.at[pl.ds(0, tm)], sem).wait()


def _dispatch(dest3, hx, p_tot):
    n = hx.shape[0]
    tm = TOKEN_TILE
    xs0 = jnp.zeros((p_tot, ROW_W), F32)
    return pl.pallas_call(
        _dispatch_kernel,
        grid=(n // tm,),
        in_specs=[pl.BlockSpec((1, 1, tm), lambda i: (i, 0, 0), memory_space=pltpu.SMEM),
                  pl.BlockSpec((tm, ROW_W), lambda i: (i, 0)),
                  pl.BlockSpec(memory_space=pl.ANY)],
        out_specs=pl.BlockSpec(memory_space=pl.ANY),
        out_shape=jax.ShapeDtypeStruct((p_tot, ROW_W), F32),
        input_output_aliases={2: 0},
        scratch_shapes=[pltpu.SemaphoreType.DMA],
        compiler_params=_cparams("arbitrary"),
        name="dispatch",
    )(dest3, hx, xs0)


def _ffn_kernel(ea_ref, eb_ref, nact_ref, xs_ref, wg_a_ref, wu_a_ref, wd_a_ref, wg_b_ref, wu_b_ref, wd_b_ref,
                ys_ref, wgu_a, wdn_a, wgu_b, wdn_b):
    j = pl.program_id(0)
    active = j < nact_ref[0]
    jp = jnp.maximum(j - 1, 0)

    @pl.when((j == 0) | (ea_ref[j] != ea_ref[jp]))
    def _():
        wgu_a[:, 0:D_EXPERT] = wg_a_ref[...].astype(BF16)
        wgu_a[:, D_EXPERT:2 * D_EXPERT] = wu_a_ref[...].astype(BF16)
        wdn_a[...] = wd_a_ref[...].astype(BF16)

    @pl.when((j == 0) | (eb_ref[j] != eb_ref[jp]))
    def _():
        wgu_b[:, 0:D_EXPERT] = wg_b_ref[...].astype(BF16)
        wgu_b[:, D_EXPERT:2 * D_EXPERT] = wu_b_ref[...].astype(BF16)
        wdn_b[...] = wd_b_ref[...].astype(BF16)

    @pl.when(jnp.logical_not(active))
    def _():
        ys_ref[...] = jnp.zeros_like(ys_ref)

    @pl.when(active)
    def _():
        xb = xs_ref[:, 0:D_MODEL].astype(BF16)

        def expert(wgu, wdn):
            gu = _dot(xb, wgu[...])
            g = gu[:, 0:D_EXPERT]
            act = (g * _sigmoid(g) * gu[:, D_EXPERT:2 * D_EXPERT]).astype(BF16)
            return _dot(act, wdn[...])

        ya = expert(wgu_a, wdn_a)
        yb = expert(wgu_b, wdn_b)
        ys_ref[...] = ya * xs_ref[:, D_MODEL:D_MODEL + 1] + yb * xs_ref[:, D_MODEL + 1:D_MODEL + 2]


def _ffn(ea, eb, nact, xs, layer, w_gate, w_up, w_down):
    p_tot = xs.shape[0]
    bm = FFN_BLOCK
    nblk = p_tot // bm
    wspec = lambda e_idx, shape: pl.BlockSpec((None, None) + shape,
                                              lambda j, ea, eb, na: (layer, (ea, eb)[e_idx][j], 0, 0))
    up_shape = (D_MODEL, D_EXPERT)
    dn_shape = (D_EXPERT, D_MODEL)
    return pl.pallas_call(
        _ffn_kernel,
        grid_spec=pltpu.PrefetchScalarGridSpec(
            num_scalar_prefetch=3,
            grid=(nblk,),
            in_specs=[pl.BlockSpec((bm, ROW_W), lambda j, ea, eb, na: (j, 0)),
                      wspec(0, up_shape), wspec(0, up_shape), wspec(0, dn_shape),
                      wspec(1, up_shape), wspec(1, up_shape), wspec(1, dn_shape)],
            out_specs=pl.BlockSpec((bm, D_MODEL), lambda j, ea, eb, na: (j, 0)),
            scratch_shapes=[pltpu.VMEM((D_MODEL, 2 * D_EXPERT), BF16), pltpu.VMEM(dn_shape, BF16),
                            pltpu.VMEM((D_MODEL, 2 * D_EXPERT), BF16), pltpu.VMEM(dn_shape, BF16)],
        ),
        out_shape=jax.ShapeDtypeStruct((p_tot, D_MODEL), F32),
        compiler_params=_cparams("arbitrary"),
        name="ffn",
    )(ea, eb, nact, xs, w_gate, w_up, w_down, w_gate, w_up, w_down)


def _combine_kernel(dest_ref, x1_ref, ys_ref, ng_ref, o_ref, buf_ref, sem, *, final_norm):
    tm = x1_ref.shape[0]

    def start(t, c):
        _row_copy(ys_ref, dest_ref[0, 0, t], buf_ref, t, sem).start()
        return c

    lax.fori_loop(0, tm, start, 0, unroll=ROW_DMA_UNROLL)
    pltpu.make_async_copy(ys_ref.at[pl.ds(0, tm)], buf_ref, sem).wait()
    x2 = x1_ref[...] + buf_ref[...]
    o_ref[...] = _rms(x2, ng_ref[...]) if final_norm else x2


def _combine(dest3, x1, ys, norm_g, final_norm):
    n = x1.shape[0]
    tm = TOKEN_TILE
    return pl.pallas_call(
        functools.partial(_combine_kernel, final_norm=final_norm),
        grid=(n // tm,),
        in_specs=[pl.BlockSpec((1, 1, tm), lambda i: (i, 0, 0), memory_space=pltpu.SMEM),
                  pl.BlockSpec((tm, D_MODEL), lambda i: (i, 0)),
                  pl.BlockSpec(memory_space=pl.ANY),
                  pl.BlockSpec((1, D_MODEL), lambda i: (0, 0))],
        out_specs=pl.BlockSpec((tm, D_MODEL), lambda i: (i, 0)),
        out_shape=jax.ShapeDtypeStruct((n, D_MODEL), F32),
        scratch_shapes=[pltpu.VMEM((tm, D_MODEL), F32), pltpu.SemaphoreType.DMA],
        compiler_params=_cparams("arbitrary"),
        name="combine",
    )(dest3, x1, ys, norm_g)


def _routing_tables(meta, counts, n):
    bm = FFN_BLOCK
    nblk = n // bm + N_BUCKETS
    bucket = meta[0]
    rank = meta[1]
    cnt = counts[:N_BUCKETS, 0].astype(I32)
    padded = ((cnt + bm - 1) // bm) * bm
    pad_end = jnp.cumsum(padded)
    pad_start = pad_end - padded
    onehot = bucket[:, None] == jnp.arange(N_BUCKETS, dtype=I32)[None, :]
    dest = rank + jnp.sum(jnp.where(onehot, pad_start[None, :], 0), axis=1)
    nact = (pad_end[-1] // bm).astype(I32)
    j = jnp.arange(nblk, dtype=I32)
    blk = jnp.minimum(j, jnp.maximum(nact - 1, 0))
    bkt = jnp.minimum(jnp.searchsorted(pad_end, blk * bm, side='right'), N_BUCKETS - 1).astype(I32)
    grp = bkt // PAIRS
    pidx = bkt % PAIRS
    lo_tab, hi_tab = [], []
    for a in range(EPG):
        for b in range(a + 1, EPG):
            lo_tab.append(a)
            hi_tab.append(b)
    ea = grp * EPG + jnp.asarray(lo_tab, I32)[pidx]
    eb = grp * EPG + jnp.asarray(hi_tab, I32)[pidx]
    return dest.astype(I32), ea.astype(I32), eb.astype(I32), nact.reshape(1), nblk * bm


def _layer(x2, b, s, p, final_g):
    n = b * s
    za, zb, zc, zd, zgt = _in_proj(x2, p['norm_mix_g'], p['wa'], p['wb'], p['wc'], p['wd'], p['wgt'])
    ya = _attention(za.reshape(b, s, -1), p['sinks']).reshape(n, GROUP_W)
    yb = _ssm(zb.reshape(b, s, GROUP_W), p['ssm']).reshape(n, GROUP_W)
    yc = _mlstm(zc.reshape(b, s, -1), zgt, p['conv_w'], p['gate_b'], p['mlstm_norm_g']).reshape(n, GROUP_W)
    yd = _sgu(zd, p['sgu_ln_g'], p['sgu_ln_b'], p['sgu_w'], p['sgu_bias'])
    x1, hx, meta, counts = _out_proj(x2, ya, yb, yc, yd, p['glu_w'], p['glu_b'], p['w_out'], p['norm_ffn_g'],
                                     p['rw_hi'], p['rw_lo'], p['rb'])
    dest, ea, eb, nact, p_tot = _routing_tables(meta, counts, n)
    dest3 = dest.reshape(n // TOKEN_TILE, 1, TOKEN_TILE)
    xs = _dispatch(dest3, hx, p_tot)
    ys = _ffn(ea, eb, nact, xs, p['layer'], p['w_gate'], p['w_up'], p['w_down'])
    if final_g is None:
        return _combine(dest3, x1, ys, p['norm_ffn_g'], False)
    return _combine(dest3, x1, ys, final_g, True)


def _prep_layer(l, norm_mix_g, w_in, attn_sinks, ssm_a_re, ssm_a_im, ssm_b_re, ssm_b_im, ssm_c_re, ssm_c_im,
                ssm_d, ssm_log_dt, ssm_glu_w, ssm_glu_b, mlstm_conv_w, mlstm_gate_b, mlstm_norm_g,
                sgu_ln_g, sgu_ln_b, sgu_w, sgu_b, w_out, norm_ffn_g, router_group_w, router_group_b,
                router_expert_w, router_expert_b, expert_w_gate, expert_w_up, expert_w_down):
    w = w_in[l]
    o_su = 2 * GROUP_W
    o_c = o_su + GROUP_W
    o_ci = o_c + 3 * GROUP_W
    o_co = o_ci + 2 * MLSTM_HEADS
    o_d = o_co + GROUP_W
    wc = jnp.concatenate([w[:, o_c:o_ci], w[:, o_co:o_d]], axis=1)
    wgt = w[:, o_ci:o_co].T
    rw = jnp.concatenate([router_group_w[l], router_expert_w[l]], axis=1).T.astype(F32)
    rw = jnp.pad(rw, ((0, 4), (0, 0)))
    rw_hi = rw.astype(BF16)
    rw_lo = (rw - rw_hi.astype(F32)).astype(BF16)
    rb = jnp.pad(jnp.concatenate([router_group_b[l], router_expert_b[l]]).astype(F32), (0, 4))[:, None]
    tril = jnp.tril(jnp.ones((CHUNK, CHUNK), F32))
    gw = GROUP_W // SGU_GROUPS
    return dict(
        norm_mix_g=norm_mix_g[l][None, :].astype(F32),
        wa=w[:, 0:o_su].astype(BF16), wb=w[:, o_su:o_c].astype(BF16), wc=wc.astype(BF16),
        wd=w[:, o_d:].astype(BF16), wgt=wgt.astype(BF16),
        sinks=attn_sinks[l].astype(F32),
        ssm=_ssm_matrices(ssm_a_re[l], ssm_a_im[l], ssm_b_re[l], ssm_b_im[l], ssm_c_re[l], ssm_c_im[l],
                          ssm_d[l], ssm_log_dt[l]),
        glu_w=ssm_glu_w[l].astype(BF16), glu_b=ssm_glu_b[l][None, :].astype(F32),
        conv_w=mlstm_conv_w[l].astype(F32),
        gate_b=jnp.broadcast_to(mlstm_gate_b[l].astype(F32)[:, None], (2 * MLSTM_HEADS, 128)),
        mlstm_norm_g=mlstm_norm_g[l][None, :].astype(F32),
        sgu_ln_g=sgu_ln_g[l][None, :].astype(F32), sgu_ln_b=sgu_ln_b[l][None, :].astype(F32),
        sgu_w=(sgu_w[l].astype(F32) * tril).astype(BF16),
        sgu_bias=jnp.repeat(sgu_b[l].astype(F32).T, gw, axis=1),
        w_out=w_out[l].astype(BF16), norm_ffn_g=norm_ffn_g[l][None, :].astype(F32),
        rw_hi=rw_hi, rw_lo=rw_lo, rb=rb,
        layer=l, w_gate=expert_w_gate, w_up=expert_w_up, w_down=expert_w_down,
    )


def kernel(x, norm_mix_g, w_in, attn_sinks, ssm_a_re, ssm_a_im, ssm_b_re, ssm_b_im, ssm_c_re, ssm_c_im, ssm_d, ssm_log_dt, ssm_glu_w, ssm_glu_b, mlstm_conv_w, mlstm_gate_b, mlstm_norm_g, sgu_ln_g, sgu_ln_b, sgu_w, sgu_b, w_out, norm_ffn_g, router_group_w, router_group_b, router_expert_w, router_expert_b, expert_w_gate, expert_w_up, expert_w_down, norm_final_g):
    b, s, d = x.shape
    depth = w_in.shape[0]
    x2 = x.reshape(b * s, d).astype(F32)
    for l in range(depth):
        p = _prep_layer(l, norm_mix_g, w_in, attn_sinks, ssm_a_re, ssm_a_im, ssm_b_re, ssm_b_im, ssm_c_re,
                        ssm_c_im, ssm_d, ssm_log_dt, ssm_glu_w, ssm_glu_b, mlstm_conv_w, mlstm_gate_b,
                        mlstm_norm_g, sgu_ln_g, sgu_ln_b, sgu_w, sgu_b, w_out, norm_ffn_g, router_group_w,
                        router_group_b, router_expert_w, router_expert_b, expert_w_gate, expert_w_up,
                        expert_w_down)
        final_g = norm_final_g[None, :].astype(F32) if l == depth - 1 else None
        x2 = _layer(x2, b, s, p, final_g)
    return x2.reshape(b, s, d).astype(x.dtype)
```

```python
import functools
import math

import jax
import jax.numpy as jnp
from jax import lax
from jax.experimental import pallas as pl
from jax.experimental.pallas import tpu as pltpu

F32 = jnp.float32
BF16 = jnp.bfloat16
I32 = jnp.int32

D_MODEL = 1024
GROUP_W = 256
HEAD_DIM = 64
EPS = 1e-6
NEG = -1e30
WINDOW = 128
KV_HEADS = 2
SSM_GROUP = 16
SSM_GROUPS = 16
SSM_STATE = 64
SSM_CHUNK = 16
MLSTM_HEADS = 4
CHUNK = 128
CONV_K = 4
SGU_GROUPS = 4
N_GROUPS = 4
EPG = 8
N_EXPERTS = 32
D_EXPERT = 512
PAIRS = EPG * (EPG - 1) // 2
N_BUCKETS = N_GROUPS * PAIRS
BUCKET_PAD = 128
ROW_W = D_MODEL + 128

TOKEN_TILE = 512
SEQ_TILE = 512
SSM_TILE = 4096
FFN_BLOCK = 512
FFN_SUB = 256
ROW_DMA_UNROLL = 8
VMEM_LIMIT = 56 * 1024 * 1024


def _cparams(*sem):
    return pltpu.CompilerParams(dimension_semantics=sem, vmem_limit_bytes=VMEM_LIMIT)


def _rms(x, g):
    return x * lax.rsqrt(jnp.mean(x * x, axis=-1, keepdims=True) + EPS) * g


def _gelu(x):
    return 0.5 * x * (1.0 + jnp.tanh(math.sqrt(2.0 / math.pi) * (x + 0.044715 * (x * x * x))))


def _sigmoid(x):
    return 1.0 / (1.0 + jnp.exp(-x))


def _log_sigmoid(x):
    return jnp.minimum(x, 0.0) - jnp.log(1.0 + jnp.exp(-jnp.abs(x)))


def _dot(a, b):
    return jnp.dot(a, b, preferred_element_type=F32)


def _dot_nt(a, b):
    return lax.dot_general(a, b, (((1,), (1,)), ((), ())), preferred_element_type=F32)


def _dot_tn(a, b):
    return lax.dot_general(a, b, (((0,), (0,)), ((), ())), preferred_element_type=F32)


def _split_bf16(x):
    hi = x.astype(BF16)
    lo = (x - hi.astype(F32)).astype(BF16)
    return hi, lo


def _in_proj_kernel(x_ref, g_ref, wa_ref, wb_ref, wc_ref, wd_ref, wgt_ref,
                    za_ref, zb_ref, zc_ref, zd_ref, zgt_ref):
    hb = _rms(x_ref[...], g_ref[...]).astype(BF16)
    za_ref[...] = _dot(hb, wa_ref[...]).astype(BF16)
    zb_ref[...] = _dot(hb, wb_ref[...]).astype(BF16)
    zc_ref[...] = _dot(hb, wc_ref[...]).astype(BF16)
    zd_ref[...] = _dot(hb, wd_ref[...]).astype(BF16)
    zgt_ref[...] = _dot_nt(wgt_ref[...], hb)


def _in_proj(x2, g, wa, wb, wc, wd, wgt):
    n = x2.shape[0]
    tm = TOKEN_TILE
    row = lambda w: pl.BlockSpec((tm, w), lambda i: (i, 0))
    full = lambda a: pl.BlockSpec(a.shape, lambda i: (0,) * a.ndim)
    widths = (wa.shape[1], wb.shape[1], wc.shape[1], wd.shape[1])
    return pl.pallas_call(
        _in_proj_kernel,
        grid=(n // tm,),
        in_specs=[row(D_MODEL), full(g), full(wa), full(wb), full(wc), full(wd), full(wgt)],
        out_specs=[row(w) for w in widths] + [pl.BlockSpec((wgt.shape[0], tm), lambda i: (0, i))],
        out_shape=[jax.ShapeDtypeStruct((n, w), BF16) for w in widths]
        + [jax.ShapeDtypeStruct((wgt.shape[0], n), F32)],
        compiler_params=_cparams("parallel"),
        name="in_proj",
    )(x2, g, wa, wb, wc, wd, wgt)


def _attn_kernel(sink_ref, cur_ref, prev_ref, o_ref):
    first = pl.program_id(1) == 0
    nblk = cur_ref.shape[0] // WINDOW
    row = lax.broadcasted_iota(I32, (2 * WINDOW, 2 * WINDOW), 0)
    col = lax.broadcasted_iota(I32, (2 * WINDOW, 2 * WINDOW), 1)
    qi = jnp.where(row >= WINDOW, row - WINDOW, row)
    band = (col <= qi + WINDOW) & (col > qi)
    row1 = lax.broadcasted_iota(I32, (2 * WINDOW, 1), 0)
    for j in range(nblk):
        cur = cur_ref[j * WINDOW:(j + 1) * WINDOW, :]
        if j == 0:
            prev = prev_ref[...]
            mask = band & ((col >= WINDOW) | jnp.logical_not(first))
        else:
            prev = cur_ref[(j - 1) * WINDOW:j * WINDOW, :]
            mask = band
        outs = []
        for g in range(KV_HEADS):
            q2 = jnp.concatenate([cur[:, (2 * g) * HEAD_DIM:(2 * g + 1) * HEAD_DIM],
                                  cur[:, (2 * g + 1) * HEAD_DIM:(2 * g + 2) * HEAD_DIM]], axis=0)
            ko = GROUP_W + g * HEAD_DIM
            vo = GROUP_W + KV_HEADS * HEAD_DIM + g * HEAD_DIM
            kk = jnp.concatenate([prev[:, ko:ko + HEAD_DIM], cur[:, ko:ko + HEAD_DIM]], axis=0)
            vv = jnp.concatenate([prev[:, vo:vo + HEAD_DIM], cur[:, vo:vo + HEAD_DIM]], axis=0)
            s = _dot_nt(q2, kk) * (HEAD_DIM ** -0.5)
            s = jnp.where(mask, s, NEG)
            sink = jnp.where(row1 < WINDOW, sink_ref[2 * g], sink_ref[2 * g + 1])
            m = jnp.maximum(jnp.max(s, axis=-1, keepdims=True), sink)
            p = jnp.exp(s - m)
            l = jnp.sum(p, axis=-1, keepdims=True) + jnp.exp(sink - m)
            o = _dot(p.astype(BF16), vv) / l
            outs.append(o[:WINDOW])
            outs.append(o[WINDOW:])
        o_ref[j * WINDOW:(j + 1) * WINDOW, :] = jnp.concatenate(outs, axis=-1).astype(o_ref.dtype)


def _attention(za3, sinks):
    b, s, w = za3.shape
    ts = SEQ_TILE
    per = ts // WINDOW
    return pl.pallas_call(
        _attn_kernel,
        grid=(b, s // ts),
        in_specs=[pl.BlockSpec(memory_space=pltpu.SMEM),
                  pl.BlockSpec((None, ts, w), lambda bi, i: (bi, i, 0)),
                  pl.BlockSpec((None, WINDOW, w), lambda bi, i: (bi, jnp.maximum(i * per - 1, 0), 0))],
        out_specs=pl.BlockSpec((None, ts, GROUP_W), lambda bi, i: (bi, i, 0)),
        out_shape=jax.ShapeDtypeStruct((b, s, GROUP_W), BF16),
        compiler_params=_cparams("parallel", "parallel"),
        name="attn",
    )(sinks, za3, za3)


def _ssm_kernel(zb_ref, bre_ref, bim_ref, t_ref, cre_ref, cim_ref, are_ref, aim_ref, o_ref,
                x_ref, xs_ref, u_ref, vre_ref, vim_ref, sre_ref, sim_ref, y_ref, st_ref):
    ts = zb_ref.shape[0]
    L, G, H = SSM_CHUNK, SSM_GROUPS, SSM_GROUP
    nch = ts // L
    half_g = 128 // H
    n_half = G // half_g

    @pl.when(pl.program_id(1) == 0)
    def _():
        st_ref[...] = jnp.zeros_like(st_ref)

    for hf in range(n_half):
        x_ref[hf] = zb_ref[:, hf * 128:(hf + 1) * 128].astype(F32)
    for sg in range(L):
        for hf in range(n_half):
            xs_ref[sg * n_half + hf] = x_ref[hf, pl.ds(sg, nch, stride=L), :]
    for g in range(G):
        hf = g // half_g
        lo = (g % half_g) * H
        ug = jnp.concatenate([xs_ref[sg * n_half + hf, :, lo:lo + H] for sg in range(L)], axis=-1).astype(BF16)
        u_ref[g] = ug
        vre_ref[pl.ds(g, nch, stride=G), :] = _dot(ug, bre_ref[g])
        vim_ref[pl.ds(g, nch, stride=G), :] = _dot(ug, bim_ref[g])

    are = are_ref[...]
    aim = aim_ref[...]

    def step(c, carry):
        sre, sim = carry
        r0 = pl.multiple_of(c * G, G)
        sre_ref[pl.ds(r0, G), :] = sre
        sim_ref[pl.ds(r0, G), :] = sim
        vre = vre_ref[pl.ds(r0, G), :]
        vim = vim_ref[pl.ds(r0, G), :]
        return (are * sre - aim * sim + vre, are * sim + aim * sre + vim)

    sre, sim = lax.fori_loop(0, nch, step, (st_ref[0:G, :], st_ref[G:2 * G, :]), unroll=4)
    st_ref[0:G, :] = sre
    st_ref[G:2 * G, :] = sim

    for g in range(G):
        y = _dot(u_ref[g], t_ref[g])
        y = y + _dot(sre_ref[pl.ds(g, nch, stride=G), :].astype(BF16), cre_ref[g])
        y = y + _dot(sim_ref[pl.ds(g, nch, stride=G), :].astype(BF16), cim_ref[g])
        y_ref[g] = y
    for tau in range(L):
        for hf in range(n_half):
            z = jnp.concatenate([y_ref[hf * half_g + gl, :, tau * H:(tau + 1) * H] for gl in range(half_g)], axis=-1)
            x_ref[hf, pl.ds(tau, nch, stride=L), :] = z
    for hf in range(n_half):
        o_ref[:, hf * 128:(hf + 1) * 128] = x_ref[hf].astype(o_ref.dtype)


def _ssm_matrices(a_re, a_im, b_re, b_im, c_re, c_im, d_skip, log_dt):
    L = SSM_CHUNK
    a = lax.complex(a_re.astype(F32), a_im.astype(F32))
    dt = jnp.exp(log_dt.astype(F32))[:, None]
    adt = a * dt
    a_bar = jnp.exp(adt)
    b_bar = ((a_bar - 1.0) / a)[..., None] * lax.complex(b_re.astype(F32), b_im.astype(F32))
    c_mat = lax.complex(c_re.astype(F32), c_im.astype(F32))
    lag = jnp.arange(L + 1, dtype=F32)
    pw = jnp.exp(adt[None] * lag[:, None, None])
    kern = jnp.einsum('gop,dgp,gpi->dgoi', c_mat, pw[:L], b_bar).real
    sig = jnp.arange(L)[:, None]
    tau = jnp.arange(L)[None, :]
    d = tau - sig
    kt = kern[jnp.clip(d, 0, L - 1)]
    kt = jnp.where((d >= 0)[:, :, None, None, None], kt, 0.0)
    eye = jnp.eye(SSM_GROUP, dtype=F32)
    dsk = d_skip.astype(F32).reshape(SSM_GROUPS, SSM_GROUP)
    kt = kt + (d == 0)[:, :, None, None, None] * (dsk[:, :, None] * eye)[None, None]
    t_mat = kt.transpose(2, 0, 4, 1, 3).reshape(SSM_GROUPS, L * SSM_GROUP, L * SSM_GROUP)
    bm = pw[:L][::-1][:, :, :, None] * b_bar[None]
    bm = bm.transpose(1, 0, 3, 2).reshape(SSM_GROUPS, L * SSM_GROUP, SSM_STATE)
    cm = c_mat[None] * pw[1:][:, :, None, :]
    cm = cm.transpose(1, 3, 0, 2).reshape(SSM_GROUPS, SSM_STATE, L * SSM_GROUP)
    a_chunk = pw[L]
    pad = 128 - SSM_STATE
    pc = lambda m: jnp.pad(m, ((0, 0), (0, 0), (0, pad))).astype(BF16)
    pr = lambda m: jnp.pad(m, ((0, 0), (0, pad), (0, 0))).astype(BF16)
    pa = lambda m: jnp.pad(m, ((0, 0), (0, pad)))
    return (pc(bm.real), pc(bm.imag), t_mat.astype(BF16), pr(cm.real), pr(-cm.imag),
            pa(a_chunk.real), pa(a_chunk.imag))


def _ssm(zb3, mats):
    b, s, w = zb3.shape
    L, G = SSM_CHUNK, SSM_GROUPS
    ts = min(SSM_TILE, s)
    nch = ts // L
    full = lambda a: pl.BlockSpec(a.shape, lambda bi, i: (0,) * a.ndim)
    return pl.pallas_call(
        _ssm_kernel,
        grid=(b, s // ts),
        in_specs=[pl.BlockSpec((None, ts, w), lambda bi, i: (bi, i, 0))] + [full(m) for m in mats],
        out_specs=pl.BlockSpec((None, ts, w), lambda bi, i: (bi, i, 0)),
        out_shape=jax.ShapeDtypeStruct((b, s, w), BF16),
        scratch_shapes=[pltpu.VMEM((w // 128, ts, 128), F32),
                        pltpu.VMEM((L * (w // 128), nch, 128), F32),
                        pltpu.VMEM((G, nch, L * SSM_GROUP), BF16),
                        pltpu.VMEM((nch * G, 128), F32), pltpu.VMEM((nch * G, 128), F32),
                        pltpu.VMEM((nch * G, 128), F32), pltpu.VMEM((nch * G, 128), F32),
                        pltpu.VMEM((G, nch, L * SSM_GROUP), F32),
                        pltpu.VMEM((2 * G, 128), F32)],
        compiler_params=_cparams("parallel", "arbitrary"),
        name="ssm",
    )(zb3, *mats)


def _mlstm_kernel(zc_ref, zgt_ref, convw_ref, gb_ref, ng_ref, triu_ref, bones_ref, o_ref,
                  st_ref, m_ref, tail_ref, qk_ref,
                  rt_ref, bl_ref, rm_ref, cr128_ref, cr64_ref, b64_ref, r64_ref, s_ref, hh_ref):
    ts = zc_ref.shape[0]
    hd, nh, w = HEAD_DIM, MLSTM_HEADS, GROUP_W
    nchunk = ts // CHUNK
    nr = nchunk * 8
    lane1 = lax.broadcasted_iota(I32, (1, 128), 1)
    half = [(lane1 // hd) == (h % 2) for h in range(nh)]
    grp = [slice((h * hd) // 128 * 128, (h * hd) // 128 * 128 + 128) for h in range(nh)]

    @pl.when(pl.program_id(1) == 0)
    def _():
        st_ref[...] = jnp.zeros_like(st_ref)
        m_ref[...] = jnp.zeros_like(m_ref)
        tail_ref[...] = jnp.zeros_like(tail_ref)

    r_i = lax.broadcasted_iota(I32, (CHUNK, CHUNK), 0)
    c_i = lax.broadcasted_iota(I32, (CHUNK, CHUNK), 1)
    causal = c_i <= r_i
    k64 = lax.broadcasted_iota(I32, (8, w), 0)
    j64 = lax.broadcasted_iota(I32, (8, w), 1)
    sel64 = (j64 // hd == k64 % nh).astype(BF16)
    k128 = lax.broadcasted_iota(I32, (8, nh * 128), 0)
    j128 = lax.broadcasted_iota(I32, (8, nh * 128), 1)
    sel128 = (j128 // 128 == k128 % nh).astype(BF16)

    def lanes(t):
        lo = jnp.where(lane1 < hd, t[0:1, :], t[1:2, :])
        hi = jnp.where(lane1 < hd, t[2:3, :], t[3:4, :])
        return jnp.concatenate([lo, hi], axis=1)

    rowm = lax.broadcasted_iota(I32, (nr, 128), 0) % 8
    lanem = lax.broadcasted_iota(I32, (nr, 128), 1)
    graw = jnp.concatenate([zgt_ref[:, c * CHUNK:(c + 1) * CHUNK] + gb_ref[...] for c in range(nchunk)], axis=0)
    g2 = jnp.where(rowm < nh, graw, _log_sigmoid(graw))
    ghi, glo = _split_bf16(g2)
    cum = _dot(ghi, triu_ref[...]) + _dot(glo, triu_ref[...])
    b_t = pltpu.roll(cum, nr - 4, axis=0)
    r_t = g2 - b_t
    cr = r_t
    sh = 1
    while sh < CHUNK:
        cr = jnp.maximum(cr, jnp.where(lanem >= sh, pltpu.roll(cr, sh, axis=1), NEG))
        sh *= 2
    rt_ref[...] = r_t
    bl_ref[...] = jnp.broadcast_to(b_t[:, CHUNK - 1:CHUNK], (nr, 128))
    rm_ref[...] = jnp.broadcast_to(cr[:, CHUNK - 1:CHUNK], (nr, 128))

    x = zc_ref[:, 0:2 * w].astype(F32)
    xe = jnp.concatenate([tail_ref[...], x], axis=0)
    tail_ref[...] = x[ts - 8:, :]
    cw = convw_ref[...]
    acc = x * cw[CONV_K - 1:CONV_K, :]
    for sft in range(1, CONV_K):
        acc = acc + xe[8 - sft:8 - sft + ts, :] * cw[CONV_K - 1 - sft:CONV_K - sft, :]
    lane_qk = lax.broadcasted_iota(I32, (1, 2 * w), 1)
    qk_ref[...] = acc * _sigmoid(acc) * jnp.where(lane_qk < w, 1.0, hd ** -0.5)

    def hi_lo_rows(v):
        hi = v.astype(BF16).astype(F32)
        return jnp.where(rowm < nh, hi, pltpu.roll(v - hi, 4, axis=0))

    a_cr = hi_lo_rows(cr)
    a_b = hi_lo_rows(b_t)
    a_r = hi_lo_rows(r_t)
    for c in range(nchunk):
        rows = slice(c * CHUNK, (c + 1) * CHUNK)
        t8 = slice(c * 8, (c + 1) * 8)
        a_cr_c = a_cr[t8].astype(BF16)
        cr128_ref[rows, :] = _dot_tn(a_cr_c, sel128)
        cr64_ref[rows, :] = _dot_tn(a_cr_c, sel64)
        b64_ref[rows, :] = _dot_tn(a_b[t8].astype(BF16), sel64)
        r64_ref[rows, :] = _dot_tn(a_r[t8].astype(BF16), sel64)
        for h in range(nh):
            qg = qk_ref[rows, grp[h]].astype(BF16)
            kg = qk_ref[rows, w + grp[h].start:w + grp[h].stop].astype(BF16)
            s_ref[rows, h * 128:(h + 1) * 128] = _dot_nt(qg, jnp.where(half[h], kg, jnp.zeros_like(kg)))

    lane_w = lax.broadcasted_iota(I32, (1, w), 1)
    hmask = [(lane_w // hd) == h for h in range(nh)]
    ones_blk = [m.astype(BF16) * jnp.ones((CHUNK, 1), BF16) for m in hmask]
    ones_cols = jnp.ones((CHUNK, w), BF16)
    zblk = jnp.zeros((hd, 128), F32)
    ngrp = w // 128
    m_prev = m_ref[...]
    cblk = [st_ref[h * hd:(h + 1) * hd, grp[h]] for h in range(nh)]
    nblk = [st_ref[h * hd:(h + 1) * hd, w + grp[h].start:w + grp[h].stop] for h in range(nh)]
    for c in range(nchunk):
        rows = slice(c * CHUNK, (c + 1) * CHUNK)
        t8 = slice(c * 8, (c + 1) * 8)
        r_c = rt_ref[t8, :]
        g_last = jnp.maximum(rm_ref[t8, :], m_prev)
        decay = jnp.exp(m_prev - g_last)
        mprev_l = lanes(m_prev)
        glast_l = lanes(g_last)
        gb64 = jnp.maximum(cr64_ref[rows, :], mprev_l)
        w_inter = jnp.exp(mprev_l - gb64)
        e_negm = jnp.exp(-(b64_ref[rows, :] + gb64))
        v_all = zc_ref[rows, 2 * w:3 * w]
        wcat = []
        vblocks = []
        srows = []
        for h in range(nh):
            gb128 = jnp.maximum(cr128_ref[rows, h * 128:(h + 1) * 128], m_prev[h:h + 1, :])
            d = jnp.exp(jnp.where(causal, r_c[h:h + 1, :] - gb128, NEG))
            wcat.append((d * s_ref[rows, h * 128:(h + 1) * 128]).astype(BF16))
            vblocks.append(jnp.concatenate([jnp.where(hmask[h], v_all, jnp.zeros_like(v_all)), ones_blk[h]], axis=1))
            g = (h * hd) // 128
            srows.append(jnp.concatenate([cblk[h] if j == g else zblk for j in range(ngrp)]
                                         + [nblk[h] if j == g else zblk for j in range(ngrp)], axis=1))
        s_bf = jnp.concatenate(srows, axis=0).astype(BF16)
        out_aug = _dot(jnp.concatenate(wcat, axis=1), jnp.concatenate(vblocks, axis=0))
        out_aug = out_aug + _dot((qk_ref[rows, 0:w] * w_inter).astype(BF16), s_bf)
        hh_ref[rows, :] = out_aug[:, 0:w] / jnp.maximum(jnp.abs(out_aug[:, w:2 * w]), e_negm)
        kw = (qk_ref[rows, w:2 * w] * jnp.exp(r64_ref[rows, :] - glast_l)).astype(BF16)
        upd = _dot_tn(kw, jnp.concatenate([v_all, ones_cols], axis=1))
        for h in range(nh):
            rs = slice(h * hd, (h + 1) * hd)
            cblk[h] = cblk[h] * decay[h:h + 1, :] + jnp.where(half[h], upd[rs, grp[h]], 0.0)
            nblk[h] = (nblk[h] * decay[h:h + 1, :]
                       + jnp.where(half[h], upd[rs, w + grp[h].start:w + grp[h].stop], 0.0))
        m_prev = bl_ref[t8, :] + g_last
    m_ref[...] = m_prev
    for h in range(nh):
        st_ref[h * hd:(h + 1) * hd, grp[h]] = cblk[h]
        st_ref[h * hd:(h + 1) * hd, w + grp[h].start:w + grp[h].stop] = nblk[h]

    hh = hh_ref[...]
    ms = _dot((hh * hh).astype(BF16), bones_ref[...])
    og = zc_ref[:, 3 * w:4 * w].astype(F32)
    o_ref[...] = (_sigmoid(og) * hh * lax.rsqrt(ms + EPS) * ng_ref[...]).astype(o_ref.dtype)


def _mlstm(zc3, zgt, conv_w, gate_b_rows, norm_g):
    b, s, w = zc3.shape
    ts = SEQ_TILE
    nt = s // ts
    gw = GROUP_W
    t = jnp.arange(CHUNK)
    triu = (t[:, None] <= t[None, :]).astype(BF16)
    hid = jnp.arange(gw) // HEAD_DIM
    bones = jnp.where(hid[:, None] == hid[None, :], 1.0 / HEAD_DIM, 0.0).astype(BF16)
    full = lambda a: pl.BlockSpec(a.shape, lambda bi, i: (0,) * a.ndim)
    return pl.pallas_call(
        _mlstm_kernel,
        grid=(b, nt),
        in_specs=[pl.BlockSpec((None, ts, w), lambda bi, i: (bi, i, 0)),
                  pl.BlockSpec((8, ts), lambda bi, i: (0, bi * nt + i)),
                  full(conv_w), full(gate_b_rows), full(norm_g), full(triu), full(bones)],
        out_specs=pl.BlockSpec((None, ts, gw), lambda bi, i: (bi, i, 0)),
        out_shape=jax.ShapeDtypeStruct((b, s, gw), BF16),
        scratch_shapes=[pltpu.VMEM((gw, 2 * gw), F32),
                        pltpu.VMEM((8, 128), F32),
                        pltpu.VMEM((8, 2 * gw), F32),
                        pltpu.VMEM((ts, 2 * gw), F32),
                        pltpu.VMEM((ts // CHUNK * 8, 128), F32),
                        pltpu.VMEM((ts // CHUNK * 8, 128), F32),
                        pltpu.VMEM((ts // CHUNK * 8, 128), F32),
                        pltpu.VMEM((ts, MLSTM_HEADS * 128), F32),
                        pltpu.VMEM((ts, gw), F32),
                        pltpu.VMEM((ts, gw), F32),
                        pltpu.VMEM((ts, gw), F32),
                        pltpu.VMEM((ts, MLSTM_HEADS * 128), F32),
                        pltpu.VMEM((ts, gw), F32)],
        compiler_params=_cparams("parallel", "arbitrary"),
        name="mlstm",
    )(zc3, zgt, conv_w, gate_b_rows, norm_g, triu, bones)


def _sgu_kernel(zd_ref, lng_ref, lnb_ref, w_ref, bias_ref, o_ref):
    tm = zd_ref.shape[0]
    gw = GROUP_W // SGU_GROUPS
    u = _gelu(zd_ref[:, 0:GROUP_W].astype(F32))
    v = _gelu(zd_ref[:, GROUP_W:2 * GROUP_W].astype(F32))
    mu = jnp.mean(v, axis=-1, keepdims=True)
    vc = v - mu
    var = jnp.mean(vc * vc, axis=-1, keepdims=True)
    vn = (vc * lax.rsqrt(var + EPS) * lng_ref[...] + lnb_ref[...]).astype(BF16)
    for c in range(tm // CHUNK):
        rows = slice(c * CHUNK, (c + 1) * CHUNK)
        mixed = jnp.concatenate([_dot(w_ref[g], vn[rows, g * gw:(g + 1) * gw]) for g in range(SGU_GROUPS)], axis=-1)
        o_ref[rows, :] = (u[rows, :] * (mixed + bias_ref[...])).astype(o_ref.dtype)


def _sgu(zd, ln_g, ln_b, w_tril, bias):
    n = zd.shape[0]
    tm = TOKEN_TILE
    full = lambda a: pl.BlockSpec(a.shape, lambda i: (0,) * a.ndim)
    return pl.pallas_call(
        _sgu_kernel,
        grid=(n // tm,),
        in_specs=[pl.BlockSpec((tm, 2 * GROUP_W), lambda i: (i, 0)), full(ln_g), full(ln_b), full(w_tril), full(bias)],
        out_specs=pl.BlockSpec((tm, GROUP_W), lambda i: (i, 0)),
        out_shape=jax.ShapeDtypeStruct((n, GROUP_W), BF16),
        compiler_params=_cparams("parallel"),
        name="sgu",
    )(zd, ln_g, ln_b, w_tril, bias)


def _out_proj_kernel(x_ref, ya_ref, yb_ref, yc_ref, yd_ref, gluw_ref, glub_ref, wo_ref, ng_ref,
                     rwh_ref, rwl_ref, rb_ref,
                     x1_ref, hx_ref, meta_ref, cnt_ref, carry_ref):
    tm = x_ref.shape[0]

    @pl.when(pl.program_id(0) == 0)
    def _():
        carry_ref[...] = jnp.zeros_like(carry_ref)

    yb = _gelu(yb_ref[...].astype(F32))
    yb = yb * _sigmoid(_dot(yb.astype(BF16), gluw_ref[...]) + glub_ref[...])
    mix = _dot(ya_ref[...], wo_ref[0:GROUP_W, :])
    mix = mix + _dot(yb.astype(BF16), wo_ref[GROUP_W:2 * GROUP_W, :])
    mix = mix + _dot(yc_ref[...], wo_ref[2 * GROUP_W:3 * GROUP_W, :])
    mix = mix + _dot(yd_ref[...], wo_ref[3 * GROUP_W:4 * GROUP_W, :])
    x1 = x_ref[...] + mix
    x1_ref[...] = x1
    h = _rms(x1, ng_ref[...])
    hx_ref[:, 0:D_MODEL] = h

    hh, hl = _split_bf16(h)
    wh = rwh_ref[...]
    wl = rwl_ref[...]
    logits = _dot_nt(wh, hh) + _dot_nt(wh, hl) + _dot_nt(wl, hh) + rb_ref[...]
    gl = [logits[j:j + 1, :] for j in range(N_GROUPS)]
    gmax = functools.reduce(jnp.maximum, gl)
    gsel = jnp.full((1, tm), N_GROUPS - 1, I32)
    for j in range(N_GROUPS - 2, -1, -1):
        gsel = jnp.where(gl[j] == gmax, j, gsel)
    p_g = 1.0 / functools.reduce(jnp.add, [jnp.exp(v - gmax) for v in gl])
    e_in = []
    for i in range(EPG):
        v = logits[N_GROUPS + i:N_GROUPS + i + 1, :]
        for j in range(1, N_GROUPS):
            r = N_GROUPS + j * EPG + i
            v = jnp.where(gsel == j, logits[r:r + 1, :], v)
        e_in.append(v)
    v1 = functools.reduce(jnp.maximum, e_in)
    i1 = jnp.full((1, tm), EPG - 1, I32)
    for i in range(EPG - 2, -1, -1):
        i1 = jnp.where(e_in[i] == v1, i, i1)
    rest = [jnp.where(i1 == i, NEG, e_in[i]) for i in range(EPG)]
    v2 = functools.reduce(jnp.maximum, rest)
    i2 = jnp.full((1, tm), EPG - 1, I32)
    for i in range(EPG - 2, -1, -1):
        i2 = jnp.where((rest[i] == v2) & (i1 != i), i, i2)
    e2 = jnp.exp(v2 - v1)
    w1 = p_g / (1.0 + e2)
    w2 = p_g * e2 / (1.0 + e2)
    lo = jnp.minimum(i1, i2)
    hi = jnp.maximum(i1, i2)
    w_lo = jnp.where(i1 < i2, w1, w2)
    w_hi = jnp.where(i1 < i2, w2, w1)
    bucket = gsel * PAIRS + ((lo * (2 * EPG - 1 - lo)) >> 1) + (hi - lo - 1)

    kid = lax.broadcasted_iota(I32, (BUCKET_PAD, tm), 0)
    onehot = (kid == bucket).astype(F32)
    s_i = lax.broadcasted_iota(I32, (tm, tm), 0)
    t_i = lax.broadcasted_iota(I32, (tm, tm), 1)
    prefix = _dot(onehot.astype(BF16), (s_i <= t_i).astype(BF16))
    carry = carry_ref[...]
    rank = jnp.sum(onehot * (prefix - 1.0 + carry[:, 0:1]), axis=0, keepdims=True)
    carry = carry + prefix[:, tm - 1:tm]
    carry_ref[...] = carry
    cnt_ref[...] = carry
    meta_ref[...] = jnp.concatenate([bucket, rank.astype(I32), jnp.zeros((6, tm), I32)], axis=0)
    wrows = jnp.concatenate([w_lo, w_hi, jnp.zeros((126, tm), F32)], axis=0)
    for c in range(tm // 128):
        hx_ref[c * 128:(c + 1) * 128, D_MODEL:ROW_W] = wrows[:, c * 128:(c + 1) * 128].T


def _out_proj(x2, ya, yb, yc, yd, glu_w, glu_b, w_out, norm_g, rw_hi, rw_lo, rb):
    n = x2.shape[0]
    tm = TOKEN_TILE
    row = lambda w: pl.BlockSpec((tm, w), lambda i: (i, 0))
    full = lambda a: pl.BlockSpec(a.shape, lambda i: (0,) * a.ndim)
    return pl.pallas_call(
        _out_proj_kernel,
        grid=(n // tm,),
        in_specs=[row(D_MODEL), row(GROUP_W), row(GROUP_W), row(GROUP_W), row(GROUP_W),
                  full(glu_w), full(glu_b), full(w_out), full(norm_g), full(rw_hi), full(rw_lo), full(rb)],
        out_specs=[row(D_MODEL), row(ROW_W), pl.BlockSpec((8, tm), lambda i: (0, i)),
                   pl.BlockSpec((BUCKET_PAD, 128), lambda i: (0, 0))],
        out_shape=[jax.ShapeDtypeStruct((n, D_MODEL), F32), jax.ShapeDtypeStruct((n, ROW_W), F32),
                   jax.ShapeDtypeStruct((8, n), I32), jax.ShapeDtypeStruct((BUCKET_PAD, 128), F32)],
        scratch_shapes=[pltpu.VMEM((BUCKET_PAD, 128), F32)],
        compiler_params=_cparams("arbitrary"),
        name="out_proj",
    )(x2, ya, yb, yc, yd, glu_w, glu_b, w_out, norm_g, rw_hi, rw_lo, rb)


def _row_copy(src_ref, src_row, dst_ref, dst_row, sem):
    return pltpu.make_async_copy(src_ref.at[pl.ds(src_row, 1)], dst_ref.at[pl.ds(dst_row, 1)], sem)


def _dispatch_kernel(dest_ref, hx_ref, xs_in_ref, xs_ref, sem):
    del xs_in_ref
    tm = hx_ref.shape[0]

    def start(t, c):
        _row_copy(hx_ref, t, xs_ref, dest_ref[0, 0, t], sem).start()
        return c

    lax.fori_loop(0, tm, start, 0, unroll=ROW_DMA_UNROLL)
    pltpu.make_async_copy(hx_ref, xs_ref.at[pl.ds(0, tm)], sem).wait()


def _dispatch(dest3, hx, p_tot):
    n = hx.shape[0]
    tm = TOKEN_TILE
    xs0 = jnp.zeros((p_tot, ROW_W), F32)
    return pl.pallas_call(
        _dispatch_kernel,
        grid=(n // tm,),
        in_specs=[pl.BlockSpec((1, 1, tm), lambda i: (i, 0, 0), memory_space=pltpu.SMEM),
                  pl.BlockSpec((tm, ROW_W), lambda i: (i, 0)),
                  pl.BlockSpec(memory_space=pl.ANY)],
        out_specs=pl.BlockSpec(memory_space=pl.ANY),
        out_shape=jax.ShapeDtypeStruct((p_tot, ROW_W), F32),
        input_output_aliases={2: 0},
        scratch_shapes=[pltpu.SemaphoreType.DMA],
        compiler_params=_cparams("arbitrary"),
        name="dispatch",
    )(dest3, hx, xs0)


def _ffn_kernel(ea_ref, eb_ref, valid_ref, xs_ref, wg_a_ref, wu_a_ref, wd_a_ref, wg_b_ref, wu_b_ref, wd_b_ref,
                ys_ref, wgu_a, wdn_a, wgu_b, wdn_b):
    j = pl.program_id(0)
    valid = valid_ref[j]
    jp = jnp.maximum(j - 1, 0)

    @pl.when((j == 0) | (ea_ref[j] != ea_ref[jp]))
    def _():
        wgu_a[:, 0:D_EXPERT] = wg_a_ref[...].astype(BF16)
        wgu_a[:, D_EXPERT:2 * D_EXPERT] = wu_a_ref[...].astype(BF16)
        wdn_a[...] = wd_a_ref[...].astype(BF16)

    @pl.when((j == 0) | (eb_ref[j] != eb_ref[jp]))
    def _():
        wgu_b[:, 0:D_EXPERT] = wg_b_ref[...].astype(BF16)
        wgu_b[:, D_EXPERT:2 * D_EXPERT] = wu_b_ref[...].astype(BF16)
        wdn_b[...] = wd_b_ref[...].astype(BF16)

    for sb in range(FFN_BLOCK // FFN_SUB):
        rows = slice(sb * FFN_SUB, (sb + 1) * FFN_SUB)

        @pl.when(valid <= sb * FFN_SUB)
        def _():
            ys_ref[rows, :] = jnp.zeros((FFN_SUB, D_MODEL), F32)

        @pl.when(valid > sb * FFN_SUB)
        def _():
            xb = xs_ref[rows, 0:D_MODEL].astype(BF16)

            def expert(wgu, wdn):
                gu = _dot(xb, wgu[...])
                g = gu[:, 0:D_EXPERT]
                act = (g * _sigmoid(g) * gu[:, D_EXPERT:2 * D_EXPERT]).astype(BF16)
                return _dot(act, wdn[...])

            ya = expert(wgu_a, wdn_a)
            yb = expert(wgu_b, wdn_b)
            ys_ref[rows, :] = (ya * xs_ref[rows, D_MODEL:D_MODEL + 1]
                               + yb * xs_ref[rows, D_MODEL + 1:D_MODEL + 2])


def _ffn(ea, eb, valid, xs, layer, w_gate, w_up, w_down):
    p_tot = xs.shape[0]
    bm = FFN_BLOCK
    nblk = p_tot // bm
    wspec = lambda e_idx, shape: pl.BlockSpec((None, None) + shape,
                                              lambda j, ea, eb, na: (layer, (ea, eb)[e_idx][j], 0, 0))
    up_shape = (D_MODEL, D_EXPERT)
    dn_shape = (D_EXPERT, D_MODEL)
    return pl.pallas_call(
        _ffn_kernel,
        grid_spec=pltpu.PrefetchScalarGridSpec(
            num_scalar_prefetch=3,
            grid=(nblk,),
            in_specs=[pl.BlockSpec((bm, ROW_W), lambda j, ea, eb, na: (j, 0)),
                      wspec(0, up_shape), wspec(0, up_shape), wspec(0, dn_shape),
                      wspec(1, up_shape), wspec(1, up_shape), wspec(1, dn_shape)],
            out_specs=pl.BlockSpec((bm, D_MODEL), lambda j, ea, eb, na: (j, 0)),
            scratch_shapes=[pltpu.VMEM((D_MODEL, 2 * D_EXPERT), BF16), pltpu.VMEM(dn_shape, BF16),
                            pltpu.VMEM((D_MODEL, 2 * D_EXPERT), BF16), pltpu.VMEM(dn_shape, BF16)],
        ),
        out_shape=jax.ShapeDtypeStruct((p_tot, D_MODEL), F32),
        compiler_params=_cparams("arbitrary"),
        name="ffn",
    )(ea, eb, valid, xs, w_gate, w_up, w_down, w_gate, w_up, w_down)


def _combine_kernel(dest_ref, x1_ref, ys_ref, ng_ref, o_ref, buf_ref, sem, *, final_norm):
    tm = x1_ref.shape[0]

    def start(t, c):
        _row_copy(ys_ref, dest_ref[0, 0, t], buf_ref, t, sem).start()
        return c

    lax.fori_loop(0, tm, start, 0, unroll=ROW_DMA_UNROLL)
    pltpu.make_async_copy(ys_ref.at[pl.ds(0, tm)], buf_ref, sem).wait()
    x2 = x1_ref[...] + buf_ref[...]
    o_ref[...] = _rms(x2, ng_ref[...]) if final_norm else x2


def _combine(dest3, x1, ys, norm_g, final_norm):
    n = x1.shape[0]
    tm = TOKEN_TILE
    return pl.pallas_call(
        functools.partial(_combine_kernel, final_norm=final_norm),
        grid=(n // tm,),
        in_specs=[pl.BlockSpec((1, 1, tm), lambda i: (i, 0, 0), memory_space=pltpu.SMEM),
                  pl.BlockSpec((tm, D_MODEL), lambda i: (i, 0)),
                  pl.BlockSpec(memory_space=pl.ANY),
                  pl.BlockSpec((1, D_MODEL), lambda i: (0, 0))],
        out_specs=pl.BlockSpec((tm, D_MODEL), lambda i: (i, 0)),
        out_shape=jax.ShapeDtypeStruct((n, D_MODEL), F32),
        scratch_shapes=[pltpu.VMEM((tm, D_MODEL), F32), pltpu.SemaphoreType.DMA],
        compiler_params=_cparams("arbitrary"),
        name="combine",
    )(dest3, x1, ys, norm_g)


def _routing_tables(meta, counts, n):
    bm = FFN_BLOCK
    nblk = n // bm + N_BUCKETS
    bucket = meta[0]
    rank = meta[1]
    cnt = counts[:N_BUCKETS, 0].astype(I32)
    padded = ((cnt + bm - 1) // bm) * bm
    pad_end = jnp.cumsum(padded)
    pad_start = pad_end - padded
    onehot = bucket[:, None] == jnp.arange(N_BUCKETS, dtype=I32)[None, :]
    dest = rank + jnp.sum(jnp.where(onehot, pad_start[None, :], 0), axis=1)
    nact = (pad_end[-1] // bm).astype(I32)
    j = jnp.arange(nblk, dtype=I32)
    blk = jnp.minimum(j, jnp.maximum(nact - 1, 0))
    pos = blk * bm
    bkt = jnp.minimum(jnp.sum((pad_end[None, :] <= pos[:, None]).astype(I32), axis=1), N_BUCKETS - 1)
    in_bkt = bkt[:, None] == jnp.arange(N_BUCKETS, dtype=I32)[None, :]
    sel = lambda tab: jnp.sum(jnp.where(in_bkt, tab[None, :], 0), axis=1)
    valid = jnp.where(j < nact, jnp.clip(sel(cnt) - (pos - sel(pad_start)), 0, bm), 0)
    lo_tab, hi_tab = [], []
    for g in range(N_GROUPS):
        for a in range(EPG):
            for b in range(a + 1, EPG):
                lo_tab.append(g * EPG + a)
                hi_tab.append(g * EPG + b)
    ea = sel(jnp.asarray(lo_tab, I32))
    eb = sel(jnp.asarray(hi_tab, I32))
    return dest.astype(I32), ea.astype(I32), eb.astype(I32), valid.astype(I32), nblk * bm


def _layer(x2, b, s, p, final_g):
    n = b * s
    za, zb, zc, zd, zgt = _in_proj(x2, p['norm_mix_g'], p['wa'], p['wb'], p['wc'], p['wd'], p['wgt'])
    ya = _attention(za.reshape(b, s, -1), p['sinks']).reshape(n, GROUP_W)
    yb = _ssm(zb.reshape(b, s, GROUP_W), p['ssm']).reshape(n, GROUP_W)
    yc = _mlstm(zc.reshape(b, s, -1), zgt, p['conv_w'], p['gate_b'], p['mlstm_norm_g']).reshape(n, GROUP_W)
    yd = _sgu(zd, p['sgu_ln_g'], p['sgu_ln_b'], p['sgu_w'], p['sgu_bias'])
    x1, hx, meta, counts = _out_proj(x2, ya, yb, yc, yd, p['glu_w'], p['glu_b'], p['w_out'], p['norm_ffn_g'],
                                     p['rw_hi'], p['rw_lo'], p['rb'])
    dest, ea, eb, valid, p_tot = _routing_tables(meta, counts, n)
    dest3 = dest.reshape(n // TOKEN_TILE, 1, TOKEN_TILE)
    xs = _dispatch(dest3, hx, p_tot)
    ys = _ffn(ea, eb, valid, xs, p['layer'], p['w_gate'], p['w_up'], p['w_down'])
    if final_g is None:
        return _combine(dest3, x1, ys, p['norm_ffn_g'], False)
    return _combine(dest3, x1, ys, final_g, True)


def _prep_layer(l, norm_mix_g, w_in, attn_sinks, ssm_a_re, ssm_a_im, ssm_b_re, ssm_b_im, ssm_c_re, ssm_c_im,
                ssm_d, ssm_log_dt, ssm_glu_w, ssm_glu_b, mlstm_conv_w, mlstm_gate_b, mlstm_norm_g,
                sgu_ln_g, sgu_ln_b, sgu_w, sgu_b, w_out, norm_ffn_g, router_group_w, router_group_b,
                router_expert_w, router_expert_b, expert_w_gate, expert_w_up, expert_w_down):
    w = w_in[l]
    o_su = 2 * GROUP_W
    o_c = o_su + GROUP_W
    o_ci = o_c + 3 * GROUP_W
    o_co = o_ci + 2 * MLSTM_HEADS
    o_d = o_co + GROUP_W
    wc = jnp.concatenate([w[:, o_c:o_ci], w[:, o_co:o_d]], axis=1)
    wgt = w[:, o_ci:o_co].T
    rw = jnp.concatenate([router_group_w[l], router_expert_w[l]], axis=1).T.astype(F32)
    rw = jnp.pad(rw, ((0, 4), (0, 0)))
    rw_hi = rw.astype(BF16)
    rw_lo = (rw - rw_hi.astype(F32)).astype(BF16)
    rb = jnp.pad(jnp.concatenate([router_group_b[l], router_expert_b[l]]).astype(F32), (0, 4))[:, None]
    tril = jnp.tril(jnp.ones((CHUNK, CHUNK), F32))
    gw = GROUP_W // SGU_GROUPS
    return dict(
        norm_mix_g=norm_mix_g[l][None, :].astype(F32),
        wa=w[:, 0:o_su].astype(BF16), wb=w[:, o_su:o_c].astype(BF16), wc=wc.astype(BF16),
        wd=w[:, o_d:].astype(BF16), wgt=wgt.astype(BF16),
        sinks=attn_sinks[l].astype(F32),
        ssm=_ssm_matrices(ssm_a_re[l], ssm_a_im[l], ssm_b_re[l], ssm_b_im[l], ssm_c_re[l], ssm_c_im[l],
                          ssm_d[l], ssm_log_dt[l]),
        glu_w=ssm_glu_w[l].astype(BF16), glu_b=ssm_glu_b[l][None, :].astype(F32),
        conv_w=mlstm_conv_w[l].astype(F32),
        gate_b=jnp.broadcast_to(mlstm_gate_b[l].astype(F32)[:, None], (2 * MLSTM_HEADS, 128)),
        mlstm_norm_g=mlstm_norm_g[l][None, :].astype(F32),
        sgu_ln_g=sgu_ln_g[l][None, :].astype(F32), sgu_ln_b=sgu_ln_b[l][None, :].astype(F32),
        sgu_w=(sgu_w[l].astype(F32) * tril).astype(BF16),
        sgu_bias=jnp.repeat(sgu_b[l].astype(F32).T, gw, axis=1),
        w_out=w_out[l].astype(BF16), norm_ffn_g=norm_ffn_g[l][None, :].astype(F32),
        rw_hi=rw_hi, rw_lo=rw_lo, rb=rb,
        layer=l, w_gate=expert_w_gate, w_up=expert_w_up, w_down=expert_w_down,
    )


def kernel(x, norm_mix_g, w_in, attn_sinks, ssm_a_re, ssm_a_im, ssm_b_re, ssm_b_im, ssm_c_re, ssm_c_im, ssm_d, ssm_log_dt, ssm_glu_w, ssm_glu_b, mlstm_conv_w, mlstm_gate_b, mlstm_norm_g, sgu_ln_g, sgu_ln_b, sgu_w, sgu_b, w_out, norm_ffn_g, router_group_w, router_group_b, router_expert_w, router_expert_b, expert_w_gate, expert_w_up, expert_w_down, norm_final_g):
    b, s, d = x.shape
    depth = w_in.shape[0]
    x2 = x.reshape(b * s, d).astype(F32)
    for l in range(depth):
        p = _prep_layer(l, norm_mix_g, w_in, attn_sinks, ssm_a_re, ssm_a_im, ssm_b_re, ssm_b_im, ssm_c_re,
                        ssm_c_im, ssm_d, ssm_log_dt, ssm_glu_w, ssm_glu_b, mlstm_conv_w, mlstm_gate_b,
                        mlstm_norm_g, sgu_ln_g, sgu_ln_b, sgu_w, sgu_b, w_out, norm_ffn_g, router_group_w,
                        router_group_b, router_expert_w, router_expert_b, expert_w_gate, expert_w_up,
                        expert_w_down)
        final_g = norm_final_g[None, :].astype(F32) if l == depth - 1 else None
        x2 = _layer(x2, b, s, p, final_g)
    return x2.reshape(b, s, d).astype(x.dtype)
```

```python
import functools
import math

import jax
import jax.numpy as jnp
from jax import lax
from jax.experimental import pallas as pl
from jax.experimental.pallas import tpu as pltpu

F32 = jnp.float32
BF16 = jnp.bfloat16
I32 = jnp.int32

D_MODEL = 1024
GROUP_W = 256
HEAD_DIM = 64
EPS = 1e-6
NEG = -1e30
WINDOW = 128
KV_HEADS = 2
SSM_GROUP = 16
SSM_GROUPS = 16
SSM_STATE = 64
SSM_CHUNK = 16
MLSTM_HEADS = 4
CHUNK = 128
CONV_K = 4
SGU_GROUPS = 4
N_GROUPS = 4
EPG = 8
N_EXPERTS = 32
D_EXPERT = 512
PAIRS = EPG * (EPG - 1) // 2
N_BUCKETS = N_GROUPS * PAIRS
BUCKET_PAD = 128
ROW_W = D_MODEL + 128

TOKEN_TILE = 512
SEQ_TILE = 512
SSM_TILE = 4096
SSM_SLAB = 64
FFN_BLOCK = 512
FFN_SUB = 256
ROW_DMA_UNROLL = 8
VMEM_LIMIT = 56 * 1024 * 1024


def _cparams(*sem):
    return pltpu.CompilerParams(dimension_semantics=sem, vmem_limit_bytes=VMEM_LIMIT)


def _rms(x, g):
    return x * lax.rsqrt(jnp.mean(x * x, axis=-1, keepdims=True) + EPS) * g


def _gelu(x):
    return 0.5 * x * (1.0 + jnp.tanh(math.sqrt(2.0 / math.pi) * (x + 0.044715 * (x * x * x))))


def _sigmoid(x):
    return 1.0 / (1.0 + jnp.exp(-x))


def _log_sigmoid(x):
    return jnp.minimum(x, 0.0) - jnp.log(1.0 + jnp.exp(-jnp.abs(x)))


def _dot(a, b):
    return jnp.dot(a, b, preferred_element_type=F32)


def _dot_nt(a, b):
    return lax.dot_general(a, b, (((1,), (1,)), ((), ())), preferred_element_type=F32)


def _dot_tn(a, b):
    return lax.dot_general(a, b, (((0,), (0,)), ((), ())), preferred_element_type=F32)


def _split_bf16(x):
    hi = x.astype(BF16)
    lo = (x - hi.astype(F32)).astype(BF16)
    return hi, lo


def _in_proj_kernel(x_ref, g_ref, wa_ref, wb_ref, wc_ref, wd_ref, wgt_ref,
                    za_ref, zb_ref, zc_ref, zd_ref, zgt_ref):
    hb = _rms(x_ref[...], g_ref[...]).astype(BF16)
    za_ref[...] = _dot(hb, wa_ref[...]).astype(BF16)
    zb_ref[...] = _dot(hb, wb_ref[...]).astype(BF16)
    zc_ref[...] = _dot(hb, wc_ref[...]).astype(BF16)
    zd_ref[...] = _dot(hb, wd_ref[...]).astype(BF16)
    zgt_ref[...] = _dot_nt(wgt_ref[...], hb)


def _in_proj(x2, g, wa, wb, wc, wd, wgt):
    n = x2.shape[0]
    tm = TOKEN_TILE
    row = lambda w: pl.BlockSpec((tm, w), lambda i: (i, 0))
    full = lambda a: pl.BlockSpec(a.shape, lambda i: (0,) * a.ndim)
    widths = (wa.shape[1], wb.shape[1], wc.shape[1], wd.shape[1])
    return pl.pallas_call(
        _in_proj_kernel,
        grid=(n // tm,),
        in_specs=[row(D_MODEL), full(g), full(wa), full(wb), full(wc), full(wd), full(wgt)],
        out_specs=[row(w) for w in widths] + [pl.BlockSpec((wgt.shape[0], tm), lambda i: (0, i))],
        out_shape=[jax.ShapeDtypeStruct((n, w), BF16) for w in widths]
        + [jax.ShapeDtypeStruct((wgt.shape[0], n), F32)],
        compiler_params=_cparams("parallel"),
        name="in_proj",
    )(x2, g, wa, wb, wc, wd, wgt)


def _attn_kernel(sink_ref, cur_ref, prev_ref, o_ref):
    first = pl.program_id(1) == 0
    nblk = cur_ref.shape[0] // WINDOW
    row = lax.broadcasted_iota(I32, (2 * WINDOW, 2 * WINDOW), 0)
    col = lax.broadcasted_iota(I32, (2 * WINDOW, 2 * WINDOW), 1)
    qi = jnp.where(row >= WINDOW, row - WINDOW, row)
    band = (col <= qi + WINDOW) & (col > qi)
    row1 = lax.broadcasted_iota(I32, (2 * WINDOW, 1), 0)
    for j in range(nblk):
        cur = cur_ref[j * WINDOW:(j + 1) * WINDOW, :]
        if j == 0:
            prev = prev_ref[...]
            mask = band & ((col >= WINDOW) | jnp.logical_not(first))
        else:
            prev = cur_ref[(j - 1) * WINDOW:j * WINDOW, :]
            mask = band
        outs = []
        for g in range(KV_HEADS):
            q2 = jnp.concatenate([cur[:, (2 * g) * HEAD_DIM:(2 * g + 1) * HEAD_DIM],
                                  cur[:, (2 * g + 1) * HEAD_DIM:(2 * g + 2) * HEAD_DIM]], axis=0)
            ko = GROUP_W + g * HEAD_DIM
            vo = GROUP_W + KV_HEADS * HEAD_DIM + g * HEAD_DIM
            kk = jnp.concatenate([prev[:, ko:ko + HEAD_DIM], cur[:, ko:ko + HEAD_DIM]], axis=0)
            vv = jnp.concatenate([prev[:, vo:vo + HEAD_DIM], cur[:, vo:vo + HEAD_DIM]], axis=0)
            s = _dot_nt(q2, kk) * (HEAD_DIM ** -0.5)
            s = jnp.where(mask, s, NEG)
            sink = jnp.where(row1 < WINDOW, sink_ref[2 * g], sink_ref[2 * g + 1])
            m = jnp.maximum(jnp.max(s, axis=-1, keepdims=True), sink)
            p = jnp.exp(s - m)
            l = jnp.sum(p, axis=-1, keepdims=True) + jnp.exp(sink - m)
            o = _dot(p.astype(BF16), vv) / l
            outs.append(o[:WINDOW])
            outs.append(o[WINDOW:])
        o_ref[j * WINDOW:(j + 1) * WINDOW, :] = jnp.concatenate(outs, axis=-1).astype(o_ref.dtype)


def _attention(za3, sinks):
    b, s, w = za3.shape
    ts = SEQ_TILE
    per = ts // WINDOW
    return pl.pallas_call(
        _attn_kernel,
        grid=(b, s // ts),
        in_specs=[pl.BlockSpec(memory_space=pltpu.SMEM),
                  pl.BlockSpec((None, ts, w), lambda bi, i: (bi, i, 0)),
                  pl.BlockSpec((None, WINDOW, w), lambda bi, i: (bi, jnp.maximum(i * per - 1, 0), 0))],
        out_specs=pl.BlockSpec((None, ts, GROUP_W), lambda bi, i: (bi, i, 0)),
        out_shape=jax.ShapeDtypeStruct((b, s, GROUP_W), BF16),
        compiler_params=_cparams("parallel", "parallel"),
        name="attn",
    )(sinks, za3, za3)


def _ssm_kernel(zb_ref, bre_ref, bim_ref, t_ref, cre_ref, cim_ref, are_ref, aim_ref, o_ref,
                x_ref, xs_ref, u_ref, vre_ref, vim_ref, sre_ref, sim_ref, y_ref, st_ref):
    ts = zb_ref.shape[0]
    L, G, H = SSM_CHUNK, SSM_GROUPS, SSM_GROUP
    nch = ts // L
    half_g = 128 // H
    n_half = G // half_g

    @pl.when(pl.program_id(1) == 0)
    def _():
        st_ref[...] = jnp.zeros_like(st_ref)

    blk = lax.broadcasted_iota(I32, (1, 128), 1) // H

    def block_transpose(arrs):
        a = list(arrs)
        s = half_g // 2
        while s >= 1:
            keep = (blk & s) == 0
            for i in range(half_g):
                if i & s == 0:
                    ai, aj = a[i], a[i + s]
                    a[i] = jnp.where(keep, ai, pltpu.roll(aj, s * H, axis=1))
                    a[i + s] = jnp.where(keep, pltpu.roll(ai, 128 - s * H, axis=1), aj)
            s //= 2
        return a

    for hf in range(n_half):
        x_ref[hf] = zb_ref[:, hf * 128:(hf + 1) * 128].astype(F32)
    for sg in range(L):
        for hf in range(n_half):
            xs_ref[sg * n_half + hf] = x_ref[hf, pl.ds(sg, nch, stride=L), :]
    for hf in range(n_half):
        for oc in range(L // half_g):
            for r0 in range(0, nch, SSM_SLAB):
                rows = slice(r0, r0 + SSM_SLAB)
                t = block_transpose([xs_ref[(oc * half_g + k) * n_half + hf, rows, :] for k in range(half_g)])
                for gl in range(half_g):
                    u_ref[hf * half_g + gl, rows, oc * 128:(oc + 1) * 128] = t[gl].astype(BF16)
    for g in range(G):
        ug = u_ref[g]
        vre_ref[pl.ds(g, nch, stride=G), :] = _dot(ug, bre_ref[g])
        vim_ref[pl.ds(g, nch, stride=G), :] = _dot(ug, bim_ref[g])

    are = are_ref[...]
    aim = aim_ref[...]

    def step(c, carry):
        sre, sim = carry
        r0 = pl.multiple_of(c * G, G)
        sre_ref[pl.ds(r0, G), :] = sre
        sim_ref[pl.ds(r0, G), :] = sim
        vre = vre_ref[pl.ds(r0, G), :]
        vim = vim_ref[pl.ds(r0, G), :]
        return (are * sre - aim * sim + vre, are * sim + aim * sre + vim)

    sre, sim = lax.fori_loop(0, nch, step, (st_ref[0:G, :], st_ref[G:2 * G, :]), unroll=4)
    st_ref[0:G, :] = sre
    st_ref[G:2 * G, :] = sim

    for g in range(G):
        y = _dot(u_ref[g], t_ref[g])
        y = y + _dot(sre_ref[pl.ds(g, nch, stride=G), :].astype(BF16), cre_ref[g])
        y = y + _dot(sim_ref[pl.ds(g, nch, stride=G), :].astype(BF16), cim_ref[g])
        y_ref[g] = y
    for hf in range(n_half):
        for oc in range(L // half_g):
            for r0 in range(0, nch, SSM_SLAB):
                t = block_transpose([y_ref[hf * half_g + gl, r0:r0 + SSM_SLAB, oc * 128:(oc + 1) * 128]
                                     for gl in range(half_g)])
                for k in range(half_g):
                    x_ref[hf, pl.ds(oc * half_g + k + L * r0, SSM_SLAB, stride=L), :] = t[k]
    for hf in range(n_half):
        o_ref[:, hf * 128:(hf + 1) * 128] = x_ref[hf].astype(o_ref.dtype)


def _ssm_matrices(a_re, a_im, b_re, b_im, c_re, c_im, d_skip, log_dt):
    L = SSM_CHUNK
    a = lax.complex(a_re.astype(F32), a_im.astype(F32))
    dt = jnp.exp(log_dt.astype(F32))[:, None]
    adt = a * dt
    a_bar = jnp.exp(adt)
    b_bar = ((a_bar - 1.0) / a)[..., None] * lax.complex(b_re.astype(F32), b_im.astype(F32))
    c_mat = lax.complex(c_re.astype(F32), c_im.astype(F32))
    lag = jnp.arange(L + 1, dtype=F32)
    pw = jnp.exp(adt[None] * lag[:, None, None])
    kern = jnp.einsum('gop,dgp,gpi->dgoi', c_mat, pw[:L], b_bar).real
    sig = jnp.arange(L)[:, None]
    tau = jnp.arange(L)[None, :]
    d = tau - sig
    kt = kern[jnp.clip(d, 0, L - 1)]
    kt = jnp.where((d >= 0)[:, :, None, None, None], kt, 0.0)
    eye = jnp.eye(SSM_GROUP, dtype=F32)
    dsk = d_skip.astype(F32).reshape(SSM_GROUPS, SSM_GROUP)
    kt = kt + (d == 0)[:, :, None, None, None] * (dsk[:, :, None] * eye)[None, None]
    t_mat = kt.transpose(2, 0, 4, 1, 3).reshape(SSM_GROUPS, L * SSM_GROUP, L * SSM_GROUP)
    bm = pw[:L][::-1][:, :, :, None] * b_bar[None]
    bm = bm.transpose(1, 0, 3, 2).reshape(SSM_GROUPS, L * SSM_GROUP, SSM_STATE)
    cm = c_mat[None] * pw[1:][:, :, None, :]
    cm = cm.transpose(1, 3, 0, 2).reshape(SSM_GROUPS, SSM_STATE, L * SSM_GROUP)
    a_chunk = pw[L]
    pad = 128 - SSM_STATE
    pc = lambda m: jnp.pad(m, ((0, 0), (0, 0), (0, pad))).astype(BF16)
    pr = lambda m: jnp.pad(m, ((0, 0), (0, pad), (0, 0))).astype(BF16)
    pa = lambda m: jnp.pad(m, ((0, 0), (0, pad)))
    return (pc(bm.real), pc(bm.imag), t_mat.astype(BF16), pr(cm.real), pr(-cm.imag),
            pa(a_chunk.real), pa(a_chunk.imag))


def _ssm(zb3, mats):
    b, s, w = zb3.shape
    L, G = SSM_CHUNK, SSM_GROUPS
    ts = min(SSM_TILE, s)
    nch = ts // L
    full = lambda a: pl.BlockSpec(a.shape, lambda bi, i: (0,) * a.ndim)
    return pl.pallas_call(
        _ssm_kernel,
        grid=(b, s // ts),
        in_specs=[pl.BlockSpec((None, ts, w), lambda bi, i: (bi, i, 0))] + [full(m) for m in mats],
        out_specs=pl.BlockSpec((None, ts, w), lambda bi, i: (bi, i, 0)),
        out_shape=jax.ShapeDtypeStruct((b, s, w), BF16),
        scratch_shapes=[pltpu.VMEM((w // 128, ts, 128), F32),
                        pltpu.VMEM((L * (w // 128), nch, 128), F32),
                        pltpu.VMEM((G, nch, L * SSM_GROUP), BF16),
                        pltpu.VMEM((nch * G, 128), F32), pltpu.VMEM((nch * G, 128), F32),
                        pltpu.VMEM((nch * G, 128), F32), pltpu.VMEM((nch * G, 128), F32),
                        pltpu.VMEM((G, nch, L * SSM_GROUP), F32),
                        pltpu.VMEM((2 * G, 128), F32)],
        compiler_params=_cparams("parallel", "arbitrary"),
        name="ssm",
    )(zb3, *mats)


def _mlstm_kernel(zc_ref, zgt_ref, convw_ref, gb_ref, ng_ref, triu_ref, bones_ref, o_ref,
                  st_ref, m_ref, tail_ref, qk_ref,
                  rt_ref, bl_ref, rm_ref, cr128_ref, cr64_ref, b64_ref, r64_ref, s_ref, hh_ref):
    ts = zc_ref.shape[0]
    hd, nh, w = HEAD_DIM, MLSTM_HEADS, GROUP_W
    nchunk = ts // CHUNK
    nr = nchunk * 8
    lane1 = lax.broadcasted_iota(I32, (1, 128), 1)
    half = [(lane1 // hd) == (h % 2) for h in range(nh)]
    grp = [slice((h * hd) // 128 * 128, (h * hd) // 128 * 128 + 128) for h in range(nh)]

    @pl.when(pl.program_id(1) == 0)
    def _():
        st_ref[...] = jnp.zeros_like(st_ref)
        m_ref[...] = jnp.zeros_like(m_ref)
        tail_ref[...] = jnp.zeros_like(tail_ref)

    r_i = lax.broadcasted_iota(I32, (CHUNK, CHUNK), 0)
    c_i = lax.broadcasted_iota(I32, (CHUNK, CHUNK), 1)
    causal = c_i <= r_i
    k64 = lax.broadcasted_iota(I32, (8, w), 0)
    j64 = lax.broadcasted_iota(I32, (8, w), 1)
    sel64 = (j64 // hd == k64 % nh).astype(BF16)
    k128 = lax.broadcasted_iota(I32, (8, nh * 128), 0)
    j128 = lax.broadcasted_iota(I32, (8, nh * 128), 1)
    sel128 = (j128 // 128 == k128 % nh).astype(BF16)

    def lanes(t):
        lo = jnp.where(lane1 < hd, t[0:1, :], t[1:2, :])
        hi = jnp.where(lane1 < hd, t[2:3, :], t[3:4, :])
        return jnp.concatenate([lo, hi], axis=1)

    rowm = lax.broadcasted_iota(I32, (nr, 128), 0) % 8
    lanem = lax.broadcasted_iota(I32, (nr, 128), 1)
    graw = jnp.concatenate([zgt_ref[:, c * CHUNK:(c + 1) * CHUNK] + gb_ref[...] for c in range(nchunk)], axis=0)
    g2 = jnp.where(rowm < nh, graw, _log_sigmoid(graw))
    ghi, glo = _split_bf16(g2)
    cum = _dot(ghi, triu_ref[...]) + _dot(glo, triu_ref[...])
    b_t = pltpu.roll(cum, nr - 4, axis=0)
    r_t = g2 - b_t
    cr = r_t
    sh = 1
    while sh < CHUNK:
        cr = jnp.maximum(cr, jnp.where(lanem >= sh, pltpu.roll(cr, sh, axis=1), NEG))
        sh *= 2
    rt_ref[...] = r_t
    bl_ref[...] = jnp.broadcast_to(b_t[:, CHUNK - 1:CHUNK], (nr, 128))
    rm_ref[...] = jnp.broadcast_to(cr[:, CHUNK - 1:CHUNK], (nr, 128))

    x = zc_ref[:, 0:2 * w].astype(F32)
    xe = jnp.concatenate([tail_ref[...], x], axis=0)
    tail_ref[...] = x[ts - 8:, :]
    cw = convw_ref[...]
    acc = x * cw[CONV_K - 1:CONV_K, :]
    for sft in range(1, CONV_K):
        acc = acc + xe[8 - sft:8 - sft + ts, :] * cw[CONV_K - 1 - sft:CONV_K - sft, :]
    lane_qk = lax.broadcasted_iota(I32, (1, 2 * w), 1)
    qk_ref[...] = acc * _sigmoid(acc) * jnp.where(lane_qk < w, 1.0, hd ** -0.5)

    def hi_lo_rows(v):
        hi = v.astype(BF16).astype(F32)
        return jnp.where(rowm < nh, hi, pltpu.roll(v - hi, 4, axis=0))

    a_cr = hi_lo_rows(cr)
    a_b = hi_lo_rows(b_t)
    a_r = hi_lo_rows(r_t)
    for c in range(nchunk):
        rows = slice(c * CHUNK, (c + 1) * CHUNK)
        t8 = slice(c * 8, (c + 1) * 8)
        a_cr_c = a_cr[t8].astype(BF16)
        cr128_ref[rows, :] = _dot_tn(a_cr_c, sel128)
        cr64_ref[rows, :] = _dot_tn(a_cr_c, sel64)
        b64_ref[rows, :] = _dot_tn(a_b[t8].astype(BF16), sel64)
        r64_ref[rows, :] = _dot_tn(a_r[t8].astype(BF16), sel64)
        for h in range(nh):
            qg = qk_ref[rows, grp[h]].astype(BF16)
            kg = qk_ref[rows, w + grp[h].start:w + grp[h].stop].astype(BF16)
            s_ref[rows, h * 128:(h + 1) * 128] = _dot_nt(qg, jnp.where(half[h], kg, jnp.zeros_like(kg)))

    lane_w = lax.broadcasted_iota(I32, (1, w), 1)
    hmask = [(lane_w // hd) == h for h in range(nh)]
    ones_blk = [m.astype(BF16) * jnp.ones((CHUNK, 1), BF16) for m in hmask]
    ones_cols = jnp.ones((CHUNK, w), BF16)
    zblk = jnp.zeros((hd, 128), F32)
    ngrp = w // 128
    m_prev = m_ref[...]
    cblk = [st_ref[h * hd:(h + 1) * hd, grp[h]] for h in range(nh)]
    nblk = [st_ref[h * hd:(h + 1) * hd, w + grp[h].start:w + grp[h].stop] for h in range(nh)]
    for c in range(nchunk):
        rows = slice(c * CHUNK, (c + 1) * CHUNK)
        t8 = slice(c * 8, (c + 1) * 8)
        r_c = rt_ref[t8, :]
        g_last = jnp.maximum(rm_ref[t8, :], m_prev)
        decay = jnp.exp(m_prev - g_last)
        mprev_l = lanes(m_prev)
        glast_l = lanes(g_last)
        gb64 = jnp.maximum(cr64_ref[rows, :], mprev_l)
        w_inter = jnp.exp(mprev_l - gb64)
        e_negm = jnp.exp(-(b64_ref[rows, :] + gb64))
        v_all = zc_ref[rows, 2 * w:3 * w]
        wcat = []
        vblocks = []
        srows = []
        for h in range(nh):
            gb128 = jnp.maximum(cr128_ref[rows, h * 128:(h + 1) * 128], m_prev[h:h + 1, :])
            d = jnp.exp(jnp.where(causal, r_c[h:h + 1, :] - gb128, NEG))
            wcat.append((d * s_ref[rows, h * 128:(h + 1) * 128]).astype(BF16))
            vblocks.append(jnp.concatenate([jnp.where(hmask[h], v_all, jnp.zeros_like(v_all)), ones_blk[h]], axis=1))
            g = (h * hd) // 128
            srows.append(jnp.concatenate([cblk[h] if j == g else zblk for j in range(ngrp)]
                                         + [nblk[h] if j == g else zblk for j in range(ngrp)], axis=1))
        s_bf = jnp.concatenate(srows, axis=0).astype(BF16)
        out_aug = _dot(jnp.concatenate(wcat, axis=1), jnp.concatenate(vblocks, axis=0))
        out_aug = out_aug + _dot((qk_ref[rows, 0:w] * w_inter).astype(BF16), s_bf)
        hh_ref[rows, :] = out_aug[:, 0:w] / jnp.maximum(jnp.abs(out_aug[:, w:2 * w]), e_negm)
        kw = (qk_ref[rows, w:2 * w] * jnp.exp(r64_ref[rows, :] - glast_l)).astype(BF16)
        upd = _dot_tn(kw, jnp.concatenate([v_all, ones_cols], axis=1))
        for h in range(nh):
            rs = slice(h * hd, (h + 1) * hd)
            cblk[h] = cblk[h] * decay[h:h + 1, :] + jnp.where(half[h], upd[rs, grp[h]], 0.0)
            nblk[h] = (nblk[h] * decay[h:h + 1, :]
                       + jnp.where(half[h], upd[rs, w + grp[h].start:w + grp[h].stop], 0.0))
        m_prev = bl_ref[t8, :] + g_last
    m_ref[...] = m_prev
    for h in range(nh):
        st_ref[h * hd:(h + 1) * hd, grp[h]] = cblk[h]
        st_ref[h * hd:(h + 1) * hd, w + grp[h].start:w + grp[h].stop] = nblk[h]

    hh = hh_ref[...]
    ms = _dot((hh * hh).astype(BF16), bones_ref[...])
    og = zc_ref[:, 3 * w:4 * w].astype(F32)
    o_ref[...] = (_sigmoid(og) * hh * lax.rsqrt(ms + EPS) * ng_ref[...]).astype(o_ref.dtype)


def _mlstm(zc3, zgt, conv_w, gate_b_rows, norm_g):
    b, s, w = zc3.shape
    ts = SEQ_TILE
    nt = s // ts
    gw = GROUP_W
    t = jnp.arange(CHUNK)
    triu = (t[:, None] <= t[None, :]).astype(BF16)
    hid = jnp.arange(gw) // HEAD_DIM
    bones = jnp.where(hid[:, None] == hid[None, :], 1.0 / HEAD_DIM, 0.0).astype(BF16)
    full = lambda a: pl.BlockSpec(a.shape, lambda bi, i: (0,) * a.ndim)
    return pl.pallas_call(
        _mlstm_kernel,
        grid=(b, nt),
        in_specs=[pl.BlockSpec((None, ts, w), lambda bi, i: (bi, i, 0)),
                  pl.BlockSpec((8, ts), lambda bi, i: (0, bi * nt + i)),
                  full(conv_w), full(gate_b_rows), full(norm_g), full(triu), full(bones)],
        out_specs=pl.BlockSpec((None, ts, gw), lambda bi, i: (bi, i, 0)),
        out_shape=jax.ShapeDtypeStruct((b, s, gw), BF16),
        scratch_shapes=[pltpu.VMEM((gw, 2 * gw), F32),
                        pltpu.VMEM((8, 128), F32),
                        pltpu.VMEM((8, 2 * gw), F32),
                        pltpu.VMEM((ts, 2 * gw), F32),
                        pltpu.VMEM((ts // CHUNK * 8, 128), F32),
                        pltpu.VMEM((ts // CHUNK * 8, 128), F32),
                        pltpu.VMEM((ts // CHUNK * 8, 128), F32),
                        pltpu.VMEM((ts, MLSTM_HEADS * 128), F32),
                        pltpu.VMEM((ts, gw), F32),
                        pltpu.VMEM((ts, gw), F32),
                        pltpu.VMEM((ts, gw), F32),
                        pltpu.VMEM((ts, MLSTM_HEADS * 128), F32),
                        pltpu.VMEM((ts, gw), F32)],
        compiler_params=_cparams("parallel", "arbitrary"),
        name="mlstm",
    )(zc3, zgt, conv_w, gate_b_rows, norm_g, triu, bones)


def _sgu_kernel(zd_ref, lng_ref, lnb_ref, w_ref, bias_ref, o_ref):
    tm = zd_ref.shape[0]
    gw = GROUP_W // SGU_GROUPS
    u = _gelu(zd_ref[:, 0:GROUP_W].astype(F32))
    v = _gelu(zd_ref[:, GROUP_W:2 * GROUP_W].astype(F32))
    mu = jnp.mean(v, axis=-1, keepdims=True)
    vc = v - mu
    var = jnp.mean(vc * vc, axis=-1, keepdims=True)
    vn = (vc * lax.rsqrt(var + EPS) * lng_ref[...] + lnb_ref[...]).astype(BF16)
    for c in range(tm // CHUNK):
        rows = slice(c * CHUNK, (c + 1) * CHUNK)
        mixed = jnp.concatenate([_dot(w_ref[g], vn[rows, g * gw:(g + 1) * gw]) for g in range(SGU_GROUPS)], axis=-1)
        o_ref[rows, :] = (u[rows, :] * (mixed + bias_ref[...])).astype(o_ref.dtype)


def _sgu(zd, ln_g, ln_b, w_tril, bias):
    n = zd.shape[0]
    tm = TOKEN_TILE
    full = lambda a: pl.BlockSpec(a.shape, lambda i: (0,) * a.ndim)
    return pl.pallas_call(
        _sgu_kernel,
        grid=(n // tm,),
        in_specs=[pl.BlockSpec((tm, 2 * GROUP_W), lambda i: (i, 0)), full(ln_g), full(ln_b), full(w_tril), full(bias)],
        out_specs=pl.BlockSpec((tm, GROUP_W), lambda i: (i, 0)),
        out_shape=jax.ShapeDtypeStruct((n, GROUP_W), BF16),
        compiler_params=_cparams("parallel"),
        name="sgu",
    )(zd, ln_g, ln_b, w_tril, bias)


def _out_proj_kernel(x_ref, ya_ref, yb_ref, yc_ref, yd_ref, gluw_ref, glub_ref, wo_ref, ng_ref,
                     rwh_ref, rwl_ref, rb_ref,
                     x1_ref, hx_ref, meta_ref, cnt_ref, carry_ref):
    tm = x_ref.shape[0]

    @pl.when(pl.program_id(0) == 0)
    def _():
        carry_ref[...] = jnp.zeros_like(carry_ref)

    yb = _gelu(yb_ref[...].astype(F32))
    yb = yb * _sigmoid(_dot(yb.astype(BF16), gluw_ref[...]) + glub_ref[...])
    mix = _dot(ya_ref[...], wo_ref[0:GROUP_W, :])
    mix = mix + _dot(yb.astype(BF16), wo_ref[GROUP_W:2 * GROUP_W, :])
    mix = mix + _dot(yc_ref[...], wo_ref[2 * GROUP_W:3 * GROUP_W, :])
    mix = mix + _dot(yd_ref[...], wo_ref[3 * GROUP_W:4 * GROUP_W, :])
    x1 = x_ref[...] + mix
    x1_ref[...] = x1
    h = _rms(x1, ng_ref[...])
    hx_ref[:, 0:D_MODEL] = h

    hh, hl = _split_bf16(h)
    wh = rwh_ref[...]
    wl = rwl_ref[...]
    logits = _dot_nt(wh, hh) + _dot_nt(wh, hl) + _dot_nt(wl, hh) + rb_ref[...]
    gl = [logits[j:j + 1, :] for j in range(N_GROUPS)]
    gmax = functools.reduce(jnp.maximum, gl)
    gsel = jnp.full((1, tm), N_GROUPS - 1, I32)
    for j in range(N_GROUPS - 2, -1, -1):
        gsel = jnp.where(gl[j] == gmax, j, gsel)
    p_g = 1.0 / functools.reduce(jnp.add, [jnp.exp(v - gmax) for v in gl])
    e_in = []
    for i in range(EPG):
        v = logits[N_GROUPS + i:N_GROUPS + i + 1, :]
        for j in range(1, N_GROUPS):
            r = N_GROUPS + j * EPG + i
            v = jnp.where(gsel == j, logits[r:r + 1, :], v)
        e_in.append(v)
    v1 = functools.reduce(jnp.maximum, e_in)
    i1 = jnp.full((1, tm), EPG - 1, I32)
    for i in range(EPG - 2, -1, -1):
        i1 = jnp.where(e_in[i] == v1, i, i1)
    rest = [jnp.where(i1 == i, NEG, e_in[i]) for i in range(EPG)]
    v2 = functools.reduce(jnp.maximum, rest)
    i2 = jnp.full((1, tm), EPG - 1, I32)
    for i in range(EPG - 2, -1, -1):
        i2 = jnp.where((rest[i] == v2) & (i1 != i), i, i2)
    e2 = jnp.exp(v2 - v1)
    w1 = p_g / (1.0 + e2)
    w2 = p_g * e2 / (1.0 + e2)
    lo = jnp.minimum(i1, i2)
    hi = jnp.maximum(i1, i2)
    w_lo = jnp.where(i1 < i2, w1, w2)
    w_hi = jnp.where(i1 < i2, w2, w1)
    bucket = gsel * PAIRS + ((lo * (2 * EPG - 1 - lo)) >> 1) + (hi - lo - 1)

    kid = lax.broadcasted_iota(I32, (BUCKET_PAD, tm), 0)
    onehot = (kid == bucket).astype(F32)
    s_i = lax.broadcasted_iota(I32, (tm, tm), 0)
    t_i = lax.broadcasted_iota(I32, (tm, tm), 1)
    prefix = _dot(onehot.astype(BF16), (s_i <= t_i).astype(BF16))
    carry = carry_ref[...]
    rank = jnp.sum(onehot * (prefix - 1.0 + carry[:, 0:1]), axis=0, keepdims=True)
    carry = carry + prefix[:, tm - 1:tm]
    carry_ref[...] = carry
    cnt_ref[...] = carry
    meta_ref[...] = jnp.concatenate([bucket, rank.astype(I32), jnp.zeros((6, tm), I32)], axis=0)
    wrows = jnp.concatenate([w_lo, w_hi, jnp.zeros((126, tm), F32)], axis=0)
    for c in range(tm // 128):
        hx_ref[c * 128:(c + 1) * 128, D_MODEL:ROW_W] = wrows[:, c * 128:(c + 1) * 128].T


def _out_proj(x2, ya, yb, yc, yd, glu_w, glu_b, w_out, norm_g, rw_hi, rw_lo, rb):
    n = x2.shape[0]
    tm = TOKEN_TILE
    row = lambda w: pl.BlockSpec((tm, w), lambda i: (i, 0))
    full = lambda a: pl.BlockSpec(a.shape, lambda i: (0,) * a.ndim)
    return pl.pallas_call(
        _out_proj_kernel,
        grid=(n // tm,),
        in_specs=[row(D_MODEL), row(GROUP_W), row(GROUP_W), row(GROUP_W), row(GROUP_W),
                  full(glu_w), full(glu_b), full(w_out), full(norm_g), full(rw_hi), full(rw_lo), full(rb)],
        out_specs=[row(D_MODEL), row(ROW_W), pl.BlockSpec((8, tm), lambda i: (0, i)),
                   pl.BlockSpec((BUCKET_PAD, 128), lambda i: (0, 0))],
        out_shape=[jax.ShapeDtypeStruct((n, D_MODEL), F32), jax.ShapeDtypeStruct((n, ROW_W), F32),
                   jax.ShapeDtypeStruct((8, n), I32), jax.ShapeDtypeStruct((BUCKET_PAD, 128), F32)],
        scratch_shapes=[pltpu.VMEM((BUCKET_PAD, 128), F32)],
        compiler_params=_cparams("arbitrary"),
        name="out_proj",
    )(x2, ya, yb, yc, yd, glu_w, glu_b, w_out, norm_g, rw_hi, rw_lo, rb)


def _row_copy(src_ref, src_row, dst_ref, dst_row, sem):
    return pltpu.make_async_copy(src_ref.at[pl.ds(src_row, 1)], dst_ref.at[pl.ds(dst_row, 1)], sem)


def _dispatch_kernel(dest_ref, hx_ref, xs_in_ref, xs_ref, sem):
    del xs_in_ref
    tm = hx_ref.shape[0]

    def start(t, c):
        _row_copy(hx_ref, t, xs_ref, dest_ref[0, 0, t], sem).start()
        return c

    lax.fori_loop(0, tm, start, 0, unroll=ROW_DMA_UNROLL)
    pltpu.make_async_copy(hx_ref, xs_ref.at[pl.ds(0, tm)], sem).wait()


def _dispatch(dest3, hx, p_tot):
    n = hx.shape[0]
    tm = TOKEN_TILE
    xs0 = jnp.zeros((p_tot, ROW_W), F32)
    return pl.pallas_call(
        _dispatch_kernel,
        grid=(n // tm,),
        in_specs=[pl.BlockSpec((1, 1, tm), lambda i: (i, 0, 0), memory_space=pltpu.SMEM),
                  pl.BlockSpec((tm, ROW_W), lambda i: (i, 0)),
                  pl.BlockSpec(memory_space=pl.ANY)],
        out_specs=pl.BlockSpec(memory_space=pl.ANY),
        out_shape=jax.ShapeDtypeStruct((p_tot, ROW_W), F32),
        input_output_aliases={2: 0},
        scratch_shapes=[pltpu.SemaphoreType.DMA],
        compiler_params=_cparams("arbitrary"),
        name="dispatch",
    )(dest3, hx, xs0)


def _ffn_kernel(ea_ref, eb_ref, valid_ref, xblk_ref, xs_ref, wg_a_ref, wu_a_ref, wd_a_ref, wg_b_ref, wu_b_ref,
                wd_b_ref, ys_ref, wgu_a, wdn_a, wgu_b, wdn_b):
    del xblk_ref
    j = pl.program_id(0)
    valid = valid_ref[j]
    jp = jnp.maximum(j - 1, 0)

    @pl.when((j == 0) | (ea_ref[j] != ea_ref[jp]))
    def _():
        wgu_a[:, 0:D_EXPERT] = wg_a_ref[...].astype(BF16)
        wgu_a[:, D_EXPERT:2 * D_EXPERT] = wu_a_ref[...].astype(BF16)
        wdn_a[...] = wd_a_ref[...].astype(BF16)

    @pl.when((j == 0) | (eb_ref[j] != eb_ref[jp]))
    def _():
        wgu_b[:, 0:D_EXPERT] = wg_b_ref[...].astype(BF16)
        wgu_b[:, D_EXPERT:2 * D_EXPERT] = wu_b_ref[...].astype(BF16)
        wdn_b[...] = wd_b_ref[...].astype(BF16)

    for sb in range(FFN_BLOCK // FFN_SUB):
        rows = slice(sb * FFN_SUB, (sb + 1) * FFN_SUB)

        @pl.when(valid <= sb * FFN_SUB)
        def _():
            ys_ref[rows, :] = jnp.zeros((FFN_SUB, D_MODEL), F32)

        @pl.when(valid > sb * FFN_SUB)
        def _():
            xb = xs_ref[rows, 0:D_MODEL].astype(BF16)

            def expert(wgu, wdn):
                gu = _dot(xb, wgu[...])
                g = gu[:, 0:D_EXPERT]
                act = (g * _sigmoid(g) * gu[:, D_EXPERT:2 * D_EXPERT]).astype(BF16)
                return _dot(act, wdn[...])

            ya = expert(wgu_a, wdn_a)
            yb = expert(wgu_b, wdn_b)
            ys_ref[rows, :] = (ya * xs_ref[rows, D_MODEL:D_MODEL + 1]
                               + yb * xs_ref[rows, D_MODEL + 1:D_MODEL + 2])


def _ffn(ea, eb, valid, xblk, xs, layer, w_gate, w_up, w_down):
    p_tot = xs.shape[0]
    bm = FFN_BLOCK
    nblk = p_tot // bm
    wspec = lambda e_idx, shape: pl.BlockSpec((None, None) + shape,
                                              lambda j, ea, eb, va, xb: (layer, (ea, eb)[e_idx][j], 0, 0))
    up_shape = (D_MODEL, D_EXPERT)
    dn_shape = (D_EXPERT, D_MODEL)
    return pl.pallas_call(
        _ffn_kernel,
        grid_spec=pltpu.PrefetchScalarGridSpec(
            num_scalar_prefetch=4,
            grid=(nblk,),
            in_specs=[pl.BlockSpec((bm, ROW_W), lambda j, ea, eb, va, xb: (xb[j], 0)),
                      wspec(0, up_shape), wspec(0, up_shape), wspec(0, dn_shape),
                      wspec(1, up_shape), wspec(1, up_shape), wspec(1, dn_shape)],
            out_specs=pl.BlockSpec((bm, D_MODEL), lambda j, ea, eb, va, xb: (j, 0)),
            scratch_shapes=[pltpu.VMEM((D_MODEL, 2 * D_EXPERT), BF16), pltpu.VMEM(dn_shape, BF16),
                            pltpu.VMEM((D_MODEL, 2 * D_EXPERT), BF16), pltpu.VMEM(dn_shape, BF16)],
        ),
        out_shape=jax.ShapeDtypeStruct((p_tot, D_MODEL), F32),
        compiler_params=_cparams("arbitrary"),
        name="ffn",
    )(ea, eb, valid, xblk, xs, w_gate, w_up, w_down, w_gate, w_up, w_down)


def _combine_kernel(dest_ref, x1_ref, ys_ref, ng_ref, o_ref, buf_ref, sem, *, final_norm):
    tm = x1_ref.shape[0]

    def start(t, c):
        _row_copy(ys_ref, dest_ref[0, 0, t], buf_ref, t, sem).start()
        return c

    lax.fori_loop(0, tm, start, 0, unroll=ROW_DMA_UNROLL)
    pltpu.make_async_copy(ys_ref.at[pl.ds(0, tm)], buf_ref, sem).wait()
    x2 = x1_ref[...] + buf_ref[...]
    o_ref[...] = _rms(x2, ng_ref[...]) if final_norm else x2


def _combine(dest3, x1, ys, norm_g, final_norm):
    n = x1.shape[0]
    tm = TOKEN_TILE
    return pl.pallas_call(
        functools.partial(_combine_kernel, final_norm=final_norm),
        grid=(n // tm,),
        in_specs=[pl.BlockSpec((1, 1, tm), lambda i: (i, 0, 0), memory_space=pltpu.SMEM),
                  pl.BlockSpec((tm, D_MODEL), lambda i: (i, 0)),
                  pl.BlockSpec(memory_space=pl.ANY),
                  pl.BlockSpec((1, D_MODEL), lambda i: (0, 0))],
        out_specs=pl.BlockSpec((tm, D_MODEL), lambda i: (i, 0)),
        out_shape=jax.ShapeDtypeStruct((n, D_MODEL), F32),
        scratch_shapes=[pltpu.VMEM((tm, D_MODEL), F32), pltpu.SemaphoreType.DMA],
        compiler_params=_cparams("arbitrary"),
        name="combine",
    )(dest3, x1, ys, norm_g)


def _routing_tables(meta, counts, n):
    bm = FFN_BLOCK
    nblk = n // bm + N_BUCKETS
    bucket = meta[0]
    rank = meta[1]
    cnt = counts[:N_BUCKETS, 0].astype(I32)
    padded = ((cnt + bm - 1) // bm) * bm
    pad_end = jnp.cumsum(padded)
    pad_start = pad_end - padded
    onehot = bucket[:, None] == jnp.arange(N_BUCKETS, dtype=I32)[None, :]
    dest = rank + jnp.sum(jnp.where(onehot, pad_start[None, :], 0), axis=1)
    nact = (pad_end[-1] // bm).astype(I32)
    j = jnp.arange(nblk, dtype=I32)
    blk = jnp.minimum(j, jnp.maximum(nact - 1, 0))
    pos = blk * bm
    bkt = jnp.minimum(jnp.sum((pad_end[None, :] <= pos[:, None]).astype(I32), axis=1), N_BUCKETS - 1)
    in_bkt = bkt[:, None] == jnp.arange(N_BUCKETS, dtype=I32)[None, :]
    sel = lambda tab: jnp.sum(jnp.where(in_bkt, tab[None, :], 0), axis=1)
    valid = jnp.where(j < nact, jnp.clip(sel(cnt) - (pos - sel(pad_start)), 0, bm), 0)
    lo_tab, hi_tab = [], []
    for g in range(N_GROUPS):
        for a in range(EPG):
            for b in range(a + 1, EPG):
                lo_tab.append(g * EPG + a)
                hi_tab.append(g * EPG + b)
    ea = sel(jnp.asarray(lo_tab, I32))
    eb = sel(jnp.asarray(hi_tab, I32))
    return dest.astype(I32), ea.astype(I32), eb.astype(I32), valid.astype(I32), blk.astype(I32), nblk * bm


def _layer(x2, b, s, p, final_g):
    n = b * s
    za, zb, zc, zd, zgt = _in_proj(x2, p['norm_mix_g'], p['wa'], p['wb'], p['wc'], p['wd'], p['wgt'])
    ya = _attention(za.reshape(b, s, -1), p['sinks']).reshape(n, GROUP_W)
    yb = _ssm(zb.reshape(b, s, GROUP_W), p['ssm']).reshape(n, GROUP_W)
    yc = _mlstm(zc.reshape(b, s, -1), zgt, p['conv_w'], p['gate_b'], p['mlstm_norm_g']).reshape(n, GROUP_W)
    yd = _sgu(zd, p['sgu_ln_g'], p['sgu_ln_b'], p['sgu_w'], p['sgu_bias'])
    x1, hx, meta, counts = _out_proj(x2, ya, yb, yc, yd, p['glu_w'], p['glu_b'], p['w_out'], p['norm_ffn_g'],
                                     p['rw_hi'], p['rw_lo'], p['rb'])
    dest, ea, eb, valid, xblk, p_tot = _routing_tables(meta, counts, n)
    dest3 = dest.reshape(n // TOKEN_TILE, 1, TOKEN_TILE)
    xs = _dispatch(dest3, hx, p_tot)
    ys = _ffn(ea, eb, valid, xblk, xs, p['layer'], p['w_gate'], p['w_up'], p['w_down'])
    if final_g is None:
        return _combine(dest3, x1, ys, p['norm_ffn_g'], False)
    return _combine(dest3, x1, ys, final_g, True)


def _prep_layer(l, norm_mix_g, w_in, attn_sinks, ssm_a_re, ssm_a_im, ssm_b_re, ssm_b_im, ssm_c_re, ssm_c_im,
                ssm_d, ssm_log_dt, ssm_glu_w, ssm_glu_b, mlstm_conv_w, mlstm_gate_b, mlstm_norm_g,
                sgu_ln_g, sgu_ln_b, sgu_w, sgu_b, w_out, norm_ffn_g, router_group_w, router_group_b,
                router_expert_w, router_expert_b, expert_w_gate, expert_w_up, expert_w_down):
    w = w_in[l]
    o_su = 2 * GROUP_W
    o_c = o_su + GROUP_W
    o_ci = o_c + 3 * GROUP_W
    o_co = o_ci + 2 * MLSTM_HEADS
    o_d = o_co + GROUP_W
    wc = jnp.concatenate([w[:, o_c:o_ci], w[:, o_co:o_d]], axis=1)
    wgt = w[:, o_ci:o_co].T
    rw = jnp.concatenate([router_group_w[l], router_expert_w[l]], axis=1).T.astype(F32)
    rw = jnp.pad(rw, ((0, 4), (0, 0)))
    rw_hi = rw.astype(BF16)
    rw_lo = (rw - rw_hi.astype(F32)).astype(BF16)
    rb = jnp.pad(jnp.concatenate([router_group_b[l], router_expert_b[l]]).astype(F32), (0, 4))[:, None]
    tril = jnp.tril(jnp.ones((CHUNK, CHUNK), F32))
    gw = GROUP_W // SGU_GROUPS
    return dict(
        norm_mix_g=norm_mix_g[l][None, :].astype(F32),
        wa=w[:, 0:o_su].astype(BF16), wb=w[:, o_su:o_c].astype(BF16), wc=wc.astype(BF16),
        wd=w[:, o_d:].astype(BF16), wgt=wgt.astype(BF16),
        sinks=attn_sinks[l].astype(F32),
        ssm=_ssm_matrices(ssm_a_re[l], ssm_a_im[l], ssm_b_re[l], ssm_b_im[l], ssm_c_re[l], ssm_c_im[l],
                          ssm_d[l], ssm_log_dt[l]),
        glu_w=ssm_glu_w[l].astype(BF16), glu_b=ssm_glu_b[l][None, :].astype(F32),
        conv_w=mlstm_conv_w[l].astype(F32),
        gate_b=jnp.broadcast_to(mlstm_gate_b[l].astype(F32)[:, None], (2 * MLSTM_HEADS, 128)),
        mlstm_norm_g=mlstm_norm_g[l][None, :].astype(F32),
        sgu_ln_g=sgu_ln_g[l][None, :].astype(F32), sgu_ln_b=sgu_ln_b[l][None, :].astype(F32),
        sgu_w=(sgu_w[l].astype(F32) * tril).astype(BF16),
        sgu_bias=jnp.repeat(sgu_b[l].astype(F32).T, gw, axis=1),
        w_out=w_out[l].astype(BF16), norm_ffn_g=norm_ffn_g[l][None, :].astype(F32),
        rw_hi=rw_hi, rw_lo=rw_lo, rb=rb,
        layer=l, w_gate=expert_w_gate, w_up=expert_w_up, w_down=expert_w_down,
    )


def kernel(x, norm_mix_g, w_in, attn_sinks, ssm_a_re, ssm_a_im, ssm_b_re, ssm_b_im, ssm_c_re, ssm_c_im, ssm_d, ssm_log_dt, ssm_glu_w, ssm_glu_b, mlstm_conv_w, mlstm_gate_b, mlstm_norm_g, sgu_ln_g, sgu_ln_b, sgu_w, sgu_b, w_out, norm_ffn_g, router_group_w, router_group_b, router_expert_w, router_expert_b, expert_w_gate, expert_w_up, expert_w_down, norm_final_g):
    b, s, d = x.shape
    depth = w_in.shape[0]
    x2 = x.reshape(b * s, d).astype(F32)
    for l in range(depth):
        p = _prep_layer(l, norm_mix_g, w_in, attn_sinks, ssm_a_re, ssm_a_im, ssm_b_re, ssm_b_im, ssm_c_re,
                        ssm_c_im, ssm_d, ssm_log_dt, ssm_glu_w, ssm_glu_b, mlstm_conv_w, mlstm_gate_b,
                        mlstm_norm_g, sgu_ln_g, sgu_ln_b, sgu_w, sgu_b, w_out, norm_ffn_g, router_group_w,
                        router_group_b, router_expert_w, router_expert_b, expert_w_gate, expert_w_up,
                        expert_w_down)
        final_g = norm_final_g[None, :].astype(F32) if l == depth - 1 else None
        x2 = _layer(x2, b, s, p, final_g)
    return x2.reshape(b, s, d).astype(x.dtype)
```

```python
import functools
import math

import jax
import jax.numpy as jnp
from jax import lax
from jax.experimental import pallas as pl
from jax.experimental.pallas import tpu as pltpu

F32 = jnp.float32
BF16 = jnp.bfloat16
I32 = jnp.int32
U32 = jnp.uint32

D_MODEL = 1024
GROUP_W = 256
HEAD_DIM = 64
EPS = 1e-6
NEG = -1e30
WINDOW = 128
KV_HEADS = 2
SSM_GROUP = 16
SSM_GROUPS = 16
SSM_STATE = 64
SSM_CHUNK = 16
MLSTM_HEADS = 4
CHUNK = 128
CONV_K = 4
SGU_GROUPS = 4
N_GROUPS = 4
EPG = 8
N_EXPERTS = 32
D_EXPERT = 512
PAIRS = EPG * (EPG - 1) // 2
N_BUCKETS = N_GROUPS * PAIRS
BUCKET_PAD = 128
PACK_W = D_MODEL // 2
ROW_W = PACK_W + 128

TOKEN_TILE = 512
SEQ_TILE = 512
SSM_TILE = 4096
SSM_SLAB = 64
FFN_BLOCK = 512
FFN_SUB = 256
ROW_DMA_UNROLL = 8
VMEM_LIMIT = 56 * 1024 * 1024


def _cparams(*sem):
    return pltpu.CompilerParams(dimension_semantics=sem, vmem_limit_bytes=VMEM_LIMIT)


def _rms(x, g):
    return x * lax.rsqrt(jnp.mean(x * x, axis=-1, keepdims=True) + EPS) * g


def _gelu(x):
    return 0.5 * x * (1.0 + jnp.tanh(math.sqrt(2.0 / math.pi) * (x + 0.044715 * (x * x * x))))


def _sigmoid(x):
    return 1.0 / (1.0 + jnp.exp(-x))


def _log_sigmoid(x):
    return jnp.minimum(x, 0.0) - jnp.log(1.0 + jnp.exp(-jnp.abs(x)))


def _dot(a, b):
    return jnp.dot(a, b, preferred_element_type=F32)


def _dot_nt(a, b):
    return lax.dot_general(a, b, (((1,), (1,)), ((), ())), preferred_element_type=F32)


def _dot_tn(a, b):
    return lax.dot_general(a, b, (((0,), (0,)), ((), ())), preferred_element_type=F32)


def _split_bf16(x):
    hi = x.astype(BF16)
    lo = (x - hi.astype(F32)).astype(BF16)
    return hi, lo


def _pack_rows(x):
    w = x.shape[1] // 2
    bits = lambda v: pltpu.bitcast(v.astype(BF16).astype(F32), U32)
    return (bits(x[:, 0:w]) >> 16) | (bits(x[:, w:2 * w]) & jnp.uint32(0xFFFF0000))


def _unpack_rows(p):
    return pltpu.bitcast(p << 16, F32), pltpu.bitcast(p & jnp.uint32(0xFFFF0000), F32)


def _in_proj_kernel(x_ref, g_ref, wa_ref, wb_ref, wc_ref, wd_ref, wgt_ref,
                    za_ref, zb_ref, zc_ref, zd_ref, zgt_ref):
    hb = _rms(x_ref[...], g_ref[...]).astype(BF16)
    za_ref[...] = _dot(hb, wa_ref[...]).astype(BF16)
    zb_ref[...] = _dot(hb, wb_ref[...]).astype(BF16)
    zc_ref[...] = _dot(hb, wc_ref[...]).astype(BF16)
    zd_ref[...] = _dot(hb, wd_ref[...]).astype(BF16)
    zgt_ref[...] = _dot_nt(wgt_ref[...], hb)


def _in_proj(x2, g, wa, wb, wc, wd, wgt):
    n = x2.shape[0]
    tm = TOKEN_TILE
    row = lambda w: pl.BlockSpec((tm, w), lambda i: (i, 0))
    full = lambda a: pl.BlockSpec(a.shape, lambda i: (0,) * a.ndim)
    widths = (wa.shape[1], wb.shape[1], wc.shape[1], wd.shape[1])
    return pl.pallas_call(
        _in_proj_kernel,
        grid=(n // tm,),
        in_specs=[row(D_MODEL), full(g), full(wa), full(wb), full(wc), full(wd), full(wgt)],
        out_specs=[row(w) for w in widths] + [pl.BlockSpec((wgt.shape[0], tm), lambda i: (0, i))],
        out_shape=[jax.ShapeDtypeStruct((n, w), BF16) for w in widths]
        + [jax.ShapeDtypeStruct((wgt.shape[0], n), F32)],
        compiler_params=_cparams("parallel"),
        name="in_proj",
    )(x2, g, wa, wb, wc, wd, wgt)


def _attn_kernel(sink_ref, cur_ref, prev_ref, o_ref):
    first = pl.program_id(1) == 0
    nblk = cur_ref.shape[0] // WINDOW
    row = lax.broadcasted_iota(I32, (2 * WINDOW, 2 * WINDOW), 0)
    col = lax.broadcasted_iota(I32, (2 * WINDOW, 2 * WINDOW), 1)
    qi = jnp.where(row >= WINDOW, row - WINDOW, row)
    band = (col <= qi + WINDOW) & (col > qi)
    row1 = lax.broadcasted_iota(I32, (2 * WINDOW, 1), 0)
    for j in range(nblk):
        cur = cur_ref[j * WINDOW:(j + 1) * WINDOW, :]
        if j == 0:
            prev = prev_ref[...]
            mask = band & ((col >= WINDOW) | jnp.logical_not(first))
        else:
            prev = cur_ref[(j - 1) * WINDOW:j * WINDOW, :]
            mask = band
        outs = []
        for g in range(KV_HEADS):
            q2 = jnp.concatenate([cur[:, (2 * g) * HEAD_DIM:(2 * g + 1) * HEAD_DIM],
                                  cur[:, (2 * g + 1) * HEAD_DIM:(2 * g + 2) * HEAD_DIM]], axis=0)
            ko = GROUP_W + g * HEAD_DIM
            vo = GROUP_W + KV_HEADS * HEAD_DIM + g * HEAD_DIM
            kk = jnp.concatenate([prev[:, ko:ko + HEAD_DIM], cur[:, ko:ko + HEAD_DIM]], axis=0)
            vv = jnp.concatenate([prev[:, vo:vo + HEAD_DIM], cur[:, vo:vo + HEAD_DIM]], axis=0)
            s = _dot_nt(q2, kk) * (HEAD_DIM ** -0.5)
            s = jnp.where(mask, s, NEG)
            sink = jnp.where(row1 < WINDOW, sink_ref[2 * g], sink_ref[2 * g + 1])
            m = jnp.maximum(jnp.max(s, axis=-1, keepdims=True), sink)
            p = jnp.exp(s - m)
            l = jnp.sum(p, axis=-1, keepdims=True) + jnp.exp(sink - m)
            o = _dot(p.astype(BF16), vv) / l
            outs.append(o[:WINDOW])
            outs.append(o[WINDOW:])
        o_ref[j * WINDOW:(j + 1) * WINDOW, :] = jnp.concatenate(outs, axis=-1).astype(o_ref.dtype)


def _attention(za3, sinks):
    b, s, w = za3.shape
    ts = SEQ_TILE
    per = ts // WINDOW
    return pl.pallas_call(
        _attn_kernel,
        grid=(b, s // ts),
        in_specs=[pl.BlockSpec(memory_space=pltpu.SMEM),
                  pl.BlockSpec((None, ts, w), lambda bi, i: (bi, i, 0)),
                  pl.BlockSpec((None, WINDOW, w), lambda bi, i: (bi, jnp.maximum(i * per - 1, 0), 0))],
        out_specs=pl.BlockSpec((None, ts, GROUP_W), lambda bi, i: (bi, i, 0)),
        out_shape=jax.ShapeDtypeStruct((b, s, GROUP_W), BF16),
        compiler_params=_cparams("parallel", "parallel"),
        name="attn",
    )(sinks, za3, za3)


def _ssm_kernel(zb_ref, bre_ref, bim_ref, t_ref, cre_ref, cim_ref, are_ref, aim_ref, o_ref,
                x_ref, xs_ref, u_ref, vre_ref, vim_ref, sre_ref, sim_ref, y_ref, st_ref):
    ts = zb_ref.shape[0]
    L, G, H = SSM_CHUNK, SSM_GROUPS, SSM_GROUP
    nch = ts // L
    half_g = 128 // H
    n_half = G // half_g

    @pl.when(pl.program_id(1) == 0)
    def _():
        st_ref[...] = jnp.zeros_like(st_ref)

    blk = lax.broadcasted_iota(I32, (1, 128), 1) // H

    def block_transpose(arrs):
        a = list(arrs)
        s = half_g // 2
        while s >= 1:
            keep = (blk & s) == 0
            for i in range(half_g):
                if i & s == 0:
                    ai, aj = a[i], a[i + s]
                    a[i] = jnp.where(keep, ai, pltpu.roll(aj, s * H, axis=1))
                    a[i + s] = jnp.where(keep, pltpu.roll(ai, 128 - s * H, axis=1), aj)
            s //= 2
        return a

    for hf in range(n_half):
        x_ref[hf] = zb_ref[:, hf * 128:(hf + 1) * 128].astype(F32)
    for sg in range(L):
        for hf in range(n_half):
            xs_ref[sg * n_half + hf] = x_ref[hf, pl.ds(sg, nch, stride=L), :]
    for hf in range(n_half):
        for oc in range(L // half_g):
            for r0 in range(0, nch, SSM_SLAB):
                rows = slice(r0, r0 + SSM_SLAB)
                t = block_transpose([xs_ref[(oc * half_g + k) * n_half + hf, rows, :] for k in range(half_g)])
                for gl in range(half_g):
                    u_ref[hf * half_g + gl, rows, oc * 128:(oc + 1) * 128] = t[gl].astype(BF16)
    for g in range(G):
        ug = u_ref[g]
        vre_ref[pl.ds(g, nch, stride=G), :] = _dot(ug, bre_ref[g])
        vim_ref[pl.ds(g, nch, stride=G), :] = _dot(ug, bim_ref[g])

    are = are_ref[...]
    aim = aim_ref[...]

    def step(c, carry):
        sre, sim = carry
        r0 = pl.multiple_of(c * G, G)
        sre_ref[pl.ds(r0, G), :] = sre
        sim_ref[pl.ds(r0, G), :] = sim
        vre = vre_ref[pl.ds(r0, G), :]
        vim = vim_ref[pl.ds(r0, G), :]
        return (are * sre - aim * sim + vre, are * sim + aim * sre + vim)

    sre, sim = lax.fori_loop(0, nch, step, (st_ref[0:G, :], st_ref[G:2 * G, :]), unroll=4)
    st_ref[0:G, :] = sre
    st_ref[G:2 * G, :] = sim

    for g in range(G):
        y = _dot(u_ref[g], t_ref[g])
        y = y + _dot(sre_ref[pl.ds(g, nch, stride=G), :].astype(BF16), cre_ref[g])
        y = y + _dot(sim_ref[pl.ds(g, nch, stride=G), :].astype(BF16), cim_ref[g])
        y_ref[g] = y
    for hf in range(n_half):
        for oc in range(L // half_g):
            for r0 in range(0, nch, SSM_SLAB):
                t = block_transpose([y_ref[hf * half_g + gl, r0:r0 + SSM_SLAB, oc * 128:(oc + 1) * 128]
                                     for gl in range(half_g)])
                for k in range(half_g):
                    x_ref[hf, pl.ds(oc * half_g + k + L * r0, SSM_SLAB, stride=L), :] = t[k]
    for hf in range(n_half):
        o_ref[:, hf * 128:(hf + 1) * 128] = x_ref[hf].astype(o_ref.dtype)


def _ssm_matrices(a_re, a_im, b_re, b_im, c_re, c_im, d_skip, log_dt):
    L = SSM_CHUNK
    a = lax.complex(a_re.astype(F32), a_im.astype(F32))
    dt = jnp.exp(log_dt.astype(F32))[:, None]
    adt = a * dt
    a_bar = jnp.exp(adt)
    b_bar = ((a_bar - 1.0) / a)[..., None] * lax.complex(b_re.astype(F32), b_im.astype(F32))
    c_mat = lax.complex(c_re.astype(F32), c_im.astype(F32))
    lag = jnp.arange(L + 1, dtype=F32)
    pw = jnp.exp(adt[None] * lag[:, None, None])
    kern = jnp.einsum('gop,dgp,gpi->dgoi', c_mat, pw[:L], b_bar).real
    sig = jnp.arange(L)[:, None]
    tau = jnp.arange(L)[None, :]
    d = tau - sig
    kt = kern[jnp.clip(d, 0, L - 1)]
    kt = jnp.where((d >= 0)[:, :, None, None, None], kt, 0.0)
    eye = jnp.eye(SSM_GROUP, dtype=F32)
    dsk = d_skip.astype(F32).reshape(SSM_GROUPS, SSM_GROUP)
    kt = kt + (d == 0)[:, :, None, None, None] * (dsk[:, :, None] * eye)[None, None]
    t_mat = kt.transpose(2, 0, 4, 1, 3).reshape(SSM_GROUPS, L * SSM_GROUP, L * SSM_GROUP)
    bm = pw[:L][::-1][:, :, :, None] * b_bar[None]
    bm = bm.transpose(1, 0, 3, 2).reshape(SSM_GROUPS, L * SSM_GROUP, SSM_STATE)
    cm = c_mat[None] * pw[1:][:, :, None, :]
    cm = cm.transpose(1, 3, 0, 2).reshape(SSM_GROUPS, SSM_STATE, L * SSM_GROUP)
    a_chunk = pw[L]
    pad = 128 - SSM_STATE
    pc = lambda m: jnp.pad(m, ((0, 0), (0, 0), (0, pad))).astype(BF16)
    pr = lambda m: jnp.pad(m, ((0, 0), (0, pad), (0, 0))).astype(BF16)
    pa = lambda m: jnp.pad(m, ((0, 0), (0, pad)))
    return (pc(bm.real), pc(bm.imag), t_mat.astype(BF16), pr(cm.real), pr(-cm.imag),
            pa(a_chunk.real), pa(a_chunk.imag))


def _ssm(zb3, mats):
    b, s, w = zb3.shape
    L, G = SSM_CHUNK, SSM_GROUPS
    ts = min(SSM_TILE, s)
    nch = ts // L
    full = lambda a: pl.BlockSpec(a.shape, lambda bi, i: (0,) * a.ndim)
    return pl.pallas_call(
        _ssm_kernel,
        grid=(b, s // ts),
        in_specs=[pl.BlockSpec((None, ts, w), lambda bi, i: (bi, i, 0))] + [full(m) for m in mats],
        out_specs=pl.BlockSpec((None, ts, w), lambda bi, i: (bi, i, 0)),
        out_shape=jax.ShapeDtypeStruct((b, s, w), BF16),
        scratch_shapes=[pltpu.VMEM((w // 128, ts, 128), F32),
                        pltpu.VMEM((L * (w // 128), nch, 128), F32),
                        pltpu.VMEM((G, nch, L * SSM_GROUP), BF16),
                        pltpu.VMEM((nch * G, 128), F32), pltpu.VMEM((nch * G, 128), F32),
                        pltpu.VMEM((nch * G, 128), F32), pltpu.VMEM((nch * G, 128), F32),
                        pltpu.VMEM((G, nch, L * SSM_GROUP), F32),
                        pltpu.VMEM((2 * G, 128), F32)],
        compiler_params=_cparams("parallel", "arbitrary"),
        name="ssm",
    )(zb3, *mats)


def _mlstm_kernel(zc_ref, zgt_ref, convw_ref, gb_ref, ng_ref, triu_ref, bones_ref, o_ref,
                  st_ref, m_ref, tail_ref, qk_ref,
                  rt_ref, bl_ref, rm_ref, cr128_ref, cr64_ref, b64_ref, r64_ref, s_ref, hh_ref):
    ts = zc_ref.shape[0]
    hd, nh, w = HEAD_DIM, MLSTM_HEADS, GROUP_W
    nchunk = ts // CHUNK
    nr = nchunk * 8
    lane1 = lax.broadcasted_iota(I32, (1, 128), 1)
    half = [(lane1 // hd) == (h % 2) for h in range(nh)]
    grp = [slice((h * hd) // 128 * 128, (h * hd) // 128 * 128 + 128) for h in range(nh)]

    @pl.when(pl.program_id(1) == 0)
    def _():
        st_ref[...] = jnp.zeros_like(st_ref)
        m_ref[...] = jnp.zeros_like(m_ref)
        tail_ref[...] = jnp.zeros_like(tail_ref)

    r_i = lax.broadcasted_iota(I32, (CHUNK, CHUNK), 0)
    c_i = lax.broadcasted_iota(I32, (CHUNK, CHUNK), 1)
    causal = c_i <= r_i
    k64 = lax.broadcasted_iota(I32, (8, w), 0)
    j64 = lax.broadcasted_iota(I32, (8, w), 1)
    sel64 = (j64 // hd == k64 % nh).astype(BF16)
    k128 = lax.broadcasted_iota(I32, (8, nh * 128), 0)
    j128 = lax.broadcasted_iota(I32, (8, nh * 128), 1)
    sel128 = (j128 // 128 == k128 % nh).astype(BF16)

    def lanes(t):
        lo = jnp.where(lane1 < hd, t[0:1, :], t[1:2, :])
        hi = jnp.where(lane1 < hd, t[2:3, :], t[3:4, :])
        return jnp.concatenate([lo, hi], axis=1)

    rowm = lax.broadcasted_iota(I32, (nr, 128), 0) % 8
    lanem = lax.broadcasted_iota(I32, (nr, 128), 1)
    graw = jnp.concatenate([zgt_ref[:, c * CHUNK:(c + 1) * CHUNK] + gb_ref[...] for c in range(nchunk)], axis=0)
    g2 = jnp.where(rowm < nh, graw, _log_sigmoid(graw))
    ghi, glo = _split_bf16(g2)
    cum = _dot(ghi, triu_ref[...]) + _dot(glo, triu_ref[...])
    b_t = pltpu.roll(cum, nr - 4, axis=0)
    r_t = g2 - b_t
    cr = r_t
    sh = 1
    while sh < CHUNK:
        cr = jnp.maximum(cr, jnp.where(lanem >= sh, pltpu.roll(cr, sh, axis=1), NEG))
        sh *= 2
    rt_ref[...] = r_t
    bl_ref[...] = jnp.broadcast_to(b_t[:, CHUNK - 1:CHUNK], (nr, 128))
    rm_ref[...] = jnp.broadcast_to(cr[:, CHUNK - 1:CHUNK], (nr, 128))

    x = zc_ref[:, 0:2 * w].astype(F32)
    xe = jnp.concatenate([tail_ref[...], x], axis=0)
    tail_ref[...] = x[ts - 8:, :]
    cw = convw_ref[...]
    acc = x * cw[CONV_K - 1:CONV_K, :]
    for sft in range(1, CONV_K):
        acc = acc + xe[8 - sft:8 - sft + ts, :] * cw[CONV_K - 1 - sft:CONV_K - sft, :]
    lane_qk = lax.broadcasted_iota(I32, (1, 2 * w), 1)
    qk_ref[...] = acc * _sigmoid(acc) * jnp.where(lane_qk < w, 1.0, hd ** -0.5)

    def hi_lo_rows(v):
        hi = v.astype(BF16).astype(F32)
        return jnp.where(rowm < nh, hi, pltpu.roll(v - hi, 4, axis=0))

    a_cr = hi_lo_rows(cr)
    a_b = hi_lo_rows(b_t)
    a_r = hi_lo_rows(r_t)
    for c in range(nchunk):
        rows = slice(c * CHUNK, (c + 1) * CHUNK)
        t8 = slice(c * 8, (c + 1) * 8)
        a_cr_c = a_cr[t8].astype(BF16)
        cr128_ref[rows, :] = _dot_tn(a_cr_c, sel128)
        cr64_ref[rows, :] = _dot_tn(a_cr_c, sel64)
        b64_ref[rows, :] = _dot_tn(a_b[t8].astype(BF16), sel64)
        r64_ref[rows, :] = _dot_tn(a_r[t8].astype(BF16), sel64)
        for h in range(nh):
            qg = qk_ref[rows, grp[h]].astype(BF16)
            kg = qk_ref[rows, w + grp[h].start:w + grp[h].stop].astype(BF16)
            s_ref[rows, h * 128:(h + 1) * 128] = _dot_nt(qg, jnp.where(half[h], kg, jnp.zeros_like(kg)))

    lane_w = lax.broadcasted_iota(I32, (1, w), 1)
    hmask = [(lane_w // hd) == h for h in range(nh)]
    ones_blk = [m.astype(BF16) * jnp.ones((CHUNK, 1), BF16) for m in hmask]
    ones_cols = jnp.ones((CHUNK, w), BF16)
    zblk = jnp.zeros((hd, 128), F32)
    ngrp = w // 128
    m_prev = m_ref[...]
    cblk = [st_ref[h * hd:(h + 1) * hd, grp[h]] for h in range(nh)]
    nblk = [st_ref[h * hd:(h + 1) * hd, w + grp[h].start:w + grp[h].stop] for h in range(nh)]
    for c in range(nchunk):
        rows = slice(c * CHUNK, (c + 1) * CHUNK)
        t8 = slice(c * 8, (c + 1) * 8)
        r_c = rt_ref[t8, :]
        g_last = jnp.maximum(rm_ref[t8, :], m_prev)
        decay = jnp.exp(m_prev - g_last)
        mprev_l = lanes(m_prev)
        glast_l = lanes(g_last)
        gb64 = jnp.maximum(cr64_ref[rows, :], mprev_l)
        w_inter = jnp.exp(mprev_l - gb64)
        e_negm = jnp.exp(-(b64_ref[rows, :] + gb64))
        v_all = zc_ref[rows, 2 * w:3 * w]
        wcat = []
        vblocks = []
        srows = []
        for h in range(nh):
            gb128 = jnp.maximum(cr128_ref[rows, h * 128:(h + 1) * 128], m_prev[h:h + 1, :])
            d = jnp.exp(jnp.where(causal, r_c[h:h + 1, :] - gb128, NEG))
            wcat.append((d * s_ref[rows, h * 128:(h + 1) * 128]).astype(BF16))
            vblocks.append(jnp.concatenate([jnp.where(hmask[h], v_all, jnp.zeros_like(v_all)), ones_blk[h]], axis=1))
            g = (h * hd) // 128
            srows.append(jnp.concatenate([cblk[h] if j == g else zblk for j in range(ngrp)]
                                         + [nblk[h] if j == g else zblk for j in range(ngrp)], axis=1))
        s_bf = jnp.concatenate(srows, axis=0).astype(BF16)
        out_aug = _dot(jnp.concatenate(wcat, axis=1), jnp.concatenate(vblocks, axis=0))
        out_aug = out_aug + _dot((qk_ref[rows, 0:w] * w_inter).astype(BF16), s_bf)
        hh_ref[rows, :] = out_aug[:, 0:w] / jnp.maximum(jnp.abs(out_aug[:, w:2 * w]), e_negm)
        kw = (qk_ref[rows, w:2 * w] * jnp.exp(r64_ref[rows, :] - glast_l)).astype(BF16)
        upd = _dot_tn(kw, jnp.concatenate([v_all, ones_cols], axis=1))
        for h in range(nh):
            rs = slice(h * hd, (h + 1) * hd)
            cblk[h] = cblk[h] * decay[h:h + 1, :] + jnp.where(half[h], upd[rs, grp[h]], 0.0)
            nblk[h] = (nblk[h] * decay[h:h + 1, :]
                       + jnp.where(half[h], upd[rs, w + grp[h].start:w + grp[h].stop], 0.0))
        m_prev = bl_ref[t8, :] + g_last
    m_ref[...] = m_prev
    for h in range(nh):
        st_ref[h * hd:(h + 1) * hd, grp[h]] = cblk[h]
        st_ref[h * hd:(h + 1) * hd, w + grp[h].start:w + grp[h].stop] = nblk[h]

    hh = hh_ref[...]
    ms = _dot((hh * hh).astype(BF16), bones_ref[...])
    og = zc_ref[:, 3 * w:4 * w].astype(F32)
    o_ref[...] = (_sigmoid(og) * hh * lax.rsqrt(ms + EPS) * ng_ref[...]).astype(o_ref.dtype)


def _mlstm(zc3, zgt, conv_w, gate_b_rows, norm_g):
    b, s, w = zc3.shape
    ts = SEQ_TILE
    nt = s // ts
    gw = GROUP_W
    t = jnp.arange(CHUNK)
    triu = (t[:, None] <= t[None, :]).astype(BF16)
    hid = jnp.arange(gw) // HEAD_DIM
    bones = jnp.where(hid[:, None] == hid[None, :], 1.0 / HEAD_DIM, 0.0).astype(BF16)
    full = lambda a: pl.BlockSpec(a.shape, lambda bi, i: (0,) * a.ndim)
    return pl.pallas_call(
        _mlstm_kernel,
        grid=(b, nt),
        in_specs=[pl.BlockSpec((None, ts, w), lambda bi, i: (bi, i, 0)),
                  pl.BlockSpec((8, ts), lambda bi, i: (0, bi * nt + i)),
                  full(conv_w), full(gate_b_rows), full(norm_g), full(triu), full(bones)],
        out_specs=pl.BlockSpec((None, ts, gw), lambda bi, i: (bi, i, 0)),
        out_shape=jax.ShapeDtypeStruct((b, s, gw), BF16),
        scratch_shapes=[pltpu.VMEM((gw, 2 * gw), F32),
                        pltpu.VMEM((8, 128), F32),
                        pltpu.VMEM((8, 2 * gw), F32),
                        pltpu.VMEM((ts, 2 * gw), F32),
                        pltpu.VMEM((ts // CHUNK * 8, 128), F32),
                        pltpu.VMEM((ts // CHUNK * 8, 128), F32),
                        pltpu.VMEM((ts // CHUNK * 8, 128), F32),
                        pltpu.VMEM((ts, MLSTM_HEADS * 128), F32),
                        pltpu.VMEM((ts, gw), F32),
                        pltpu.VMEM((ts, gw), F32),
                        pltpu.VMEM((ts, gw), F32),
                        pltpu.VMEM((ts, MLSTM_HEADS * 128), F32),
                        pltpu.VMEM((ts, gw), F32)],
        compiler_params=_cparams("parallel", "arbitrary"),
        name="mlstm",
    )(zc3, zgt, conv_w, gate_b_rows, norm_g, triu, bones)


def _sgu_kernel(zd_ref, lng_ref, lnb_ref, w_ref, bias_ref, o_ref):
    tm = zd_ref.shape[0]
    gw = GROUP_W // SGU_GROUPS
    u = _gelu(zd_ref[:, 0:GROUP_W].astype(F32))
    v = _gelu(zd_ref[:, GROUP_W:2 * GROUP_W].astype(F32))
    mu = jnp.mean(v, axis=-1, keepdims=True)
    vc = v - mu
    var = jnp.mean(vc * vc, axis=-1, keepdims=True)
    vn = (vc * lax.rsqrt(var + EPS) * lng_ref[...] + lnb_ref[...]).astype(BF16)
    for c in range(tm // CHUNK):
        rows = slice(c * CHUNK, (c + 1) * CHUNK)
        mixed = jnp.concatenate([_dot(w_ref[g], vn[rows, g * gw:(g + 1) * gw]) for g in range(SGU_GROUPS)], axis=-1)
        o_ref[rows, :] = (u[rows, :] * (mixed + bias_ref[...])).astype(o_ref.dtype)


def _sgu(zd, ln_g, ln_b, w_tril, bias):
    n = zd.shape[0]
    tm = TOKEN_TILE
    full = lambda a: pl.BlockSpec(a.shape, lambda i: (0,) * a.ndim)
    return pl.pallas_call(
        _sgu_kernel,
        grid=(n // tm,),
        in_specs=[pl.BlockSpec((tm, 2 * GROUP_W), lambda i: (i, 0)), full(ln_g), full(ln_b), full(w_tril), full(bias)],
        out_specs=pl.BlockSpec((tm, GROUP_W), lambda i: (i, 0)),
        out_shape=jax.ShapeDtypeStruct((n, GROUP_W), BF16),
        compiler_params=_cparams("parallel"),
        name="sgu",
    )(zd, ln_g, ln_b, w_tril, bias)


def _out_proj_kernel(x_ref, ya_ref, yb_ref, yc_ref, yd_ref, gluw_ref, glub_ref, wo_ref, ng_ref,
                     rwh_ref, rwl_ref, rb_ref,
                     x1_ref, hx_ref, meta_ref, cnt_ref, carry_ref):
    tm = x_ref.shape[0]

    @pl.when(pl.program_id(0) == 0)
    def _():
        carry_ref[...] = jnp.zeros_like(carry_ref)

    yb = _gelu(yb_ref[...].astype(F32))
    yb = yb * _sigmoid(_dot(yb.astype(BF16), gluw_ref[...]) + glub_ref[...])
    mix = _dot(ya_ref[...], wo_ref[0:GROUP_W, :])
    mix = mix + _dot(yb.astype(BF16), wo_ref[GROUP_W:2 * GROUP_W, :])
    mix = mix + _dot(yc_ref[...], wo_ref[2 * GROUP_W:3 * GROUP_W, :])
    mix = mix + _dot(yd_ref[...], wo_ref[3 * GROUP_W:4 * GROUP_W, :])
    x1 = x_ref[...] + mix
    x1_ref[...] = x1
    h = _rms(x1, ng_ref[...])
    hx_ref[:, 0:PACK_W] = _pack_rows(h)

    hh, hl = _split_bf16(h)
    wh = rwh_ref[...]
    wl = rwl_ref[...]
    logits = _dot_nt(wh, hh) + _dot_nt(wh, hl) + _dot_nt(wl, hh) + rb_ref[...]
    gl = [logits[j:j + 1, :] for j in range(N_GROUPS)]
    gmax = functools.reduce(jnp.maximum, gl)
    gsel = jnp.full((1, tm), N_GROUPS - 1, I32)
    for j in range(N_GROUPS - 2, -1, -1):
        gsel = jnp.where(gl[j] == gmax, j, gsel)
    p_g = 1.0 / functools.reduce(jnp.add, [jnp.exp(v - gmax) for v in gl])
    e_in = []
    for i in range(EPG):
        v = logits[N_GROUPS + i:N_GROUPS + i + 1, :]
        for j in range(1, N_GROUPS):
            r = N_GROUPS + j * EPG + i
            v = jnp.where(gsel == j, logits[r:r + 1, :], v)
        e_in.append(v)
    v1 = functools.reduce(jnp.maximum, e_in)
    i1 = jnp.full((1, tm), EPG - 1, I32)
    for i in range(EPG - 2, -1, -1):
        i1 = jnp.where(e_in[i] == v1, i, i1)
    rest = [jnp.where(i1 == i, NEG, e_in[i]) for i in range(EPG)]
    v2 = functools.reduce(jnp.maximum, rest)
    i2 = jnp.full((1, tm), EPG - 1, I32)
    for i in range(EPG - 2, -1, -1):
        i2 = jnp.where((rest[i] == v2) & (i1 != i), i, i2)
    e2 = jnp.exp(v2 - v1)
    w1 = p_g / (1.0 + e2)
    w2 = p_g * e2 / (1.0 + e2)
    lo = jnp.minimum(i1, i2)
    hi = jnp.maximum(i1, i2)
    w_lo = jnp.where(i1 < i2, w1, w2)
    w_hi = jnp.where(i1 < i2, w2, w1)
    bucket = gsel * PAIRS + ((lo * (2 * EPG - 1 - lo)) >> 1) + (hi - lo - 1)

    kid = lax.broadcasted_iota(I32, (BUCKET_PAD, tm), 0)
    onehot = (kid == bucket).astype(F32)
    s_i = lax.broadcasted_iota(I32, (tm, tm), 0)
    t_i = lax.broadcasted_iota(I32, (tm, tm), 1)
    prefix = _dot(onehot.astype(BF16), (s_i <= t_i).astype(BF16))
    carry = carry_ref[...]
    rank = jnp.sum(onehot * (prefix - 1.0 + carry[:, 0:1]), axis=0, keepdims=True)
    carry = carry + prefix[:, tm - 1:tm]
    carry_ref[...] = carry
    cnt_ref[...] = carry
    meta_ref[...] = jnp.concatenate([bucket, rank.astype(I32), jnp.zeros((6, tm), I32)], axis=0)
    wrows = jnp.concatenate([w_lo, w_hi, jnp.zeros((126, tm), F32)], axis=0)
    for c in range(tm // 128):
        hx_ref[c * 128:(c + 1) * 128, PACK_W:ROW_W] = pltpu.bitcast(wrows[:, c * 128:(c + 1) * 128].T, U32)


def _out_proj(x2, ya, yb, yc, yd, glu_w, glu_b, w_out, norm_g, rw_hi, rw_lo, rb):
    n = x2.shape[0]
    tm = TOKEN_TILE
    row = lambda w: pl.BlockSpec((tm, w), lambda i: (i, 0))
    full = lambda a: pl.BlockSpec(a.shape, lambda i: (0,) * a.ndim)
    return pl.pallas_call(
        _out_proj_kernel,
        grid=(n // tm,),
        in_specs=[row(D_MODEL), row(GROUP_W), row(GROUP_W), row(GROUP_W), row(GROUP_W),
                  full(glu_w), full(glu_b), full(w_out), full(norm_g), full(rw_hi), full(rw_lo), full(rb)],
        out_specs=[row(D_MODEL), row(ROW_W), pl.BlockSpec((8, tm), lambda i: (0, i)),
                   pl.BlockSpec((BUCKET_PAD, 128), lambda i: (0, 0))],
        out_shape=[jax.ShapeDtypeStruct((n, D_MODEL), F32), jax.ShapeDtypeStruct((n, ROW_W), U32),
                   jax.ShapeDtypeStruct((8, n), I32), jax.ShapeDtypeStruct((BUCKET_PAD, 128), F32)],
        scratch_shapes=[pltpu.VMEM((BUCKET_PAD, 128), F32)],
        compiler_params=_cparams("arbitrary"),
        name="out_proj",
    )(x2, ya, yb, yc, yd, glu_w, glu_b, w_out, norm_g, rw_hi, rw_lo, rb)


def _row_copy(src_ref, src_row, dst_ref, dst_row, sem):
    return pltpu.make_async_copy(src_ref.at[pl.ds(src_row, 1)], dst_ref.at[pl.ds(dst_row, 1)], sem)


def _dispatch_kernel(dest_ref, hx_ref, xs_in_ref, xs_ref, sem):
    del xs_in_ref
    tm = hx_ref.shape[0]

    def start(t, c):
        _row_copy(hx_ref, t, xs_ref, dest_ref[0, 0, t], sem).start()
        return c

    lax.fori_loop(0, tm, start, 0, unroll=ROW_DMA_UNROLL)
    pltpu.make_async_copy(hx_ref, xs_ref.at[pl.ds(0, tm)], sem).wait()


def _dispatch(dest3, hx, p_tot):
    n = hx.shape[0]
    tm = TOKEN_TILE
    xs0 = jnp.zeros((p_tot, ROW_W), U32)
    return pl.pallas_call(
        _dispatch_kernel,
        grid=(n // tm,),
        in_specs=[pl.BlockSpec((1, 1, tm), lambda i: (i, 0, 0), memory_space=pltpu.SMEM),
                  pl.BlockSpec((tm, ROW_W), lambda i: (i, 0)),
                  pl.BlockSpec(memory_space=pl.ANY)],
        out_specs=pl.BlockSpec(memory_space=pl.ANY),
        out_shape=jax.ShapeDtypeStruct((p_tot, ROW_W), U32),
        input_output_aliases={2: 0},
        scratch_shapes=[pltpu.SemaphoreType.DMA],
        compiler_params=_cparams("arbitrary"),
        name="dispatch",
    )(dest3, hx, xs0)


def _ffn_kernel(ea_ref, eb_ref, valid_ref, xblk_ref, xs_ref, wg_a_ref, wu_a_ref, wd_a_ref, wg_b_ref, wu_b_ref,
                wd_b_ref, ys_ref, wgu_a, wdn_a, wgu_b, wdn_b):
    del xblk_ref
    j = pl.program_id(0)
    valid = valid_ref[j]
    jp = jnp.maximum(j - 1, 0)

    @pl.when((j == 0) | (ea_ref[j] != ea_ref[jp]))
    def _():
        wgu_a[:, 0:D_EXPERT] = wg_a_ref[...].astype(BF16)
        wgu_a[:, D_EXPERT:2 * D_EXPERT] = wu_a_ref[...].astype(BF16)
        wdn_a[...] = wd_a_ref[...].astype(BF16)

    @pl.when((j == 0) | (eb_ref[j] != eb_ref[jp]))
    def _():
        wgu_b[:, 0:D_EXPERT] = wg_b_ref[...].astype(BF16)
        wgu_b[:, D_EXPERT:2 * D_EXPERT] = wu_b_ref[...].astype(BF16)
        wdn_b[...] = wd_b_ref[...].astype(BF16)

    for sb in range(FFN_BLOCK // FFN_SUB):
        rows = slice(sb * FFN_SUB, (sb + 1) * FFN_SUB)

        @pl.when(valid <= sb * FFN_SUB)
        def _():
            ys_ref[rows, :] = jnp.zeros((FFN_SUB, PACK_W), U32)

        @pl.when(valid > sb * FFN_SUB)
        def _():
            x_lo, x_hi = _unpack_rows(xs_ref[rows, 0:PACK_W])
            xb = jnp.concatenate([x_lo.astype(BF16), x_hi.astype(BF16)], axis=1)
            wts = pltpu.bitcast(xs_ref[rows, PACK_W:ROW_W], F32)

            def expert(wgu, wdn):
                gu = _dot(xb, wgu[...])
                g = gu[:, 0:D_EXPERT]
                act = (g * _sigmoid(g) * gu[:, D_EXPERT:2 * D_EXPERT]).astype(BF16)
                return _dot(act, wdn[...])

            ya = expert(wgu_a, wdn_a)
            yb = expert(wgu_b, wdn_b)
            ys_ref[rows, :] = _pack_rows(ya * wts[:, 0:1] + yb * wts[:, 1:2])


def _ffn(ea, eb, valid, xblk, xs, layer, w_gate, w_up, w_down):
    p_tot = xs.shape[0]
    bm = FFN_BLOCK
    nblk = p_tot // bm
    wspec = lambda e_idx, shape: pl.BlockSpec((None, None) + shape,
                                              lambda j, ea, eb, va, xb: (layer, (ea, eb)[e_idx][j], 0, 0))
    up_shape = (D_MODEL, D_EXPERT)
    dn_shape = (D_EXPERT, D_MODEL)
    return pl.pallas_call(
        _ffn_kernel,
        grid_spec=pltpu.PrefetchScalarGridSpec(
            num_scalar_prefetch=4,
            grid=(nblk,),
            in_specs=[pl.BlockSpec((bm, ROW_W), lambda j, ea, eb, va, xb: (xb[j], 0)),
                      wspec(0, up_shape), wspec(0, up_shape), wspec(0, dn_shape),
                      wspec(1, up_shape), wspec(1, up_shape), wspec(1, dn_shape)],
            out_specs=pl.BlockSpec((bm, PACK_W), lambda j, ea, eb, va, xb: (j, 0)),
            scratch_shapes=[pltpu.VMEM((D_MODEL, 2 * D_EXPERT), BF16), pltpu.VMEM(dn_shape, BF16),
                            pltpu.VMEM((D_MODEL, 2 * D_EXPERT), BF16), pltpu.VMEM(dn_shape, BF16)],
        ),
        out_shape=jax.ShapeDtypeStruct((p_tot, PACK_W), U32),
        compiler_params=_cparams("arbitrary"),
        name="ffn",
    )(ea, eb, valid, xblk, xs, w_gate, w_up, w_down, w_gate, w_up, w_down)


def _combine_kernel(dest_ref, x1_ref, ys_ref, ng_ref, o_ref, buf_ref, sem, *, final_norm):
    tm = x1_ref.shape[0]

    def start(t, c):
        _row_copy(ys_ref, dest_ref[0, 0, t], buf_ref, t, sem).start()
        return c

    lax.fori_loop(0, tm, start, 0, unroll=ROW_DMA_UNROLL)
    pltpu.make_async_copy(ys_ref.at[pl.ds(0, tm)], buf_ref, sem).wait()
    y_lo, y_hi = _unpack_rows(buf_ref[...])
    x2 = x1_ref[...] + jnp.concatenate([y_lo, y_hi], axis=1)
    o_ref[...] = _rms(x2, ng_ref[...]) if final_norm else x2


def _combine(dest3, x1, ys, norm_g, final_norm):
    n = x1.shape[0]
    tm = TOKEN_TILE
    return pl.pallas_call(
        functools.partial(_combine_kernel, final_norm=final_norm),
        grid=(n // tm,),
        in_specs=[pl.BlockSpec((1, 1, tm), lambda i: (i, 0, 0), memory_space=pltpu.SMEM),
                  pl.BlockSpec((tm, D_MODEL), lambda i: (i, 0)),
                  pl.BlockSpec(memory_space=pl.ANY),
                  pl.BlockSpec((1, D_MODEL), lambda i: (0, 0))],
        out_specs=pl.BlockSpec((tm, D_MODEL), lambda i: (i, 0)),
        out_shape=jax.ShapeDtypeStruct((n, D_MODEL), F32),
        scratch_shapes=[pltpu.VMEM((tm, PACK_W), U32), pltpu.SemaphoreType.DMA],
        compiler_params=_cparams("arbitrary"),
        name="combine",
    )(dest3, x1, ys, norm_g)


def _routing_tables(meta, counts, n):
    bm = FFN_BLOCK
    nblk = n // bm + N_BUCKETS
    bucket = meta[0]
    rank = meta[1]
    cnt = counts[:N_BUCKETS, 0].astype(I32)
    padded = ((cnt + bm - 1) // bm) * bm
    pad_end = jnp.cumsum(padded)
    pad_start = pad_end - padded
    onehot = bucket[:, None] == jnp.arange(N_BUCKETS, dtype=I32)[None, :]
    dest = rank + jnp.sum(jnp.where(onehot, pad_start[None, :], 0), axis=1)
    nact = (pad_end[-1] // bm).astype(I32)
    j = jnp.arange(nblk, dtype=I32)
    blk = jnp.minimum(j, jnp.maximum(nact - 1, 0))
    pos = blk * bm
    bkt = jnp.minimum(jnp.sum((pad_end[None, :] <= pos[:, None]).astype(I32), axis=1), N_BUCKETS - 1)
    in_bkt = bkt[:, None] == jnp.arange(N_BUCKETS, dtype=I32)[None, :]
    sel = lambda tab: jnp.sum(jnp.where(in_bkt, tab[None, :], 0), axis=1)
    valid = jnp.where(j < nact, jnp.clip(sel(cnt) - (pos - sel(pad_start)), 0, bm), 0)
    lo_tab, hi_tab = [], []
    for g in range(N_GROUPS):
        for a in range(EPG):
            for b in range(a + 1, EPG):
                lo_tab.append(g * EPG + a)
                hi_tab.append(g * EPG + b)
    ea = sel(jnp.asarray(lo_tab, I32))
    eb = sel(jnp.asarray(hi_tab, I32))
    return dest.astype(I32), ea.astype(I32), eb.astype(I32), valid.astype(I32), blk.astype(I32), nblk * bm


def _layer(x2, b, s, p, final_g):
    n = b * s
    za, zb, zc, zd, zgt = _in_proj(x2, p['norm_mix_g'], p['wa'], p['wb'], p['wc'], p['wd'], p['wgt'])
    ya = _attention(za.reshape(b, s, -1), p['sinks']).reshape(n, GROUP_W)
    yb = _ssm(zb.reshape(b, s, GROUP_W), p['ssm']).reshape(n, GROUP_W)
    yc = _mlstm(zc.reshape(b, s, -1), zgt, p['conv_w'], p['gate_b'], p['mlstm_norm_g']).reshape(n, GROUP_W)
    yd = _sgu(zd, p['sgu_ln_g'], p['sgu_ln_b'], p['sgu_w'], p['sgu_bias'])
    x1, hx, meta, counts = _out_proj(x2, ya, yb, yc, yd, p['glu_w'], p['glu_b'], p['w_out'], p['norm_ffn_g'],
                                     p['rw_hi'], p['rw_lo'], p['rb'])
    dest, ea, eb, valid, xblk, p_tot = _routing_tables(meta, counts, n)
    dest3 = dest.reshape(n // TOKEN_TILE, 1, TOKEN_TILE)
    xs = _dispatch(dest3, hx, p_tot)
    ys = _ffn(ea, eb, valid, xblk, xs, p['layer'], p['w_gate'], p['w_up'], p['w_down'])
    if final_g is None:
        return _combine(dest3, x1, ys, p['norm_ffn_g'], False)
    return _combine(dest3, x1, ys, final_g, True)


def _prep_layer(l, norm_mix_g, w_in, attn_sinks, ssm_a_re, ssm_a_im, ssm_b_re, ssm_b_im, ssm_c_re, ssm_c_im,
                ssm_d, ssm_log_dt, ssm_glu_w, ssm_glu_b, mlstm_conv_w, mlstm_gate_b, mlstm_norm_g,
                sgu_ln_g, sgu_ln_b, sgu_w, sgu_b, w_out, norm_ffn_g, router_group_w, router_group_b,
                router_expert_w, router_expert_b, expert_w_gate, expert_w_up, expert_w_down):
    w = w_in[l]
    o_su = 2 * GROUP_W
    o_c = o_su + GROUP_W
    o_ci = o_c + 3 * GROUP_W
    o_co = o_ci + 2 * MLSTM_HEADS
    o_d = o_co + GROUP_W
    wc = jnp.concatenate([w[:, o_c:o_ci], w[:, o_co:o_d]], axis=1)
    wgt = w[:, o_ci:o_co].T
    rw = jnp.concatenate([router_group_w[l], router_expert_w[l]], axis=1).T.astype(F32)
    rw = jnp.pad(rw, ((0, 4), (0, 0)))
    rw_hi = rw.astype(BF16)
    rw_lo = (rw - rw_hi.astype(F32)).astype(BF16)
    rb = jnp.pad(jnp.concatenate([router_group_b[l], router_expert_b[l]]).astype(F32), (0, 4))[:, None]
    tril = jnp.tril(jnp.ones((CHUNK, CHUNK), F32))
    gw = GROUP_W // SGU_GROUPS
    return dict(
        norm_mix_g=norm_mix_g[l][None, :].astype(F32),
        wa=w[:, 0:o_su].astype(BF16), wb=w[:, o_su:o_c].astype(BF16), wc=wc.astype(BF16),
        wd=w[:, o_d:].astype(BF16), wgt=wgt.astype(BF16),
        sinks=attn_sinks[l].astype(F32),
        ssm=_ssm_matrices(ssm_a_re[l], ssm_a_im[l], ssm_b_re[l], ssm_b_im[l], ssm_c_re[l], ssm_c_im[l],
                          ssm_d[l], ssm_log_dt[l]),
        glu_w=ssm_glu_w[l].astype(BF16), glu_b=ssm_glu_b[l][None, :].astype(F32),
        conv_w=mlstm_conv_w[l].astype(F32),
        gate_b=jnp.broadcast_to(mlstm_gate_b[l].astype(F32)[:, None], (2 * MLSTM_HEADS, 128)),
        mlstm_norm_g=mlstm_norm_g[l][None, :].astype(F32),
        sgu_ln_g=sgu_ln_g[l][None, :].astype(F32), sgu_ln_b=sgu_ln_b[l][None, :].astype(F32),
        sgu_w=(sgu_w[l].astype(F32) * tril).astype(BF16),
        sgu_bias=jnp.repeat(sgu_b[l].astype(F32).T, gw, axis=1),
        w_out=w_out[l].astype(BF16), norm_ffn_g=norm_ffn_g[l][None, :].astype(F32),
        rw_hi=rw_hi, rw_lo=rw_lo, rb=rb,
        layer=l, w_gate=expert_w_gate, w_up=expert_w_up, w_down=expert_w_down,
    )


def kernel(x, norm_mix_g, w_in, attn_sinks, ssm_a_re, ssm_a_im, ssm_b_re, ssm_b_im, ssm_c_re, ssm_c_im, ssm_d, ssm_log_dt, ssm_glu_w, ssm_glu_b, mlstm_conv_w, mlstm_gate_b, mlstm_norm_g, sgu_ln_g, sgu_ln_b, sgu_w, sgu_b, w_out, norm_ffn_g, router_group_w, router_group_b, router_expert_w, router_expert_b, expert_w_gate, expert_w_up, expert_w_down, norm_final_g):
    b, s, d = x.shape
    depth = w_in.shape[0]
    x2 = x.reshape(b * s, d).astype(F32)
    for l in range(depth):
        p = _prep_layer(l, norm_mix_g, w_in, attn_sinks, ssm_a_re, ssm_a_im, ssm_b_re, ssm_b_im, ssm_c_re,
                        ssm_c_im, ssm_d, ssm_log_dt, ssm_glu_w, ssm_glu_b, mlstm_conv_w, mlstm_gate_b,
                        mlstm_norm_g, sgu_ln_g, sgu_ln_b, sgu_w, sgu_b, w_out, norm_ffn_g, router_group_w,
                        router_group_b, router_expert_w, router_expert_b, expert_w_gate, expert_w_up,
                        expert_w_down)
        final_g = norm_final_g[None, :].astype(F32) if l == depth - 1 else None
        x2 = _layer(x2, b, s, p, final_g)
    return x2.reshape(b, s, d).astype(x.dtype)
```

```python
import functools
import math

import jax
import jax.numpy as jnp
from jax import lax
from jax.experimental import pallas as pl
from jax.experimental.pallas import tpu as pltpu

F32 = jnp.float32
BF16 = jnp.bfloat16
I32 = jnp.int32
U32 = jnp.uint32

D_MODEL = 1024
GROUP_W = 256
HEAD_DIM = 64
EPS = 1e-6
NEG = -1e30
WINDOW = 128
KV_HEADS = 2
ATTN_REP = 2
SSM_GROUP = 16
SSM_GROUPS = 16
SSM_STATE = 64
SSM_CHUNK = 16
MLSTM_HEADS = 4
CHUNK = 128
CONV_K = 4
SGU_GROUPS = 4
N_GROUPS = 4
EPG = 8
N_EXPERTS = 32
D_EXPERT = 512
PAIRS = EPG * (EPG - 1) // 2
N_BUCKETS = N_GROUPS * PAIRS
BUCKET_PAD = 128
PACK_W = D_MODEL // 2
ROW_W = PACK_W + 128

TOKEN_TILE = 512
SEQ_TILE = 512
SSM_TILE = 4096
SSM_SLAB = 64
FFN_BLOCK = 512
FFN_SUB = 256
ROW_DMA_UNROLL = 8
VMEM_LIMIT = 56 * 1024 * 1024


def _cparams(*sem):
    return pltpu.CompilerParams(dimension_semantics=sem, vmem_limit_bytes=VMEM_LIMIT)


def _rms(x, g):
    return x * lax.rsqrt(jnp.mean(x * x, axis=-1, keepdims=True) + EPS) * g


def _gelu(x):
    return 0.5 * x * (1.0 + jnp.tanh(math.sqrt(2.0 / math.pi) * (x + 0.044715 * (x * x * x))))


def _sigmoid(x):
    return 1.0 / (1.0 + jnp.exp(-x))


def _log_sigmoid(x):
    return jnp.minimum(x, 0.0) - jnp.log(1.0 + jnp.exp(-jnp.abs(x)))


def _dot(a, b):
    return jnp.dot(a, b, preferred_element_type=F32)


def _dot_nt(a, b):
    return lax.dot_general(a, b, (((1,), (1,)), ((), ())), preferred_element_type=F32)


def _dot_tn(a, b):
    return lax.dot_general(a, b, (((0,), (0,)), ((), ())), preferred_element_type=F32)


def _split_bf16(x):
    hi = x.astype(BF16)
    lo = (x - hi.astype(F32)).astype(BF16)
    return hi, lo


def _pack_rows(x):
    w = x.shape[1] // 2
    bits = lambda v: pltpu.bitcast(v.astype(BF16).astype(F32), U32)
    return (bits(x[:, 0:w]) >> 16) | (bits(x[:, w:2 * w]) & jnp.uint32(0xFFFF0000))


def _unpack_rows(p):
    return pltpu.bitcast(p << 16, F32), pltpu.bitcast(p & jnp.uint32(0xFFFF0000), F32)


def _in_proj_kernel(x_ref, g_ref, wa_ref, wb_ref, wc_ref, wd_ref, wgt_ref,
                    za_ref, zb_ref, zc_ref, zd_ref, zgt_ref):
    hb = _rms(x_ref[...], g_ref[...]).astype(BF16)
    za_ref[...] = _dot(hb, wa_ref[...]).astype(BF16)
    zb_ref[...] = _dot(hb, wb_ref[...]).astype(BF16)
    zc_ref[...] = _dot(hb, wc_ref[...]).astype(BF16)
    zd_ref[...] = _dot(hb, wd_ref[...]).astype(BF16)
    zgt_ref[...] = _dot_nt(wgt_ref[...], hb)


def _in_proj(x2, g, wa, wb, wc, wd, wgt):
    n = x2.shape[0]
    tm = TOKEN_TILE
    row = lambda w: pl.BlockSpec((tm, w), lambda i: (i, 0))
    full = lambda a: pl.BlockSpec(a.shape, lambda i: (0,) * a.ndim)
    widths = (wa.shape[1], wb.shape[1], wc.shape[1], wd.shape[1])
    return pl.pallas_call(
        _in_proj_kernel,
        grid=(n // tm,),
        in_specs=[row(D_MODEL), full(g), full(wa), full(wb), full(wc), full(wd), full(wgt)],
        out_specs=[row(w) for w in widths] + [pl.BlockSpec((wgt.shape[0], tm), lambda i: (0, i))],
        out_shape=[jax.ShapeDtypeStruct((n, w), BF16) for w in widths]
        + [jax.ShapeDtypeStruct((wgt.shape[0], n), F32)],
        compiler_params=_cparams("parallel"),
        name="in_proj",
    )(x2, g, wa, wb, wc, wd, wgt)


def _attn_kernel(sink_ref, cur_ref, prev_ref, o_ref):
    first = pl.program_id(1) == 0
    nblk = cur_ref.shape[0] // WINDOW
    row = lax.broadcasted_iota(I32, (WINDOW, 2 * WINDOW), 0)
    col = lax.broadcasted_iota(I32, (WINDOW, 2 * WINDOW), 1)
    band = (col <= row + WINDOW) & (col > row)
    lane = lax.broadcasted_iota(I32, (1, 128), 1)
    kv_lanes = [(lane // HEAD_DIM) == g for g in range(KV_HEADS)]
    ones = jnp.ones((2 * WINDOW, 128), BF16)
    ko = 2 * 128
    vo = 3 * 128
    for j in range(nblk):
        cur = cur_ref[j * WINDOW:(j + 1) * WINDOW, :]
        if j == 0:
            prev = prev_ref[...]
            mask = band & ((col >= WINDOW) | jnp.logical_not(first))
        else:
            prev = cur_ref[(j - 1) * WINDOW:j * WINDOW, :]
            mask = band
        kk = jnp.concatenate([prev[:, ko:ko + 128], cur[:, ko:ko + 128]], axis=0)
        vaug = jnp.concatenate([jnp.concatenate([prev[:, vo:vo + 128], cur[:, vo:vo + 128]], axis=0), ones], axis=1)
        outs = []
        for r in range(ATTN_REP):
            qg = cur[:, r * 128:(r + 1) * 128]
            og = []
            for g in range(KV_HEADS):
                s = _dot_nt(jnp.where(kv_lanes[g], qg, jnp.zeros_like(qg)), kk)
                s = jnp.where(mask, s, NEG)
                sink = sink_ref[g * ATTN_REP + r]
                m = jnp.maximum(jnp.max(s, axis=-1, keepdims=True), sink)
                oa = _dot(jnp.exp(s - m).astype(BF16), vaug)
                og.append(oa[:, 0:128] / (oa[:, 128:256] + jnp.exp(sink - m)))
            outs.append(jnp.where(kv_lanes[0], og[0], og[1]))
        o_ref[j * WINDOW:(j + 1) * WINDOW, :] = jnp.concatenate(outs, axis=-1).astype(o_ref.dtype)


def _attention(za3, sinks):
    b, s, w = za3.shape
    ts = SEQ_TILE
    per = ts // WINDOW
    return pl.pallas_call(
        _attn_kernel,
        grid=(b, s // ts),
        in_specs=[pl.BlockSpec(memory_space=pltpu.SMEM),
                  pl.BlockSpec((None, ts, w), lambda bi, i: (bi, i, 0)),
                  pl.BlockSpec((None, WINDOW, w), lambda bi, i: (bi, jnp.maximum(i * per - 1, 0), 0))],
        out_specs=pl.BlockSpec((None, ts, GROUP_W), lambda bi, i: (bi, i, 0)),
        out_shape=jax.ShapeDtypeStruct((b, s, GROUP_W), BF16),
        compiler_params=_cparams("parallel", "parallel"),
        name="attn",
    )(sinks, za3, za3)


def _ssm_kernel(zb_ref, bre_ref, bim_ref, t_ref, cre_ref, cim_ref, are_ref, aim_ref, o_ref,
                x_ref, xs_ref, u_ref, vre_ref, vim_ref, sre_ref, sim_ref, y_ref, st_ref):
    ts = zb_ref.shape[0]
    L, G, H = SSM_CHUNK, SSM_GROUPS, SSM_GROUP
    nch = ts // L
    half_g = 128 // H
    n_half = G // half_g

    @pl.when(pl.program_id(1) == 0)
    def _():
        st_ref[...] = jnp.zeros_like(st_ref)

    blk = lax.broadcasted_iota(I32, (1, 128), 1) // H

    def block_transpose(arrs):
        a = list(arrs)
        s = half_g // 2
        while s >= 1:
            keep = (blk & s) == 0
            for i in range(half_g):
                if i & s == 0:
                    ai, aj = a[i], a[i + s]
                    a[i] = jnp.where(keep, ai, pltpu.roll(aj, s * H, axis=1))
                    a[i + s] = jnp.where(keep, pltpu.roll(ai, 128 - s * H, axis=1), aj)
            s //= 2
        return a

    for hf in range(n_half):
        x_ref[hf] = zb_ref[:, hf * 128:(hf + 1) * 128].astype(F32)
    for sg in range(L):
        for hf in range(n_half):
            xs_ref[sg * n_half + hf] = x_ref[hf, pl.ds(sg, nch, stride=L), :]
    for hf in range(n_half):
        for oc in range(L // half_g):
            for r0 in range(0, nch, SSM_SLAB):
                rows = slice(r0, r0 + SSM_SLAB)
                t = block_transpose([xs_ref[(oc * half_g + k) * n_half + hf, rows, :] for k in range(half_g)])
                for gl in range(half_g):
                    u_ref[hf * half_g + gl, rows, oc * 128:(oc + 1) * 128] = t[gl].astype(BF16)
    for g in range(G):
        ug = u_ref[g]
        vre_ref[pl.ds(g, nch, stride=G), :] = _dot(ug, bre_ref[g])
        vim_ref[pl.ds(g, nch, stride=G), :] = _dot(ug, bim_ref[g])

    are = are_ref[...]
    aim = aim_ref[...]

    def step(c, carry):
        sre, sim = carry
        r0 = pl.multiple_of(c * G, G)
        sre_ref[pl.ds(r0, G), :] = sre
        sim_ref[pl.ds(r0, G), :] = sim
        vre = vre_ref[pl.ds(r0, G), :]
        vim = vim_ref[pl.ds(r0, G), :]
        return (are * sre - aim * sim + vre, are * sim + aim * sre + vim)

    sre, sim = lax.fori_loop(0, nch, step, (st_ref[0:G, :], st_ref[G:2 * G, :]), unroll=4)
    st_ref[0:G, :] = sre
    st_ref[G:2 * G, :] = sim

    for g in range(G):
        y = _dot(u_ref[g], t_ref[g])
        y = y + _dot(sre_ref[pl.ds(g, nch, stride=G), :].astype(BF16), cre_ref[g])
        y = y + _dot(sim_ref[pl.ds(g, nch, stride=G), :].astype(BF16), cim_ref[g])
        y_ref[g] = y
    for hf in range(n_half):
        for oc in range(L // half_g):
            for r0 in range(0, nch, SSM_SLAB):
                t = block_transpose([y_ref[hf * half_g + gl, r0:r0 + SSM_SLAB, oc * 128:(oc + 1) * 128]
                                     for gl in range(half_g)])
                for k in range(half_g):
                    x_ref[hf, pl.ds(oc * half_g + k + L * r0, SSM_SLAB, stride=L), :] = t[k]
    for hf in range(n_half):
        o_ref[:, hf * 128:(hf + 1) * 128] = x_ref[hf].astype(o_ref.dtype)


def _ssm_matrices(a_re, a_im, b_re, b_im, c_re, c_im, d_skip, log_dt):
    L = SSM_CHUNK
    a = lax.complex(a_re.astype(F32), a_im.astype(F32))
    dt = jnp.exp(log_dt.astype(F32))[:, None]
    adt = a * dt
    a_bar = jnp.exp(adt)
    b_bar = ((a_bar - 1.0) / a)[..., None] * lax.complex(b_re.astype(F32), b_im.astype(F32))
    c_mat = lax.complex(c_re.astype(F32), c_im.astype(F32))
    lag = jnp.arange(L + 1, dtype=F32)
    pw = jnp.exp(adt[None] * lag[:, None, None])
    kern = jnp.einsum('gop,dgp,gpi->dgoi', c_mat, pw[:L], b_bar).real
    sig = jnp.arange(L)[:, None]
    tau = jnp.arange(L)[None, :]
    d = tau - sig
    kt = kern[jnp.clip(d, 0, L - 1)]
    kt = jnp.where((d >= 0)[:, :, None, None, None], kt, 0.0)
    eye = jnp.eye(SSM_GROUP, dtype=F32)
    dsk = d_skip.astype(F32).reshape(SSM_GROUPS, SSM_GROUP)
    kt = kt + (d == 0)[:, :, None, None, None] * (dsk[:, :, None] * eye)[None, None]
    t_mat = kt.transpose(2, 0, 4, 1, 3).reshape(SSM_GROUPS, L * SSM_GROUP, L * SSM_GROUP)
    bm = pw[:L][::-1][:, :, :, None] * b_bar[None]
    bm = bm.transpose(1, 0, 3, 2).reshape(SSM_GROUPS, L * SSM_GROUP, SSM_STATE)
    cm = c_mat[None] * pw[1:][:, :, None, :]
    cm = cm.transpose(1, 3, 0, 2).reshape(SSM_GROUPS, SSM_STATE, L * SSM_GROUP)
    a_chunk = pw[L]
    pad = 128 - SSM_STATE
    pc = lambda m: jnp.pad(m, ((0, 0), (0, 0), (0, pad))).astype(BF16)
    pr = lambda m: jnp.pad(m, ((0, 0), (0, pad), (0, 0))).astype(BF16)
    pa = lambda m: jnp.pad(m, ((0, 0), (0, pad)))
    return (pc(bm.real), pc(bm.imag), t_mat.astype(BF16), pr(cm.real), pr(-cm.imag),
            pa(a_chunk.real), pa(a_chunk.imag))


def _ssm(zb3, mats):
    b, s, w = zb3.shape
    L, G = SSM_CHUNK, SSM_GROUPS
    ts = min(SSM_TILE, s)
    nch = ts // L
    full = lambda a: pl.BlockSpec(a.shape, lambda bi, i: (0,) * a.ndim)
    return pl.pallas_call(
        _ssm_kernel,
        grid=(b, s // ts),
        in_specs=[pl.BlockSpec((None, ts, w), lambda bi, i: (bi, i, 0))] + [full(m) for m in mats],
        out_specs=pl.BlockSpec((None, ts, w), lambda bi, i: (bi, i, 0)),
        out_shape=jax.ShapeDtypeStruct((b, s, w), BF16),
        scratch_shapes=[pltpu.VMEM((w // 128, ts, 128), F32),
                        pltpu.VMEM((L * (w // 128), nch, 128), F32),
                        pltpu.VMEM((G, nch, L * SSM_GROUP), BF16),
                        pltpu.VMEM((nch * G, 128), F32), pltpu.VMEM((nch * G, 128), F32),
                        pltpu.VMEM((nch * G, 128), F32), pltpu.VMEM((nch * G, 128), F32),
                        pltpu.VMEM((G, nch, L * SSM_GROUP), F32),
                        pltpu.VMEM((2 * G, 128), F32)],
        compiler_params=_cparams("parallel", "arbitrary"),
        name="ssm",
    )(zb3, *mats)


def _mlstm_kernel(zc_ref, zgt_ref, convw_ref, gb_ref, ng_ref, triu_ref, bones_ref, o_ref,
                  st_ref, m_ref, tail_ref, qk_ref,
                  rt_ref, bl_ref, rm_ref, cr128_ref, cr64_ref, b64_ref, r64_ref, s_ref, hh_ref):
    ts = zc_ref.shape[0]
    hd, nh, w = HEAD_DIM, MLSTM_HEADS, GROUP_W
    nchunk = ts // CHUNK
    nr = nchunk * 8
    lane1 = lax.broadcasted_iota(I32, (1, 128), 1)
    half = [(lane1 // hd) == (h % 2) for h in range(nh)]
    grp = [slice((h * hd) // 128 * 128, (h * hd) // 128 * 128 + 128) for h in range(nh)]

    @pl.when(pl.program_id(1) == 0)
    def _():
        st_ref[...] = jnp.zeros_like(st_ref)
        m_ref[...] = jnp.zeros_like(m_ref)
        tail_ref[...] = jnp.zeros_like(tail_ref)

    r_i = lax.broadcasted_iota(I32, (CHUNK, CHUNK), 0)
    c_i = lax.broadcasted_iota(I32, (CHUNK, CHUNK), 1)
    causal = c_i <= r_i
    k64 = lax.broadcasted_iota(I32, (8, w), 0)
    j64 = lax.broadcasted_iota(I32, (8, w), 1)
    sel64 = (j64 // hd == k64 % nh).astype(BF16)
    k128 = lax.broadcasted_iota(I32, (8, nh * 128), 0)
    j128 = lax.broadcasted_iota(I32, (8, nh * 128), 1)
    sel128 = (j128 // 128 == k128 % nh).astype(BF16)

    def lanes(t):
        lo = jnp.where(lane1 < hd, t[0:1, :], t[1:2, :])
        hi = jnp.where(lane1 < hd, t[2:3, :], t[3:4, :])
        return jnp.concatenate([lo, hi], axis=1)

    rowm = lax.broadcasted_iota(I32, (nr, 128), 0) % 8
    lanem = lax.broadcasted_iota(I32, (nr, 128), 1)
    graw = jnp.concatenate([zgt_ref[:, c * CHUNK:(c + 1) * CHUNK] + gb_ref[...] for c in range(nchunk)], axis=0)
    g2 = jnp.where(rowm < nh, graw, _log_sigmoid(graw))
    ghi, glo = _split_bf16(g2)
    cum = _dot(ghi, triu_ref[...]) + _dot(glo, triu_ref[...])
    b_t = pltpu.roll(cum, nr - 4, axis=0)
    r_t = g2 - b_t
    cr = r_t
    sh = 1
    while sh < CHUNK:
        cr = jnp.maximum(cr, jnp.where(lanem >= sh, pltpu.roll(cr, sh, axis=1), NEG))
        sh *= 2
    rt_ref[...] = r_t
    bl_ref[...] = jnp.broadcast_to(b_t[:, CHUNK - 1:CHUNK], (nr, 128))
    rm_ref[...] = jnp.broadcast_to(cr[:, CHUNK - 1:CHUNK], (nr, 128))

    x = zc_ref[:, 0:2 * w].astype(F32)
    xe = jnp.concatenate([tail_ref[...], x], axis=0)
    tail_ref[...] = x[ts - 8:, :]
    cw = convw_ref[...]
    acc = x * cw[CONV_K - 1:CONV_K, :]
    for sft in range(1, CONV_K):
        acc = acc + xe[8 - sft:8 - sft + ts, :] * cw[CONV_K - 1 - sft:CONV_K - sft, :]
    lane_qk = lax.broadcasted_iota(I32, (1, 2 * w), 1)
    qk_ref[...] = acc * _sigmoid(acc) * jnp.where(lane_qk < w, 1.0, hd ** -0.5)

    def hi_lo_rows(v):
        hi = v.astype(BF16).astype(F32)
        return jnp.where(rowm < nh, hi, pltpu.roll(v - hi, 4, axis=0))

    a_cr = hi_lo_rows(cr)
    a_b = hi_lo_rows(b_t)
    a_r = hi_lo_rows(r_t)
    for c in range(nchunk):
        rows = slice(c * CHUNK, (c + 1) * CHUNK)
        t8 = slice(c * 8, (c + 1) * 8)
        a_cr_c = a_cr[t8].astype(BF16)
        cr128_ref[rows, :] = _dot_tn(a_cr_c, sel128)
        cr64_ref[rows, :] = _dot_tn(a_cr_c, sel64)
        b64_ref[rows, :] = _dot_tn(a_b[t8].astype(BF16), sel64)
        r64_ref[rows, :] = _dot_tn(a_r[t8].astype(BF16), sel64)
        for h in range(nh):
            qg = qk_ref[rows, grp[h]].astype(BF16)
            kg = qk_ref[rows, w + grp[h].start:w + grp[h].stop].astype(BF16)
            s_ref[rows, h * 128:(h + 1) * 128] = _dot_nt(qg, jnp.where(half[h], kg, jnp.zeros_like(kg)))

    lane_w = lax.broadcasted_iota(I32, (1, w), 1)
    hmask = [(lane_w // hd) == h for h in range(nh)]
    ones_blk = [m.astype(BF16) * jnp.ones((CHUNK, 1), BF16) for m in hmask]
    ones_cols = jnp.ones((CHUNK, w), BF16)
    zblk = jnp.zeros((hd, 128), F32)
    ngrp = w // 128
    m_prev = m_ref[...]
    cblk = [st_ref[h * hd:(h + 1) * hd, grp[h]] for h in range(nh)]
    nblk = [st_ref[h * hd:(h + 1) * hd, w + grp[h].start:w + grp[h].stop] for h in range(nh)]
    for c in range(nchunk):
        rows = slice(c * CHUNK, (c + 1) * CHUNK)
        t8 = slice(c * 8, (c + 1) * 8)
        r_c = rt_ref[t8, :]
        g_last = jnp.maximum(rm_ref[t8, :], m_prev)
        decay = jnp.exp(m_prev - g_last)
        mprev_l = lanes(m_prev)
        glast_l = lanes(g_last)
        gb64 = jnp.maximum(cr64_ref[rows, :], mprev_l)
        w_inter = jnp.exp(mprev_l - gb64)
        e_negm = jnp.exp(-(b64_ref[rows, :] + gb64))
        v_all = zc_ref[rows, 2 * w:3 * w]
        wcat = []
        vblocks = []
        srows = []
        for h in range(nh):
            gb128 = jnp.maximum(cr128_ref[rows, h * 128:(h + 1) * 128], m_prev[h:h + 1, :])
            d = jnp.exp(jnp.where(causal, r_c[h:h + 1, :] - gb128, NEG))
            wcat.append((d * s_ref[rows, h * 128:(h + 1) * 128]).astype(BF16))
            vblocks.append(jnp.concatenate([jnp.where(hmask[h], v_all, jnp.zeros_like(v_all)), ones_blk[h]], axis=1))
            g = (h * hd) // 128
            srows.append(jnp.concatenate([cblk[h] if j == g else zblk for j in range(ngrp)]
                                         + [nblk[h] if j == g else zblk for j in range(ngrp)], axis=1))
        s_bf = jnp.concatenate(srows, axis=0).astype(BF16)
        out_aug = _dot(jnp.concatenate(wcat, axis=1), jnp.concatenate(vblocks, axis=0))
        out_aug = out_aug + _dot((qk_ref[rows, 0:w] * w_inter).astype(BF16), s_bf)
        hh_ref[rows, :] = out_aug[:, 0:w] / jnp.maximum(jnp.abs(out_aug[:, w:2 * w]), e_negm)
        kw = (qk_ref[rows, w:2 * w] * jnp.exp(r64_ref[rows, :] - glast_l)).astype(BF16)
        upd = _dot_tn(kw, jnp.concatenate([v_all, ones_cols], axis=1))
        for h in range(nh):
            rs = slice(h * hd, (h + 1) * hd)
            cblk[h] = cblk[h] * decay[h:h + 1, :] + jnp.where(half[h], upd[rs, grp[h]], 0.0)
            nblk[h] = (nblk[h] * decay[h:h + 1, :]
                       + jnp.where(half[h], upd[rs, w + grp[h].start:w + grp[h].stop], 0.0))
        m_prev = bl_ref[t8, :] + g_last
    m_ref[...] = m_prev
    for h in range(nh):
        st_ref[h * hd:(h + 1) * hd, grp[h]] = cblk[h]
        st_ref[h * hd:(h + 1) * hd, w + grp[h].start:w + grp[h].stop] = nblk[h]

    hh = hh_ref[...]
    ms = _dot((hh * hh).astype(BF16), bones_ref[...])
    og = zc_ref[:, 3 * w:4 * w].astype(F32)
    o_ref[...] = (_sigmoid(og) * hh * lax.rsqrt(ms + EPS) * ng_ref[...]).astype(o_ref.dtype)


def _mlstm(zc3, zgt, conv_w, gate_b_rows, norm_g):
    b, s, w = zc3.shape
    ts = SEQ_TILE
    nt = s // ts
    gw = GROUP_W
    t = jnp.arange(CHUNK)
    triu = (t[:, None] <= t[None, :]).astype(BF16)
    hid = jnp.arange(gw) // HEAD_DIM
    bones = jnp.where(hid[:, None] == hid[None, :], 1.0 / HEAD_DIM, 0.0).astype(BF16)
    full = lambda a: pl.BlockSpec(a.shape, lambda bi, i: (0,) * a.ndim)
    return pl.pallas_call(
        _mlstm_kernel,
        grid=(b, nt),
        in_specs=[pl.BlockSpec((None, ts, w), lambda bi, i: (bi, i, 0)),
                  pl.BlockSpec((8, ts), lambda bi, i: (0, bi * nt + i)),
                  full(conv_w), full(gate_b_rows), full(norm_g), full(triu), full(bones)],
        out_specs=pl.BlockSpec((None, ts, gw), lambda bi, i: (bi, i, 0)),
        out_shape=jax.ShapeDtypeStruct((b, s, gw), BF16),
        scratch_shapes=[pltpu.VMEM((gw, 2 * gw), F32),
                        pltpu.VMEM((8, 128), F32),
                        pltpu.VMEM((8, 2 * gw), F32),
                        pltpu.VMEM((ts, 2 * gw), F32),
                        pltpu.VMEM((ts // CHUNK * 8, 128), F32),
                        pltpu.VMEM((ts // CHUNK * 8, 128), F32),
                        pltpu.VMEM((ts // CHUNK * 8, 128), F32),
                        pltpu.VMEM((ts, MLSTM_HEADS * 128), F32),
                        pltpu.VMEM((ts, gw), F32),
                        pltpu.VMEM((ts, gw), F32),
                        pltpu.VMEM((ts, gw), F32),
                        pltpu.VMEM((ts, MLSTM_HEADS * 128), F32),
                        pltpu.VMEM((ts, gw), F32)],
        compiler_params=_cparams("parallel", "arbitrary"),
        name="mlstm",
    )(zc3, zgt, conv_w, gate_b_rows, norm_g, triu, bones)


def _sgu_kernel(zd_ref, lng_ref, lnb_ref, w_ref, bias_ref, o_ref):
    tm = zd_ref.shape[0]
    gw = GROUP_W // SGU_GROUPS
    u = _gelu(zd_ref[:, 0:GROUP_W].astype(F32))
    v = _gelu(zd_ref[:, GROUP_W:2 * GROUP_W].astype(F32))
    mu = jnp.mean(v, axis=-1, keepdims=True)
    vc = v - mu
    var = jnp.mean(vc * vc, axis=-1, keepdims=True)
    vn = (vc * lax.rsqrt(var + EPS) * lng_ref[...] + lnb_ref[...]).astype(BF16)
    for c in range(tm // CHUNK):
        rows = slice(c * CHUNK, (c + 1) * CHUNK)
        mixed = jnp.concatenate([_dot(w_ref[g], vn[rows, g * gw:(g + 1) * gw]) for g in range(SGU_GROUPS)], axis=-1)
        o_ref[rows, :] = (u[rows, :] * (mixed + bias_ref[...])).astype(o_ref.dtype)


def _sgu(zd, ln_g, ln_b, w_tril, bias):
    n = zd.shape[0]
    tm = TOKEN_TILE
    full = lambda a: pl.BlockSpec(a.shape, lambda i: (0,) * a.ndim)
    return pl.pallas_call(
        _sgu_kernel,
        grid=(n // tm,),
        in_specs=[pl.BlockSpec((tm, 2 * GROUP_W), lambda i: (i, 0)), full(ln_g), full(ln_b), full(w_tril), full(bias)],
        out_specs=pl.BlockSpec((tm, GROUP_W), lambda i: (i, 0)),
        out_shape=jax.ShapeDtypeStruct((n, GROUP_W), BF16),
        compiler_params=_cparams("parallel"),
        name="sgu",
    )(zd, ln_g, ln_b, w_tril, bias)


def _out_proj_kernel(x_ref, ya_ref, yb_ref, yc_ref, yd_ref, gluw_ref, glub_ref, wo_ref, ng_ref,
                     rwh_ref, rwl_ref, rb_ref,
                     x1_ref, hx_ref, meta_ref, cnt_ref, carry_ref):
    tm = x_ref.shape[0]

    @pl.when(pl.program_id(0) == 0)
    def _():
        carry_ref[...] = jnp.zeros_like(carry_ref)

    yb = _gelu(yb_ref[...].astype(F32))
    yb = yb * _sigmoid(_dot(yb.astype(BF16), gluw_ref[...]) + glub_ref[...])
    mix = _dot(ya_ref[...], wo_ref[0:GROUP_W, :])
    mix = mix + _dot(yb.astype(BF16), wo_ref[GROUP_W:2 * GROUP_W, :])
    mix = mix + _dot(yc_ref[...], wo_ref[2 * GROUP_W:3 * GROUP_W, :])
    mix = mix + _dot(yd_ref[...], wo_ref[3 * GROUP_W:4 * GROUP_W, :])
    x1 = x_ref[...] + mix
    x1_ref[...] = x1
    h = _rms(x1, ng_ref[...])
    hx_ref[:, 0:PACK_W] = _pack_rows(h)

    hh, hl = _split_bf16(h)
    wh = rwh_ref[...]
    wl = rwl_ref[...]
    logits = _dot_nt(wh, hh) + _dot_nt(wh, hl) + _dot_nt(wl, hh) + rb_ref[...]
    gl = [logits[j:j + 1, :] for j in range(N_GROUPS)]
    gmax = functools.reduce(jnp.maximum, gl)
    gsel = jnp.full((1, tm), N_GROUPS - 1, I32)
    for j in range(N_GROUPS - 2, -1, -1):
        gsel = jnp.where(gl[j] == gmax, j, gsel)
    p_g = 1.0 / functools.reduce(jnp.add, [jnp.exp(v - gmax) for v in gl])
    e_in = []
    for i in range(EPG):
        v = logits[N_GROUPS + i:N_GROUPS + i + 1, :]
        for j in range(1, N_GROUPS):
            r = N_GROUPS + j * EPG + i
            v = jnp.where(gsel == j, logits[r:r + 1, :], v)
        e_in.append(v)
    v1 = functools.reduce(jnp.maximum, e_in)
    i1 = jnp.full((1, tm), EPG - 1, I32)
    for i in range(EPG - 2, -1, -1):
        i1 = jnp.where(e_in[i] == v1, i, i1)
    rest = [jnp.where(i1 == i, NEG, e_in[i]) for i in range(EPG)]
    v2 = functools.reduce(jnp.maximum, rest)
    i2 = jnp.full((1, tm), EPG - 1, I32)
    for i in range(EPG - 2, -1, -1):
        i2 = jnp.where((rest[i] == v2) & (i1 != i), i, i2)
    e2 = jnp.exp(v2 - v1)
    w1 = p_g / (1.0 + e2)
    w2 = p_g * e2 / (1.0 + e2)
    lo = jnp.minimum(i1, i2)
    hi = jnp.maximum(i1, i2)
    w_lo = jnp.where(i1 < i2, w1, w2)
    w_hi = jnp.where(i1 < i2, w2, w1)
    bucket = gsel * PAIRS + ((lo * (2 * EPG - 1 - lo)) >> 1) + (hi - lo - 1)

    kid = lax.broadcasted_iota(I32, (BUCKET_PAD, tm), 0)
    onehot = (kid == bucket).astype(F32)
    s_i = lax.broadcasted_iota(I32, (tm, tm), 0)
    t_i = lax.broadcasted_iota(I32, (tm, tm), 1)
    prefix = _dot(onehot.astype(BF16), (s_i <= t_i).astype(BF16))
    carry = carry_ref[...]
    rank = jnp.sum(onehot * (prefix - 1.0 + carry[:, 0:1]), axis=0, keepdims=True)
    carry = carry + prefix[:, tm - 1:tm]
    carry_ref[...] = carry
    cnt_ref[...] = carry
    meta_ref[...] = jnp.concatenate([bucket, rank.astype(I32), jnp.zeros((6, tm), I32)], axis=0)
    wrows = jnp.concatenate([w_lo, w_hi, jnp.zeros((126, tm), F32)], axis=0)
    for c in range(tm // 128):
        hx_ref[c * 128:(c + 1) * 128, PACK_W:ROW_W] = pltpu.bitcast(wrows[:, c * 128:(c + 1) * 128].T, U32)


def _out_proj(x2, ya, yb, yc, yd, glu_w, glu_b, w_out, norm_g, rw_hi, rw_lo, rb):
    n = x2.shape[0]
    tm = TOKEN_TILE
    row = lambda w: pl.BlockSpec((tm, w), lambda i: (i, 0))
    full = lambda a: pl.BlockSpec(a.shape, lambda i: (0,) * a.ndim)
    return pl.pallas_call(
        _out_proj_kernel,
        grid=(n // tm,),
        in_specs=[row(D_MODEL), row(GROUP_W), row(GROUP_W), row(GROUP_W), row(GROUP_W),
                  full(glu_w), full(glu_b), full(w_out), full(norm_g), full(rw_hi), full(rw_lo), full(rb)],
        out_specs=[row(D_MODEL), row(ROW_W), pl.BlockSpec((8, tm), lambda i: (0, i)),
                   pl.BlockSpec((BUCKET_PAD, 128), lambda i: (0, 0))],
        out_shape=[jax.ShapeDtypeStruct((n, D_MODEL), F32), jax.ShapeDtypeStruct((n, ROW_W), U32),
                   jax.ShapeDtypeStruct((8, n), I32), jax.ShapeDtypeStruct((BUCKET_PAD, 128), F32)],
        scratch_shapes=[pltpu.VMEM((BUCKET_PAD, 128), F32)],
        compiler_params=_cparams("arbitrary"),
        name="out_proj",
    )(x2, ya, yb, yc, yd, glu_w, glu_b, w_out, norm_g, rw_hi, rw_lo, rb)


def _row_copy(src_ref, src_row, dst_ref, dst_row, sem):
    return pltpu.make_async_copy(src_ref.at[pl.ds(src_row, 1)], dst_ref.at[pl.ds(dst_row, 1)], sem)


def _dispatch_kernel(dest_ref, hx_ref, xs_in_ref, xs_ref, sem):
    del xs_in_ref
    tm = hx_ref.shape[0]

    def start(t, c):
        _row_copy(hx_ref, t, xs_ref, dest_ref[0, 0, t], sem).start()
        return c

    lax.fori_loop(0, tm, start, 0, unroll=ROW_DMA_UNROLL)
    pltpu.make_async_copy(hx_ref, xs_ref.at[pl.ds(0, tm)], sem).wait()


def _dispatch(dest3, hx, p_tot):
    n = hx.shape[0]
    tm = TOKEN_TILE
    xs0 = jnp.zeros((p_tot, ROW_W), U32)
    return pl.pallas_call(
        _dispatch_kernel,
        grid=(n // tm,),
        in_specs=[pl.BlockSpec((1, 1, tm), lambda i: (i, 0, 0), memory_space=pltpu.SMEM),
                  pl.BlockSpec((tm, ROW_W), lambda i: (i, 0)),
                  pl.BlockSpec(memory_space=pl.ANY)],
        out_specs=pl.BlockSpec(memory_space=pl.ANY),
        out_shape=jax.ShapeDtypeStruct((p_tot, ROW_W), U32),
        input_output_aliases={2: 0},
        scratch_shapes=[pltpu.SemaphoreType.DMA],
        compiler_params=_cparams("arbitrary"),
        name="dispatch",
    )(dest3, hx, xs0)


def _ffn_kernel(ea_ref, eb_ref, valid_ref, xblk_ref, xs_ref, wg_a_ref, wu_a_ref, wd_a_ref, wg_b_ref, wu_b_ref,
                wd_b_ref, ys_ref, wgu_a, wdn_a, wgu_b, wdn_b):
    del xblk_ref
    j = pl.program_id(0)
    valid = valid_ref[j]
    jp = jnp.maximum(j - 1, 0)

    @pl.when((j == 0) | (ea_ref[j] != ea_ref[jp]))
    def _():
        wgu_a[:, 0:D_EXPERT] = wg_a_ref[...].astype(BF16)
        wgu_a[:, D_EXPERT:2 * D_EXPERT] = wu_a_ref[...].astype(BF16)
        wdn_a[...] = wd_a_ref[...].astype(BF16)

    @pl.when((j == 0) | (eb_ref[j] != eb_ref[jp]))
    def _():
        wgu_b[:, 0:D_EXPERT] = wg_b_ref[...].astype(BF16)
        wgu_b[:, D_EXPERT:2 * D_EXPERT] = wu_b_ref[...].astype(BF16)
        wdn_b[...] = wd_b_ref[...].astype(BF16)

    for sb in range(FFN_BLOCK // FFN_SUB):
        rows = slice(sb * FFN_SUB, (sb + 1) * FFN_SUB)

        @pl.when(valid <= sb * FFN_SUB)
        def _():
            ys_ref[rows, :] = jnp.zeros((FFN_SUB, PACK_W), U32)

        @pl.when(valid > sb * FFN_SUB)
        def _():
            x_lo, x_hi = _unpack_rows(xs_ref[rows, 0:PACK_W])
            xb = jnp.concatenate([x_lo.astype(BF16), x_hi.astype(BF16)], axis=1)
            wts = pltpu.bitcast(xs_ref[rows, PACK_W:ROW_W], F32)

            def expert(wgu, wdn):
                gu = _dot(xb, wgu[...])
                g = gu[:, 0:D_EXPERT]
                act = (g * _sigmoid(g) * gu[:, D_EXPERT:2 * D_EXPERT]).astype(BF16)
                return _dot(act, wdn[...])

            ya = expert(wgu_a, wdn_a)
            yb = expert(wgu_b, wdn_b)
            ys_ref[rows, :] = _pack_rows(ya * wts[:, 0:1] + yb * wts[:, 1:2])


def _ffn(ea, eb, valid, xblk, xs, layer, w_gate, w_up, w_down):
    p_tot = xs.shape[0]
    bm = FFN_BLOCK
    nblk = p_tot // bm
    wspec = lambda e_idx, shape: pl.BlockSpec((None, None) + shape,
                                              lambda j, ea, eb, va, xb: (layer, (ea, eb)[e_idx][j], 0, 0))
    up_shape = (D_MODEL, D_EXPERT)
    dn_shape = (D_EXPERT, D_MODEL)
    return pl.pallas_call(
        _ffn_kernel,
        grid_spec=pltpu.PrefetchScalarGridSpec(
            num_scalar_prefetch=4,
            grid=(nblk,),
            in_specs=[pl.BlockSpec((bm, ROW_W), lambda j, ea, eb, va, xb: (xb[j], 0)),
                      wspec(0, up_shape), wspec(0, up_shape), wspec(0, dn_shape),
                      wspec(1, up_shape), wspec(1, up_shape), wspec(1, dn_shape)],
            out_specs=pl.BlockSpec((bm, PACK_W), lambda j, ea, eb, va, xb: (j, 0)),
            scratch_shapes=[pltpu.VMEM((D_MODEL, 2 * D_EXPERT), BF16), pltpu.VMEM(dn_shape, BF16),
                            pltpu.VMEM((D_MODEL, 2 * D_EXPERT), BF16), pltpu.VMEM(dn_shape, BF16)],
        ),
        out_shape=jax.ShapeDtypeStruct((p_tot, PACK_W), U32),
        compiler_params=_cparams("arbitrary"),
        name="ffn",
    )(ea, eb, valid, xblk, xs, w_gate, w_up, w_down, w_gate, w_up, w_down)


def _combine_kernel(dest_ref, x1_ref, ys_ref, ng_ref, o_ref, buf_ref, sem, *, final_norm):
    tm = x1_ref.shape[0]

    def start(t, c):
        _row_copy(ys_ref, dest_ref[0, 0, t], buf_ref, t, sem).start()
        return c

    lax.fori_loop(0, tm, start, 0, unroll=ROW_DMA_UNROLL)
    pltpu.make_async_copy(ys_ref.at[pl.ds(0, tm)], buf_ref, sem).wait()
    y_lo, y_hi = _unpack_rows(buf_ref[...])
    x2 = x1_ref[...] + jnp.concatenate([y_lo, y_hi], axis=1)
    o_ref[...] = _rms(x2, ng_ref[...]) if final_norm else x2


def _combine(dest3, x1, ys, norm_g, final_norm):
    n = x1.shape[0]
    tm = TOKEN_TILE
    return pl.pallas_call(
        functools.partial(_combine_kernel, final_norm=final_norm),
        grid=(n // tm,),
        in_specs=[pl.BlockSpec((1, 1, tm), lambda i: (i, 0, 0), memory_space=pltpu.SMEM),
                  pl.BlockSpec((tm, D_MODEL), lambda i: (i, 0)),
                  pl.BlockSpec(memory_space=pl.ANY),
                  pl.BlockSpec((1, D_MODEL), lambda i: (0, 0))],
        out_specs=pl.BlockSpec((tm, D_MODEL), lambda i: (i, 0)),
        out_shape=jax.ShapeDtypeStruct((n, D_MODEL), F32),
        scratch_shapes=[pltpu.VMEM((tm, PACK_W), U32), pltpu.SemaphoreType.DMA],
        compiler_params=_cparams("arbitrary"),
        name="combine",
    )(dest3, x1, ys, norm_g)


def _routing_tables(meta, counts, n):
    bm = FFN_BLOCK
    nblk = n // bm + N_BUCKETS
    bucket = meta[0]
    rank = meta[1]
    cnt = counts[:N_BUCKETS, 0].astype(I32)
    padded = ((cnt + bm - 1) // bm) * bm
    pad_end = jnp.cumsum(padded)
    pad_start = pad_end - padded
    onehot = bucket[:, None] == jnp.arange(N_BUCKETS, dtype=I32)[None, :]
    dest = rank + jnp.sum(jnp.where(onehot, pad_start[None, :], 0), axis=1)
    nact = (pad_end[-1] // bm).astype(I32)
    j = jnp.arange(nblk, dtype=I32)
    blk = jnp.minimum(j, jnp.maximum(nact - 1, 0))
    pos = blk * bm
    bkt = jnp.minimum(jnp.sum((pad_end[None, :] <= pos[:, None]).astype(I32), axis=1), N_BUCKETS - 1)
    in_bkt = bkt[:, None] == jnp.arange(N_BUCKETS, dtype=I32)[None, :]
    sel = lambda tab: jnp.sum(jnp.where(in_bkt, tab[None, :], 0), axis=1)
    valid = jnp.where(j < nact, jnp.clip(sel(cnt) - (pos - sel(pad_start)), 0, bm), 0)
    lo_tab, hi_tab = [], []
    for g in range(N_GROUPS):
        for a in range(EPG):
            for b in range(a + 1, EPG):
                lo_tab.append(g * EPG + a)
                hi_tab.append(g * EPG + b)
    ea = sel(jnp.asarray(lo_tab, I32))
    eb = sel(jnp.asarray(hi_tab, I32))
    return dest.astype(I32), ea.astype(I32), eb.astype(I32), valid.astype(I32), blk.astype(I32), nblk * bm


def _layer(x2, b, s, p, final_g):
    n = b * s
    za, zb, zc, zd, zgt = _in_proj(x2, p['norm_mix_g'], p['wa'], p['wb'], p['wc'], p['wd'], p['wgt'])
    ya = _attention(za.reshape(b, s, -1), p['sinks']).reshape(n, GROUP_W)
    yb = _ssm(zb.reshape(b, s, GROUP_W), p['ssm']).reshape(n, GROUP_W)
    yc = _mlstm(zc.reshape(b, s, -1), zgt, p['conv_w'], p['gate_b'], p['mlstm_norm_g']).reshape(n, GROUP_W)
    yd = _sgu(zd, p['sgu_ln_g'], p['sgu_ln_b'], p['sgu_w'], p['sgu_bias'])
    x1, hx, meta, counts = _out_proj(x2, ya, yb, yc, yd, p['glu_w'], p['glu_b'], p['w_out'], p['norm_ffn_g'],
                                     p['rw_hi'], p['rw_lo'], p['rb'])
    dest, ea, eb, valid, xblk, p_tot = _routing_tables(meta, counts, n)
    dest3 = dest.reshape(n // TOKEN_TILE, 1, TOKEN_TILE)
    xs = _dispatch(dest3, hx, p_tot)
    ys = _ffn(ea, eb, valid, xblk, xs, p['layer'], p['w_gate'], p['w_up'], p['w_down'])
    if final_g is None:
        return _combine(dest3, x1, ys, p['norm_ffn_g'], False)
    return _combine(dest3, x1, ys, final_g, True)


def _prep_layer(l, norm_mix_g, w_in, attn_sinks, ssm_a_re, ssm_a_im, ssm_b_re, ssm_b_im, ssm_c_re, ssm_c_im,
                ssm_d, ssm_log_dt, ssm_glu_w, ssm_glu_b, mlstm_conv_w, mlstm_gate_b, mlstm_norm_g,
                sgu_ln_g, sgu_ln_b, sgu_w, sgu_b, w_out, norm_ffn_g, router_group_w, router_group_b,
                router_expert_w, router_expert_b, expert_w_gate, expert_w_up, expert_w_down):
    w = w_in[l]
    o_su = 2 * GROUP_W
    o_c = o_su + GROUP_W
    o_ci = o_c + 3 * GROUP_W
    o_co = o_ci + 2 * MLSTM_HEADS
    o_d = o_co + GROUP_W
    wc = jnp.concatenate([w[:, o_c:o_ci], w[:, o_co:o_d]], axis=1)
    wgt = w[:, o_ci:o_co].T
    head_cols = lambda m, hq: m[..., hq * HEAD_DIM:(hq + 1) * HEAD_DIM]
    order = [g * ATTN_REP + r for r in range(ATTN_REP) for g in range(KV_HEADS)]
    wa = jnp.concatenate([head_cols(w, hq) * (HEAD_DIM ** -0.5) for hq in order] + [w[:, GROUP_W:o_su]], axis=1)
    wo = jnp.concatenate([w_out[l][hq * HEAD_DIM:(hq + 1) * HEAD_DIM] for hq in order] + [w_out[l][GROUP_W:]], axis=0)
    rw = jnp.concatenate([router_group_w[l], router_expert_w[l]], axis=1).T.astype(F32)
    rw = jnp.pad(rw, ((0, 4), (0, 0)))
    rw_hi = rw.astype(BF16)
    rw_lo = (rw - rw_hi.astype(F32)).astype(BF16)
    rb = jnp.pad(jnp.concatenate([router_group_b[l], router_expert_b[l]]).astype(F32), (0, 4))[:, None]
    tril = jnp.tril(jnp.ones((CHUNK, CHUNK), F32))
    gw = GROUP_W // SGU_GROUPS
    return dict(
        norm_mix_g=norm_mix_g[l][None, :].astype(F32),
        wa=wa.astype(BF16), wb=w[:, o_su:o_c].astype(BF16), wc=wc.astype(BF16),
        wd=w[:, o_d:].astype(BF16), wgt=wgt.astype(BF16),
        sinks=attn_sinks[l].astype(F32),
        ssm=_ssm_matrices(ssm_a_re[l], ssm_a_im[l], ssm_b_re[l], ssm_b_im[l], ssm_c_re[l], ssm_c_im[l],
                          ssm_d[l], ssm_log_dt[l]),
        glu_w=ssm_glu_w[l].astype(BF16), glu_b=ssm_glu_b[l][None, :].astype(F32),
        conv_w=mlstm_conv_w[l].astype(F32),
        gate_b=jnp.broadcast_to(mlstm_gate_b[l].astype(F32)[:, None], (2 * MLSTM_HEADS, 128)),
        mlstm_norm_g=mlstm_norm_g[l][None, :].astype(F32),
        sgu_ln_g=sgu_ln_g[l][None, :].astype(F32), sgu_ln_b=sgu_ln_b[l][None, :].astype(F32),
        sgu_w=(sgu_w[l].astype(F32) * tril).astype(BF16),
        sgu_bias=jnp.repeat(sgu_b[l].astype(F32).T, gw, axis=1),
        w_out=wo.astype(BF16), norm_ffn_g=norm_ffn_g[l][None, :].astype(F32),
        rw_hi=rw_hi, rw_lo=rw_lo, rb=rb,
        layer=l, w_gate=expert_w_gate, w_up=expert_w_up, w_down=expert_w_down,
    )


def kernel(x, norm_mix_g, w_in, attn_sinks, ssm_a_re, ssm_a_im, ssm_b_re, ssm_b_im, ssm_c_re, ssm_c_im, ssm_d, ssm_log_dt, ssm_glu_w, ssm_glu_b, mlstm_conv_w, mlstm_gate_b, mlstm_norm_g, sgu_ln_g, sgu_ln_b, sgu_w, sgu_b, w_out, norm_ffn_g, router_group_w, router_group_b, router_expert_w, router_expert_b, expert_w_gate, expert_w_up, expert_w_down, norm_final_g):
    b, s, d = x.shape
    depth = w_in.shape[0]
    x2 = x.reshape(b * s, d).astype(F32)
    for l in range(depth):
        p = _prep_layer(l, norm_mix_g, w_in, attn_sinks, ssm_a_re, ssm_a_im, ssm_b_re, ssm_b_im, ssm_c_re,
                        ssm_c_im, ssm_d, ssm_log_dt, ssm_glu_w, ssm_glu_b, mlstm_conv_w, mlstm_gate_b,
                        mlstm_norm_g, sgu_ln_g, sgu_ln_b, sgu_w, sgu_b, w_out, norm_ffn_g, router_group_w,
                        router_group_b, router_expert_w, router_expert_b, expert_w_gate, expert_w_up,
                        expert_w_down)
        final_g = norm_final_g[None, :].astype(F32) if l == depth - 1 else None
        x2 = _layer(x2, b, s, p, final_g)
    return x2.reshape(b, s, d).astype(x.dtype)
```

```python
import functools
import math

import jax
import jax.numpy as jnp
from jax import lax
from jax.experimental import pallas as pl
from jax.experimental.pallas import tpu as pltpu

F32 = jnp.float32
BF16 = jnp.bfloat16
I32 = jnp.int32
U32 = jnp.uint32

D_MODEL = 1024
GROUP_W = 256
HEAD_DIM = 64
EPS = 1e-6
NEG = -1e30
WINDOW = 128
KV_HEADS = 2
ATTN_REP = 2
SSM_GROUP = 16
SSM_GROUPS = 16
SSM_STATE = 64
SSM_CHUNK = 16
MLSTM_HEADS = 4
CHUNK = 128
CONV_K = 4
SGU_GROUPS = 4
N_GROUPS = 4
EPG = 8
N_EXPERTS = 32
D_EXPERT = 512
PAIRS = EPG * (EPG - 1) // 2
N_BUCKETS = N_GROUPS * PAIRS
BUCKET_PAD = 128
PACK_W = D_MODEL // 2
ROW_W = PACK_W + 128

TOKEN_TILE = 512
SEQ_TILE = 512
SSM_TILE = 4096
SSM_SLAB = 64
FFN_BLOCK = 512
FFN_SUB = 256
ROW_DMA_UNROLL = 8
VMEM_LIMIT = 56 * 1024 * 1024


def _cparams(*sem):
    return pltpu.CompilerParams(dimension_semantics=sem, vmem_limit_bytes=VMEM_LIMIT)


def _rms(x, g):
    return x * lax.rsqrt(jnp.mean(x * x, axis=-1, keepdims=True) + EPS) * g


def _gelu(x):
    return 0.5 * x * (1.0 + jnp.tanh(math.sqrt(2.0 / math.pi) * (x + 0.044715 * (x * x * x))))


def _sigmoid(x):
    return 1.0 / (1.0 + jnp.exp(-x))


def _log_sigmoid(x):
    return jnp.minimum(x, 0.0) - jnp.log(1.0 + jnp.exp(-jnp.abs(x)))


def _dot(a, b):
    return jnp.dot(a, b, preferred_element_type=F32)


def _dot_nt(a, b):
    return lax.dot_general(a, b, (((1,), (1,)), ((), ())), preferred_element_type=F32)


def _dot_tn(a, b):
    return lax.dot_general(a, b, (((0,), (0,)), ((), ())), preferred_element_type=F32)


def _split_bf16(x):
    hi = x.astype(BF16)
    lo = (x - hi.astype(F32)).astype(BF16)
    return hi, lo


def _pack_rows(x):
    w = x.shape[1] // 2
    bits = lambda v: pltpu.bitcast(v.astype(BF16).astype(F32), U32)
    return (bits(x[:, 0:w]) >> 16) | (bits(x[:, w:2 * w]) & jnp.uint32(0xFFFF0000))


def _unpack_rows(p):
    return pltpu.bitcast(p << 16, F32), pltpu.bitcast(p & jnp.uint32(0xFFFF0000), F32)


def _in_proj_kernel(x_ref, g_ref, wa_ref, wb_ref, wc_ref, wd_ref, wgt_ref,
                    za_ref, zb_ref, zc_ref, zd_ref, zgt_ref):
    hb = _rms(x_ref[...], g_ref[...]).astype(BF16)
    za_ref[...] = _dot(hb, wa_ref[...]).astype(BF16)
    zb_ref[...] = _dot(hb, wb_ref[...]).astype(BF16)
    zc_ref[...] = _dot(hb, wc_ref[...]).astype(BF16)
    zd_ref[...] = _dot(hb, wd_ref[...]).astype(BF16)
    zgt_ref[...] = _dot_nt(wgt_ref[...], hb)


def _in_proj(x2, g, wa, wb, wc, wd, wgt):
    n = x2.shape[0]
    tm = TOKEN_TILE
    row = lambda w: pl.BlockSpec((tm, w), lambda i: (i, 0))
    full = lambda a: pl.BlockSpec(a.shape, lambda i: (0,) * a.ndim)
    widths = (wa.shape[1], wb.shape[1], wc.shape[1], wd.shape[1])
    return pl.pallas_call(
        _in_proj_kernel,
        grid=(n // tm,),
        in_specs=[row(D_MODEL), full(g), full(wa), full(wb), full(wc), full(wd), full(wgt)],
        out_specs=[row(w) for w in widths] + [pl.BlockSpec((wgt.shape[0], tm), lambda i: (0, i))],
        out_shape=[jax.ShapeDtypeStruct((n, w), BF16) for w in widths]
        + [jax.ShapeDtypeStruct((wgt.shape[0], n), F32)],
        compiler_params=_cparams("parallel"),
        name="in_proj",
    )(x2, g, wa, wb, wc, wd, wgt)


def _attn_kernel(sink_ref, cur_ref, prev_ref, o_ref):
    first = pl.program_id(1) == 0
    nblk = cur_ref.shape[0] // WINDOW
    row = lax.broadcasted_iota(I32, (WINDOW, 2 * WINDOW), 0)
    col = lax.broadcasted_iota(I32, (WINDOW, 2 * WINDOW), 1)
    band = (col <= row + WINDOW) & (col > row)
    lane = lax.broadcasted_iota(I32, (1, 128), 1)
    kv_lanes = [(lane // HEAD_DIM) == g for g in range(KV_HEADS)]
    ones = jnp.ones((2 * WINDOW, 128), BF16)
    ko = 2 * 128
    vo = 3 * 128
    for j in range(nblk):
        cur = cur_ref[j * WINDOW:(j + 1) * WINDOW, :]
        if j == 0:
            prev = prev_ref[...]
            mask = band & ((col >= WINDOW) | jnp.logical_not(first))
        else:
            prev = cur_ref[(j - 1) * WINDOW:j * WINDOW, :]
            mask = band
        kk = jnp.concatenate([prev[:, ko:ko + 128], cur[:, ko:ko + 128]], axis=0)
        vaug = jnp.concatenate([jnp.concatenate([prev[:, vo:vo + 128], cur[:, vo:vo + 128]], axis=0), ones], axis=1)
        outs = []
        for r in range(ATTN_REP):
            qg = cur[:, r * 128:(r + 1) * 128]
            og = []
            for g in range(KV_HEADS):
                s = _dot_nt(jnp.where(kv_lanes[g], qg, jnp.zeros_like(qg)), kk)
                s = jnp.where(mask, s, NEG)
                sink = sink_ref[g * ATTN_REP + r]
                m = jnp.maximum(jnp.max(s, axis=-1, keepdims=True), sink)
                oa = _dot(jnp.exp(s - m).astype(BF16), vaug)
                og.append(oa[:, 0:128] / (oa[:, 128:256] + jnp.exp(sink - m)))
            outs.append(jnp.where(kv_lanes[0], og[0], og[1]))
        o_ref[j * WINDOW:(j + 1) * WINDOW, :] = jnp.concatenate(outs, axis=-1).astype(o_ref.dtype)


def _attention(za3, sinks):
    b, s, w = za3.shape
    ts = SEQ_TILE
    per = ts // WINDOW
    return pl.pallas_call(
        _attn_kernel,
        grid=(b, s // ts),
        in_specs=[pl.BlockSpec(memory_space=pltpu.SMEM),
                  pl.BlockSpec((None, ts, w), lambda bi, i: (bi, i, 0)),
                  pl.BlockSpec((None, WINDOW, w), lambda bi, i: (bi, jnp.maximum(i * per - 1, 0), 0))],
        out_specs=pl.BlockSpec((None, ts, GROUP_W), lambda bi, i: (bi, i, 0)),
        out_shape=jax.ShapeDtypeStruct((b, s, GROUP_W), BF16),
        compiler_params=_cparams("parallel", "parallel"),
        name="attn",
    )(sinks, za3, za3)


def _ssm_kernel(zb_ref, bre_ref, bim_ref, t_ref, cre_ref, cim_ref, are_ref, aim_ref, o_ref,
                x_ref, xs_ref, u_ref, vre_ref, vim_ref, sre_ref, sim_ref, y_ref, st_ref):
    ts = zb_ref.shape[0]
    L, G, H = SSM_CHUNK, SSM_GROUPS, SSM_GROUP
    nch = ts // L
    half_g = 128 // H
    n_half = G // half_g

    @pl.when(pl.program_id(1) == 0)
    def _():
        st_ref[...] = jnp.zeros_like(st_ref)

    blk = lax.broadcasted_iota(I32, (1, 128), 1) // H

    def block_transpose(arrs):
        a = list(arrs)
        s = half_g // 2
        while s >= 1:
            keep = (blk & s) == 0
            for i in range(half_g):
                if i & s == 0:
                    ai, aj = a[i], a[i + s]
                    a[i] = jnp.where(keep, ai, pltpu.roll(aj, s * H, axis=1))
                    a[i + s] = jnp.where(keep, pltpu.roll(ai, 128 - s * H, axis=1), aj)
            s //= 2
        return a

    for hf in range(n_half):
        x_ref[hf] = zb_ref[:, hf * 128:(hf + 1) * 128].astype(F32)
    for sg in range(L):
        for hf in range(n_half):
            xs_ref[sg * n_half + hf] = x_ref[hf, pl.ds(sg, nch, stride=L), :]
    for hf in range(n_half):
        for oc in range(L // half_g):
            for r0 in range(0, nch, SSM_SLAB):
                rows = slice(r0, r0 + SSM_SLAB)
                t = block_transpose([xs_ref[(oc * half_g + k) * n_half + hf, rows, :] for k in range(half_g)])
                for gl in range(half_g):
                    u_ref[hf * half_g + gl, rows, oc * 128:(oc + 1) * 128] = t[gl].astype(BF16)
    for g in range(G):
        ug = u_ref[g]
        vre_ref[pl.ds(g, nch, stride=G), :] = _dot(ug, bre_ref[g])
        vim_ref[pl.ds(g, nch, stride=G), :] = _dot(ug, bim_ref[g])

    are = are_ref[...]
    aim = aim_ref[...]

    def step(c, carry):
        sre, sim = carry
        r0 = pl.multiple_of(c * G, G)
        sre_ref[pl.ds(r0, G), :] = sre
        sim_ref[pl.ds(r0, G), :] = sim
        vre = vre_ref[pl.ds(r0, G), :]
        vim = vim_ref[pl.ds(r0, G), :]
        return (are * sre - aim * sim + vre, are * sim + aim * sre + vim)

    sre, sim = lax.fori_loop(0, nch, step, (st_ref[0:G, :], st_ref[G:2 * G, :]), unroll=4)
    st_ref[0:G, :] = sre
    st_ref[G:2 * G, :] = sim

    for g in range(G):
        y = _dot(u_ref[g], t_ref[g])
        y = y + _dot(sre_ref[pl.ds(g, nch, stride=G), :].astype(BF16), cre_ref[g])
        y = y + _dot(sim_ref[pl.ds(g, nch, stride=G), :].astype(BF16), cim_ref[g])
        y_ref[g] = y
    for hf in range(n_half):
        for oc in range(L // half_g):
            for r0 in range(0, nch, SSM_SLAB):
                t = block_transpose([y_ref[hf * half_g + gl, r0:r0 + SSM_SLAB, oc * 128:(oc + 1) * 128]
                                     for gl in range(half_g)])
                for k in range(half_g):
                    x_ref[hf, pl.ds(oc * half_g + k + L * r0, SSM_SLAB, stride=L), :] = t[k]
    for hf in range(n_half):
        o_ref[:, hf * 128:(hf + 1) * 128] = x_ref[hf].astype(o_ref.dtype)


def _ssm_matrices(a_re, a_im, b_re, b_im, c_re, c_im, d_skip, log_dt):
    L = SSM_CHUNK
    a = lax.complex(a_re.astype(F32), a_im.astype(F32))
    dt = jnp.exp(log_dt.astype(F32))[:, None]
    adt = a * dt
    a_bar = jnp.exp(adt)
    b_bar = ((a_bar - 1.0) / a)[..., None] * lax.complex(b_re.astype(F32), b_im.astype(F32))
    c_mat = lax.complex(c_re.astype(F32), c_im.astype(F32))
    lag = jnp.arange(L + 1, dtype=F32)
    pw = jnp.exp(adt[None] * lag[:, None, None])
    kern = jnp.einsum('gop,dgp,gpi->dgoi', c_mat, pw[:L], b_bar).real
    sig = jnp.arange(L)[:, None]
    tau = jnp.arange(L)[None, :]
    d = tau - sig
    kt = kern[jnp.clip(d, 0, L - 1)]
    kt = jnp.where((d >= 0)[:, :, None, None, None], kt, 0.0)
    eye = jnp.eye(SSM_GROUP, dtype=F32)
    dsk = d_skip.astype(F32).reshape(SSM_GROUPS, SSM_GROUP)
    kt = kt + (d == 0)[:, :, None, None, None] * (dsk[:, :, None] * eye)[None, None]
    t_mat = kt.transpose(2, 0, 4, 1, 3).reshape(SSM_GROUPS, L * SSM_GROUP, L * SSM_GROUP)
    bm = pw[:L][::-1][:, :, :, None] * b_bar[None]
    bm = bm.transpose(1, 0, 3, 2).reshape(SSM_GROUPS, L * SSM_GROUP, SSM_STATE)
    cm = c_mat[None] * pw[1:][:, :, None, :]
    cm = cm.transpose(1, 3, 0, 2).reshape(SSM_GROUPS, SSM_STATE, L * SSM_GROUP)
    a_chunk = pw[L]
    pad = 128 - SSM_STATE
    pc = lambda m: jnp.pad(m, ((0, 0), (0, 0), (0, pad))).astype(BF16)
    pr = lambda m: jnp.pad(m, ((0, 0), (0, pad), (0, 0))).astype(BF16)
    pa = lambda m: jnp.pad(m, ((0, 0), (0, pad)))
    return (pc(bm.real), pc(bm.imag), t_mat.astype(BF16), pr(cm.real), pr(-cm.imag),
            pa(a_chunk.real), pa(a_chunk.imag))


def _ssm(zb3, mats):
    b, s, w = zb3.shape
    L, G = SSM_CHUNK, SSM_GROUPS
    ts = min(SSM_TILE, s)
    nch = ts // L
    full = lambda a: pl.BlockSpec(a.shape, lambda bi, i: (0,) * a.ndim)
    return pl.pallas_call(
        _ssm_kernel,
        grid=(b, s // ts),
        in_specs=[pl.BlockSpec((None, ts, w), lambda bi, i: (bi, i, 0))] + [full(m) for m in mats],
        out_specs=pl.BlockSpec((None, ts, w), lambda bi, i: (bi, i, 0)),
        out_shape=jax.ShapeDtypeStruct((b, s, w), BF16),
        scratch_shapes=[pltpu.VMEM((w // 128, ts, 128), F32),
                        pltpu.VMEM((L * (w // 128), nch, 128), F32),
                        pltpu.VMEM((G, nch, L * SSM_GROUP), BF16),
                        pltpu.VMEM((nch * G, 128), F32), pltpu.VMEM((nch * G, 128), F32),
                        pltpu.VMEM((nch * G, 128), F32), pltpu.VMEM((nch * G, 128), F32),
                        pltpu.VMEM((G, nch, L * SSM_GROUP), F32),
                        pltpu.VMEM((2 * G, 128), F32)],
        compiler_params=_cparams("parallel", "arbitrary"),
        name="ssm",
    )(zb3, *mats)


def _mlstm_kernel(zc_ref, zgt_ref, convw_ref, gb_ref, ng_ref, triu_ref, bones_ref, o_ref,
                  st_ref, m_ref, tail_ref, qk_ref,
                  rt_ref, bl_ref, rm_ref, cr128_ref, cr64_ref, b64_ref, r64_ref, s_ref, hh_ref):
    ts = zc_ref.shape[0]
    hd, nh, w = HEAD_DIM, MLSTM_HEADS, GROUP_W
    nchunk = ts // CHUNK
    nr = nchunk * 8
    lane1 = lax.broadcasted_iota(I32, (1, 128), 1)
    half = [(lane1 // hd) == (h % 2) for h in range(nh)]
    grp = [slice((h * hd) // 128 * 128, (h * hd) // 128 * 128 + 128) for h in range(nh)]

    @pl.when(pl.program_id(1) == 0)
    def _():
        st_ref[...] = jnp.zeros_like(st_ref)
        m_ref[...] = jnp.zeros_like(m_ref)
        tail_ref[...] = jnp.zeros_like(tail_ref)

    r_i = lax.broadcasted_iota(I32, (CHUNK, CHUNK), 0)
    c_i = lax.broadcasted_iota(I32, (CHUNK, CHUNK), 1)
    causal = c_i <= r_i
    k64 = lax.broadcasted_iota(I32, (8, w), 0)
    j64 = lax.broadcasted_iota(I32, (8, w), 1)
    sel64 = (j64 // hd == k64 % nh).astype(BF16)
    k128 = lax.broadcasted_iota(I32, (8, nh * 128), 0)
    j128 = lax.broadcasted_iota(I32, (8, nh * 128), 1)
    sel128 = (j128 // 128 == k128 % nh).astype(BF16)

    def lanes(t):
        lo = jnp.where(lane1 < hd, t[0:1, :], t[1:2, :])
        hi = jnp.where(lane1 < hd, t[2:3, :], t[3:4, :])
        return jnp.concatenate([lo, hi], axis=1)

    rowm = lax.broadcasted_iota(I32, (nr, 128), 0) % 8
    lanem = lax.broadcasted_iota(I32, (nr, 128), 1)
    graw = jnp.concatenate([zgt_ref[:, c * CHUNK:(c + 1) * CHUNK] + gb_ref[...] for c in range(nchunk)], axis=0)
    g2 = jnp.where(rowm < nh, graw, _log_sigmoid(graw))
    ghi, glo = _split_bf16(g2)
    cum = _dot(ghi, triu_ref[...]) + _dot(glo, triu_ref[...])
    b_t = pltpu.roll(cum, nr - 4, axis=0)
    r_t = g2 - b_t
    cr = r_t
    sh = 1
    while sh < CHUNK:
        cr = jnp.maximum(cr, jnp.where(lanem >= sh, pltpu.roll(cr, sh, axis=1), NEG))
        sh *= 2
    rt_ref[...] = r_t
    bl_ref[...] = jnp.broadcast_to(b_t[:, CHUNK - 1:CHUNK], (nr, 128))
    rm_ref[...] = jnp.broadcast_to(cr[:, CHUNK - 1:CHUNK], (nr, 128))

    x = zc_ref[:, 0:2 * w].astype(F32)
    xe = jnp.concatenate([tail_ref[...], x], axis=0)
    tail_ref[...] = x[ts - 8:, :]
    cw = convw_ref[...]
    acc = x * cw[CONV_K - 1:CONV_K, :]
    for sft in range(1, CONV_K):
        acc = acc + xe[8 - sft:8 - sft + ts, :] * cw[CONV_K - 1 - sft:CONV_K - sft, :]
    lane_qk = lax.broadcasted_iota(I32, (1, 2 * w), 1)
    qk_ref[...] = acc * _sigmoid(acc) * jnp.where(lane_qk < w, 1.0, hd ** -0.5)

    def hi_lo_rows(v):
        hi = v.astype(BF16).astype(F32)
        return jnp.where(rowm < nh, hi, pltpu.roll(v - hi, 4, axis=0))

    a_cr = hi_lo_rows(cr)
    a_b = hi_lo_rows(b_t)
    a_r = hi_lo_rows(r_t)
    for c in range(nchunk):
        rows = slice(c * CHUNK, (c + 1) * CHUNK)
        t8 = slice(c * 8, (c + 1) * 8)
        a_cr_c = a_cr[t8].astype(BF16)
        cr128_ref[rows, :] = _dot_tn(a_cr_c, sel128)
        cr64_ref[rows, :] = _dot_tn(a_cr_c, sel64)
        b64_ref[rows, :] = _dot_tn(a_b[t8].astype(BF16), sel64)
        r64_ref[rows, :] = _dot_tn(a_r[t8].astype(BF16), sel64)
        for h in range(nh):
            qg = qk_ref[rows, grp[h]].astype(BF16)
            kg = qk_ref[rows, w + grp[h].start:w + grp[h].stop].astype(BF16)
            s_ref[rows, h * 128:(h + 1) * 128] = _dot_nt(qg, jnp.where(half[h], kg, jnp.zeros_like(kg)))

    lane_w = lax.broadcasted_iota(I32, (1, w), 1)
    hmask = [(lane_w // hd) == h for h in range(nh)]
    ones_blk = [m.astype(BF16) * jnp.ones((CHUNK, 1), BF16) for m in hmask]
    ones_cols = jnp.ones((CHUNK, w), BF16)
    zblk = jnp.zeros((hd, 128), F32)
    ngrp = w // 128
    m_prev = m_ref[...]
    cblk = [st_ref[h * hd:(h + 1) * hd, grp[h]] for h in range(nh)]
    nblk = [st_ref[h * hd:(h + 1) * hd, w + grp[h].start:w + grp[h].stop] for h in range(nh)]
    for c in range(nchunk):
        rows = slice(c * CHUNK, (c + 1) * CHUNK)
        t8 = slice(c * 8, (c + 1) * 8)
        r_c = rt_ref[t8, :]
        g_last = jnp.maximum(rm_ref[t8, :], m_prev)
        decay = jnp.exp(m_prev - g_last)
        mprev_l = lanes(m_prev)
        glast_l = lanes(g_last)
        gb64 = jnp.maximum(cr64_ref[rows, :], mprev_l)
        w_inter = jnp.exp(mprev_l - gb64)
        e_negm = jnp.exp(-(b64_ref[rows, :] + gb64))
        v_all = zc_ref[rows, 2 * w:3 * w]
        wcat = []
        vblocks = []
        srows = []
        for h in range(nh):
            gb128 = jnp.maximum(cr128_ref[rows, h * 128:(h + 1) * 128], m_prev[h:h + 1, :])
            d = jnp.exp(jnp.where(causal, r_c[h:h + 1, :] - gb128, NEG))
            wcat.append((d * s_ref[rows, h * 128:(h + 1) * 128]).astype(BF16))
            vblocks.append(jnp.concatenate([jnp.where(hmask[h], v_all, jnp.zeros_like(v_all)), ones_blk[h]], axis=1))
            g = (h * hd) // 128
            srows.append(jnp.concatenate([cblk[h] if j == g else zblk for j in range(ngrp)]
                                         + [nblk[h] if j == g else zblk for j in range(ngrp)], axis=1))
        s_bf = jnp.concatenate(srows, axis=0).astype(BF16)
        out_aug = _dot(jnp.concatenate(wcat, axis=1), jnp.concatenate(vblocks, axis=0))
        out_aug = out_aug + _dot((qk_ref[rows, 0:w] * w_inter).astype(BF16), s_bf)
        hh_ref[rows, :] = out_aug[:, 0:w] / jnp.maximum(jnp.abs(out_aug[:, w:2 * w]), e_negm)
        kw = (qk_ref[rows, w:2 * w] * jnp.exp(r64_ref[rows, :] - glast_l)).astype(BF16)
        upd = _dot_tn(kw, jnp.concatenate([v_all, ones_cols], axis=1))
        for h in range(nh):
            rs = slice(h * hd, (h + 1) * hd)
            cblk[h] = cblk[h] * decay[h:h + 1, :] + jnp.where(half[h], upd[rs, grp[h]], 0.0)
            nblk[h] = (nblk[h] * decay[h:h + 1, :]
                       + jnp.where(half[h], upd[rs, w + grp[h].start:w + grp[h].stop], 0.0))
        m_prev = bl_ref[t8, :] + g_last
    m_ref[...] = m_prev
    for h in range(nh):
        st_ref[h * hd:(h + 1) * hd, grp[h]] = cblk[h]
        st_ref[h * hd:(h + 1) * hd, w + grp[h].start:w + grp[h].stop] = nblk[h]

    hh = hh_ref[...]
    ms = _dot((hh * hh).astype(BF16), bones_ref[...])
    og = zc_ref[:, 3 * w:4 * w].astype(F32)
    o_ref[...] = (_sigmoid(og) * hh * lax.rsqrt(ms + EPS) * ng_ref[...]).astype(o_ref.dtype)


def _mlstm(zc3, zgt, conv_w, gate_b_rows, norm_g):
    b, s, w = zc3.shape
    ts = SEQ_TILE
    nt = s // ts
    gw = GROUP_W
    t = jnp.arange(CHUNK)
    triu = (t[:, None] <= t[None, :]).astype(BF16)
    hid = jnp.arange(gw) // HEAD_DIM
    bones = jnp.where(hid[:, None] == hid[None, :], 1.0 / HEAD_DIM, 0.0).astype(BF16)
    full = lambda a: pl.BlockSpec(a.shape, lambda bi, i: (0,) * a.ndim)
    return pl.pallas_call(
        _mlstm_kernel,
        grid=(b, nt),
        in_specs=[pl.BlockSpec((None, ts, w), lambda bi, i: (bi, i, 0)),
                  pl.BlockSpec((8, ts), lambda bi, i: (0, bi * nt + i)),
                  full(conv_w), full(gate_b_rows), full(norm_g), full(triu), full(bones)],
        out_specs=pl.BlockSpec((None, ts, gw), lambda bi, i: (bi, i, 0)),
        out_shape=jax.ShapeDtypeStruct((b, s, gw), BF16),
        scratch_shapes=[pltpu.VMEM((gw, 2 * gw), F32),
                        pltpu.VMEM((8, 128), F32),
                        pltpu.VMEM((8, 2 * gw), F32),
                        pltpu.VMEM((ts, 2 * gw), F32),
                        pltpu.VMEM((ts // CHUNK * 8, 128), F32),
                        pltpu.VMEM((ts // CHUNK * 8, 128), F32),
                        pltpu.VMEM((ts // CHUNK * 8, 128), F32),
                        pltpu.VMEM((ts, MLSTM_HEADS * 128), F32),
                        pltpu.VMEM((ts, gw), F32),
                        pltpu.VMEM((ts, gw), F32),
                        pltpu.VMEM((ts, gw), F32),
                        pltpu.VMEM((ts, MLSTM_HEADS * 128), F32),
                        pltpu.VMEM((ts, gw), F32)],
        compiler_params=_cparams("parallel", "arbitrary"),
        name="mlstm",
    )(zc3, zgt, conv_w, gate_b_rows, norm_g, triu, bones)


def _sgu_kernel(zd_ref, lng_ref, lnb_ref, w_ref, bias_ref, o_ref):
    tm = zd_ref.shape[0]
    gw = GROUP_W // SGU_GROUPS
    u = _gelu(zd_ref[:, 0:GROUP_W].astype(F32))
    v = _gelu(zd_ref[:, GROUP_W:2 * GROUP_W].astype(F32))
    mu = jnp.mean(v, axis=-1, keepdims=True)
    vc = v - mu
    var = jnp.mean(vc * vc, axis=-1, keepdims=True)
    vn = (vc * lax.rsqrt(var + EPS) * lng_ref[...] + lnb_ref[...]).astype(BF16)
    for c in range(tm // CHUNK):
        rows = slice(c * CHUNK, (c + 1) * CHUNK)
        mixed = jnp.concatenate([_dot(w_ref[g], vn[rows, g * gw:(g + 1) * gw]) for g in range(SGU_GROUPS)], axis=-1)
        o_ref[rows, :] = (u[rows, :] * (mixed + bias_ref[...])).astype(o_ref.dtype)


def _sgu(zd, ln_g, ln_b, w_tril, bias):
    n = zd.shape[0]
    tm = TOKEN_TILE
    full = lambda a: pl.BlockSpec(a.shape, lambda i: (0,) * a.ndim)
    return pl.pallas_call(
        _sgu_kernel,
        grid=(n // tm,),
        in_specs=[pl.BlockSpec((tm, 2 * GROUP_W), lambda i: (i, 0)), full(ln_g), full(ln_b), full(w_tril), full(bias)],
        out_specs=pl.BlockSpec((tm, GROUP_W), lambda i: (i, 0)),
        out_shape=jax.ShapeDtypeStruct((n, GROUP_W), BF16),
        compiler_params=_cparams("parallel"),
        name="sgu",
    )(zd, ln_g, ln_b, w_tril, bias)


def _out_proj_kernel(x_ref, ya_ref, yb_ref, yc_ref, yd_ref, gluw_ref, glub_ref, wo_ref, ng_ref,
                     rwh_ref, rwl_ref, rb_ref,
                     x1_ref, hx_ref, meta_ref, cnt_ref, carry_ref):
    tm = x_ref.shape[0]

    @pl.when(pl.program_id(0) == 0)
    def _():
        carry_ref[...] = jnp.zeros_like(carry_ref)

    yb = _gelu(yb_ref[...].astype(F32))
    yb = yb * _sigmoid(_dot(yb.astype(BF16), gluw_ref[...]) + glub_ref[...])
    mix = _dot(ya_ref[...], wo_ref[0:GROUP_W, :])
    mix = mix + _dot(yb.astype(BF16), wo_ref[GROUP_W:2 * GROUP_W, :])
    mix = mix + _dot(yc_ref[...], wo_ref[2 * GROUP_W:3 * GROUP_W, :])
    mix = mix + _dot(yd_ref[...], wo_ref[3 * GROUP_W:4 * GROUP_W, :])
    x1 = x_ref[...] + mix
    x1_ref[...] = x1
    h = _rms(x1, ng_ref[...])
    hx_ref[:, 0:PACK_W] = _pack_rows(h)

    hh, hl = _split_bf16(h)
    wh = rwh_ref[...]
    wl = rwl_ref[...]
    logits = _dot_nt(wh, hh) + _dot_nt(wh, hl) + _dot_nt(wl, hh) + rb_ref[...]
    gl = [logits[j:j + 1, :] for j in range(N_GROUPS)]
    gmax = functools.reduce(jnp.maximum, gl)
    gsel = jnp.full((1, tm), N_GROUPS - 1, I32)
    for j in range(N_GROUPS - 2, -1, -1):
        gsel = jnp.where(gl[j] == gmax, j, gsel)
    p_g = 1.0 / functools.reduce(jnp.add, [jnp.exp(v - gmax) for v in gl])
    e_in = []
    for i in range(EPG):
        v = logits[N_GROUPS + i:N_GROUPS + i + 1, :]
        for j in range(1, N_GROUPS):
            r = N_GROUPS + j * EPG + i
            v = jnp.where(gsel == j, logits[r:r + 1, :], v)
        e_in.append(v)
    v1 = functools.reduce(jnp.maximum, e_in)
    i1 = jnp.full((1, tm), EPG - 1, I32)
    for i in range(EPG - 2, -1, -1):
        i1 = jnp.where(e_in[i] == v1, i, i1)
    rest = [jnp.where(i1 == i, NEG, e_in[i]) for i in range(EPG)]
    v2 = functools.reduce(jnp.maximum, rest)
    i2 = jnp.full((1, tm), EPG - 1, I32)
    for i in range(EPG - 2, -1, -1):
        i2 = jnp.where((rest[i] == v2) & (i1 != i), i, i2)
    e2 = jnp.exp(v2 - v1)
    w1 = p_g / (1.0 + e2)
    w2 = p_g * e2 / (1.0 + e2)
    lo = jnp.minimum(i1, i2)
    hi = jnp.maximum(i1, i2)
    w_lo = jnp.where(i1 < i2, w1, w2)
    w_hi = jnp.where(i1 < i2, w2, w1)
    bucket = gsel * PAIRS + ((lo * (2 * EPG - 1 - lo)) >> 1) + (hi - lo - 1)

    kid = lax.broadcasted_iota(I32, (BUCKET_PAD, tm), 0)
    onehot = (kid == bucket).astype(F32)
    s_i = lax.broadcasted_iota(I32, (tm, tm), 0)
    t_i = lax.broadcasted_iota(I32, (tm, tm), 1)
    prefix = _dot(onehot.astype(BF16), (s_i <= t_i).astype(BF16))
    carry = carry_ref[...]
    rank = jnp.sum(onehot * (prefix - 1.0 + carry[:, 0:1]), axis=0, keepdims=True)
    carry = carry + prefix[:, tm - 1:tm]
    carry_ref[...] = carry
    cnt_ref[...] = carry
    meta_ref[...] = jnp.concatenate([bucket, rank.astype(I32), jnp.zeros((6, tm), I32)], axis=0)
    wrows = jnp.concatenate([w_lo, w_hi, jnp.zeros((126, tm), F32)], axis=0)
    for c in range(tm // 128):
        hx_ref[c * 128:(c + 1) * 128, PACK_W:ROW_W] = pltpu.bitcast(wrows[:, c * 128:(c + 1) * 128].T, U32)


def _out_proj(x2, ya, yb, yc, yd, glu_w, glu_b, w_out, norm_g, rw_hi, rw_lo, rb):
    n = x2.shape[0]
    tm = TOKEN_TILE
    row = lambda w: pl.BlockSpec((tm, w), lambda i: (i, 0))
    full = lambda a: pl.BlockSpec(a.shape, lambda i: (0,) * a.ndim)
    return pl.pallas_call(
        _out_proj_kernel,
        grid=(n // tm,),
        in_specs=[row(D_MODEL), row(GROUP_W), row(GROUP_W), row(GROUP_W), row(GROUP_W),
                  full(glu_w), full(glu_b), full(w_out), full(norm_g), full(rw_hi), full(rw_lo), full(rb)],
        out_specs=[row(D_MODEL), row(ROW_W), pl.BlockSpec((8, tm), lambda i: (0, i)),
                   pl.BlockSpec((BUCKET_PAD, 128), lambda i: (0, 0))],
        out_shape=[jax.ShapeDtypeStruct((n, D_MODEL), F32), jax.ShapeDtypeStruct((n, ROW_W), U32),
                   jax.ShapeDtypeStruct((8, n), I32), jax.ShapeDtypeStruct((BUCKET_PAD, 128), F32)],
        scratch_shapes=[pltpu.VMEM((BUCKET_PAD, 128), F32)],
        compiler_params=_cparams("arbitrary"),
        name="out_proj",
    )(x2, ya, yb, yc, yd, glu_w, glu_b, w_out, norm_g, rw_hi, rw_lo, rb)


def _row_copy(src_ref, src_row, dst_ref, dst_row, sem):
    return pltpu.make_async_copy(src_ref.at[pl.ds(src_row, 1)], dst_ref.at[pl.ds(dst_row, 1)], sem)


def _dispatch_kernel(dest_ref, hx_ref, xs_in_ref, xs_ref, sem):
    del xs_in_ref
    tm = hx_ref.shape[0]

    def start(t, c):
        _row_copy(hx_ref, t, xs_ref, dest_ref[0, 0, t], sem).start()
        return c

    lax.fori_loop(0, tm, start, 0, unroll=ROW_DMA_UNROLL)
    pltpu.make_async_copy(hx_ref, xs_ref.at[pl.ds(0, tm)], sem).wait()


def _dispatch(dest3, hx, p_tot):
    n = hx.shape[0]
    tm = TOKEN_TILE
    xs0 = jnp.zeros((p_tot, ROW_W), U32)
    return pl.pallas_call(
        _dispatch_kernel,
        grid=(n // tm,),
        in_specs=[pl.BlockSpec((1, 1, tm), lambda i: (i, 0, 0), memory_space=pltpu.SMEM),
                  pl.BlockSpec((tm, ROW_W), lambda i: (i, 0)),
                  pl.BlockSpec(memory_space=pl.ANY)],
        out_specs=pl.BlockSpec(memory_space=pl.ANY),
        out_shape=jax.ShapeDtypeStruct((p_tot, ROW_W), U32),
        input_output_aliases={2: 0},
        scratch_shapes=[pltpu.SemaphoreType.DMA],
        compiler_params=_cparams("arbitrary"),
        name="dispatch",
    )(dest3, hx, xs0)


def _ffn_kernel(valid_ref, xblk_ref, run_ref, exprun_ref, nrun_ref, xs_ref, wg_hbm, wu_hbm, wd_hbm, ys_ref,
                stg_g, stg_u, stg_d, wgu_a, wdn_a, wgu_b, wdn_b, sem, *, layer):
    del xblk_ref
    j = pl.program_id(0)
    valid = valid_ref[j]
    jp = jnp.maximum(j - 1, 0)

    def expert_copies(s, r, slot):
        e = exprun_ref[s, r]
        return [pltpu.make_async_copy(src.at[layer, e], dst.at[s, slot], sem.at[s, slot])
                for src, dst in ((wg_hbm, stg_g), (wu_hbm, stg_u), (wd_hbm, stg_d))]

    for s, (wgu, wdn) in enumerate(((wgu_a, wdn_a), (wgu_b, wdn_b))):
        r = run_ref[s, j]

        @pl.when((j == 0) | (r != run_ref[s, jp]))
        def _():
            slot = r % 2

            @pl.when(j == 0)
            def _():
                for c in expert_copies(s, r, slot):
                    c.start()

            for c in expert_copies(s, r, slot):
                c.wait()

            @pl.when(r + 1 < nrun_ref[s])
            def _():
                for c in expert_copies(s, r + 1, 1 - slot):
                    c.start()

            wgu[:, 0:D_EXPERT] = stg_g[s, slot].astype(BF16)
            wgu[:, D_EXPERT:2 * D_EXPERT] = stg_u[s, slot].astype(BF16)
            wdn[...] = stg_d[s, slot].astype(BF16)

    for sb in range(FFN_BLOCK // FFN_SUB):
        rows = slice(sb * FFN_SUB, (sb + 1) * FFN_SUB)

        @pl.when(valid <= sb * FFN_SUB)
        def _():
            ys_ref[rows, :] = jnp.zeros((FFN_SUB, PACK_W), U32)

        @pl.when(valid > sb * FFN_SUB)
        def _():
            x_lo, x_hi = _unpack_rows(xs_ref[rows, 0:PACK_W])
            xb = jnp.concatenate([x_lo.astype(BF16), x_hi.astype(BF16)], axis=1)
            wts = pltpu.bitcast(xs_ref[rows, PACK_W:ROW_W], F32)

            def expert(wgu, wdn):
                gu = _dot(xb, wgu[...])
                g = gu[:, 0:D_EXPERT]
                act = (g * _sigmoid(g) * gu[:, D_EXPERT:2 * D_EXPERT]).astype(BF16)
                return _dot(act, wdn[...])

            ya = expert(wgu_a, wdn_a)
            yb = expert(wgu_b, wdn_b)
            ys_ref[rows, :] = _pack_rows(ya * wts[:, 0:1] + yb * wts[:, 1:2])


def _ffn(valid, xblk, run, exprun, nrun, xs, layer, w_gate, w_up, w_down):
    p_tot = xs.shape[0]
    bm = FFN_BLOCK
    nblk = p_tot // bm
    up_shape = (D_MODEL, D_EXPERT)
    dn_shape = (D_EXPERT, D_MODEL)
    hbm = pl.BlockSpec(memory_space=pl.ANY)
    return pl.pallas_call(
        functools.partial(_ffn_kernel, layer=layer),
        grid_spec=pltpu.PrefetchScalarGridSpec(
            num_scalar_prefetch=5,
            grid=(nblk,),
            in_specs=[pl.BlockSpec((bm, ROW_W), lambda j, va, xb, ru, er, nr: (xb[j], 0)), hbm, hbm, hbm],
            out_specs=pl.BlockSpec((bm, PACK_W), lambda j, va, xb, ru, er, nr: (j, 0)),
            scratch_shapes=[pltpu.VMEM((2, 2) + up_shape, F32), pltpu.VMEM((2, 2) + up_shape, F32),
                            pltpu.VMEM((2, 2) + dn_shape, F32),
                            pltpu.VMEM((D_MODEL, 2 * D_EXPERT), BF16), pltpu.VMEM(dn_shape, BF16),
                            pltpu.VMEM((D_MODEL, 2 * D_EXPERT), BF16), pltpu.VMEM(dn_shape, BF16),
                            pltpu.SemaphoreType.DMA((2, 2))],
        ),
        out_shape=jax.ShapeDtypeStruct((p_tot, PACK_W), U32),
        compiler_params=_cparams("arbitrary"),
        name="ffn",
    )(valid, xblk, run, exprun, nrun, xs, w_gate, w_up, w_down)


def _combine_kernel(dest_ref, x1_ref, ys_ref, ng_ref, o_ref, buf_ref, sem, *, final_norm):
    tm = x1_ref.shape[0]

    def start(t, c):
        _row_copy(ys_ref, dest_ref[0, 0, t], buf_ref, t, sem).start()
        return c

    lax.fori_loop(0, tm, start, 0, unroll=ROW_DMA_UNROLL)
    pltpu.make_async_copy(ys_ref.at[pl.ds(0, tm)], buf_ref, sem).wait()
    y_lo, y_hi = _unpack_rows(buf_ref[...])
    x2 = x1_ref[...] + jnp.concatenate([y_lo, y_hi], axis=1)
    o_ref[...] = _rms(x2, ng_ref[...]) if final_norm else x2


def _combine(dest3, x1, ys, norm_g, final_norm):
    n = x1.shape[0]
    tm = TOKEN_TILE
    return pl.pallas_call(
        functools.partial(_combine_kernel, final_norm=final_norm),
        grid=(n // tm,),
        in_specs=[pl.BlockSpec((1, 1, tm), lambda i: (i, 0, 0), memory_space=pltpu.SMEM),
                  pl.BlockSpec((tm, D_MODEL), lambda i: (i, 0)),
                  pl.BlockSpec(memory_space=pl.ANY),
                  pl.BlockSpec((1, D_MODEL), lambda i: (0, 0))],
        out_specs=pl.BlockSpec((tm, D_MODEL), lambda i: (i, 0)),
        out_shape=jax.ShapeDtypeStruct((n, D_MODEL), F32),
        scratch_shapes=[pltpu.VMEM((tm, PACK_W), U32), pltpu.SemaphoreType.DMA],
        compiler_params=_cparams("arbitrary"),
        name="combine",
    )(dest3, x1, ys, norm_g)


def _routing_tables(meta, counts, n):
    bm = FFN_BLOCK
    nblk = n // bm + N_BUCKETS
    bucket = meta[0]
    rank = meta[1]
    cnt = counts[:N_BUCKETS, 0].astype(I32)
    padded = ((cnt + bm - 1) // bm) * bm
    pad_end = jnp.cumsum(padded)
    pad_start = pad_end - padded
    onehot = bucket[:, None] == jnp.arange(N_BUCKETS, dtype=I32)[None, :]
    dest = rank + jnp.sum(jnp.where(onehot, pad_start[None, :], 0), axis=1)
    nact = (pad_end[-1] // bm).astype(I32)
    j = jnp.arange(nblk, dtype=I32)
    blk = jnp.minimum(j, jnp.maximum(nact - 1, 0))
    pos = blk * bm
    bkt = jnp.minimum(jnp.sum((pad_end[None, :] <= pos[:, None]).astype(I32), axis=1), N_BUCKETS - 1)
    in_bkt = bkt[:, None] == jnp.arange(N_BUCKETS, dtype=I32)[None, :]
    sel = lambda tab: jnp.sum(jnp.where(in_bkt, tab[None, :], 0), axis=1)
    valid = jnp.where(j < nact, jnp.clip(sel(cnt) - (pos - sel(pad_start)), 0, bm), 0)
    lo_tab, hi_tab = [], []
    for g in range(N_GROUPS):
        for a in range(EPG):
            for b in range(a + 1, EPG):
                lo_tab.append(g * EPG + a)
                hi_tab.append(g * EPG + b)
    experts = jnp.stack([sel(jnp.asarray(lo_tab, I32)), sel(jnp.asarray(hi_tab, I32))]).astype(I32)
    change = jnp.concatenate([jnp.zeros((2, 1), I32), (experts[:, 1:] != experts[:, :-1]).astype(I32)], axis=1)
    run = jnp.cumsum(change, axis=1).astype(I32)
    is_run = run[:, :, None] == j[None, None, :]
    exprun = jnp.max(jnp.where(is_run, experts[:, :, None], 0), axis=1).astype(I32)
    nrun = run[:, -1] + 1
    return dest.astype(I32), valid.astype(I32), blk.astype(I32), run, exprun, nrun.astype(I32), nblk * bm


def _layer(x2, b, s, p, final_g):
    n = b * s
    za, zb, zc, zd, zgt = _in_proj(x2, p['norm_mix_g'], p['wa'], p['wb'], p['wc'], p['wd'], p['wgt'])
    ya = _attention(za.reshape(b, s, -1), p['sinks']).reshape(n, GROUP_W)
    yb = _ssm(zb.reshape(b, s, GROUP_W), p['ssm']).reshape(n, GROUP_W)
    yc = _mlstm(zc.reshape(b, s, -1), zgt, p['conv_w'], p['gate_b'], p['mlstm_norm_g']).reshape(n, GROUP_W)
    yd = _sgu(zd, p['sgu_ln_g'], p['sgu_ln_b'], p['sgu_w'], p['sgu_bias'])
    x1, hx, meta, counts = _out_proj(x2, ya, yb, yc, yd, p['glu_w'], p['glu_b'], p['w_out'], p['norm_ffn_g'],
                                     p['rw_hi'], p['rw_lo'], p['rb'])
    dest, valid, xblk, run, exprun, nrun, p_tot = _routing_tables(meta, counts, n)
    dest3 = dest.reshape(n // TOKEN_TILE, 1, TOKEN_TILE)
    xs = _dispatch(dest3, hx, p_tot)
    ys = _ffn(valid, xblk, run, exprun, nrun, xs, p['layer'], p['w_gate'], p['w_up'], p['w_down'])
    if final_g is None:
        return _combine(dest3, x1, ys, p['norm_ffn_g'], False)
    return _combine(dest3, x1, ys, final_g, True)


def _prep_layer(l, norm_mix_g, w_in, attn_sinks, ssm_a_re, ssm_a_im, ssm_b_re, ssm_b_im, ssm_c_re, ssm_c_im,
                ssm_d, ssm_log_dt, ssm_glu_w, ssm_glu_b, mlstm_conv_w, mlstm_gate_b, mlstm_norm_g,
                sgu_ln_g, sgu_ln_b, sgu_w, sgu_b, w_out, norm_ffn_g, router_group_w, router_group_b,
                router_expert_w, router_expert_b, expert_w_gate, expert_w_up, expert_w_down):
    w = w_in[l]
    o_su = 2 * GROUP_W
    o_c = o_su + GROUP_W
    o_ci = o_c + 3 * GROUP_W
    o_co = o_ci + 2 * MLSTM_HEADS
    o_d = o_co + GROUP_W
    wc = jnp.concatenate([w[:, o_c:o_ci], w[:, o_co:o_d]], axis=1)
    wgt = w[:, o_ci:o_co].T
    head_cols = lambda m, hq: m[..., hq * HEAD_DIM:(hq + 1) * HEAD_DIM]
    order = [g * ATTN_REP + r for r in range(ATTN_REP) for g in range(KV_HEADS)]
    wa = jnp.concatenate([head_cols(w, hq) * (HEAD_DIM ** -0.5) for hq in order] + [w[:, GROUP_W:o_su]], axis=1)
    wo = jnp.concatenate([w_out[l][hq * HEAD_DIM:(hq + 1) * HEAD_DIM] for hq in order] + [w_out[l][GROUP_W:]], axis=0)
    rw = jnp.concatenate([router_group_w[l], router_expert_w[l]], axis=1).T.astype(F32)
    rw = jnp.pad(rw, ((0, 4), (0, 0)))
    rw_hi = rw.astype(BF16)
    rw_lo = (rw - rw_hi.astype(F32)).astype(BF16)
    rb = jnp.pad(jnp.concatenate([router_group_b[l], router_expert_b[l]]).astype(F32), (0, 4))[:, None]
    tril = jnp.tril(jnp.ones((CHUNK, CHUNK), F32))
    gw = GROUP_W // SGU_GROUPS
    return dict(
        norm_mix_g=norm_mix_g[l][None, :].astype(F32),
        wa=wa.astype(BF16), wb=w[:, o_su:o_c].astype(BF16), wc=wc.astype(BF16),
        wd=w[:, o_d:].astype(BF16), wgt=wgt.astype(BF16),
        sinks=attn_sinks[l].astype(F32),
        ssm=_ssm_matrices(ssm_a_re[l], ssm_a_im[l], ssm_b_re[l], ssm_b_im[l], ssm_c_re[l], ssm_c_im[l],
                          ssm_d[l], ssm_log_dt[l]),
        glu_w=ssm_glu_w[l].astype(BF16), glu_b=ssm_glu_b[l][None, :].astype(F32),
        conv_w=mlstm_conv_w[l].astype(F32),
        gate_b=jnp.broadcast_to(mlstm_gate_b[l].astype(F32)[:, None], (2 * MLSTM_HEADS, 128)),
        mlstm_norm_g=mlstm_norm_g[l][None, :].astype(F32),
        sgu_ln_g=sgu_ln_g[l][None, :].astype(F32), sgu_ln_b=sgu_ln_b[l][None, :].astype(F32),
        sgu_w=(sgu_w[l].astype(F32) * tril).astype(BF16),
        sgu_bias=jnp.repeat(sgu_b[l].astype(F32).T, gw, axis=1),
        w_out=wo.astype(BF16), norm_ffn_g=norm_ffn_g[l][None, :].astype(F32),
        rw_hi=rw_hi, rw_lo=rw_lo, rb=rb,
        layer=l, w_gate=expert_w_gate, w_up=expert_w_up, w_down=expert_w_down,
    )


def kernel(x, norm_mix_g, w_in, attn_sinks, ssm_a_re, ssm_a_im, ssm_b_re, ssm_b_im, ssm_c_re, ssm_c_im, ssm_d, ssm_log_dt, ssm_glu_w, ssm_glu_b, mlstm_conv_w, mlstm_gate_b, mlstm_norm_g, sgu_ln_g, sgu_ln_b, sgu_w, sgu_b, w_out, norm_ffn_g, router_group_w, router_group_b, router_expert_w, router_expert_b, expert_w_gate, expert_w_up, expert_w_down, norm_final_g):
    b, s, d = x.shape
    depth = w_in.shape[0]
    x2 = x.reshape(b * s, d).astype(F32)
    for l in range(depth):
        p = _prep_layer(l, norm_mix_g, w_in, attn_sinks, ssm_a_re, ssm_a_im, ssm_b_re, ssm_b_im, ssm_c_re,
                        ssm_c_im, ssm_d, ssm_log_dt, ssm_glu_w, ssm_glu_b, mlstm_conv_w, mlstm_gate_b,
                        mlstm_norm_g, sgu_ln_g, sgu_ln_b, sgu_w, sgu_b, w_out, norm_ffn_g, router_group_w,
                        router_group_b, router_expert_w, router_expert_b, expert_w_gate, expert_w_up,
                        expert_w_down)
        final_g = norm_final_g[None, :].astype(F32) if l == depth - 1 else None
        x2 = _layer(x2, b, s, p, final_g)
    return x2.reshape(b, s, d).astype(x.dtype)
```

```python
import functools
import math

import jax
import jax.numpy as jnp
from jax import lax
from jax.experimental import pallas as pl
from jax.experimental.pallas import tpu as pltpu

F32 = jnp.float32
BF16 = jnp.bfloat16
I32 = jnp.int32
U32 = jnp.uint32

D_MODEL = 1024
GROUP_W = 256
HEAD_DIM = 64
EPS = 1e-6
NEG = -1e30
WINDOW = 128
KV_HEADS = 2
ATTN_REP = 2
SSM_GROUP = 16
SSM_GROUPS = 16
SSM_STATE = 64
SSM_CHUNK = 16
MLSTM_HEADS = 4
CHUNK = 128
CONV_K = 4
SGU_GROUPS = 4
N_GROUPS = 4
EPG = 8
N_EXPERTS = 32
D_EXPERT = 512
PAIRS = EPG * (EPG - 1) // 2
N_BUCKETS = N_GROUPS * PAIRS
BUCKET_PAD = 128
PACK_W = D_MODEL // 2
ROW_W = PACK_W + 128

TOKEN_TILE = 512
SEQ_TILE = 512
SSM_TILE = 4096
SSM_SLAB = 64
FFN_BLOCK = 512
FFN_SUB = 256
SUBLANES = 8
VMEM_LIMIT = 56 * 1024 * 1024


def _cparams(*sem):
    return pltpu.CompilerParams(dimension_semantics=sem, vmem_limit_bytes=VMEM_LIMIT)


def _rms(x, g):
    return x * lax.rsqrt(jnp.mean(x * x, axis=-1, keepdims=True) + EPS) * g


def _gelu(x):
    return 0.5 * x * (1.0 + jnp.tanh(math.sqrt(2.0 / math.pi) * (x + 0.044715 * (x * x * x))))


def _sigmoid(x):
    return 1.0 / (1.0 + jnp.exp(-x))


def _log_sigmoid(x):
    return jnp.minimum(x, 0.0) - jnp.log(1.0 + jnp.exp(-jnp.abs(x)))


def _dot(a, b):
    return jnp.dot(a, b, preferred_element_type=F32)


def _dot_nt(a, b):
    return lax.dot_general(a, b, (((1,), (1,)), ((), ())), preferred_element_type=F32)


def _dot_tn(a, b):
    return lax.dot_general(a, b, (((0,), (0,)), ((), ())), preferred_element_type=F32)


def _split_bf16(x):
    hi = x.astype(BF16)
    lo = (x - hi.astype(F32)).astype(BF16)
    return hi, lo


def _pack_rows(x):
    w = x.shape[1] // 2
    bits = lambda v: pltpu.bitcast(v.astype(BF16).astype(F32), U32)
    return (bits(x[:, 0:w]) >> 16) | (bits(x[:, w:2 * w]) & jnp.uint32(0xFFFF0000))


def _unpack_rows(p):
    return pltpu.bitcast(p << 16, F32), pltpu.bitcast(p & jnp.uint32(0xFFFF0000), F32)


def _in_proj_kernel(x_ref, g_ref, wa_ref, wb_ref, wc_ref, wd_ref, wgt_ref,
                    za_ref, zb_ref, zc_ref, zd_ref, zgt_ref):
    hb = _rms(x_ref[...], g_ref[...]).astype(BF16)
    za_ref[...] = _dot(hb, wa_ref[...]).astype(BF16)
    zb_ref[...] = _dot(hb, wb_ref[...]).astype(BF16)
    zc_ref[...] = _dot(hb, wc_ref[...]).astype(BF16)
    zd_ref[...] = _dot(hb, wd_ref[...]).astype(BF16)
    zgt_ref[...] = _dot_nt(wgt_ref[...], hb)


def _in_proj(x2, g, wa, wb, wc, wd, wgt):
    n = x2.shape[0]
    tm = TOKEN_TILE
    row = lambda w: pl.BlockSpec((tm, w), lambda i: (i, 0))
    full = lambda a: pl.BlockSpec(a.shape, lambda i: (0,) * a.ndim)
    widths = (wa.shape[1], wb.shape[1], wc.shape[1], wd.shape[1])
    return pl.pallas_call(
        _in_proj_kernel,
        grid=(n // tm,),
        in_specs=[row(D_MODEL), full(g), full(wa), full(wb), full(wc), full(wd), full(wgt)],
        out_specs=[row(w) for w in widths] + [pl.BlockSpec((wgt.shape[0], tm), lambda i: (0, i))],
        out_shape=[jax.ShapeDtypeStruct((n, w), BF16) for w in widths]
        + [jax.ShapeDtypeStruct((wgt.shape[0], n), F32)],
        compiler_params=_cparams("parallel"),
        name="in_proj",
    )(x2, g, wa, wb, wc, wd, wgt)


def _attn_kernel(sink_ref, cur_ref, prev_ref, o_ref):
    first = pl.program_id(1) == 0
    nblk = cur_ref.shape[0] // WINDOW
    row = lax.broadcasted_iota(I32, (WINDOW, 2 * WINDOW), 0)
    col = lax.broadcasted_iota(I32, (WINDOW, 2 * WINDOW), 1)
    band = (col <= row + WINDOW) & (col > row)
    lane = lax.broadcasted_iota(I32, (1, 128), 1)
    kv_lanes = [(lane // HEAD_DIM) == g for g in range(KV_HEADS)]
    ones = jnp.ones((2 * WINDOW, 128), BF16)
    ko = 2 * 128
    vo = 3 * 128
    for j in range(nblk):
        cur = cur_ref[j * WINDOW:(j + 1) * WINDOW, :]
        if j == 0:
            prev = prev_ref[...]
            mask = band & ((col >= WINDOW) | jnp.logical_not(first))
        else:
            prev = cur_ref[(j - 1) * WINDOW:j * WINDOW, :]
            mask = band
        kk = jnp.concatenate([prev[:, ko:ko + 128], cur[:, ko:ko + 128]], axis=0)
        vaug = jnp.concatenate([jnp.concatenate([prev[:, vo:vo + 128], cur[:, vo:vo + 128]], axis=0), ones], axis=1)
        outs = []
        for r in range(ATTN_REP):
            qg = cur[:, r * 128:(r + 1) * 128]
            og = []
            for g in range(KV_HEADS):
                s = _dot_nt(jnp.where(kv_lanes[g], qg, jnp.zeros_like(qg)), kk)
                s = jnp.where(mask, s, NEG)
                sink = sink_ref[g * ATTN_REP + r]
                m = jnp.maximum(jnp.max(s, axis=-1, keepdims=True), sink)
                oa = _dot(jnp.exp(s - m).astype(BF16), vaug)
                og.append(oa[:, 0:128] / (oa[:, 128:256] + jnp.exp(sink - m)))
            outs.append(jnp.where(kv_lanes[0], og[0], og[1]))
        o_ref[j * WINDOW:(j + 1) * WINDOW, :] = jnp.concatenate(outs, axis=-1).astype(o_ref.dtype)


def _attention(za3, sinks):
    b, s, w = za3.shape
    ts = SEQ_TILE
    per = ts // WINDOW
    return pl.pallas_call(
        _attn_kernel,
        grid=(b, s // ts),
        in_specs=[pl.BlockSpec(memory_space=pltpu.SMEM),
                  pl.BlockSpec((None, ts, w), lambda bi, i: (bi, i, 0)),
                  pl.BlockSpec((None, WINDOW, w), lambda bi, i: (bi, jnp.maximum(i * per - 1, 0), 0))],
        out_specs=pl.BlockSpec((None, ts, GROUP_W), lambda bi, i: (bi, i, 0)),
        out_shape=jax.ShapeDtypeStruct((b, s, GROUP_W), BF16),
        compiler_params=_cparams("parallel", "parallel"),
        name="attn",
    )(sinks, za3, za3)


def _ssm_kernel(zb_ref, bre_ref, bim_ref, t_ref, cre_ref, cim_ref, are_ref, aim_ref, o_ref,
                x_ref, xs_ref, u_ref, vre_ref, vim_ref, sre_ref, sim_ref, y_ref, st_ref):
    ts = zb_ref.shape[0]
    L, G, H = SSM_CHUNK, SSM_GROUPS, SSM_GROUP
    nch = ts // L
    half_g = 128 // H
    n_half = G // half_g

    @pl.when(pl.program_id(1) == 0)
    def _():
        st_ref[...] = jnp.zeros_like(st_ref)

    blk = lax.broadcasted_iota(I32, (1, 128), 1) // H

    def block_transpose(arrs):
        a = list(arrs)
        s = half_g // 2
        while s >= 1:
            keep = (blk & s) == 0
            for i in range(half_g):
                if i & s == 0:
                    ai, aj = a[i], a[i + s]
                    a[i] = jnp.where(keep, ai, pltpu.roll(aj, s * H, axis=1))
                    a[i + s] = jnp.where(keep, pltpu.roll(ai, 128 - s * H, axis=1), aj)
            s //= 2
        return a

    for hf in range(n_half):
        x_ref[hf] = zb_ref[:, hf * 128:(hf + 1) * 128].astype(F32)
    for sg in range(L):
        for hf in range(n_half):
            xs_ref[sg * n_half + hf] = x_ref[hf, pl.ds(sg, nch, stride=L), :]
    for hf in range(n_half):
        for oc in range(L // half_g):
            for r0 in range(0, nch, SSM_SLAB):
                rows = slice(r0, r0 + SSM_SLAB)
                t = block_transpose([xs_ref[(oc * half_g + k) * n_half + hf, rows, :] for k in range(half_g)])
                for gl in range(half_g):
                    u_ref[hf * half_g + gl, rows, oc * 128:(oc + 1) * 128] = t[gl].astype(BF16)
    for g in range(G):
        ug = u_ref[g]
        vre_ref[pl.ds(g, nch, stride=G), :] = _dot(ug, bre_ref[g])
        vim_ref[pl.ds(g, nch, stride=G), :] = _dot(ug, bim_ref[g])

    are = are_ref[...]
    aim = aim_ref[...]

    def step(c, carry):
        sre, sim = carry
        r0 = pl.multiple_of(c * G, G)
        sre_ref[pl.ds(r0, G), :] = sre
        sim_ref[pl.ds(r0, G), :] = sim
        vre = vre_ref[pl.ds(r0, G), :]
        vim = vim_ref[pl.ds(r0, G), :]
        return (are * sre - aim * sim + vre, are * sim + aim * sre + vim)

    sre, sim = lax.fori_loop(0, nch, step, (st_ref[0:G, :], st_ref[G:2 * G, :]), unroll=4)
    st_ref[0:G, :] = sre
    st_ref[G:2 * G, :] = sim

    for g in range(G):
        y = _dot(u_ref[g], t_ref[g])
        y = y + _dot(sre_ref[pl.ds(g, nch, stride=G), :].astype(BF16), cre_ref[g])
        y = y + _dot(sim_ref[pl.ds(g, nch, stride=G), :].astype(BF16), cim_ref[g])
        y_ref[g] = y
    for hf in range(n_half):
        for oc in range(L // half_g):
            for r0 in range(0, nch, SSM_SLAB):
                t = block_transpose([y_ref[hf * half_g + gl, r0:r0 + SSM_SLAB, oc * 128:(oc + 1) * 128]
                                     for gl in range(half_g)])
                for k in range(half_g):
                    x_ref[hf, pl.ds(oc * half_g + k + L * r0, SSM_SLAB, stride=L), :] = t[k]
    for hf in range(n_half):
        o_ref[:, hf * 128:(hf + 1) * 128] = x_ref[hf].astype(o_ref.dtype)


def _ssm_matrices(a_re, a_im, b_re, b_im, c_re, c_im, d_skip, log_dt):
    L = SSM_CHUNK
    a = lax.complex(a_re.astype(F32), a_im.astype(F32))
    dt = jnp.exp(log_dt.astype(F32))[:, None]
    adt = a * dt
    a_bar = jnp.exp(adt)
    b_bar = ((a_bar - 1.0) / a)[..., None] * lax.complex(b_re.astype(F32), b_im.astype(F32))
    c_mat = lax.complex(c_re.astype(F32), c_im.astype(F32))
    lag = jnp.arange(L + 1, dtype=F32)
    pw = jnp.exp(adt[None] * lag[:, None, None])
    kern = jnp.einsum('gop,dgp,gpi->dgoi', c_mat, pw[:L], b_bar).real
    sig = jnp.arange(L)[:, None]
    tau = jnp.arange(L)[None, :]
    d = tau - sig
    kt = kern[jnp.clip(d, 0, L - 1)]
    kt = jnp.where((d >= 0)[:, :, None, None, None], kt, 0.0)
    eye = jnp.eye(SSM_GROUP, dtype=F32)
    dsk = d_skip.astype(F32).reshape(SSM_GROUPS, SSM_GROUP)
    kt = kt + (d == 0)[:, :, None, None, None] * (dsk[:, :, None] * eye)[None, None]
    t_mat = kt.transpose(2, 0, 4, 1, 3).reshape(SSM_GROUPS, L * SSM_GROUP, L * SSM_GROUP)
    bm = pw[:L][::-1][:, :, :, None] * b_bar[None]
    bm = bm.transpose(1, 0, 3, 2).reshape(SSM_GROUPS, L * SSM_GROUP, SSM_STATE)
    cm = c_mat[None] * pw[1:][:, :, None, :]
    cm = cm.transpose(1, 3, 0, 2).reshape(SSM_GROUPS, SSM_STATE, L * SSM_GROUP)
    a_chunk = pw[L]
    pad = 128 - SSM_STATE
    pc = lambda m: jnp.pad(m, ((0, 0), (0, 0), (0, pad))).astype(BF16)
    pr = lambda m: jnp.pad(m, ((0, 0), (0, pad), (0, 0))).astype(BF16)
    pa = lambda m: jnp.pad(m, ((0, 0), (0, pad)))
    return (pc(bm.real), pc(bm.imag), t_mat.astype(BF16), pr(cm.real), pr(-cm.imag),
            pa(a_chunk.real), pa(a_chunk.imag))


def _ssm(zb3, mats):
    b, s, w = zb3.shape
    L, G = SSM_CHUNK, SSM_GROUPS
    ts = min(SSM_TILE, s)
    nch = ts // L
    full = lambda a: pl.BlockSpec(a.shape, lambda bi, i: (0,) * a.ndim)
    return pl.pallas_call(
        _ssm_kernel,
        grid=(b, s // ts),
        in_specs=[pl.BlockSpec((None, ts, w), lambda bi, i: (bi, i, 0))] + [full(m) for m in mats],
        out_specs=pl.BlockSpec((None, ts, w), lambda bi, i: (bi, i, 0)),
        out_shape=jax.ShapeDtypeStruct((b, s, w), BF16),
        scratch_shapes=[pltpu.VMEM((w // 128, ts, 128), F32),
                        pltpu.VMEM((L * (w // 128), nch, 128), F32),
                        pltpu.VMEM((G, nch, L * SSM_GROUP), BF16),
                        pltpu.VMEM((nch * G, 128), F32), pltpu.VMEM((nch * G, 128), F32),
                        pltpu.VMEM((nch * G, 128), F32), pltpu.VMEM((nch * G, 128), F32),
                        pltpu.VMEM((G, nch, L * SSM_GROUP), F32),
                        pltpu.VMEM((2 * G, 128), F32)],
        compiler_params=_cparams("parallel", "arbitrary"),
        name="ssm",
    )(zb3, *mats)


def _mlstm_kernel(zc_ref, zgt_ref, convw_ref, gb_ref, ng_ref, triu_ref, bones_ref, o_ref,
                  st_ref, m_ref, tail_ref, qk_ref,
                  rt_ref, bl_ref, rm_ref, cr128_ref, cr64_ref, b64_ref, r64_ref, s_ref, hh_ref):
    ts = zc_ref.shape[0]
    hd, nh, w = HEAD_DIM, MLSTM_HEADS, GROUP_W
    nchunk = ts // CHUNK
    nr = nchunk * 8
    lane1 = lax.broadcasted_iota(I32, (1, 128), 1)
    half = [(lane1 // hd) == (h % 2) for h in range(nh)]
    grp = [slice((h * hd) // 128 * 128, (h * hd) // 128 * 128 + 128) for h in range(nh)]

    @pl.when(pl.program_id(1) == 0)
    def _():
        st_ref[...] = jnp.zeros_like(st_ref)
        m_ref[...] = jnp.zeros_like(m_ref)
        tail_ref[...] = jnp.zeros_like(tail_ref)

    r_i = lax.broadcasted_iota(I32, (CHUNK, CHUNK), 0)
    c_i = lax.broadcasted_iota(I32, (CHUNK, CHUNK), 1)
    causal = c_i <= r_i
    k64 = lax.broadcasted_iota(I32, (8, w), 0)
    j64 = lax.broadcasted_iota(I32, (8, w), 1)
    sel64 = (j64 // hd == k64 % nh).astype(BF16)
    k128 = lax.broadcasted_iota(I32, (8, nh * 128), 0)
    j128 = lax.broadcasted_iota(I32, (8, nh * 128), 1)
    sel128 = (j128 // 128 == k128 % nh).astype(BF16)

    def lanes(t):
        lo = jnp.where(lane1 < hd, t[0:1, :], t[1:2, :])
        hi = jnp.where(lane1 < hd, t[2:3, :], t[3:4, :])
        return jnp.concatenate([lo, hi], axis=1)

    rowm = lax.broadcasted_iota(I32, (nr, 128), 0) % 8
    lanem = lax.broadcasted_iota(I32, (nr, 128), 1)
    graw = jnp.concatenate([zgt_ref[:, c * CHUNK:(c + 1) * CHUNK] + gb_ref[...] for c in range(nchunk)], axis=0)
    g2 = jnp.where(rowm < nh, graw, _log_sigmoid(graw))
    ghi, glo = _split_bf16(g2)
    cum = _dot(ghi, triu_ref[...]) + _dot(glo, triu_ref[...])
    b_t = pltpu.roll(cum, nr - 4, axis=0)
    r_t = g2 - b_t
    cr = r_t
    sh = 1
    while sh < CHUNK:
        cr = jnp.maximum(cr, jnp.where(lanem >= sh, pltpu.roll(cr, sh, axis=1), NEG))
        sh *= 2
    rt_ref[...] = r_t
    bl_ref[...] = jnp.broadcast_to(b_t[:, CHUNK - 1:CHUNK], (nr, 128))
    rm_ref[...] = jnp.broadcast_to(cr[:, CHUNK - 1:CHUNK], (nr, 128))

    x = zc_ref[:, 0:2 * w].astype(F32)
    xe = jnp.concatenate([tail_ref[...], x], axis=0)
    tail_ref[...] = x[ts - 8:, :]
    cw = convw_ref[...]
    acc = x * cw[CONV_K - 1:CONV_K, :]
    for sft in range(1, CONV_K):
        acc = acc + xe[8 - sft:8 - sft + ts, :] * cw[CONV_K - 1 - sft:CONV_K - sft, :]
    lane_qk = lax.broadcasted_iota(I32, (1, 2 * w), 1)
    qk_ref[...] = acc * _sigmoid(acc) * jnp.where(lane_qk < w, 1.0, hd ** -0.5)

    def hi_lo_rows(v):
        hi = v.astype(BF16).astype(F32)
        return jnp.where(rowm < nh, hi, pltpu.roll(v - hi, 4, axis=0))

    a_cr = hi_lo_rows(cr)
    a_b = hi_lo_rows(b_t)
    a_r = hi_lo_rows(r_t)
    for c in range(nchunk):
        rows = slice(c * CHUNK, (c + 1) * CHUNK)
        t8 = slice(c * 8, (c + 1) * 8)
        a_cr_c = a_cr[t8].astype(BF16)
        cr128_ref[rows, :] = _dot_tn(a_cr_c, sel128)
        cr64_ref[rows, :] = _dot_tn(a_cr_c, sel64)
        b64_ref[rows, :] = _dot_tn(a_b[t8].astype(BF16), sel64)
        r64_ref[rows, :] = _dot_tn(a_r[t8].astype(BF16), sel64)
        for h in range(nh):
            qg = qk_ref[rows, grp[h]].astype(BF16)
            kg = qk_ref[rows, w + grp[h].start:w + grp[h].stop].astype(BF16)
            s_ref[rows, h * 128:(h + 1) * 128] = _dot_nt(qg, jnp.where(half[h], kg, jnp.zeros_like(kg)))

    lane_w = lax.broadcasted_iota(I32, (1, w), 1)
    hmask = [(lane_w // hd) == h for h in range(nh)]
    ones_blk = [m.astype(BF16) * jnp.ones((CHUNK, 1), BF16) for m in hmask]
    ones_cols = jnp.ones((CHUNK, w), BF16)
    zblk = jnp.zeros((hd, 128), F32)
    ngrp = w // 128
    m_prev = m_ref[...]
    cblk = [st_ref[h * hd:(h + 1) * hd, grp[h]] for h in range(nh)]
    nblk = [st_ref[h * hd:(h + 1) * hd, w + grp[h].start:w + grp[h].stop] for h in range(nh)]
    for c in range(nchunk):
        rows = slice(c * CHUNK, (c + 1) * CHUNK)
        t8 = slice(c * 8, (c + 1) * 8)
        r_c = rt_ref[t8, :]
        g_last = jnp.maximum(rm_ref[t8, :], m_prev)
        decay = jnp.exp(m_prev - g_last)
        mprev_l = lanes(m_prev)
        glast_l = lanes(g_last)
        gb64 = jnp.maximum(cr64_ref[rows, :], mprev_l)
        w_inter = jnp.exp(mprev_l - gb64)
        e_negm = jnp.exp(-(b64_ref[rows, :] + gb64))
        v_all = zc_ref[rows, 2 * w:3 * w]
        wcat = []
        vblocks = []
        srows = []
        for h in range(nh):
            gb128 = jnp.maximum(cr128_ref[rows, h * 128:(h + 1) * 128], m_prev[h:h + 1, :])
            d = jnp.exp(jnp.where(causal, r_c[h:h + 1, :] - gb128, NEG))
            wcat.append((d * s_ref[rows, h * 128:(h + 1) * 128]).astype(BF16))
            vblocks.append(jnp.concatenate([jnp.where(hmask[h], v_all, jnp.zeros_like(v_all)), ones_blk[h]], axis=1))
            g = (h * hd) // 128
            srows.append(jnp.concatenate([cblk[h] if j == g else zblk for j in range(ngrp)]
                                         + [nblk[h] if j == g else zblk for j in range(ngrp)], axis=1))
        s_bf = jnp.concatenate(srows, axis=0).astype(BF16)
        out_aug = _dot(jnp.concatenate(wcat, axis=1), jnp.concatenate(vblocks, axis=0))
        out_aug = out_aug + _dot((qk_ref[rows, 0:w] * w_inter).astype(BF16), s_bf)
        hh_ref[rows, :] = out_aug[:, 0:w] / jnp.maximum(jnp.abs(out_aug[:, w:2 * w]), e_negm)
        kw = (qk_ref[rows, w:2 * w] * jnp.exp(r64_ref[rows, :] - glast_l)).astype(BF16)
        upd = _dot_tn(kw, jnp.concatenate([v_all, ones_cols], axis=1))
        for h in range(nh):
            rs = slice(h * hd, (h + 1) * hd)
            cblk[h] = cblk[h] * decay[h:h + 1, :] + jnp.where(half[h], upd[rs, grp[h]], 0.0)
            nblk[h] = (nblk[h] * decay[h:h + 1, :]
                       + jnp.where(half[h], upd[rs, w + grp[h].start:w + grp[h].stop], 0.0))
        m_prev = bl_ref[t8, :] + g_last
    m_ref[...] = m_prev
    for h in range(nh):
        st_ref[h * hd:(h + 1) * hd, grp[h]] = cblk[h]
        st_ref[h * hd:(h + 1) * hd, w + grp[h].start:w + grp[h].stop] = nblk[h]

    hh = hh_ref[...]
    ms = _dot((hh * hh).astype(BF16), bones_ref[...])
    og = zc_ref[:, 3 * w:4 * w].astype(F32)
    o_ref[...] = (_sigmoid(og) * hh * lax.rsqrt(ms + EPS) * ng_ref[...]).astype(o_ref.dtype)


def _mlstm(zc3, zgt, conv_w, gate_b_rows, norm_g):
    b, s, w = zc3.shape
    ts = SEQ_TILE
    nt = s // ts
    gw = GROUP_W
    t = jnp.arange(CHUNK)
    triu = (t[:, None] <= t[None, :]).astype(BF16)
    hid = jnp.arange(gw) // HEAD_DIM
    bones = jnp.where(hid[:, None] == hid[None, :], 1.0 / HEAD_DIM, 0.0).astype(BF16)
    full = lambda a: pl.BlockSpec(a.shape, lambda bi, i: (0,) * a.ndim)
    return pl.pallas_call(
        _mlstm_kernel,
        grid=(b, nt),
        in_specs=[pl.BlockSpec((None, ts, w), lambda bi, i: (bi, i, 0)),
                  pl.BlockSpec((8, ts), lambda bi, i: (0, bi * nt + i)),
                  full(conv_w), full(gate_b_rows), full(norm_g), full(triu), full(bones)],
        out_specs=pl.BlockSpec((None, ts, gw), lambda bi, i: (bi, i, 0)),
        out_shape=jax.ShapeDtypeStruct((b, s, gw), BF16),
        scratch_shapes=[pltpu.VMEM((gw, 2 * gw), F32),
                        pltpu.VMEM((8, 128), F32),
                        pltpu.VMEM((8, 2 * gw), F32),
                        pltpu.VMEM((ts, 2 * gw), F32),
                        pltpu.VMEM((ts // CHUNK * 8, 128), F32),
                        pltpu.VMEM((ts // CHUNK * 8, 128), F32),
                        pltpu.VMEM((ts // CHUNK * 8, 128), F32),
                        pltpu.VMEM((ts, MLSTM_HEADS * 128), F32),
                        pltpu.VMEM((ts, gw), F32),
                        pltpu.VMEM((ts, gw), F32),
                        pltpu.VMEM((ts, gw), F32),
                        pltpu.VMEM((ts, MLSTM_HEADS * 128), F32),
                        pltpu.VMEM((ts, gw), F32)],
        compiler_params=_cparams("parallel", "arbitrary"),
        name="mlstm",
    )(zc3, zgt, conv_w, gate_b_rows, norm_g, triu, bones)


def _sgu_kernel(zd_ref, lng_ref, lnb_ref, w_ref, bias_ref, o_ref):
    tm = zd_ref.shape[0]
    gw = GROUP_W // SGU_GROUPS
    u = _gelu(zd_ref[:, 0:GROUP_W].astype(F32))
    v = _gelu(zd_ref[:, GROUP_W:2 * GROUP_W].astype(F32))
    mu = jnp.mean(v, axis=-1, keepdims=True)
    vc = v - mu
    var = jnp.mean(vc * vc, axis=-1, keepdims=True)
    vn = (vc * lax.rsqrt(var + EPS) * lng_ref[...] + lnb_ref[...]).astype(BF16)
    for c in range(tm // CHUNK):
        rows = slice(c * CHUNK, (c + 1) * CHUNK)
        mixed = jnp.concatenate([_dot(w_ref[g], vn[rows, g * gw:(g + 1) * gw]) for g in range(SGU_GROUPS)], axis=-1)
        o_ref[rows, :] = (u[rows, :] * (mixed + bias_ref[...])).astype(o_ref.dtype)


def _sgu(zd, ln_g, ln_b, w_tril, bias):
    n = zd.shape[0]
    tm = TOKEN_TILE
    full = lambda a: pl.BlockSpec(a.shape, lambda i: (0,) * a.ndim)
    return pl.pallas_call(
        _sgu_kernel,
        grid=(n // tm,),
        in_specs=[pl.BlockSpec((tm, 2 * GROUP_W), lambda i: (i, 0)), full(ln_g), full(ln_b), full(w_tril), full(bias)],
        out_specs=pl.BlockSpec((tm, GROUP_W), lambda i: (i, 0)),
        out_shape=jax.ShapeDtypeStruct((n, GROUP_W), BF16),
        compiler_params=_cparams("parallel"),
        name="sgu",
    )(zd, ln_g, ln_b, w_tril, bias)


def _out_proj_kernel(x_ref, ya_ref, yb_ref, yc_ref, yd_ref, gluw_ref, glub_ref, wo_ref, ng_ref,
                     rwh_ref, rwl_ref, rb_ref,
                     x1_ref, hx_ref, meta_ref, cnt_ref, carry_ref):
    tm = x_ref.shape[0]

    @pl.when(pl.program_id(0) == 0)
    def _():
        carry_ref[...] = jnp.zeros_like(carry_ref)

    yb = _gelu(yb_ref[...].astype(F32))
    yb = yb * _sigmoid(_dot(yb.astype(BF16), gluw_ref[...]) + glub_ref[...])
    mix = _dot(ya_ref[...], wo_ref[0:GROUP_W, :])
    mix = mix + _dot(yb.astype(BF16), wo_ref[GROUP_W:2 * GROUP_W, :])
    mix = mix + _dot(yc_ref[...], wo_ref[2 * GROUP_W:3 * GROUP_W, :])
    mix = mix + _dot(yd_ref[...], wo_ref[3 * GROUP_W:4 * GROUP_W, :])
    x1 = x_ref[...] + mix
    x1_ref[...] = x1
    h = _rms(x1, ng_ref[...])
    hx_ref[:, 0:PACK_W] = _pack_rows(h)

    hh, hl = _split_bf16(h)
    wh = rwh_ref[...]
    wl = rwl_ref[...]
    logits = _dot_nt(wh, hh) + _dot_nt(wh, hl) + _dot_nt(wl, hh) + rb_ref[...]
    gl = [logits[j:j + 1, :] for j in range(N_GROUPS)]
    gmax = functools.reduce(jnp.maximum, gl)
    gsel = jnp.full((1, tm), N_GROUPS - 1, I32)
    for j in range(N_GROUPS - 2, -1, -1):
        gsel = jnp.where(gl[j] == gmax, j, gsel)
    p_g = 1.0 / functools.reduce(jnp.add, [jnp.exp(v - gmax) for v in gl])
    e_in = []
    for i in range(EPG):
        v = logits[N_GROUPS + i:N_GROUPS + i + 1, :]
        for j in range(1, N_GROUPS):
            r = N_GROUPS + j * EPG + i
            v = jnp.where(gsel == j, logits[r:r + 1, :], v)
        e_in.append(v)
    v1 = functools.reduce(jnp.maximum, e_in)
    i1 = jnp.full((1, tm), EPG - 1, I32)
    for i in range(EPG - 2, -1, -1):
        i1 = jnp.where(e_in[i] == v1, i, i1)
    rest = [jnp.where(i1 == i, NEG, e_in[i]) for i in range(EPG)]
    v2 = functools.reduce(jnp.maximum, rest)
    i2 = jnp.full((1, tm), EPG - 1, I32)
    for i in range(EPG - 2, -1, -1):
        i2 = jnp.where((rest[i] == v2) & (i1 != i), i, i2)
    e2 = jnp.exp(v2 - v1)
    w1 = p_g / (1.0 + e2)
    w2 = p_g * e2 / (1.0 + e2)
    lo = jnp.minimum(i1, i2)
    hi = jnp.maximum(i1, i2)
    w_lo = jnp.where(i1 < i2, w1, w2)
    w_hi = jnp.where(i1 < i2, w2, w1)
    bucket = gsel * PAIRS + ((lo * (2 * EPG - 1 - lo)) >> 1) + (hi - lo - 1)

    kid = lax.broadcasted_iota(I32, (BUCKET_PAD, tm), 0)
    onehot = (kid == bucket).astype(F32)
    s_i = lax.broadcasted_iota(I32, (tm, tm), 0)
    t_i = lax.broadcasted_iota(I32, (tm, tm), 1)
    prefix = _dot(onehot.astype(BF16), (s_i <= t_i).astype(BF16))
    carry = carry_ref[...]
    rank = jnp.sum(onehot * (prefix - 1.0 + carry[:, 0:1]), axis=0, keepdims=True)
    carry = carry + prefix[:, tm - 1:tm]
    carry_ref[...] = carry
    cnt_ref[...] = carry
    meta_ref[...] = jnp.concatenate([bucket, rank.astype(I32), jnp.zeros((6, tm), I32)], axis=0)
    wrows = jnp.concatenate([w_lo, w_hi, jnp.zeros((126, tm), F32)], axis=0)
    for c in range(tm // 128):
        hx_ref[c * 128:(c + 1) * 128, PACK_W:ROW_W] = pltpu.bitcast(wrows[:, c * 128:(c + 1) * 128].T, U32)


def _out_proj(x2, ya, yb, yc, yd, glu_w, glu_b, w_out, norm_g, rw_hi, rw_lo, rb):
    n = x2.shape[0]
    tm = TOKEN_TILE
    row = lambda w: pl.BlockSpec((tm, w), lambda i: (i, 0))
    full = lambda a: pl.BlockSpec(a.shape, lambda i: (0,) * a.ndim)
    return pl.pallas_call(
        _out_proj_kernel,
        grid=(n // tm,),
        in_specs=[row(D_MODEL), row(GROUP_W), row(GROUP_W), row(GROUP_W), row(GROUP_W),
                  full(glu_w), full(glu_b), full(w_out), full(norm_g), full(rw_hi), full(rw_lo), full(rb)],
        out_specs=[row(D_MODEL), row(ROW_W), pl.BlockSpec((8, tm), lambda i: (0, i)),
                   pl.BlockSpec((BUCKET_PAD, 128), lambda i: (0, 0))],
        out_shape=[jax.ShapeDtypeStruct((n, D_MODEL), F32), jax.ShapeDtypeStruct((n, ROW_W), U32),
                   jax.ShapeDtypeStruct((8, n), I32), jax.ShapeDtypeStruct((BUCKET_PAD, 128), F32)],
        scratch_shapes=[pltpu.VMEM((BUCKET_PAD, 128), F32)],
        compiler_params=_cparams("arbitrary"),
        name="out_proj",
    )(x2, ya, yb, yc, yd, glu_w, glu_b, w_out, norm_g, rw_hi, rw_lo, rb)


def _dispatch_kernel(dest_ref, hx_ref, xs_in_ref, xs_ref, sem):
    del xs_in_ref
    nq = hx_ref.shape[0]

    def start(q, c):
        for r in range(SUBLANES):
            d = dest_ref[0, 0, q * SUBLANES + r]
            pltpu.make_async_copy(hx_ref.at[q, pl.ds(r, 1)], xs_ref.at[pl.ds(d, 1)], sem).start()
        return c

    lax.fori_loop(0, nq, start, 0)
    pltpu.make_async_copy(hx_ref, hx_ref, sem).wait()


def _dispatch(dest3, hx, p_tot):
    n = hx.shape[0]
    tm = TOKEN_TILE
    xs0 = jnp.zeros((p_tot, ROW_W), U32)
    hx = hx.reshape(n // SUBLANES, SUBLANES, ROW_W)
    return pl.pallas_call(
        _dispatch_kernel,
        grid=(n // tm,),
        in_specs=[pl.BlockSpec((1, 1, tm), lambda i: (i, 0, 0), memory_space=pltpu.SMEM),
                  pl.BlockSpec((tm // SUBLANES, SUBLANES, ROW_W), lambda i: (i, 0, 0)),
                  pl.BlockSpec(memory_space=pl.ANY)],
        out_specs=pl.BlockSpec(memory_space=pl.ANY),
        out_shape=jax.ShapeDtypeStruct((p_tot, ROW_W), U32),
        input_output_aliases={2: 0},
        scratch_shapes=[pltpu.SemaphoreType.DMA],
        compiler_params=_cparams("arbitrary"),
        name="dispatch",
    )(dest3, hx, xs0)


def _ffn_kernel(valid_ref, xblk_ref, run_ref, exprun_ref, nrun_ref, xs_ref, wg_hbm, wu_hbm, wd_hbm, ys_ref,
                stg_g, stg_u, stg_d, wgu_a, wdn_a, wgu_b, wdn_b, sem, *, layer):
    del xblk_ref
    j = pl.program_id(0)
    valid = valid_ref[j]
    jp = jnp.maximum(j - 1, 0)

    def expert_copies(s, r, slot):
        e = exprun_ref[s, r]
        return [pltpu.make_async_copy(src.at[layer, e], dst.at[s, slot], sem.at[s, slot])
                for src, dst in ((wg_hbm, stg_g), (wu_hbm, stg_u), (wd_hbm, stg_d))]

    for s, (wgu, wdn) in enumerate(((wgu_a, wdn_a), (wgu_b, wdn_b))):
        r = run_ref[s, j]

        @pl.when((j == 0) | (r != run_ref[s, jp]))
        def _():
            slot = r % 2

            @pl.when(j == 0)
            def _():
                for c in expert_copies(s, r, slot):
                    c.start()

            for c in expert_copies(s, r, slot):
                c.wait()

            @pl.when(r + 1 < nrun_ref[s])
            def _():
                for c in expert_copies(s, r + 1, 1 - slot):
                    c.start()

            wgu[:, 0:D_EXPERT] = stg_g[s, slot].astype(BF16)
            wgu[:, D_EXPERT:2 * D_EXPERT] = stg_u[s, slot].astype(BF16)
            wdn[...] = stg_d[s, slot].astype(BF16)

    for sb in range(FFN_BLOCK // FFN_SUB):
        rows = slice(sb * FFN_SUB, (sb + 1) * FFN_SUB)

        @pl.when(valid <= sb * FFN_SUB)
        def _():
            ys_ref[rows, :] = jnp.zeros((FFN_SUB, PACK_W), U32)

        @pl.when(valid > sb * FFN_SUB)
        def _():
            x_lo, x_hi = _unpack_rows(xs_ref[rows, 0:PACK_W])
            xb = jnp.concatenate([x_lo.astype(BF16), x_hi.astype(BF16)], axis=1)
            wts = pltpu.bitcast(xs_ref[rows, PACK_W:ROW_W], F32)

            def expert(wgu, wdn):
                gu = _dot(xb, wgu[...])
                g = gu[:, 0:D_EXPERT]
                act = (g * _sigmoid(g) * gu[:, D_EXPERT:2 * D_EXPERT]).astype(BF16)
                return _dot(act, wdn[...])

            ya = expert(wgu_a, wdn_a)
            yb = expert(wgu_b, wdn_b)
            ys_ref[rows, :] = _pack_rows(ya * wts[:, 0:1] + yb * wts[:, 1:2])


def _ffn(valid, xblk, run, exprun, nrun, xs, layer, w_gate, w_up, w_down):
    p_tot = xs.shape[0]
    bm = FFN_BLOCK
    nblk = p_tot // bm
    up_shape = (D_MODEL, D_EXPERT)
    dn_shape = (D_EXPERT, D_MODEL)
    hbm = pl.BlockSpec(memory_space=pl.ANY)
    return pl.pallas_call(
        functools.partial(_ffn_kernel, layer=layer),
        grid_spec=pltpu.PrefetchScalarGridSpec(
            num_scalar_prefetch=5,
            grid=(nblk,),
            in_specs=[pl.BlockSpec((bm, ROW_W), lambda j, va, xb, ru, er, nr: (xb[j], 0)), hbm, hbm, hbm],
            out_specs=pl.BlockSpec((bm, PACK_W), lambda j, va, xb, ru, er, nr: (j, 0)),
            scratch_shapes=[pltpu.VMEM((2, 2) + up_shape, F32), pltpu.VMEM((2, 2) + up_shape, F32),
                            pltpu.VMEM((2, 2) + dn_shape, F32),
                            pltpu.VMEM((D_MODEL, 2 * D_EXPERT), BF16), pltpu.VMEM(dn_shape, BF16),
                            pltpu.VMEM((D_MODEL, 2 * D_EXPERT), BF16), pltpu.VMEM(dn_shape, BF16),
                            pltpu.SemaphoreType.DMA((2, 2))],
        ),
        out_shape=jax.ShapeDtypeStruct((p_tot, PACK_W), U32),
        compiler_params=_cparams("arbitrary"),
        name="ffn",
    )(valid, xblk, run, exprun, nrun, xs, w_gate, w_up, w_down)


def _combine_kernel(dest_ref, dnext_ref, x1_ref, ys_ref, ng_ref, o_ref, buf_ref, sem, *, final_norm):
    i = pl.program_id(0)
    tm = x1_ref.shape[0]
    nq = tm // SUBLANES
    slot = i % 2

    def gather(d_ref, sl):
        def start(q, c):
            for r in range(SUBLANES):
                d = d_ref[0, 0, q * SUBLANES + r]
                pltpu.make_async_copy(ys_ref.at[pl.ds(d, 1)], buf_ref.at[sl, q, pl.ds(r, 1)], sem.at[sl]).start()
            return c

        lax.fori_loop(0, nq, start, 0)

    @pl.when(i == 0)
    def _():
        gather(dest_ref, 0)

    pltpu.make_async_copy(buf_ref.at[slot], buf_ref.at[slot], sem.at[slot]).wait()

    @pl.when(i + 1 < pl.num_programs(0))
    def _():
        gather(dnext_ref, 1 - slot)

    y_lo, y_hi = _unpack_rows(buf_ref[slot].reshape(tm, PACK_W))
    x2 = x1_ref[...] + jnp.concatenate([y_lo, y_hi], axis=1)
    o_ref[...] = _rms(x2, ng_ref[...]) if final_norm else x2


def _combine(dest3, x1, ys, norm_g, final_norm):
    n = x1.shape[0]
    tm = TOKEN_TILE
    last = n // tm - 1
    return pl.pallas_call(
        functools.partial(_combine_kernel, final_norm=final_norm),
        grid=(n // tm,),
        in_specs=[pl.BlockSpec((1, 1, tm), lambda i: (i, 0, 0), memory_space=pltpu.SMEM),
                  pl.BlockSpec((1, 1, tm), lambda i: (jnp.minimum(i + 1, last), 0, 0), memory_space=pltpu.SMEM),
                  pl.BlockSpec((tm, D_MODEL), lambda i: (i, 0)),
                  pl.BlockSpec(memory_space=pl.ANY),
                  pl.BlockSpec((1, D_MODEL), lambda i: (0, 0))],
        out_specs=pl.BlockSpec((tm, D_MODEL), lambda i: (i, 0)),
        out_shape=jax.ShapeDtypeStruct((n, D_MODEL), F32),
        scratch_shapes=[pltpu.VMEM((2, tm // SUBLANES, SUBLANES, PACK_W), U32), pltpu.SemaphoreType.DMA((2,))],
        compiler_params=_cparams("arbitrary"),
        name="combine",
    )(dest3, dest3, x1, ys, norm_g)


def _routing_tables(meta, counts, n):
    bm = FFN_BLOCK
    nblk = n // bm + N_BUCKETS
    bucket = meta[0]
    rank = meta[1]
    cnt = counts[:N_BUCKETS, 0].astype(I32)
    padded = ((cnt + bm - 1) // bm) * bm
    pad_end = jnp.cumsum(padded)
    pad_start = pad_end - padded
    onehot = bucket[:, None] == jnp.arange(N_BUCKETS, dtype=I32)[None, :]
    dest = rank + jnp.sum(jnp.where(onehot, pad_start[None, :], 0), axis=1)
    nact = (pad_end[-1] // bm).astype(I32)
    j = jnp.arange(nblk, dtype=I32)
    blk = jnp.minimum(j, jnp.maximum(nact - 1, 0))
    pos = blk * bm
    bkt = jnp.minimum(jnp.sum((pad_end[None, :] <= pos[:, None]).astype(I32), axis=1), N_BUCKETS - 1)
    in_bkt = bkt[:, None] == jnp.arange(N_BUCKETS, dtype=I32)[None, :]
    sel = lambda tab: jnp.sum(jnp.where(in_bkt, tab[None, :], 0), axis=1)
    valid = jnp.where(j < nact, jnp.clip(sel(cnt) - (pos - sel(pad_start)), 0, bm), 0)
    lo_tab, hi_tab = [], []
    for g in range(N_GROUPS):
        for a in range(EPG):
            for b in range(a + 1, EPG):
                lo_tab.append(g * EPG + a)
                hi_tab.append(g * EPG + b)
    experts = jnp.stack([sel(jnp.asarray(lo_tab, I32)), sel(jnp.asarray(hi_tab, I32))]).astype(I32)
    change = jnp.concatenate([jnp.zeros((2, 1), I32), (experts[:, 1:] != experts[:, :-1]).astype(I32)], axis=1)
    run = jnp.cumsum(change, axis=1).astype(I32)
    is_run = run[:, :, None] == j[None, None, :]
    exprun = jnp.max(jnp.where(is_run, experts[:, :, None], 0), axis=1).astype(I32)
    nrun = run[:, -1] + 1
    return dest.astype(I32), valid.astype(I32), blk.astype(I32), run, exprun, nrun.astype(I32), nblk * bm


def _layer(x2, b, s, p, final_g):
    n = b * s
    za, zb, zc, zd, zgt = _in_proj(x2, p['norm_mix_g'], p['wa'], p['wb'], p['wc'], p['wd'], p['wgt'])
    ya = _attention(za.reshape(b, s, -1), p['sinks']).reshape(n, GROUP_W)
    yb = _ssm(zb.reshape(b, s, GROUP_W), p['ssm']).reshape(n, GROUP_W)
    yc = _mlstm(zc.reshape(b, s, -1), zgt, p['conv_w'], p['gate_b'], p['mlstm_norm_g']).reshape(n, GROUP_W)
    yd = _sgu(zd, p['sgu_ln_g'], p['sgu_ln_b'], p['sgu_w'], p['sgu_bias'])
    x1, hx, meta, counts = _out_proj(x2, ya, yb, yc, yd, p['glu_w'], p['glu_b'], p['w_out'], p['norm_ffn_g'],
                                     p['rw_hi'], p['rw_lo'], p['rb'])
    dest, valid, xblk, run, exprun, nrun, p_tot = _routing_tables(meta, counts, n)
    dest3 = dest.reshape(n // TOKEN_TILE, 1, TOKEN_TILE)
    xs = _dispatch(dest3, hx, p_tot)
    ys = _ffn(valid, xblk, run, exprun, nrun, xs, p['layer'], p['w_gate'], p['w_up'], p['w_down'])
    if final_g is None:
        return _combine(dest3, x1, ys, p['norm_ffn_g'], False)
    return _combine(dest3, x1, ys, final_g, True)


def _prep_layer(l, norm_mix_g, w_in, attn_sinks, ssm_a_re, ssm_a_im, ssm_b_re, ssm_b_im, ssm_c_re, ssm_c_im,
                ssm_d, ssm_log_dt, ssm_glu_w, ssm_glu_b, mlstm_conv_w, mlstm_gate_b, mlstm_norm_g,
                sgu_ln_g, sgu_ln_b, sgu_w, sgu_b, w_out, norm_ffn_g, router_group_w, router_group_b,
                router_expert_w, router_expert_b, expert_w_gate, expert_w_up, expert_w_down):
    w = w_in[l]
    o_su = 2 * GROUP_W
    o_c = o_su + GROUP_W
    o_ci = o_c + 3 * GROUP_W
    o_co = o_ci + 2 * MLSTM_HEADS
    o_d = o_co + GROUP_W
    wc = jnp.concatenate([w[:, o_c:o_ci], w[:, o_co:o_d]], axis=1)
    wgt = w[:, o_ci:o_co].T
    head_cols = lambda m, hq: m[..., hq * HEAD_DIM:(hq + 1) * HEAD_DIM]
    order = [g * ATTN_REP + r for r in range(ATTN_REP) for g in range(KV_HEADS)]
    wa = jnp.concatenate([head_cols(w, hq) * (HEAD_DIM ** -0.5) for hq in order] + [w[:, GROUP_W:o_su]], axis=1)
    wo = jnp.concatenate([w_out[l][hq * HEAD_DIM:(hq + 1) * HEAD_DIM] for hq in order] + [w_out[l][GROUP_W:]], axis=0)
    rw = jnp.concatenate([router_group_w[l], router_expert_w[l]], axis=1).T.astype(F32)
    rw = jnp.pad(rw, ((0, 4), (0, 0)))
    rw_hi = rw.astype(BF16)
    rw_lo = (rw - rw_hi.astype(F32)).astype(BF16)
    rb = jnp.pad(jnp.concatenate([router_group_b[l], router_expert_b[l]]).astype(F32), (0, 4))[:, None]
    tril = jnp.tril(jnp.ones((CHUNK, CHUNK), F32))
    gw = GROUP_W // SGU_GROUPS
    return dict(
        norm_mix_g=norm_mix_g[l][None, :].astype(F32),
        wa=wa.astype(BF16), wb=w[:, o_su:o_c].astype(BF16), wc=wc.astype(BF16),
        wd=w[:, o_d:].astype(BF16), wgt=wgt.astype(BF16),
        sinks=attn_sinks[l].astype(F32),
        ssm=_ssm_matrices(ssm_a_re[l], ssm_a_im[l], ssm_b_re[l], ssm_b_im[l], ssm_c_re[l], ssm_c_im[l],
                          ssm_d[l], ssm_log_dt[l]),
        glu_w=ssm_glu_w[l].astype(BF16), glu_b=ssm_glu_b[l][None, :].astype(F32),
        conv_w=mlstm_conv_w[l].astype(F32),
        gate_b=jnp.broadcast_to(mlstm_gate_b[l].astype(F32)[:, None], (2 * MLSTM_HEADS, 128)),
        mlstm_norm_g=mlstm_norm_g[l][None, :].astype(F32),
        sgu_ln_g=sgu_ln_g[l][None, :].astype(F32), sgu_ln_b=sgu_ln_b[l][None, :].astype(F32),
        sgu_w=(sgu_w[l].astype(F32) * tril).astype(BF16),
        sgu_bias=jnp.repeat(sgu_b[l].astype(F32).T, gw, axis=1),
        w_out=wo.astype(BF16), norm_ffn_g=norm_ffn_g[l][None, :].astype(F32),
        rw_hi=rw_hi, rw_lo=rw_lo, rb=rb,
        layer=l, w_gate=expert_w_gate, w_up=expert_w_up, w_down=expert_w_down,
    )


def kernel(x, norm_mix_g, w_in, attn_sinks, ssm_a_re, ssm_a_im, ssm_b_re, ssm_b_im, ssm_c_re, ssm_c_im, ssm_d, ssm_log_dt, ssm_glu_w, ssm_glu_b, mlstm_conv_w, mlstm_gate_b, mlstm_norm_g, sgu_ln_g, sgu_ln_b, sgu_w, sgu_b, w_out, norm_ffn_g, router_group_w, router_group_b, router_expert_w, router_expert_b, expert_w_gate, expert_w_up, expert_w_down, norm_final_g):
    b, s, d = x.shape
    depth = w_in.shape[0]
    x2 = x.reshape(b * s, d).astype(F32)
    for l in range(depth):
        p = _prep_layer(l, norm_mix_g, w_in, attn_sinks, ssm_a_re, ssm_a_im, ssm_b_re, ssm_b_im, ssm_c_re,
                        ssm_c_im, ssm_d, ssm_log_dt, ssm_glu_w, ssm_glu_b, mlstm_conv_w, mlstm_gate_b,
                        mlstm_norm_g, sgu_ln_g, sgu_ln_b, sgu_w, sgu_b, w_out, norm_ffn_g, router_group_w,
                        router_group_b, router_expert_w, router_expert_b, expert_w_gate, expert_w_up,
                        expert_w_down)
        final_g = norm_final_g[None, :].astype(F32) if l == depth - 1 else None
        x2 = _layer(x2, b, s, p, final_g)
    return x2.reshape(b, s, d).astype(x.dtype)
```

```python
import functools
import math

import jax
import jax.numpy as jnp
from jax import lax
from jax.experimental import pallas as pl
from jax.experimental.pallas import tpu as pltpu

F32 = jnp.float32
BF16 = jnp.bfloat16
I32 = jnp.int32
U32 = jnp.uint32

D_MODEL = 1024
GROUP_W = 256
HEAD_DIM = 64
EPS = 1e-6
NEG = -1e30
WINDOW = 128
KV_HEADS = 2
ATTN_REP = 2
SSM_GROUP = 16
SSM_GROUPS = 16
SSM_STATE = 64
SSM_CHUNK = 16
MLSTM_HEADS = 4
CHUNK = 128
CONV_K = 4
SGU_GROUPS = 4
N_GROUPS = 4
EPG = 8
N_EXPERTS = 32
D_EXPERT = 512
PAIRS = EPG * (EPG - 1) // 2
N_BUCKETS = N_GROUPS * PAIRS
BUCKET_PAD = 128
PACK_W = D_MODEL // 2
ROW_W = PACK_W + 128

TOKEN_TILE = 512
SEQ_TILE = 512
SSM_TILE = 4096
SSM_SLAB = 64
FFN_BLOCK = 512
FFN_SUB = 256
SUBLANES = 8
VMEM_LIMIT = 56 * 1024 * 1024


def _cparams(*sem):
    return pltpu.CompilerParams(dimension_semantics=sem, vmem_limit_bytes=VMEM_LIMIT)


def _rms(x, g):
    return x * lax.rsqrt(jnp.mean(x * x, axis=-1, keepdims=True) + EPS) * g


def _gelu(x):
    return 0.5 * x * (1.0 + jnp.tanh(math.sqrt(2.0 / math.pi) * (x + 0.044715 * (x * x * x))))


def _sigmoid(x):
    return 1.0 / (1.0 + jnp.exp(-x))


def _log_sigmoid(x):
    return jnp.minimum(x, 0.0) - jnp.log(1.0 + jnp.exp(-jnp.abs(x)))


def _dot(a, b):
    return jnp.dot(a, b, preferred_element_type=F32)


def _dot_nt(a, b):
    return lax.dot_general(a, b, (((1,), (1,)), ((), ())), preferred_element_type=F32)


def _dot_tn(a, b):
    return lax.dot_general(a, b, (((0,), (0,)), ((), ())), preferred_element_type=F32)


def _split_bf16(x):
    hi = x.astype(BF16)
    lo = (x - hi.astype(F32)).astype(BF16)
    return hi, lo


def _pack_rows(x):
    w = x.shape[1] // 2
    bits = lambda v: pltpu.bitcast(v.astype(BF16).astype(F32), U32)
    return (bits(x[:, 0:w]) >> 16) | (bits(x[:, w:2 * w]) & jnp.uint32(0xFFFF0000))


def _unpack_rows(p):
    return pltpu.bitcast(p << 16, F32), pltpu.bitcast(p & jnp.uint32(0xFFFF0000), F32)


def _in_proj_kernel(x_ref, g_ref, wa_ref, wb_ref, wc_ref, wd_ref, wgt_ref,
                    za_ref, zb_ref, zc_ref, zd_ref, zgt_ref):
    hb = _rms(x_ref[...], g_ref[...]).astype(BF16)
    za_ref[...] = _dot(hb, wa_ref[...]).astype(BF16)
    zb_ref[...] = _dot(hb, wb_ref[...]).astype(BF16)
    zc_ref[...] = _dot(hb, wc_ref[...]).astype(BF16)
    zd_ref[...] = _dot(hb, wd_ref[...]).astype(BF16)
    zgt_ref[...] = _dot_nt(wgt_ref[...], hb)


def _in_proj(x2, g, wa, wb, wc, wd, wgt):
    n = x2.shape[0]
    tm = TOKEN_TILE
    row = lambda w: pl.BlockSpec((tm, w), lambda i: (i, 0))
    full = lambda a: pl.BlockSpec(a.shape, lambda i: (0,) * a.ndim)
    widths = (wa.shape[1], wb.shape[1], wc.shape[1], wd.shape[1])
    return pl.pallas_call(
        _in_proj_kernel,
        grid=(n // tm,),
        in_specs=[row(D_MODEL), full(g), full(wa), full(wb), full(wc), full(wd), full(wgt)],
        out_specs=[row(w) for w in widths] + [pl.BlockSpec((wgt.shape[0], tm), lambda i: (0, i))],
        out_shape=[jax.ShapeDtypeStruct((n, w), BF16) for w in widths]
        + [jax.ShapeDtypeStruct((wgt.shape[0], n), F32)],
        compiler_params=_cparams("parallel"),
        name="in_proj",
    )(x2, g, wa, wb, wc, wd, wgt)


def _attn_kernel(sink_ref, cur_ref, prev_ref, o_ref):
    first = pl.program_id(1) == 0
    nblk = cur_ref.shape[0] // WINDOW
    row = lax.broadcasted_iota(I32, (WINDOW, 2 * WINDOW), 0)
    col = lax.broadcasted_iota(I32, (WINDOW, 2 * WINDOW), 1)
    band = (col <= row + WINDOW) & (col > row)
    lane = lax.broadcasted_iota(I32, (1, 128), 1)
    kv_lanes = [(lane // HEAD_DIM) == g for g in range(KV_HEADS)]
    ones = jnp.ones((2 * WINDOW, 128), BF16)
    ko = 2 * 128
    vo = 3 * 128
    for j in range(nblk):
        cur = cur_ref[j * WINDOW:(j + 1) * WINDOW, :]
        if j == 0:
            prev = prev_ref[...]
            mask = band & ((col >= WINDOW) | jnp.logical_not(first))
        else:
            prev = cur_ref[(j - 1) * WINDOW:j * WINDOW, :]
            mask = band
        kk = jnp.concatenate([prev[:, ko:ko + 128], cur[:, ko:ko + 128]], axis=0)
        vaug = jnp.concatenate([jnp.concatenate([prev[:, vo:vo + 128], cur[:, vo:vo + 128]], axis=0), ones], axis=1)
        outs = []
        for r in range(ATTN_REP):
            qg = cur[:, r * 128:(r + 1) * 128]
            og = []
            for g in range(KV_HEADS):
                s = _dot_nt(jnp.where(kv_lanes[g], qg, jnp.zeros_like(qg)), kk)
                s = jnp.where(mask, s, NEG)
                sink = sink_ref[g * ATTN_REP + r]
                m = jnp.maximum(jnp.max(s, axis=-1, keepdims=True), sink)
                oa = _dot(jnp.exp(s - m).astype(BF16), vaug)
                og.append(oa[:, 0:128] / (oa[:, 128:256] + jnp.exp(sink - m)))
            outs.append(jnp.where(kv_lanes[0], og[0], og[1]))
        o_ref[j * WINDOW:(j + 1) * WINDOW, :] = jnp.concatenate(outs, axis=-1).astype(o_ref.dtype)


def _attention(za3, sinks):
    b, s, w = za3.shape
    ts = SEQ_TILE
    per = ts // WINDOW
    return pl.pallas_call(
        _attn_kernel,
        grid=(b, s // ts),
        in_specs=[pl.BlockSpec(memory_space=pltpu.SMEM),
                  pl.BlockSpec((None, ts, w), lambda bi, i: (bi, i, 0)),
                  pl.BlockSpec((None, WINDOW, w), lambda bi, i: (bi, jnp.maximum(i * per - 1, 0), 0))],
        out_specs=pl.BlockSpec((None, ts, GROUP_W), lambda bi, i: (bi, i, 0)),
        out_shape=jax.ShapeDtypeStruct((b, s, GROUP_W), BF16),
        compiler_params=_cparams("parallel", "parallel"),
        name="attn",
    )(sinks, za3, za3)


def _ssm_kernel(zb_ref, bre_ref, bim_ref, t_ref, cre_ref, cim_ref, are_ref, aim_ref, o_ref,
                x_ref, xs_ref, u_ref, vre_ref, vim_ref, sre_ref, sim_ref, y_ref, st_ref):
    ts = zb_ref.shape[0]
    L, G, H = SSM_CHUNK, SSM_GROUPS, SSM_GROUP
    nch = ts // L
    half_g = 128 // H
    n_half = G // half_g

    @pl.when(pl.program_id(1) == 0)
    def _():
        st_ref[...] = jnp.zeros_like(st_ref)

    blk = lax.broadcasted_iota(I32, (1, 128), 1) // H

    def block_transpose(arrs):
        a = list(arrs)
        s = half_g // 2
        while s >= 1:
            keep = (blk & s) == 0
            for i in range(half_g):
                if i & s == 0:
                    ai, aj = a[i], a[i + s]
                    a[i] = jnp.where(keep, ai, pltpu.roll(aj, s * H, axis=1))
                    a[i + s] = jnp.where(keep, pltpu.roll(ai, 128 - s * H, axis=1), aj)
            s //= 2
        return a

    for hf in range(n_half):
        x_ref[hf] = zb_ref[:, hf * 128:(hf + 1) * 128].astype(F32)
    for sg in range(L):
        for hf in range(n_half):
            xs_ref[sg * n_half + hf] = x_ref[hf, pl.ds(sg, nch, stride=L), :]
    for hf in range(n_half):
        for oc in range(L // half_g):
            for r0 in range(0, nch, SSM_SLAB):
                rows = slice(r0, r0 + SSM_SLAB)
                t = block_transpose([xs_ref[(oc * half_g + k) * n_half + hf, rows, :] for k in range(half_g)])
                for gl in range(half_g):
                    u_ref[hf * half_g + gl, rows, oc * 128:(oc + 1) * 128] = t[gl].astype(BF16)
    for g in range(G):
        ug = u_ref[g]
        vre_ref[pl.ds(g, nch, stride=G), :] = _dot(ug, bre_ref[g])
        vim_ref[pl.ds(g, nch, stride=G), :] = _dot(ug, bim_ref[g])

    are = are_ref[...]
    aim = aim_ref[...]

    def step(c, carry):
        sre, sim = carry
        r0 = pl.multiple_of(c * G, G)
        sre_ref[pl.ds(r0, G), :] = sre
        sim_ref[pl.ds(r0, G), :] = sim
        vre = vre_ref[pl.ds(r0, G), :]
        vim = vim_ref[pl.ds(r0, G), :]
        return (are * sre - aim * sim + vre, are * sim + aim * sre + vim)

    sre, sim = lax.fori_loop(0, nch, step, (st_ref[0:G, :], st_ref[G:2 * G, :]), unroll=4)
    st_ref[0:G, :] = sre
    st_ref[G:2 * G, :] = sim

    for g in range(G):
        y = _dot(u_ref[g], t_ref[g])
        y = y + _dot(sre_ref[pl.ds(g, nch, stride=G), :].astype(BF16), cre_ref[g])
        y = y + _dot(sim_ref[pl.ds(g, nch, stride=G), :].astype(BF16), cim_ref[g])
        y_ref[g] = y
    for hf in range(n_half):
        for oc in range(L // half_g):
            for r0 in range(0, nch, SSM_SLAB):
                t = block_transpose([y_ref[hf * half_g + gl, r0:r0 + SSM_SLAB, oc * 128:(oc + 1) * 128]
                                     for gl in range(half_g)])
                for k in range(half_g):
                    x_ref[hf, pl.ds(oc * half_g + k + L * r0, SSM_SLAB, stride=L), :] = t[k]
    for hf in range(n_half):
        o_ref[:, hf * 128:(hf + 1) * 128] = x_ref[hf].astype(o_ref.dtype)


def _ssm_matrices(a_re, a_im, b_re, b_im, c_re, c_im, d_skip, log_dt):
    L = SSM_CHUNK
    a = lax.complex(a_re.astype(F32), a_im.astype(F32))
    dt = jnp.exp(log_dt.astype(F32))[:, None]
    adt = a * dt
    a_bar = jnp.exp(adt)
    b_bar = ((a_bar - 1.0) / a)[..., None] * lax.complex(b_re.astype(F32), b_im.astype(F32))
    c_mat = lax.complex(c_re.astype(F32), c_im.astype(F32))
    lag = jnp.arange(L + 1, dtype=F32)
    pw = jnp.exp(adt[None] * lag[:, None, None])
    kern = jnp.einsum('gop,dgp,gpi->dgoi', c_mat, pw[:L], b_bar).real
    sig = jnp.arange(L)[:, None]
    tau = jnp.arange(L)[None, :]
    d = tau - sig
    kt = kern[jnp.clip(d, 0, L - 1)]
    kt = jnp.where((d >= 0)[:, :, None, None, None], kt, 0.0)
    eye = jnp.eye(SSM_GROUP, dtype=F32)
    dsk = d_skip.astype(F32).reshape(SSM_GROUPS, SSM_GROUP)
    kt = kt + (d == 0)[:, :, None, None, None] * (dsk[:, :, None] * eye)[None, None]
    t_mat = kt.transpose(2, 0, 4, 1, 3).reshape(SSM_GROUPS, L * SSM_GROUP, L * SSM_GROUP)
    bm = pw[:L][::-1][:, :, :, None] * b_bar[None]
    bm = bm.transpose(1, 0, 3, 2).reshape(SSM_GROUPS, L * SSM_GROUP, SSM_STATE)
    cm = c_mat[None] * pw[1:][:, :, None, :]
    cm = cm.transpose(1, 3, 0, 2).reshape(SSM_GROUPS, SSM_STATE, L * SSM_GROUP)
    a_chunk = pw[L]
    pad = 128 - SSM_STATE
    pc = lambda m: jnp.pad(m, ((0, 0), (0, 0), (0, pad))).astype(BF16)
    pr = lambda m: jnp.pad(m, ((0, 0), (0, pad), (0, 0))).astype(BF16)
    pa = lambda m: jnp.pad(m, ((0, 0), (0, pad)))
    return (pc(bm.real), pc(bm.imag), t_mat.astype(BF16), pr(cm.real), pr(-cm.imag),
            pa(a_chunk.real), pa(a_chunk.imag))


def _ssm(zb3, mats):
    b, s, w = zb3.shape
    L, G = SSM_CHUNK, SSM_GROUPS
    ts = min(SSM_TILE, s)
    nch = ts // L
    full = lambda a: pl.BlockSpec(a.shape, lambda bi, i: (0,) * a.ndim)
    return pl.pallas_call(
        _ssm_kernel,
        grid=(b, s // ts),
        in_specs=[pl.BlockSpec((None, ts, w), lambda bi, i: (bi, i, 0))] + [full(m) for m in mats],
        out_specs=pl.BlockSpec((None, ts, w), lambda bi, i: (bi, i, 0)),
        out_shape=jax.ShapeDtypeStruct((b, s, w), BF16),
        scratch_shapes=[pltpu.VMEM((w // 128, ts, 128), F32),
                        pltpu.VMEM((L * (w // 128), nch, 128), F32),
                        pltpu.VMEM((G, nch, L * SSM_GROUP), BF16),
                        pltpu.VMEM((nch * G, 128), F32), pltpu.VMEM((nch * G, 128), F32),
                        pltpu.VMEM((nch * G, 128), F32), pltpu.VMEM((nch * G, 128), F32),
                        pltpu.VMEM((G, nch, L * SSM_GROUP), F32),
                        pltpu.VMEM((2 * G, 128), F32)],
        compiler_params=_cparams("parallel", "arbitrary"),
        name="ssm",
    )(zb3, *mats)


def _mlstm_kernel(zc_ref, zgt_ref, convw_ref, gb_ref, ng_ref, triu_ref, bones_ref, o_ref,
                  st_ref, m_ref, tail_ref, qk_ref,
                  rt_ref, bl_ref, rm_ref, cr128_ref, cr64_ref, b64_ref, r64_ref, s_ref, hh_ref):
    ts = zc_ref.shape[0]
    hd, nh, w = HEAD_DIM, MLSTM_HEADS, GROUP_W
    nchunk = ts // CHUNK
    nr = nchunk * 8
    lane1 = lax.broadcasted_iota(I32, (1, 128), 1)
    half = [(lane1 // hd) == (h % 2) for h in range(nh)]
    grp = [slice((h * hd) // 128 * 128, (h * hd) // 128 * 128 + 128) for h in range(nh)]

    @pl.when(pl.program_id(1) == 0)
    def _():
        st_ref[...] = jnp.zeros_like(st_ref)
        m_ref[...] = jnp.zeros_like(m_ref)
        tail_ref[...] = jnp.zeros_like(tail_ref)

    r_i = lax.broadcasted_iota(I32, (CHUNK, CHUNK), 0)
    c_i = lax.broadcasted_iota(I32, (CHUNK, CHUNK), 1)
    causal = c_i <= r_i
    k64 = lax.broadcasted_iota(I32, (8, w), 0)
    j64 = lax.broadcasted_iota(I32, (8, w), 1)
    sel64 = (j64 // hd == k64 % nh).astype(BF16)
    k128 = lax.broadcasted_iota(I32, (8, nh * 128), 0)
    j128 = lax.broadcasted_iota(I32, (8, nh * 128), 1)
    sel128 = (j128 // 128 == k128 % nh).astype(BF16)

    def lanes(t):
        lo = jnp.where(lane1 < hd, t[0:1, :], t[1:2, :])
        hi = jnp.where(lane1 < hd, t[2:3, :], t[3:4, :])
        return jnp.concatenate([lo, hi], axis=1)

    rowm = lax.broadcasted_iota(I32, (nr, 128), 0) % 8
    lanem = lax.broadcasted_iota(I32, (nr, 128), 1)
    graw = jnp.concatenate([zgt_ref[:, c * CHUNK:(c + 1) * CHUNK] + gb_ref[...] for c in range(nchunk)], axis=0)
    g2 = jnp.where(rowm < nh, graw, _log_sigmoid(graw))
    ghi, glo = _split_bf16(g2)
    cum = _dot(ghi, triu_ref[...]) + _dot(glo, triu_ref[...])
    b_t = pltpu.roll(cum, nr - 4, axis=0)
    r_t = g2 - b_t
    cr = r_t
    sh = 1
    while sh < CHUNK:
        cr = jnp.maximum(cr, jnp.where(lanem >= sh, pltpu.roll(cr, sh, axis=1), NEG))
        sh *= 2
    rt_ref[...] = r_t
    bl_ref[...] = jnp.broadcast_to(b_t[:, CHUNK - 1:CHUNK], (nr, 128))
    rm_ref[...] = jnp.broadcast_to(cr[:, CHUNK - 1:CHUNK], (nr, 128))

    x = zc_ref[:, 0:2 * w].astype(F32)
    xe = jnp.concatenate([tail_ref[...], x], axis=0)
    tail_ref[...] = x[ts - 8:, :]
    cw = convw_ref[...]
    acc = x * cw[CONV_K - 1:CONV_K, :]
    for sft in range(1, CONV_K):
        acc = acc + xe[8 - sft:8 - sft + ts, :] * cw[CONV_K - 1 - sft:CONV_K - sft, :]
    lane_qk = lax.broadcasted_iota(I32, (1, 2 * w), 1)
    qk_ref[...] = acc * _sigmoid(acc) * jnp.where(lane_qk < w, 1.0, hd ** -0.5)

    def hi_lo_rows(v):
        hi = v.astype(BF16).astype(F32)
        return jnp.where(rowm < nh, hi, pltpu.roll(v - hi, 4, axis=0))

    a_cr = hi_lo_rows(cr)
    a_b = hi_lo_rows(b_t)
    a_r = hi_lo_rows(r_t)
    for c in range(nchunk):
        rows = slice(c * CHUNK, (c + 1) * CHUNK)
        t8 = slice(c * 8, (c + 1) * 8)
        a_cr_c = a_cr[t8].astype(BF16)
        cr128_ref[rows, :] = _dot_tn(a_cr_c, sel128)
        cr64_ref[rows, :] = _dot_tn(a_cr_c, sel64)
        b64_ref[rows, :] = _dot_tn(a_b[t8].astype(BF16), sel64)
        r64_ref[rows, :] = _dot_tn(a_r[t8].astype(BF16), sel64)
        for h in range(nh):
            qg = qk_ref[rows, grp[h]].astype(BF16)
            kg = qk_ref[rows, w + grp[h].start:w + grp[h].stop].astype(BF16)
            s_ref[rows, h * 128:(h + 1) * 128] = _dot_nt(qg, jnp.where(half[h], kg, jnp.zeros_like(kg)))

    lane_w = lax.broadcasted_iota(I32, (1, w), 1)
    hmask = [(lane_w // hd) == h for h in range(nh)]
    ones_blk = [m.astype(BF16) * jnp.ones((CHUNK, 1), BF16) for m in hmask]
    ones_cols = jnp.ones((CHUNK, w), BF16)
    zblk = jnp.zeros((hd, 128), F32)
    ngrp = w // 128
    m_prev = m_ref[...]
    cblk = [st_ref[h * hd:(h + 1) * hd, grp[h]] for h in range(nh)]
    nblk = [st_ref[h * hd:(h + 1) * hd, w + grp[h].start:w + grp[h].stop] for h in range(nh)]
    for c in range(nchunk):
        rows = slice(c * CHUNK, (c + 1) * CHUNK)
        t8 = slice(c * 8, (c + 1) * 8)
        r_c = rt_ref[t8, :]
        g_last = jnp.maximum(rm_ref[t8, :], m_prev)
        decay = jnp.exp(m_prev - g_last)
        mprev_l = lanes(m_prev)
        glast_l = lanes(g_last)
        gb64 = jnp.maximum(cr64_ref[rows, :], mprev_l)
        w_inter = jnp.exp(mprev_l - gb64)
        e_negm = jnp.exp(-(b64_ref[rows, :] + gb64))
        v_all = zc_ref[rows, 2 * w:3 * w]
        wcat = []
        vblocks = []
        srows = []
        for h in range(nh):
            gb128 = jnp.maximum(cr128_ref[rows, h * 128:(h + 1) * 128], m_prev[h:h + 1, :])
            d = jnp.exp(jnp.where(causal, r_c[h:h + 1, :] - gb128, NEG))
            wcat.append((d * s_ref[rows, h * 128:(h + 1) * 128]).astype(BF16))
            vblocks.append(jnp.concatenate([jnp.where(hmask[h], v_all, jnp.zeros_like(v_all)), ones_blk[h]], axis=1))
            g = (h * hd) // 128
            srows.append(jnp.concatenate([cblk[h] if j == g else zblk for j in range(ngrp)]
                                         + [nblk[h] if j == g else zblk for j in range(ngrp)], axis=1))
        s_bf = jnp.concatenate(srows, axis=0).astype(BF16)
        out_aug = _dot(jnp.concatenate(wcat, axis=1), jnp.concatenate(vblocks, axis=0))
        out_aug = out_aug + _dot((qk_ref[rows, 0:w] * w_inter).astype(BF16), s_bf)
        hh_ref[rows, :] = out_aug[:, 0:w] / jnp.maximum(jnp.abs(out_aug[:, w:2 * w]), e_negm)
        kw = (qk_ref[rows, w:2 * w] * jnp.exp(r64_ref[rows, :] - glast_l)).astype(BF16)
        upd = _dot_tn(kw, jnp.concatenate([v_all, ones_cols], axis=1))
        for h in range(nh):
            rs = slice(h * hd, (h + 1) * hd)
            cblk[h] = cblk[h] * decay[h:h + 1, :] + jnp.where(half[h], upd[rs, grp[h]], 0.0)
            nblk[h] = (nblk[h] * decay[h:h + 1, :]
                       + jnp.where(half[h], upd[rs, w + grp[h].start:w + grp[h].stop], 0.0))
        m_prev = bl_ref[t8, :] + g_last
    m_ref[...] = m_prev
    for h in range(nh):
        st_ref[h * hd:(h + 1) * hd, grp[h]] = cblk[h]
        st_ref[h * hd:(h + 1) * hd, w + grp[h].start:w + grp[h].stop] = nblk[h]

    hh = hh_ref[...]
    ms = _dot((hh * hh).astype(BF16), bones_ref[...])
    og = zc_ref[:, 3 * w:4 * w].astype(F32)
    o_ref[...] = (_sigmoid(og) * hh * lax.rsqrt(ms + EPS) * ng_ref[...]).astype(o_ref.dtype)


def _mlstm(zc3, zgt, conv_w, gate_b_rows, norm_g):
    b, s, w = zc3.shape
    ts = SEQ_TILE
    nt = s // ts
    gw = GROUP_W
    t = jnp.arange(CHUNK)
    triu = (t[:, None] <= t[None, :]).astype(BF16)
    hid = jnp.arange(gw) // HEAD_DIM
    bones = jnp.where(hid[:, None] == hid[None, :], 1.0 / HEAD_DIM, 0.0).astype(BF16)
    full = lambda a: pl.BlockSpec(a.shape, lambda bi, i: (0,) * a.ndim)
    return pl.pallas_call(
        _mlstm_kernel,
        grid=(b, nt),
        in_specs=[pl.BlockSpec((None, ts, w), lambda bi, i: (bi, i, 0)),
                  pl.BlockSpec((8, ts), lambda bi, i: (0, bi * nt + i)),
                  full(conv_w), full(gate_b_rows), full(norm_g), full(triu), full(bones)],
        out_specs=pl.BlockSpec((None, ts, gw), lambda bi, i: (bi, i, 0)),
        out_shape=jax.ShapeDtypeStruct((b, s, gw), BF16),
        scratch_shapes=[pltpu.VMEM((gw, 2 * gw), F32),
                        pltpu.VMEM((8, 128), F32),
                        pltpu.VMEM((8, 2 * gw), F32),
                        pltpu.VMEM((ts, 2 * gw), F32),
                        pltpu.VMEM((ts // CHUNK * 8, 128), F32),
                        pltpu.VMEM((ts // CHUNK * 8, 128), F32),
                        pltpu.VMEM((ts // CHUNK * 8, 128), F32),
                        pltpu.VMEM((ts, MLSTM_HEADS * 128), F32),
                        pltpu.VMEM((ts, gw), F32),
                        pltpu.VMEM((ts, gw), F32),
                        pltpu.VMEM((ts, gw), F32),
                        pltpu.VMEM((ts, MLSTM_HEADS * 128), F32),
                        pltpu.VMEM((ts, gw), F32)],
        compiler_params=_cparams("parallel", "arbitrary"),
        name="mlstm",
    )(zc3, zgt, conv_w, gate_b_rows, norm_g, triu, bones)


def _sgu_kernel(zd_ref, lng_ref, lnb_ref, w_ref, bias_ref, o_ref):
    tm = zd_ref.shape[0]
    gw = GROUP_W // SGU_GROUPS
    u = _gelu(zd_ref[:, 0:GROUP_W].astype(F32))
    v = _gelu(zd_ref[:, GROUP_W:2 * GROUP_W].astype(F32))
    mu = jnp.mean(v, axis=-1, keepdims=True)
    vc = v - mu
    var = jnp.mean(vc * vc, axis=-1, keepdims=True)
    vn = (vc * lax.rsqrt(var + EPS) * lng_ref[...] + lnb_ref[...]).astype(BF16)
    for c in range(tm // CHUNK):
        rows = slice(c * CHUNK, (c + 1) * CHUNK)
        mixed = jnp.concatenate([_dot(w_ref[g], vn[rows, g * gw:(g + 1) * gw]) for g in range(SGU_GROUPS)], axis=-1)
        o_ref[rows, :] = (u[rows, :] * (mixed + bias_ref[...])).astype(o_ref.dtype)


def _sgu(zd, ln_g, ln_b, w_tril, bias):
    n = zd.shape[0]
    tm = TOKEN_TILE
    full = lambda a: pl.BlockSpec(a.shape, lambda i: (0,) * a.ndim)
    return pl.pallas_call(
        _sgu_kernel,
        grid=(n // tm,),
        in_specs=[pl.BlockSpec((tm, 2 * GROUP_W), lambda i: (i, 0)), full(ln_g), full(ln_b), full(w_tril), full(bias)],
        out_specs=pl.BlockSpec((tm, GROUP_W), lambda i: (i, 0)),
        out_shape=jax.ShapeDtypeStruct((n, GROUP_W), BF16),
        compiler_params=_cparams("parallel"),
        name="sgu",
    )(zd, ln_g, ln_b, w_tril, bias)


def _out_proj_kernel(x_ref, ya_ref, yb_ref, yc_ref, yd_ref, gluw_ref, glub_ref, wo_ref, ng_ref,
                     rwh_ref, rwl_ref, rb_ref, triu_ref,
                     x1_ref, hx_ref, meta_ref, cnt_ref, carry_ref):
    tm = x_ref.shape[0]

    @pl.when(pl.program_id(0) == 0)
    def _():
        carry_ref[...] = jnp.zeros_like(carry_ref)

    yb = _gelu(yb_ref[...].astype(F32))
    yb = yb * _sigmoid(_dot(yb.astype(BF16), gluw_ref[...]) + glub_ref[...])
    mix = _dot(ya_ref[...], wo_ref[0:GROUP_W, :])
    mix = mix + _dot(yb.astype(BF16), wo_ref[GROUP_W:2 * GROUP_W, :])
    mix = mix + _dot(yc_ref[...], wo_ref[2 * GROUP_W:3 * GROUP_W, :])
    mix = mix + _dot(yd_ref[...], wo_ref[3 * GROUP_W:4 * GROUP_W, :])
    x1 = x_ref[...] + mix
    x1_ref[...] = x1
    h = _rms(x1, ng_ref[...])
    hx_ref[:, 0:PACK_W] = _pack_rows(h)

    hh, hl = _split_bf16(h)
    wh = rwh_ref[...]
    wl = rwl_ref[...]
    logits = _dot_nt(wh, hh) + _dot_nt(wh, hl) + _dot_nt(wl, hh) + rb_ref[...]
    gl = [logits[j:j + 1, :] for j in range(N_GROUPS)]
    gmax = functools.reduce(jnp.maximum, gl)
    gsel = jnp.full((1, tm), N_GROUPS - 1, I32)
    for j in range(N_GROUPS - 2, -1, -1):
        gsel = jnp.where(gl[j] == gmax, j, gsel)
    p_g = 1.0 / functools.reduce(jnp.add, [jnp.exp(v - gmax) for v in gl])
    e_in = []
    for i in range(EPG):
        v = logits[N_GROUPS + i:N_GROUPS + i + 1, :]
        for j in range(1, N_GROUPS):
            r = N_GROUPS + j * EPG + i
            v = jnp.where(gsel == j, logits[r:r + 1, :], v)
        e_in.append(v)
    v1 = functools.reduce(jnp.maximum, e_in)
    i1 = jnp.full((1, tm), EPG - 1, I32)
    for i in range(EPG - 2, -1, -1):
        i1 = jnp.where(e_in[i] == v1, i, i1)
    rest = [jnp.where(i1 == i, NEG, e_in[i]) for i in range(EPG)]
    v2 = functools.reduce(jnp.maximum, rest)
    i2 = jnp.full((1, tm), EPG - 1, I32)
    for i in range(EPG - 2, -1, -1):
        i2 = jnp.where((rest[i] == v2) & (i1 != i), i, i2)
    e2 = jnp.exp(v2 - v1)
    w1 = p_g / (1.0 + e2)
    w2 = p_g * e2 / (1.0 + e2)
    lo = jnp.minimum(i1, i2)
    hi = jnp.maximum(i1, i2)
    w_lo = jnp.where(i1 < i2, w1, w2)
    w_hi = jnp.where(i1 < i2, w2, w1)
    bucket = gsel * PAIRS + ((lo * (2 * EPG - 1 - lo)) >> 1) + (hi - lo - 1)

    kid = lax.broadcasted_iota(I32, (BUCKET_PAD, tm), 0)
    onehot = (kid == bucket).astype(F32)
    prefix = _dot(onehot.astype(BF16), triu_ref[...])
    carry = carry_ref[...]
    rank = jnp.sum(onehot * (prefix - 1.0 + carry[:, 0:1]), axis=0, keepdims=True)
    carry = carry + prefix[:, tm - 1:tm]
    carry_ref[...] = carry
    cnt_ref[...] = carry
    meta_ref[...] = jnp.concatenate([bucket, rank.astype(I32), jnp.zeros((6, tm), I32)], axis=0)
    wrows = jnp.concatenate([w_lo, w_hi, jnp.zeros((126, tm), F32)], axis=0)
    for c in range(tm // 128):
        hx_ref[c * 128:(c + 1) * 128, PACK_W:ROW_W] = pltpu.bitcast(wrows[:, c * 128:(c + 1) * 128].T, U32)


def _out_proj(x2, ya, yb, yc, yd, glu_w, glu_b, w_out, norm_g, rw_hi, rw_lo, rb):
    n = x2.shape[0]
    tm = TOKEN_TILE
    row = lambda w: pl.BlockSpec((tm, w), lambda i: (i, 0))
    full = lambda a: pl.BlockSpec(a.shape, lambda i: (0,) * a.ndim)
    t = jnp.arange(tm)
    triu = (t[:, None] <= t[None, :]).astype(BF16)
    return pl.pallas_call(
        _out_proj_kernel,
        grid=(n // tm,),
        in_specs=[row(D_MODEL), row(GROUP_W), row(GROUP_W), row(GROUP_W), row(GROUP_W),
                  full(glu_w), full(glu_b), full(w_out), full(norm_g), full(rw_hi), full(rw_lo), full(rb),
                  full(triu)],
        out_specs=[row(D_MODEL), row(ROW_W), pl.BlockSpec((8, tm), lambda i: (0, i)),
                   pl.BlockSpec((BUCKET_PAD, 128), lambda i: (0, 0))],
        out_shape=[jax.ShapeDtypeStruct((n, D_MODEL), F32), jax.ShapeDtypeStruct((n, ROW_W), U32),
                   jax.ShapeDtypeStruct((8, n), I32), jax.ShapeDtypeStruct((BUCKET_PAD, 128), F32)],
        scratch_shapes=[pltpu.VMEM((BUCKET_PAD, 128), F32)],
        compiler_params=_cparams("arbitrary"),
        name="out_proj",
    )(x2, ya, yb, yc, yd, glu_w, glu_b, w_out, norm_g, rw_hi, rw_lo, rb, triu)


def _dispatch_kernel(dest_ref, hx_ref, xs_in_ref, xs_ref, sem):
    del xs_in_ref
    nq = hx_ref.shape[0]

    def start(q, c):
        for r in range(SUBLANES):
            d = dest_ref[0, 0, q * SUBLANES + r]
            pltpu.make_async_copy(hx_ref.at[q, pl.ds(r, 1)], xs_ref.at[pl.ds(d, 1)], sem).start()
        return c

    lax.fori_loop(0, nq, start, 0)
    pltpu.make_async_copy(hx_ref, hx_ref, sem).wait()


def _dispatch(dest3, hx, p_tot):
    n = hx.shape[0]
    tm = TOKEN_TILE
    xs0 = jnp.zeros((p_tot, ROW_W), U32)
    hx = hx.reshape(n // SUBLANES, SUBLANES, ROW_W)
    return pl.pallas_call(
        _dispatch_kernel,
        grid=(n // tm,),
        in_specs=[pl.BlockSpec((1, 1, tm), lambda i: (i, 0, 0), memory_space=pltpu.SMEM),
                  pl.BlockSpec((tm // SUBLANES, SUBLANES, ROW_W), lambda i: (i, 0, 0)),
                  pl.BlockSpec(memory_space=pl.ANY)],
        out_specs=pl.BlockSpec(memory_space=pl.ANY),
        out_shape=jax.ShapeDtypeStruct((p_tot, ROW_W), U32),
        input_output_aliases={2: 0},
        scratch_shapes=[pltpu.SemaphoreType.DMA],
        compiler_params=_cparams("arbitrary"),
        name="dispatch",
    )(dest3, hx, xs0)


def _ffn_kernel(valid_ref, xblk_ref, run_ref, exprun_ref, nrun_ref, xs_ref, wg_hbm, wu_hbm, wd_hbm, ys_ref,
                stg_g, stg_u, stg_d, wgu_a, wdn_a, wgu_b, wdn_b, sem, *, layer):
    del xblk_ref
    j = pl.program_id(0)
    valid = valid_ref[j]
    jp = jnp.maximum(j - 1, 0)

    def expert_copies(s, r, slot):
        e = exprun_ref[s, r]
        return [pltpu.make_async_copy(src.at[layer, e], dst.at[s, slot], sem.at[s, slot])
                for src, dst in ((wg_hbm, stg_g), (wu_hbm, stg_u), (wd_hbm, stg_d))]

    for s, (wgu, wdn) in enumerate(((wgu_a, wdn_a), (wgu_b, wdn_b))):
        r = run_ref[s, j]

        @pl.when((j == 0) | (r != run_ref[s, jp]))
        def _():
            slot = r % 2

            @pl.when(j == 0)
            def _():
                for c in expert_copies(s, r, slot):
                    c.start()

            for c in expert_copies(s, r, slot):
                c.wait()

            @pl.when(r + 1 < nrun_ref[s])
            def _():
                for c in expert_copies(s, r + 1, 1 - slot):
                    c.start()

            wgu[:, 0:D_EXPERT] = stg_g[s, slot].astype(BF16)
            wgu[:, D_EXPERT:2 * D_EXPERT] = stg_u[s, slot].astype(BF16)
            wdn[...] = stg_d[s, slot].astype(BF16)

    def ffn_rows(rows):
        x_lo, x_hi = _unpack_rows(xs_ref[rows, 0:PACK_W])
        xb = jnp.concatenate([x_lo.astype(BF16), x_hi.astype(BF16)], axis=1)
        wts = pltpu.bitcast(xs_ref[rows, PACK_W:ROW_W], F32)

        def expert(wgu, wdn):
            gu = _dot(xb, wgu[...])
            g = gu[:, 0:D_EXPERT]
            act = (g * _sigmoid(g) * gu[:, D_EXPERT:2 * D_EXPERT]).astype(BF16)
            return _dot(act, wdn[...])

        ys_ref[rows, :] = _pack_rows(expert(wgu_a, wdn_a) * wts[:, 0:1] + expert(wgu_b, wdn_b) * wts[:, 1:2])

    half = FFN_SUB // 2
    for sb in range(FFN_BLOCK // FFN_SUB):
        r0 = sb * FFN_SUB
        left = valid - r0

        @pl.when(left <= 0)
        def _():
            ys_ref[r0:r0 + FFN_SUB, :] = jnp.zeros((FFN_SUB, PACK_W), U32)

        @pl.when((left > 0) & (left <= half))
        def _():
            ffn_rows(slice(r0, r0 + half))
            ys_ref[r0 + half:r0 + FFN_SUB, :] = jnp.zeros((half, PACK_W), U32)

        @pl.when(left > half)
        def _():
            ffn_rows(slice(r0, r0 + FFN_SUB))


def _ffn(valid, xblk, run, exprun, nrun, xs, layer, w_gate, w_up, w_down):
    p_tot = xs.shape[0]
    bm = FFN_BLOCK
    nblk = p_tot // bm
    up_shape = (D_MODEL, D_EXPERT)
    dn_shape = (D_EXPERT, D_MODEL)
    hbm = pl.BlockSpec(memory_space=pl.ANY)
    return pl.pallas_call(
        functools.partial(_ffn_kernel, layer=layer),
        grid_spec=pltpu.PrefetchScalarGridSpec(
            num_scalar_prefetch=5,
            grid=(nblk,),
            in_specs=[pl.BlockSpec((bm, ROW_W), lambda j, va, xb, ru, er, nr: (xb[j], 0)), hbm, hbm, hbm],
            out_specs=pl.BlockSpec((bm, PACK_W), lambda j, va, xb, ru, er, nr: (j, 0)),
            scratch_shapes=[pltpu.VMEM((2, 2) + up_shape, F32), pltpu.VMEM((2, 2) + up_shape, F32),
                            pltpu.VMEM((2, 2) + dn_shape, F32),
                            pltpu.VMEM((D_MODEL, 2 * D_EXPERT), BF16), pltpu.VMEM(dn_shape, BF16),
                            pltpu.VMEM((D_MODEL, 2 * D_EXPERT), BF16), pltpu.VMEM(dn_shape, BF16),
                            pltpu.SemaphoreType.DMA((2, 2))],
        ),
        out_shape=jax.ShapeDtypeStruct((p_tot, PACK_W), U32),
        compiler_params=_cparams("arbitrary"),
        name="ffn",
    )(valid, xblk, run, exprun, nrun, xs, w_gate, w_up, w_down)


def _combine_kernel(dest_ref, dnext_ref, x1_ref, ys_ref, ng_ref, o_ref, buf_ref, sem, *, final_norm):
    i = pl.program_id(0)
    tm = x1_ref.shape[0]
    nq = tm // SUBLANES
    slot = i % 2

    def gather(d_ref, sl):
        def start(q, c):
            for r in range(SUBLANES):
                d = d_ref[0, 0, q * SUBLANES + r]
                pltpu.make_async_copy(ys_ref.at[pl.ds(d, 1)], buf_ref.at[sl, q, pl.ds(r, 1)], sem.at[sl]).start()
            return c

        lax.fori_loop(0, nq, start, 0)

    @pl.when(i == 0)
    def _():
        gather(dest_ref, 0)

    pltpu.make_async_copy(buf_ref.at[slot], buf_ref.at[slot], sem.at[slot]).wait()

    @pl.when(i + 1 < pl.num_programs(0))
    def _():
        gather(dnext_ref, 1 - slot)

    y_lo, y_hi = _unpack_rows(buf_ref[slot].reshape(tm, PACK_W))
    x2 = x1_ref[...] + jnp.concatenate([y_lo, y_hi], axis=1)
    o_ref[...] = _rms(x2, ng_ref[...]) if final_norm else x2


def _combine(dest3, x1, ys, norm_g, final_norm):
    n = x1.shape[0]
    tm = TOKEN_TILE
    last = n // tm - 1
    return pl.pallas_call(
        functools.partial(_combine_kernel, final_norm=final_norm),
        grid=(n // tm,),
        in_specs=[pl.BlockSpec((1, 1, tm), lambda i: (i, 0, 0), memory_space=pltpu.SMEM),
                  pl.BlockSpec((1, 1, tm), lambda i: (jnp.minimum(i + 1, last), 0, 0), memory_space=pltpu.SMEM),
                  pl.BlockSpec((tm, D_MODEL), lambda i: (i, 0)),
                  pl.BlockSpec(memory_space=pl.ANY),
                  pl.BlockSpec((1, D_MODEL), lambda i: (0, 0))],
        out_specs=pl.BlockSpec((tm, D_MODEL), lambda i: (i, 0)),
        out_shape=jax.ShapeDtypeStruct((n, D_MODEL), F32),
        scratch_shapes=[pltpu.VMEM((2, tm // SUBLANES, SUBLANES, PACK_W), U32), pltpu.SemaphoreType.DMA((2,))],
        compiler_params=_cparams("arbitrary"),
        name="combine",
    )(dest3, dest3, x1, ys, norm_g)


def _routing_tables(meta, counts, n):
    bm = FFN_BLOCK
    nblk = n // bm + N_BUCKETS
    bucket = meta[0]
    rank = meta[1]
    cnt = counts[:N_BUCKETS, 0].astype(I32)
    padded = ((cnt + bm - 1) // bm) * bm
    pad_end = jnp.cumsum(padded)
    pad_start = pad_end - padded
    onehot = bucket[:, None] == jnp.arange(N_BUCKETS, dtype=I32)[None, :]
    dest = rank + jnp.sum(jnp.where(onehot, pad_start[None, :], 0), axis=1)
    nact = (pad_end[-1] // bm).astype(I32)
    j = jnp.arange(nblk, dtype=I32)
    blk = jnp.minimum(j, jnp.maximum(nact - 1, 0))
    pos = blk * bm
    bkt = jnp.minimum(jnp.sum((pad_end[None, :] <= pos[:, None]).astype(I32), axis=1), N_BUCKETS - 1)
    in_bkt = bkt[:, None] == jnp.arange(N_BUCKETS, dtype=I32)[None, :]
    sel = lambda tab: jnp.sum(jnp.where(in_bkt, tab[None, :], 0), axis=1)
    valid = jnp.where(j < nact, jnp.clip(sel(cnt) - (pos - sel(pad_start)), 0, bm), 0)
    lo_tab, hi_tab = [], []
    for g in range(N_GROUPS):
        for a in range(EPG):
            for b in range(a + 1, EPG):
                lo_tab.append(g * EPG + a)
                hi_tab.append(g * EPG + b)
    experts = jnp.stack([sel(jnp.asarray(lo_tab, I32)), sel(jnp.asarray(hi_tab, I32))]).astype(I32)
    change = jnp.concatenate([jnp.zeros((2, 1), I32), (experts[:, 1:] != experts[:, :-1]).astype(I32)], axis=1)
    run = jnp.cumsum(change, axis=1).astype(I32)
    is_run = run[:, :, None] == j[None, None, :]
    exprun = jnp.max(jnp.where(is_run, experts[:, :, None], 0), axis=1).astype(I32)
    nrun = run[:, -1] + 1
    return dest.astype(I32), valid.astype(I32), blk.astype(I32), run, exprun, nrun.astype(I32), nblk * bm


def _layer(x2, b, s, p, final_g):
    n = b * s
    za, zb, zc, zd, zgt = _in_proj(x2, p['norm_mix_g'], p['wa'], p['wb'], p['wc'], p['wd'], p['wgt'])
    ya = _attention(za.reshape(b, s, -1), p['sinks']).reshape(n, GROUP_W)
    yb = _ssm(zb.reshape(b, s, GROUP_W), p['ssm']).reshape(n, GROUP_W)
    yc = _mlstm(zc.reshape(b, s, -1), zgt, p['conv_w'], p['gate_b'], p['mlstm_norm_g']).reshape(n, GROUP_W)
    yd = _sgu(zd, p['sgu_ln_g'], p['sgu_ln_b'], p['sgu_w'], p['sgu_bias'])
    x1, hx, meta, counts = _out_proj(x2, ya, yb, yc, yd, p['glu_w'], p['glu_b'], p['w_out'], p['norm_ffn_g'],
                                     p['rw_hi'], p['rw_lo'], p['rb'])
    dest, valid, xblk, run, exprun, nrun, p_tot = _routing_tables(meta, counts, n)
    dest3 = dest.reshape(n // TOKEN_TILE, 1, TOKEN_TILE)
    xs = _dispatch(dest3, hx, p_tot)
    ys = _ffn(valid, xblk, run, exprun, nrun, xs, p['layer'], p['w_gate'], p['w_up'], p['w_down'])
    if final_g is None:
        return _combine(dest3, x1, ys, p['norm_ffn_g'], False)
    return _combine(dest3, x1, ys, final_g, True)


def _prep_layer(l, norm_mix_g, w_in, attn_sinks, ssm_a_re, ssm_a_im, ssm_b_re, ssm_b_im, ssm_c_re, ssm_c_im,
                ssm_d, ssm_log_dt, ssm_glu_w, ssm_glu_b, mlstm_conv_w, mlstm_gate_b, mlstm_norm_g,
                sgu_ln_g, sgu_ln_b, sgu_w, sgu_b, w_out, norm_ffn_g, router_group_w, router_group_b,
                router_expert_w, router_expert_b, expert_w_gate, expert_w_up, expert_w_down):
    w = w_in[l]
    o_su = 2 * GROUP_W
    o_c = o_su + GROUP_W
    o_ci = o_c + 3 * GROUP_W
    o_co = o_ci + 2 * MLSTM_HEADS
    o_d = o_co + GROUP_W
    wc = jnp.concatenate([w[:, o_c:o_ci], w[:, o_co:o_d]], axis=1)
    wgt = w[:, o_ci:o_co].T
    head_cols = lambda m, hq: m[..., hq * HEAD_DIM:(hq + 1) * HEAD_DIM]
    order = [g * ATTN_REP + r for r in range(ATTN_REP) for g in range(KV_HEADS)]
    wa = jnp.concatenate([head_cols(w, hq) * (HEAD_DIM ** -0.5) for hq in order] + [w[:, GROUP_W:o_su]], axis=1)
    wo = jnp.concatenate([w_out[l][hq * HEAD_DIM:(hq + 1) * HEAD_DIM] for hq in order] + [w_out[l][GROUP_W:]], axis=0)
    rw = jnp.concatenate([router_group_w[l], router_expert_w[l]], axis=1).T.astype(F32)
    rw = jnp.pad(rw, ((0, 4), (0, 0)))
    rw_hi = rw.astype(BF16)
    rw_lo = (rw - rw_hi.astype(F32)).astype(BF16)
    rb = jnp.pad(jnp.concatenate([router_group_b[l], router_expert_b[l]]).astype(F32), (0, 4))[:, None]
    tril = jnp.tril(jnp.ones((CHUNK, CHUNK), F32))
    gw = GROUP_W // SGU_GROUPS
    return dict(
        norm_mix_g=norm_mix_g[l][None, :].astype(F32),
        wa=wa.astype(BF16), wb=w[:, o_su:o_c].astype(BF16), wc=wc.astype(BF16),
        wd=w[:, o_d:].astype(BF16), wgt=wgt.astype(BF16),
        sinks=attn_sinks[l].astype(F32),
        ssm=_ssm_matrices(ssm_a_re[l], ssm_a_im[l], ssm_b_re[l], ssm_b_im[l], ssm_c_re[l], ssm_c_im[l],
                          ssm_d[l], ssm_log_dt[l]),
        glu_w=ssm_glu_w[l].astype(BF16), glu_b=ssm_glu_b[l][None, :].astype(F32),
        conv_w=mlstm_conv_w[l].astype(F32),
        gate_b=jnp.broadcast_to(mlstm_gate_b[l].astype(F32)[:, None], (2 * MLSTM_HEADS, 128)),
        mlstm_norm_g=mlstm_norm_g[l][None, :].astype(F32),
        sgu_ln_g=sgu_ln_g[l][None, :].astype(F32), sgu_ln_b=sgu_ln_b[l][None, :].astype(F32),
        sgu_w=(sgu_w[l].astype(F32) * tril).astype(BF16),
        sgu_bias=jnp.repeat(sgu_b[l].astype(F32).T, gw, axis=1),
        w_out=wo.astype(BF16), norm_ffn_g=norm_ffn_g[l][None, :].astype(F32),
        rw_hi=rw_hi, rw_lo=rw_lo, rb=rb,
        layer=l, w_gate=expert_w_gate, w_up=expert_w_up, w_down=expert_w_down,
    )


def kernel(x, norm_mix_g, w_in, attn_sinks, ssm_a_re, ssm_a_im, ssm_b_re, ssm_b_im, ssm_c_re, ssm_c_im, ssm_d, ssm_log_dt, ssm_glu_w, ssm_glu_b, mlstm_conv_w, mlstm_gate_b, mlstm_norm_g, sgu_ln_g, sgu_ln_b, sgu_w, sgu_b, w_out, norm_ffn_g, router_group_w, router_group_b, router_expert_w, router_expert_b, expert_w_gate, expert_w_up, expert_w_down, norm_final_g):
    b, s, d = x.shape
    depth = w_in.shape[0]
    x2 = x.reshape(b * s, d).astype(F32)
    for l in range(depth):
        p = _prep_layer(l, norm_mix_g, w_in, attn_sinks, ssm_a_re, ssm_a_im, ssm_b_re, ssm_b_im, ssm_c_re,
                        ssm_c_im, ssm_d, ssm_log_dt, ssm_glu_w, ssm_glu_b, mlstm_conv_w, mlstm_gate_b,
                        mlstm_norm_g, sgu_ln_g, sgu_ln_b, sgu_w, sgu_b, w_out, norm_ffn_g, router_group_w,
                        router_group_b, router_expert_w, router_expert_b, expert_w_gate, expert_w_up,
                        expert_w_down)
        final_g = norm_final_g[None, :].astype(F32) if l == depth - 1 else None
        x2 = _layer(x2, b, s, p, final_g)
    return x2.reshape(b, s, d).astype(x.dtype)
```

```python
import functools
import math

import jax
import jax.numpy as jnp
from jax import lax
from jax.experimental import pallas as pl
from jax.experimental.pallas import tpu as pltpu

F32 = jnp.float32
BF16 = jnp.bfloat16
I32 = jnp.int32
U32 = jnp.uint32

D_MODEL = 1024
GROUP_W = 256
HEAD_DIM = 64
EPS = 1e-6
NEG = -1e30
WINDOW = 128
KV_HEADS = 2
ATTN_REP = 2
SSM_GROUP = 16
SSM_GROUPS = 16
SSM_STATE = 64
SSM_CHUNK = 16
MLSTM_HEADS = 4
CHUNK = 128
CONV_K = 4
SGU_GROUPS = 4
N_GROUPS = 4
EPG = 8
N_EXPERTS = 32
D_EXPERT = 512
PAIRS = EPG * (EPG - 1) // 2
N_BUCKETS = N_GROUPS * PAIRS
BUCKET_PAD = 128
PACK_W = D_MODEL // 2
ROW_W = PACK_W + 128

TOKEN_TILE = 512
MOE_TILE = 1024
SEQ_TILE = 512
SSM_TILE = 4096
SSM_SLAB = 64
FFN_BLOCK = 512
FFN_SUB = 256
SUBLANES = 8
VMEM_LIMIT = 56 * 1024 * 1024


def _cparams(*sem):
    return pltpu.CompilerParams(dimension_semantics=sem, vmem_limit_bytes=VMEM_LIMIT)


def _rms(x, g):
    return x * lax.rsqrt(jnp.mean(x * x, axis=-1, keepdims=True) + EPS) * g


def _gelu(x):
    return 0.5 * x * (1.0 + jnp.tanh(math.sqrt(2.0 / math.pi) * (x + 0.044715 * (x * x * x))))


def _sigmoid(x):
    return 1.0 / (1.0 + jnp.exp(-x))


def _log_sigmoid(x):
    return jnp.minimum(x, 0.0) - jnp.log(1.0 + jnp.exp(-jnp.abs(x)))


def _dot(a, b):
    return jnp.dot(a, b, preferred_element_type=F32)


def _dot_nt(a, b):
    return lax.dot_general(a, b, (((1,), (1,)), ((), ())), preferred_element_type=F32)


def _dot_tn(a, b):
    return lax.dot_general(a, b, (((0,), (0,)), ((), ())), preferred_element_type=F32)


def _split_bf16(x):
    hi = x.astype(BF16)
    lo = (x - hi.astype(F32)).astype(BF16)
    return hi, lo


def _pack_rows(x):
    w = x.shape[1] // 2
    bits = lambda v: pltpu.bitcast(v.astype(BF16).astype(F32), U32)
    return (bits(x[:, 0:w]) >> 16) | (bits(x[:, w:2 * w]) & jnp.uint32(0xFFFF0000))


def _unpack_rows(p):
    return pltpu.bitcast(p << 16, F32), pltpu.bitcast(p & jnp.uint32(0xFFFF0000), F32)


def _in_proj_kernel(x_ref, g_ref, wa_ref, wb_ref, wc_ref, wd_ref, wgt_ref,
                    za_ref, zb_ref, zc_ref, zd_ref, zgt_ref):
    hb = _rms(x_ref[...], g_ref[...]).astype(BF16)
    za_ref[...] = _dot(hb, wa_ref[...]).astype(BF16)
    zb_ref[...] = _dot(hb, wb_ref[...]).astype(BF16)
    zc_ref[...] = _dot(hb, wc_ref[...]).astype(BF16)
    zd_ref[...] = _dot(hb, wd_ref[...]).astype(BF16)
    zgt_ref[...] = _dot_nt(wgt_ref[...], hb)


def _in_proj(x2, g, wa, wb, wc, wd, wgt):
    n = x2.shape[0]
    tm = TOKEN_TILE
    row = lambda w: pl.BlockSpec((tm, w), lambda i: (i, 0))
    full = lambda a: pl.BlockSpec(a.shape, lambda i: (0,) * a.ndim)
    widths = (wa.shape[1], wb.shape[1], wc.shape[1], wd.shape[1])
    return pl.pallas_call(
        _in_proj_kernel,
        grid=(n // tm,),
        in_specs=[row(D_MODEL), full(g), full(wa), full(wb), full(wc), full(wd), full(wgt)],
        out_specs=[row(w) for w in widths] + [pl.BlockSpec((wgt.shape[0], tm), lambda i: (0, i))],
        out_shape=[jax.ShapeDtypeStruct((n, w), BF16) for w in widths]
        + [jax.ShapeDtypeStruct((wgt.shape[0], n), F32)],
        compiler_params=_cparams("parallel"),
        name="in_proj",
    )(x2, g, wa, wb, wc, wd, wgt)


def _attn_kernel(sink_ref, cur_ref, prev_ref, o_ref):
    first = pl.program_id(1) == 0
    nblk = cur_ref.shape[0] // WINDOW
    row = lax.broadcasted_iota(I32, (WINDOW, 2 * WINDOW), 0)
    col = lax.broadcasted_iota(I32, (WINDOW, 2 * WINDOW), 1)
    band = (col <= row + WINDOW) & (col > row)
    lane = lax.broadcasted_iota(I32, (1, 128), 1)
    kv_lanes = [(lane // HEAD_DIM) == g for g in range(KV_HEADS)]
    ones = jnp.ones((2 * WINDOW, 128), BF16)
    ko = 2 * 128
    vo = 3 * 128
    for j in range(nblk):
        cur = cur_ref[j * WINDOW:(j + 1) * WINDOW, :]
        if j == 0:
            prev = prev_ref[...]
            mask = band & ((col >= WINDOW) | jnp.logical_not(first))
        else:
            prev = cur_ref[(j - 1) * WINDOW:j * WINDOW, :]
            mask = band
        kk = jnp.concatenate([prev[:, ko:ko + 128], cur[:, ko:ko + 128]], axis=0)
        vaug = jnp.concatenate([jnp.concatenate([prev[:, vo:vo + 128], cur[:, vo:vo + 128]], axis=0), ones], axis=1)
        outs = []
        for r in range(ATTN_REP):
            qg = cur[:, r * 128:(r + 1) * 128]
            og = []
            for g in range(KV_HEADS):
                s = _dot_nt(jnp.where(kv_lanes[g], qg, jnp.zeros_like(qg)), kk)
                s = jnp.where(mask, s, NEG)
                sink = sink_ref[g * ATTN_REP + r]
                m = jnp.maximum(jnp.max(s, axis=-1, keepdims=True), sink)
                oa = _dot(jnp.exp(s - m).astype(BF16), vaug)
                og.append(oa[:, 0:128] / (oa[:, 128:256] + jnp.exp(sink - m)))
            outs.append(jnp.where(kv_lanes[0], og[0], og[1]))
        o_ref[j * WINDOW:(j + 1) * WINDOW, :] = jnp.concatenate(outs, axis=-1).astype(o_ref.dtype)


def _attention(za3, sinks):
    b, s, w = za3.shape
    ts = SEQ_TILE
    per = ts // WINDOW
    return pl.pallas_call(
        _attn_kernel,
        grid=(b, s // ts),
        in_specs=[pl.BlockSpec(memory_space=pltpu.SMEM),
                  pl.BlockSpec((None, ts, w), lambda bi, i: (bi, i, 0)),
                  pl.BlockSpec((None, WINDOW, w), lambda bi, i: (bi, jnp.maximum(i * per - 1, 0), 0))],
        out_specs=pl.BlockSpec((None, ts, GROUP_W), lambda bi, i: (bi, i, 0)),
        out_shape=jax.ShapeDtypeStruct((b, s, GROUP_W), BF16),
        compiler_params=_cparams("parallel", "parallel"),
        name="attn",
    )(sinks, za3, za3)


def _ssm_kernel(zb_ref, bre_ref, bim_ref, t_ref, cre_ref, cim_ref, are_ref, aim_ref, o_ref,
                x_ref, xs_ref, u_ref, vre_ref, vim_ref, sre_ref, sim_ref, y_ref, st_ref):
    ts = zb_ref.shape[0]
    L, G, H = SSM_CHUNK, SSM_GROUPS, SSM_GROUP
    nch = ts // L
    half_g = 128 // H
    n_half = G // half_g

    @pl.when(pl.program_id(1) == 0)
    def _():
        st_ref[...] = jnp.zeros_like(st_ref)

    blk = lax.broadcasted_iota(I32, (1, 128), 1) // H

    def block_transpose(arrs):
        a = list(arrs)
        s = half_g // 2
        while s >= 1:
            keep = (blk & s) == 0
            for i in range(half_g):
                if i & s == 0:
                    ai, aj = a[i], a[i + s]
                    a[i] = jnp.where(keep, ai, pltpu.roll(aj, s * H, axis=1))
                    a[i + s] = jnp.where(keep, pltpu.roll(ai, 128 - s * H, axis=1), aj)
            s //= 2
        return a

    for hf in range(n_half):
        x_ref[hf] = zb_ref[:, hf * 128:(hf + 1) * 128].astype(F32)
    for sg in range(L):
        for hf in range(n_half):
            xs_ref[sg * n_half + hf] = x_ref[hf, pl.ds(sg, nch, stride=L), :]
    for hf in range(n_half):
        for oc in range(L // half_g):
            for r0 in range(0, nch, SSM_SLAB):
                rows = slice(r0, r0 + SSM_SLAB)
                t = block_transpose([xs_ref[(oc * half_g + k) * n_half + hf, rows, :] for k in range(half_g)])
                for gl in range(half_g):
                    u_ref[hf * half_g + gl, rows, oc * 128:(oc + 1) * 128] = t[gl].astype(BF16)
    for g in range(G):
        ug = u_ref[g]
        vre_ref[pl.ds(g, nch, stride=G), :] = _dot(ug, bre_ref[g])
        vim_ref[pl.ds(g, nch, stride=G), :] = _dot(ug, bim_ref[g])

    are = are_ref[...]
    aim = aim_ref[...]

    def step(c, carry):
        sre, sim = carry
        r0 = pl.multiple_of(c * G, G)
        sre_ref[pl.ds(r0, G), :] = sre
        sim_ref[pl.ds(r0, G), :] = sim
        vre = vre_ref[pl.ds(r0, G), :]
        vim = vim_ref[pl.ds(r0, G), :]
        return (are * sre - aim * sim + vre, are * sim + aim * sre + vim)

    sre, sim = lax.fori_loop(0, nch, step, (st_ref[0:G, :], st_ref[G:2 * G, :]), unroll=4)
    st_ref[0:G, :] = sre
    st_ref[G:2 * G, :] = sim

    for g in range(G):
        y = _dot(u_ref[g], t_ref[g])
        y = y + _dot(sre_ref[pl.ds(g, nch, stride=G), :].astype(BF16), cre_ref[g])
        y = y + _dot(sim_ref[pl.ds(g, nch, stride=G), :].astype(BF16), cim_ref[g])
        y_ref[g] = y
    for hf in range(n_half):
        for oc in range(L // half_g):
            for r0 in range(0, nch, SSM_SLAB):
                t = block_transpose([y_ref[hf * half_g + gl, r0:r0 + SSM_SLAB, oc * 128:(oc + 1) * 128]
                                     for gl in range(half_g)])
                for k in range(half_g):
                    x_ref[hf, pl.ds(oc * half_g + k + L * r0, SSM_SLAB, stride=L), :] = t[k]
    for hf in range(n_half):
        o_ref[:, hf * 128:(hf + 1) * 128] = x_ref[hf].astype(o_ref.dtype)


def _ssm_matrices(a_re, a_im, b_re, b_im, c_re, c_im, d_skip, log_dt):
    L = SSM_CHUNK
    a = lax.complex(a_re.astype(F32), a_im.astype(F32))
    dt = jnp.exp(log_dt.astype(F32))[:, None]
    adt = a * dt
    a_bar = jnp.exp(adt)
    b_bar = ((a_bar - 1.0) / a)[..., None] * lax.complex(b_re.astype(F32), b_im.astype(F32))
    c_mat = lax.complex(c_re.astype(F32), c_im.astype(F32))
    lag = jnp.arange(L + 1, dtype=F32)
    pw = jnp.exp(adt[None] * lag[:, None, None])
    kern = jnp.einsum('gop,dgp,gpi->dgoi', c_mat, pw[:L], b_bar).real
    sig = jnp.arange(L)[:, None]
    tau = jnp.arange(L)[None, :]
    d = tau - sig
    kt = kern[jnp.clip(d, 0, L - 1)]
    kt = jnp.where((d >= 0)[:, :, None, None, None], kt, 0.0)
    eye = jnp.eye(SSM_GROUP, dtype=F32)
    dsk = d_skip.astype(F32).reshape(SSM_GROUPS, SSM_GROUP)
    kt = kt + (d == 0)[:, :, None, None, None] * (dsk[:, :, None] * eye)[None, None]
    t_mat = kt.transpose(2, 0, 4, 1, 3).reshape(SSM_GROUPS, L * SSM_GROUP, L * SSM_GROUP)
    bm = pw[:L][::-1][:, :, :, None] * b_bar[None]
    bm = bm.transpose(1, 0, 3, 2).reshape(SSM_GROUPS, L * SSM_GROUP, SSM_STATE)
    cm = c_mat[None] * pw[1:][:, :, None, :]
    cm = cm.transpose(1, 3, 0, 2).reshape(SSM_GROUPS, SSM_STATE, L * SSM_GROUP)
    a_chunk = pw[L]
    pad = 128 - SSM_STATE
    pc = lambda m: jnp.pad(m, ((0, 0), (0, 0), (0, pad))).astype(BF16)
    pr = lambda m: jnp.pad(m, ((0, 0), (0, pad), (0, 0))).astype(BF16)
    pa = lambda m: jnp.pad(m, ((0, 0), (0, pad)))
    return (pc(bm.real), pc(bm.imag), t_mat.astype(BF16), pr(cm.real), pr(-cm.imag),
            pa(a_chunk.real), pa(a_chunk.imag))


def _ssm(zb3, mats):
    b, s, w = zb3.shape
    L, G = SSM_CHUNK, SSM_GROUPS
    ts = min(SSM_TILE, s)
    nch = ts // L
    full = lambda a: pl.BlockSpec(a.shape, lambda bi, i: (0,) * a.ndim)
    return pl.pallas_call(
        _ssm_kernel,
        grid=(b, s // ts),
        in_specs=[pl.BlockSpec((None, ts, w), lambda bi, i: (bi, i, 0))] + [full(m) for m in mats],
        out_specs=pl.BlockSpec((None, ts, w), lambda bi, i: (bi, i, 0)),
        out_shape=jax.ShapeDtypeStruct((b, s, w), BF16),
        scratch_shapes=[pltpu.VMEM((w // 128, ts, 128), F32),
                        pltpu.VMEM((L * (w // 128), nch, 128), F32),
                        pltpu.VMEM((G, nch, L * SSM_GROUP), BF16),
                        pltpu.VMEM((nch * G, 128), F32), pltpu.VMEM((nch * G, 128), F32),
                        pltpu.VMEM((nch * G, 128), F32), pltpu.VMEM((nch * G, 128), F32),
                        pltpu.VMEM((G, nch, L * SSM_GROUP), F32),
                        pltpu.VMEM((2 * G, 128), F32)],
        compiler_params=_cparams("parallel", "arbitrary"),
        name="ssm",
    )(zb3, *mats)


def _mlstm_kernel(zc_ref, zgt_ref, convw_ref, gb_ref, ng_ref, triu_ref, bones_ref, o_ref,
                  st_ref, m_ref, tail_ref, qk_ref,
                  rt_ref, bl_ref, rm_ref, cr128_ref, cr64_ref, b64_ref, r64_ref, s_ref, hh_ref):
    ts = zc_ref.shape[0]
    hd, nh, w = HEAD_DIM, MLSTM_HEADS, GROUP_W
    nchunk = ts // CHUNK
    nr = nchunk * 8
    lane1 = lax.broadcasted_iota(I32, (1, 128), 1)
    half = [(lane1 // hd) == (h % 2) for h in range(nh)]
    grp = [slice((h * hd) // 128 * 128, (h * hd) // 128 * 128 + 128) for h in range(nh)]

    @pl.when(pl.program_id(1) == 0)
    def _():
        st_ref[...] = jnp.zeros_like(st_ref)
        m_ref[...] = jnp.zeros_like(m_ref)
        tail_ref[...] = jnp.zeros_like(tail_ref)

    r_i = lax.broadcasted_iota(I32, (CHUNK, CHUNK), 0)
    c_i = lax.broadcasted_iota(I32, (CHUNK, CHUNK), 1)
    causal = c_i <= r_i
    k64 = lax.broadcasted_iota(I32, (8, w), 0)
    j64 = lax.broadcasted_iota(I32, (8, w), 1)
    sel64 = (j64 // hd == k64 % nh).astype(BF16)
    k128 = lax.broadcasted_iota(I32, (8, nh * 128), 0)
    j128 = lax.broadcasted_iota(I32, (8, nh * 128), 1)
    sel128 = (j128 // 128 == k128 % nh).astype(BF16)

    def lanes(t):
        lo = jnp.where(lane1 < hd, t[0:1, :], t[1:2, :])
        hi = jnp.where(lane1 < hd, t[2:3, :], t[3:4, :])
        return jnp.concatenate([lo, hi], axis=1)

    rowm = lax.broadcasted_iota(I32, (nr, 128), 0) % 8
    lanem = lax.broadcasted_iota(I32, (nr, 128), 1)
    graw = jnp.concatenate([zgt_ref[:, c * CHUNK:(c + 1) * CHUNK] + gb_ref[...] for c in range(nchunk)], axis=0)
    g2 = jnp.where(rowm < nh, graw, _log_sigmoid(graw))
    ghi, glo = _split_bf16(g2)
    cum = _dot(ghi, triu_ref[...]) + _dot(glo, triu_ref[...])
    b_t = pltpu.roll(cum, nr - 4, axis=0)
    r_t = g2 - b_t
    cr = r_t
    sh = 1
    while sh < CHUNK:
        cr = jnp.maximum(cr, jnp.where(lanem >= sh, pltpu.roll(cr, sh, axis=1), NEG))
        sh *= 2
    rt_ref[...] = r_t
    bl_ref[...] = jnp.broadcast_to(b_t[:, CHUNK - 1:CHUNK], (nr, 128))
    rm_ref[...] = jnp.broadcast_to(cr[:, CHUNK - 1:CHUNK], (nr, 128))

    x = zc_ref[:, 0:2 * w].astype(F32)
    xe = jnp.concatenate([tail_ref[...], x], axis=0)
    tail_ref[...] = x[ts - 8:, :]
    cw = convw_ref[...]
    acc = x * cw[CONV_K - 1:CONV_K, :]
    for sft in range(1, CONV_K):
        acc = acc + xe[8 - sft:8 - sft + ts, :] * cw[CONV_K - 1 - sft:CONV_K - sft, :]
    lane_qk = lax.broadcasted_iota(I32, (1, 2 * w), 1)
    qk_ref[...] = acc * _sigmoid(acc) * jnp.where(lane_qk < w, 1.0, hd ** -0.5)

    def hi_lo_rows(v):
        hi = v.astype(BF16).astype(F32)
        return jnp.where(rowm < nh, hi, pltpu.roll(v - hi, 4, axis=0))

    a_cr = hi_lo_rows(cr)
    a_b = hi_lo_rows(b_t)
    a_r = hi_lo_rows(r_t)
    for c in range(nchunk):
        rows = slice(c * CHUNK, (c + 1) * CHUNK)
        t8 = slice(c * 8, (c + 1) * 8)
        a_cr_c = a_cr[t8].astype(BF16)
        cr128_ref[rows, :] = _dot_tn(a_cr_c, sel128)
        cr64_ref[rows, :] = _dot_tn(a_cr_c, sel64)
        b64_ref[rows, :] = _dot_tn(a_b[t8].astype(BF16), sel64)
        r64_ref[rows, :] = _dot_tn(a_r[t8].astype(BF16), sel64)
        for h in range(nh):
            qg = qk_ref[rows, grp[h]].astype(BF16)
            kg = qk_ref[rows, w + grp[h].start:w + grp[h].stop].astype(BF16)
            s_ref[rows, h * 128:(h + 1) * 128] = _dot_nt(qg, jnp.where(half[h], kg, jnp.zeros_like(kg)))

    lane_w = lax.broadcasted_iota(I32, (1, w), 1)
    hmask = [(lane_w // hd) == h for h in range(nh)]
    ones_blk = [m.astype(BF16) * jnp.ones((CHUNK, 1), BF16) for m in hmask]
    ones_cols = jnp.ones((CHUNK, w), BF16)
    zblk = jnp.zeros((hd, 128), F32)
    ngrp = w // 128
    m_prev = m_ref[...]
    cblk = [st_ref[h * hd:(h + 1) * hd, grp[h]] for h in range(nh)]
    nblk = [st_ref[h * hd:(h + 1) * hd, w + grp[h].start:w + grp[h].stop] for h in range(nh)]
    for c in range(nchunk):
        rows = slice(c * CHUNK, (c + 1) * CHUNK)
        t8 = slice(c * 8, (c + 1) * 8)
        r_c = rt_ref[t8, :]
        g_last = jnp.maximum(rm_ref[t8, :], m_prev)
        decay = jnp.exp(m_prev - g_last)
        mprev_l = lanes(m_prev)
        glast_l = lanes(g_last)
        gb64 = jnp.maximum(cr64_ref[rows, :], mprev_l)
        w_inter = jnp.exp(mprev_l - gb64)
        e_negm = jnp.exp(-(b64_ref[rows, :] + gb64))
        v_all = zc_ref[rows, 2 * w:3 * w]
        wcat = []
        vblocks = []
        srows = []
        for h in range(nh):
            gb128 = jnp.maximum(cr128_ref[rows, h * 128:(h + 1) * 128], m_prev[h:h + 1, :])
            d = jnp.exp(jnp.where(causal, r_c[h:h + 1, :] - gb128, NEG))
            wcat.append((d * s_ref[rows, h * 128:(h + 1) * 128]).astype(BF16))
            vblocks.append(jnp.concatenate([jnp.where(hmask[h], v_all, jnp.zeros_like(v_all)), ones_blk[h]], axis=1))
            g = (h * hd) // 128
            srows.append(jnp.concatenate([cblk[h] if j == g else zblk for j in range(ngrp)]
                                         + [nblk[h] if j == g else zblk for j in range(ngrp)], axis=1))
        s_bf = jnp.concatenate(srows, axis=0).astype(BF16)
        out_aug = _dot(jnp.concatenate(wcat, axis=1), jnp.concatenate(vblocks, axis=0))
        out_aug = out_aug + _dot((qk_ref[rows, 0:w] * w_inter).astype(BF16), s_bf)
        hh_ref[rows, :] = out_aug[:, 0:w] / jnp.maximum(jnp.abs(out_aug[:, w:2 * w]), e_negm)
        kw = (qk_ref[rows, w:2 * w] * jnp.exp(r64_ref[rows, :] - glast_l)).astype(BF16)
        upd = _dot_tn(kw, jnp.concatenate([v_all, ones_cols], axis=1))
        for h in range(nh):
            rs = slice(h * hd, (h + 1) * hd)
            cblk[h] = cblk[h] * decay[h:h + 1, :] + jnp.where(half[h], upd[rs, grp[h]], 0.0)
            nblk[h] = (nblk[h] * decay[h:h + 1, :]
                       + jnp.where(half[h], upd[rs, w + grp[h].start:w + grp[h].stop], 0.0))
        m_prev = bl_ref[t8, :] + g_last
    m_ref[...] = m_prev
    for h in range(nh):
        st_ref[h * hd:(h + 1) * hd, grp[h]] = cblk[h]
        st_ref[h * hd:(h + 1) * hd, w + grp[h].start:w + grp[h].stop] = nblk[h]

    hh = hh_ref[...]
    ms = _dot((hh * hh).astype(BF16), bones_ref[...])
    og = zc_ref[:, 3 * w:4 * w].astype(F32)
    o_ref[...] = (_sigmoid(og) * hh * lax.rsqrt(ms + EPS) * ng_ref[...]).astype(o_ref.dtype)


def _mlstm(zc3, zgt, conv_w, gate_b_rows, norm_g):
    b, s, w = zc3.shape
    ts = SEQ_TILE
    nt = s // ts
    gw = GROUP_W
    t = jnp.arange(CHUNK)
    triu = (t[:, None] <= t[None, :]).astype(BF16)
    hid = jnp.arange(gw) // HEAD_DIM
    bones = jnp.where(hid[:, None] == hid[None, :], 1.0 / HEAD_DIM, 0.0).astype(BF16)
    full = lambda a: pl.BlockSpec(a.shape, lambda bi, i: (0,) * a.ndim)
    return pl.pallas_call(
        _mlstm_kernel,
        grid=(b, nt),
        in_specs=[pl.BlockSpec((None, ts, w), lambda bi, i: (bi, i, 0)),
                  pl.BlockSpec((8, ts), lambda bi, i: (0, bi * nt + i)),
                  full(conv_w), full(gate_b_rows), full(norm_g), full(triu), full(bones)],
        out_specs=pl.BlockSpec((None, ts, gw), lambda bi, i: (bi, i, 0)),
        out_shape=jax.ShapeDtypeStruct((b, s, gw), BF16),
        scratch_shapes=[pltpu.VMEM((gw, 2 * gw), F32),
                        pltpu.VMEM((8, 128), F32),
                        pltpu.VMEM((8, 2 * gw), F32),
                        pltpu.VMEM((ts, 2 * gw), F32),
                        pltpu.VMEM((ts // CHUNK * 8, 128), F32),
                        pltpu.VMEM((ts // CHUNK * 8, 128), F32),
                        pltpu.VMEM((ts // CHUNK * 8, 128), F32),
                        pltpu.VMEM((ts, MLSTM_HEADS * 128), F32),
                        pltpu.VMEM((ts, gw), F32),
                        pltpu.VMEM((ts, gw), F32),
                        pltpu.VMEM((ts, gw), F32),
                        pltpu.VMEM((ts, MLSTM_HEADS * 128), F32),
                        pltpu.VMEM((ts, gw), F32)],
        compiler_params=_cparams("parallel", "arbitrary"),
        name="mlstm",
    )(zc3, zgt, conv_w, gate_b_rows, norm_g, triu, bones)


def _sgu_kernel(zd_ref, lng_ref, lnb_ref, w_ref, bias_ref, o_ref):
    tm = zd_ref.shape[0]
    gw = GROUP_W // SGU_GROUPS
    u = _gelu(zd_ref[:, 0:GROUP_W].astype(F32))
    v = _gelu(zd_ref[:, GROUP_W:2 * GROUP_W].astype(F32))
    mu = jnp.mean(v, axis=-1, keepdims=True)
    vc = v - mu
    var = jnp.mean(vc * vc, axis=-1, keepdims=True)
    vn = (vc * lax.rsqrt(var + EPS) * lng_ref[...] + lnb_ref[...]).astype(BF16)
    for c in range(tm // CHUNK):
        rows = slice(c * CHUNK, (c + 1) * CHUNK)
        mixed = jnp.concatenate([_dot(w_ref[g], vn[rows, g * gw:(g + 1) * gw]) for g in range(SGU_GROUPS)], axis=-1)
        o_ref[rows, :] = (u[rows, :] * (mixed + bias_ref[...])).astype(o_ref.dtype)


def _sgu(zd, ln_g, ln_b, w_tril, bias):
    n = zd.shape[0]
    tm = TOKEN_TILE
    full = lambda a: pl.BlockSpec(a.shape, lambda i: (0,) * a.ndim)
    return pl.pallas_call(
        _sgu_kernel,
        grid=(n // tm,),
        in_specs=[pl.BlockSpec((tm, 2 * GROUP_W), lambda i: (i, 0)), full(ln_g), full(ln_b), full(w_tril), full(bias)],
        out_specs=pl.BlockSpec((tm, GROUP_W), lambda i: (i, 0)),
        out_shape=jax.ShapeDtypeStruct((n, GROUP_W), BF16),
        compiler_params=_cparams("parallel"),
        name="sgu",
    )(zd, ln_g, ln_b, w_tril, bias)


def _out_proj_kernel(x_ref, ya_ref, yb_ref, yc_ref, yd_ref, gluw_ref, glub_ref, wo_ref, ng_ref,
                     rwh_ref, rwl_ref, rb_ref, triu_ref,
                     x1_ref, hx_ref, meta_ref, cnt_ref, carry_ref):
    tm = x_ref.shape[0]

    @pl.when(pl.program_id(0) == 0)
    def _():
        carry_ref[...] = jnp.zeros_like(carry_ref)

    yb = _gelu(yb_ref[...].astype(F32))
    yb = yb * _sigmoid(_dot(yb.astype(BF16), gluw_ref[...]) + glub_ref[...])
    mix = _dot(ya_ref[...], wo_ref[0:GROUP_W, :])
    mix = mix + _dot(yb.astype(BF16), wo_ref[GROUP_W:2 * GROUP_W, :])
    mix = mix + _dot(yc_ref[...], wo_ref[2 * GROUP_W:3 * GROUP_W, :])
    mix = mix + _dot(yd_ref[...], wo_ref[3 * GROUP_W:4 * GROUP_W, :])
    x1 = x_ref[...] + mix
    x1_ref[...] = x1
    h = _rms(x1, ng_ref[...])
    hx_ref[:, 0:PACK_W] = _pack_rows(h)

    hh, hl = _split_bf16(h)
    wh = rwh_ref[...]
    wl = rwl_ref[...]
    logits = _dot_nt(wh, hh) + _dot_nt(wh, hl) + _dot_nt(wl, hh) + rb_ref[...]
    gl = [logits[j:j + 1, :] for j in range(N_GROUPS)]
    gmax = functools.reduce(jnp.maximum, gl)
    gsel = jnp.full((1, tm), N_GROUPS - 1, I32)
    for j in range(N_GROUPS - 2, -1, -1):
        gsel = jnp.where(gl[j] == gmax, j, gsel)
    p_g = 1.0 / functools.reduce(jnp.add, [jnp.exp(v - gmax) for v in gl])
    e_in = []
    for i in range(EPG):
        v = logits[N_GROUPS + i:N_GROUPS + i + 1, :]
        for j in range(1, N_GROUPS):
            r = N_GROUPS + j * EPG + i
            v = jnp.where(gsel == j, logits[r:r + 1, :], v)
        e_in.append(v)
    v1 = functools.reduce(jnp.maximum, e_in)
    i1 = jnp.full((1, tm), EPG - 1, I32)
    for i in range(EPG - 2, -1, -1):
        i1 = jnp.where(e_in[i] == v1, i, i1)
    rest = [jnp.where(i1 == i, NEG, e_in[i]) for i in range(EPG)]
    v2 = functools.reduce(jnp.maximum, rest)
    i2 = jnp.full((1, tm), EPG - 1, I32)
    for i in range(EPG - 2, -1, -1):
        i2 = jnp.where((rest[i] == v2) & (i1 != i), i, i2)
    e2 = jnp.exp(v2 - v1)
    w1 = p_g / (1.0 + e2)
    w2 = p_g * e2 / (1.0 + e2)
    lo = jnp.minimum(i1, i2)
    hi = jnp.maximum(i1, i2)
    w_lo = jnp.where(i1 < i2, w1, w2)
    w_hi = jnp.where(i1 < i2, w2, w1)
    bucket = gsel * PAIRS + ((lo * (2 * EPG - 1 - lo)) >> 1) + (hi - lo - 1)

    kid = lax.broadcasted_iota(I32, (BUCKET_PAD, tm), 0)
    onehot = (kid == bucket).astype(F32)
    prefix = _dot(onehot.astype(BF16), triu_ref[...])
    carry = carry_ref[...]
    rank = jnp.sum(onehot * (prefix - 1.0 + carry[:, 0:1]), axis=0, keepdims=True)
    carry = carry + prefix[:, tm - 1:tm]
    carry_ref[...] = carry
    cnt_ref[...] = carry
    meta_ref[...] = jnp.concatenate([bucket, rank.astype(I32), jnp.zeros((6, tm), I32)], axis=0)
    wrows = jnp.concatenate([w_lo, w_hi, jnp.zeros((126, tm), F32)], axis=0)
    for c in range(tm // 128):
        hx_ref[c * 128:(c + 1) * 128, PACK_W:ROW_W] = pltpu.bitcast(wrows[:, c * 128:(c + 1) * 128].T, U32)


def _out_proj(x2, ya, yb, yc, yd, glu_w, glu_b, w_out, norm_g, rw_hi, rw_lo, rb):
    n = x2.shape[0]
    tm = TOKEN_TILE
    row = lambda w: pl.BlockSpec((tm, w), lambda i: (i, 0))
    full = lambda a: pl.BlockSpec(a.shape, lambda i: (0,) * a.ndim)
    t = jnp.arange(tm)
    triu = (t[:, None] <= t[None, :]).astype(BF16)
    return pl.pallas_call(
        _out_proj_kernel,
        grid=(n // tm,),
        in_specs=[row(D_MODEL), row(GROUP_W), row(GROUP_W), row(GROUP_W), row(GROUP_W),
                  full(glu_w), full(glu_b), full(w_out), full(norm_g), full(rw_hi), full(rw_lo), full(rb),
                  full(triu)],
        out_specs=[row(D_MODEL), row(ROW_W), pl.BlockSpec((8, tm), lambda i: (0, i)),
                   pl.BlockSpec((BUCKET_PAD, 128), lambda i: (0, 0))],
        out_shape=[jax.ShapeDtypeStruct((n, D_MODEL), F32), jax.ShapeDtypeStruct((n, ROW_W), U32),
                   jax.ShapeDtypeStruct((8, n), I32), jax.ShapeDtypeStruct((BUCKET_PAD, 128), F32)],
        scratch_shapes=[pltpu.VMEM((BUCKET_PAD, 128), F32)],
        compiler_params=_cparams("arbitrary"),
        name="out_proj",
    )(x2, ya, yb, yc, yd, glu_w, glu_b, w_out, norm_g, rw_hi, rw_lo, rb, triu)


def _dispatch_kernel(dest_ref, hx_ref, xs_in_ref, xs_ref, sem):
    del xs_in_ref
    nq = hx_ref.shape[0]

    def start(q, c):
        for r in range(SUBLANES):
            d = dest_ref[0, 0, q * SUBLANES + r]
            pltpu.make_async_copy(hx_ref.at[q, pl.ds(r, 1)], xs_ref.at[pl.ds(d, 1)], sem).start()
        return c

    lax.fori_loop(0, nq, start, 0)
    pltpu.make_async_copy(hx_ref, hx_ref, sem).wait()


def _dispatch(dest3, hx, p_tot):
    n = hx.shape[0]
    tm = MOE_TILE
    xs0 = jnp.zeros((p_tot, ROW_W), U32)
    hx = hx.reshape(n // SUBLANES, SUBLANES, ROW_W)
    return pl.pallas_call(
        _dispatch_kernel,
        grid=(n // tm,),
        in_specs=[pl.BlockSpec((1, 1, tm), lambda i: (i, 0, 0), memory_space=pltpu.SMEM),
                  pl.BlockSpec((tm // SUBLANES, SUBLANES, ROW_W), lambda i: (i, 0, 0)),
                  pl.BlockSpec(memory_space=pl.ANY)],
        out_specs=pl.BlockSpec(memory_space=pl.ANY),
        out_shape=jax.ShapeDtypeStruct((p_tot, ROW_W), U32),
        input_output_aliases={2: 0},
        scratch_shapes=[pltpu.SemaphoreType.DMA],
        compiler_params=_cparams("arbitrary"),
        name="dispatch",
    )(dest3, hx, xs0)


def _ffn_kernel(valid_ref, xblk_ref, run_ref, exprun_ref, nrun_ref, xs_ref, wg_hbm, wu_hbm, wd_hbm, ys_ref,
                stg_g, stg_u, stg_d, wgu_a, wdn_a, wgu_b, wdn_b, sem, *, layer):
    del xblk_ref
    j = pl.program_id(0)
    valid = valid_ref[j]
    jp = jnp.maximum(j - 1, 0)

    def expert_copies(s, r, slot):
        e = exprun_ref[s, r]
        return [pltpu.make_async_copy(src.at[layer, e], dst.at[s, slot], sem.at[s, slot])
                for src, dst in ((wg_hbm, stg_g), (wu_hbm, stg_u), (wd_hbm, stg_d))]

    for s, (wgu, wdn) in enumerate(((wgu_a, wdn_a), (wgu_b, wdn_b))):
        r = run_ref[s, j]

        @pl.when((j == 0) | (r != run_ref[s, jp]))
        def _():
            slot = r % 2

            @pl.when(j == 0)
            def _():
                for c in expert_copies(s, r, slot):
                    c.start()

            for c in expert_copies(s, r, slot):
                c.wait()

            @pl.when(r + 1 < nrun_ref[s])
            def _():
                for c in expert_copies(s, r + 1, 1 - slot):
                    c.start()

            wgu[:, 0:D_EXPERT] = stg_g[s, slot].astype(BF16)
            wgu[:, D_EXPERT:2 * D_EXPERT] = stg_u[s, slot].astype(BF16)
            wdn[...] = stg_d[s, slot].astype(BF16)

    def ffn_rows(rows):
        x_lo, x_hi = _unpack_rows(xs_ref[rows, 0:PACK_W])
        xb = jnp.concatenate([x_lo.astype(BF16), x_hi.astype(BF16)], axis=1)
        wts = pltpu.bitcast(xs_ref[rows, PACK_W:ROW_W], F32)

        def expert(wgu, wdn):
            gu = _dot(xb, wgu[...])
            g = gu[:, 0:D_EXPERT]
            act = (g * _sigmoid(g) * gu[:, D_EXPERT:2 * D_EXPERT]).astype(BF16)
            return _dot(act, wdn[...])

        ys_ref[rows, :] = _pack_rows(expert(wgu_a, wdn_a) * wts[:, 0:1] + expert(wgu_b, wdn_b) * wts[:, 1:2])

    half = FFN_SUB // 2
    for sb in range(FFN_BLOCK // FFN_SUB):
        r0 = sb * FFN_SUB
        left = valid - r0

        @pl.when(left <= 0)
        def _():
            ys_ref[r0:r0 + FFN_SUB, :] = jnp.zeros((FFN_SUB, PACK_W), U32)

        @pl.when((left > 0) & (left <= half))
        def _():
            ffn_rows(slice(r0, r0 + half))
            ys_ref[r0 + half:r0 + FFN_SUB, :] = jnp.zeros((half, PACK_W), U32)

        @pl.when(left > half)
        def _():
            ffn_rows(slice(r0, r0 + FFN_SUB))


def _ffn(valid, xblk, run, exprun, nrun, xs, layer, w_gate, w_up, w_down):
    p_tot = xs.shape[0]
    bm = FFN_BLOCK
    nblk = p_tot // bm
    up_shape = (D_MODEL, D_EXPERT)
    dn_shape = (D_EXPERT, D_MODEL)
    hbm = pl.BlockSpec(memory_space=pl.ANY)
    return pl.pallas_call(
        functools.partial(_ffn_kernel, layer=layer),
        grid_spec=pltpu.PrefetchScalarGridSpec(
            num_scalar_prefetch=5,
            grid=(nblk,),
            in_specs=[pl.BlockSpec((bm, ROW_W), lambda j, va, xb, ru, er, nr: (xb[j], 0)), hbm, hbm, hbm],
            out_specs=pl.BlockSpec((bm, PACK_W), lambda j, va, xb, ru, er, nr: (j, 0)),
            scratch_shapes=[pltpu.VMEM((2, 2) + up_shape, F32), pltpu.VMEM((2, 2) + up_shape, F32),
                            pltpu.VMEM((2, 2) + dn_shape, F32),
                            pltpu.VMEM((D_MODEL, 2 * D_EXPERT), BF16), pltpu.VMEM(dn_shape, BF16),
                            pltpu.VMEM((D_MODEL, 2 * D_EXPERT), BF16), pltpu.VMEM(dn_shape, BF16),
                            pltpu.SemaphoreType.DMA((2, 2))],
        ),
        out_shape=jax.ShapeDtypeStruct((p_tot, PACK_W), U32),
        compiler_params=_cparams("arbitrary"),
        name="ffn",
    )(valid, xblk, run, exprun, nrun, xs, w_gate, w_up, w_down)


def _combine_kernel(dest_ref, dnext_ref, x1_ref, ys_ref, ng_ref, o_ref, buf_ref, sem, *, final_norm):
    i = pl.program_id(0)
    tm = x1_ref.shape[0]
    nq = tm // SUBLANES
    slot = i % 2

    def gather(d_ref, sl):
        def start(q, c):
            for r in range(SUBLANES):
                d = d_ref[0, 0, q * SUBLANES + r]
                pltpu.make_async_copy(ys_ref.at[pl.ds(d, 1)], buf_ref.at[sl, q, pl.ds(r, 1)], sem.at[sl]).start()
            return c

        lax.fori_loop(0, nq, start, 0)

    @pl.when(i == 0)
    def _():
        gather(dest_ref, 0)

    pltpu.make_async_copy(buf_ref.at[slot], buf_ref.at[slot], sem.at[slot]).wait()

    @pl.when(i + 1 < pl.num_programs(0))
    def _():
        gather(dnext_ref, 1 - slot)

    y_lo, y_hi = _unpack_rows(buf_ref[slot].reshape(tm, PACK_W))
    x2 = x1_ref[...] + jnp.concatenate([y_lo, y_hi], axis=1)
    o_ref[...] = _rms(x2, ng_ref[...]) if final_norm else x2


def _combine(dest3, x1, ys, norm_g, final_norm):
    n = x1.shape[0]
    tm = MOE_TILE
    last = n // tm - 1
    return pl.pallas_call(
        functools.partial(_combine_kernel, final_norm=final_norm),
        grid=(n // tm,),
        in_specs=[pl.BlockSpec((1, 1, tm), lambda i: (i, 0, 0), memory_space=pltpu.SMEM),
                  pl.BlockSpec((1, 1, tm), lambda i: (jnp.minimum(i + 1, last), 0, 0), memory_space=pltpu.SMEM),
                  pl.BlockSpec((tm, D_MODEL), lambda i: (i, 0)),
                  pl.BlockSpec(memory_space=pl.ANY),
                  pl.BlockSpec((1, D_MODEL), lambda i: (0, 0))],
        out_specs=pl.BlockSpec((tm, D_MODEL), lambda i: (i, 0)),
        out_shape=jax.ShapeDtypeStruct((n, D_MODEL), F32),
        scratch_shapes=[pltpu.VMEM((2, tm // SUBLANES, SUBLANES, PACK_W), U32), pltpu.SemaphoreType.DMA((2,))],
        compiler_params=_cparams("arbitrary"),
        name="combine",
    )(dest3, dest3, x1, ys, norm_g)


def _routing_tables(meta, counts, n):
    bm = FFN_BLOCK
    nblk = n // bm + N_BUCKETS
    bucket = meta[0]
    rank = meta[1]
    cnt = counts[:N_BUCKETS, 0].astype(I32)
    padded = ((cnt + bm - 1) // bm) * bm
    pad_end = jnp.cumsum(padded)
    pad_start = pad_end - padded
    onehot = bucket[:, None] == jnp.arange(N_BUCKETS, dtype=I32)[None, :]
    dest = rank + jnp.sum(jnp.where(onehot, pad_start[None, :], 0), axis=1)
    nact = (pad_end[-1] // bm).astype(I32)
    j = jnp.arange(nblk, dtype=I32)
    blk = jnp.minimum(j, jnp.maximum(nact - 1, 0))
    pos = blk * bm
    bkt = jnp.minimum(jnp.sum((pad_end[None, :] <= pos[:, None]).astype(I32), axis=1), N_BUCKETS - 1)
    in_bkt = bkt[:, None] == jnp.arange(N_BUCKETS, dtype=I32)[None, :]
    sel = lambda tab: jnp.sum(jnp.where(in_bkt, tab[None, :], 0), axis=1)
    valid = jnp.where(j < nact, jnp.clip(sel(cnt) - (pos - sel(pad_start)), 0, bm), 0)
    lo_tab, hi_tab = [], []
    for g in range(N_GROUPS):
        for a in range(EPG):
            for b in range(a + 1, EPG):
                lo_tab.append(g * EPG + a)
                hi_tab.append(g * EPG + b)
    experts = jnp.stack([sel(jnp.asarray(lo_tab, I32)), sel(jnp.asarray(hi_tab, I32))]).astype(I32)
    change = jnp.concatenate([jnp.zeros((2, 1), I32), (experts[:, 1:] != experts[:, :-1]).astype(I32)], axis=1)
    run = jnp.cumsum(change, axis=1).astype(I32)
    is_run = run[:, :, None] == j[None, None, :]
    exprun = jnp.max(jnp.where(is_run, experts[:, :, None], 0), axis=1).astype(I32)
    nrun = run[:, -1] + 1
    return dest.astype(I32), valid.astype(I32), blk.astype(I32), run, exprun, nrun.astype(I32), nblk * bm


def _layer(x2, b, s, p, final_g):
    n = b * s
    za, zb, zc, zd, zgt = _in_proj(x2, p['norm_mix_g'], p['wa'], p['wb'], p['wc'], p['wd'], p['wgt'])
    ya = _attention(za.reshape(b, s, -1), p['sinks']).reshape(n, GROUP_W)
    yb = _ssm(zb.reshape(b, s, GROUP_W), p['ssm']).reshape(n, GROUP_W)
    yc = _mlstm(zc.reshape(b, s, -1), zgt, p['conv_w'], p['gate_b'], p['mlstm_norm_g']).reshape(n, GROUP_W)
    yd = _sgu(zd, p['sgu_ln_g'], p['sgu_ln_b'], p['sgu_w'], p['sgu_bias'])
    x1, hx, meta, counts = _out_proj(x2, ya, yb, yc, yd, p['glu_w'], p['glu_b'], p['w_out'], p['norm_ffn_g'],
                                     p['rw_hi'], p['rw_lo'], p['rb'])
    dest, valid, xblk, run, exprun, nrun, p_tot = _routing_tables(meta, counts, n)
    dest3 = dest.reshape(n // MOE_TILE, 1, MOE_TILE)
    xs = _dispatch(dest3, hx, p_tot)
    ys = _ffn(valid, xblk, run, exprun, nrun, xs, p['layer'], p['w_gate'], p['w_up'], p['w_down'])
    if final_g is None:
        return _combine(dest3, x1, ys, p['norm_ffn_g'], False)
    return _combine(dest3, x1, ys, final_g, True)


def _prep_layer(l, norm_mix_g, w_in, attn_sinks, ssm_a_re, ssm_a_im, ssm_b_re, ssm_b_im, ssm_c_re, ssm_c_im,
                ssm_d, ssm_log_dt, ssm_glu_w, ssm_glu_b, mlstm_conv_w, mlstm_gate_b, mlstm_norm_g,
                sgu_ln_g, sgu_ln_b, sgu_w, sgu_b, w_out, norm_ffn_g, router_group_w, router_group_b,
                router_expert_w, router_expert_b, expert_w_gate, expert_w_up, expert_w_down):
    w = w_in[l]
    o_su = 2 * GROUP_W
    o_c = o_su + GROUP_W
    o_ci = o_c + 3 * GROUP_W
    o_co = o_ci + 2 * MLSTM_HEADS
    o_d = o_co + GROUP_W
    wc = jnp.concatenate([w[:, o_c:o_ci], w[:, o_co:o_d]], axis=1)
    wgt = w[:, o_ci:o_co].T
    head_cols = lambda m, hq: m[..., hq * HEAD_DIM:(hq + 1) * HEAD_DIM]
    order = [g * ATTN_REP + r for r in range(ATTN_REP) for g in range(KV_HEADS)]
    wa = jnp.concatenate([head_cols(w, hq) * (HEAD_DIM ** -0.5) for hq in order] + [w[:, GROUP_W:o_su]], axis=1)
    wo = jnp.concatenate([w_out[l][hq * HEAD_DIM:(hq + 1) * HEAD_DIM] for hq in order] + [w_out[l][GROUP_W:]], axis=0)
    rw = jnp.concatenate([router_group_w[l], router_expert_w[l]], axis=1).T.astype(F32)
    rw = jnp.pad(rw, ((0, 4), (0, 0)))
    rw_hi = rw.astype(BF16)
    rw_lo = (rw - rw_hi.astype(F32)).astype(BF16)
    rb = jnp.pad(jnp.concatenate([router_group_b[l], router_expert_b[l]]).astype(F32), (0, 4))[:, None]
    tril = jnp.tril(jnp.ones((CHUNK, CHUNK), F32))
    gw = GROUP_W // SGU_GROUPS
    return dict(
        norm_mix_g=norm_mix_g[l][None, :].astype(F32),
        wa=wa.astype(BF16), wb=w[:, o_su:o_c].astype(BF16), wc=wc.astype(BF16),
        wd=w[:, o_d:].astype(BF16), wgt=wgt.astype(BF16),
        sinks=attn_sinks[l].astype(F32),
        ssm=_ssm_matrices(ssm_a_re[l], ssm_a_im[l], ssm_b_re[l], ssm_b_im[l], ssm_c_re[l], ssm_c_im[l],
                          ssm_d[l], ssm_log_dt[l]),
        glu_w=ssm_glu_w[l].astype(BF16), glu_b=ssm_glu_b[l][None, :].astype(F32),
        conv_w=mlstm_conv_w[l].astype(F32),
        gate_b=jnp.broadcast_to(mlstm_gate_b[l].astype(F32)[:, None], (2 * MLSTM_HEADS, 128)),
        mlstm_norm_g=mlstm_norm_g[l][None, :].astype(F32),
        sgu_ln_g=sgu_ln_g[l][None, :].astype(F32), sgu_ln_b=sgu_ln_b[l][None, :].astype(F32),
        sgu_w=(sgu_w[l].astype(F32) * tril).astype(BF16),
        sgu_bias=jnp.repeat(sgu_b[l].astype(F32).T, gw, axis=1),
        w_out=wo.astype(BF16), norm_ffn_g=norm_ffn_g[l][None, :].astype(F32),
        rw_hi=rw_hi, rw_lo=rw_lo, rb=rb,
        layer=l, w_gate=expert_w_gate, w_up=expert_w_up, w_down=expert_w_down,
    )


def kernel(x, norm_mix_g, w_in, attn_sinks, ssm_a_re, ssm_a_im, ssm_b_re, ssm_b_im, ssm_c_re, ssm_c_im, ssm_d, ssm_log_dt, ssm_glu_w, ssm_glu_b, mlstm_conv_w, mlstm_gate_b, mlstm_norm_g, sgu_ln_g, sgu_ln_b, sgu_w, sgu_b, w_out, norm_ffn_g, router_group_w, router_group_b, router_expert_w, router_expert_b, expert_w_gate, expert_w_up, expert_w_down, norm_final_g):
    b, s, d = x.shape
    depth = w_in.shape[0]
    x2 = x.reshape(b * s, d).astype(F32)
    for l in range(depth):
        p = _prep_layer(l, norm_mix_g, w_in, attn_sinks, ssm_a_re, ssm_a_im, ssm_b_re, ssm_b_im, ssm_c_re,
                        ssm_c_im, ssm_d, ssm_log_dt, ssm_glu_w, ssm_glu_b, mlstm_conv_w, mlstm_gate_b,
                        mlstm_norm_g, sgu_ln_g, sgu_ln_b, sgu_w, sgu_b, w_out, norm_ffn_g, router_group_w,
                        router_group_b, router_expert_w, router_expert_b, expert_w_gate, expert_w_up,
                        expert_w_down)
        final_g = norm_final_g[None, :].astype(F32) if l == depth - 1 else None
        x2 = _layer(x2, b, s, p, final_g)
    return x2.reshape(b, s, d).astype(x.dtype)
```

```python
import functools
import math

import jax
import jax.numpy as jnp
from jax import lax
from jax.experimental import pallas as pl
from jax.experimental.pallas import tpu as pltpu

F32 = jnp.float32
BF16 = jnp.bfloat16
I32 = jnp.int32
U32 = jnp.uint32

D_MODEL = 1024
GROUP_W = 256
HEAD_DIM = 64
EPS = 1e-6
NEG = -1e30
WINDOW = 128
KV_HEADS = 2
ATTN_REP = 2
SSM_GROUP = 16
SSM_GROUPS = 16
SSM_STATE = 64
SSM_CHUNK = 16
MLSTM_HEADS = 4
CHUNK = 128
CONV_K = 4
SGU_GROUPS = 4
N_GROUPS = 4
EPG = 8
N_EXPERTS = 32
D_EXPERT = 512
PAIRS = EPG * (EPG - 1) // 2
N_BUCKETS = N_GROUPS * PAIRS
BUCKET_PAD = 128
PACK_W = D_MODEL // 2
ROW_W = PACK_W + 128

TOKEN_TILE = 512
MOE_TILE = 2048
SEQ_TILE = 512
SSM_TILE = 4096
SSM_SLAB = 64
FFN_BLOCK = 512
FFN_SUB = 256
SUBLANES = 8
VMEM_LIMIT = 56 * 1024 * 1024


def _cparams(*sem):
    return pltpu.CompilerParams(dimension_semantics=sem, vmem_limit_bytes=VMEM_LIMIT)


def _rms(x, g):
    return x * lax.rsqrt(jnp.mean(x * x, axis=-1, keepdims=True) + EPS) * g


def _gelu(x):
    return 0.5 * x * (1.0 + jnp.tanh(math.sqrt(2.0 / math.pi) * (x + 0.044715 * (x * x * x))))


def _sigmoid(x):
    return 1.0 / (1.0 + jnp.exp(-x))


def _log_sigmoid(x):
    return jnp.minimum(x, 0.0) - jnp.log(1.0 + jnp.exp(-jnp.abs(x)))


def _dot(a, b):
    return jnp.dot(a, b, preferred_element_type=F32)


def _dot_nt(a, b):
    return lax.dot_general(a, b, (((1,), (1,)), ((), ())), preferred_element_type=F32)


def _dot_tn(a, b):
    return lax.dot_general(a, b, (((0,), (0,)), ((), ())), preferred_element_type=F32)


def _split_bf16(x):
    hi = x.astype(BF16)
    lo = (x - hi.astype(F32)).astype(BF16)
    return hi, lo


def _pack_rows(x):
    w = x.shape[1] // 2
    bits = lambda v: pltpu.bitcast(v.astype(BF16).astype(F32), U32)
    return (bits(x[:, 0:w]) >> 16) | (bits(x[:, w:2 * w]) & jnp.uint32(0xFFFF0000))


def _unpack_rows(p):
    return pltpu.bitcast(p << 16, F32), pltpu.bitcast(p & jnp.uint32(0xFFFF0000), F32)


def _in_proj_kernel(x_ref, g_ref, wa_ref, wb_ref, wc_ref, wd_ref, wgt_ref,
                    za_ref, zb_ref, zc_ref, zd_ref, zgt_ref):
    hb = _rms(x_ref[...], g_ref[...]).astype(BF16)
    za_ref[...] = _dot(hb, wa_ref[...]).astype(BF16)
    zb_ref[...] = _dot(hb, wb_ref[...]).astype(BF16)
    zc_ref[...] = _dot(hb, wc_ref[...]).astype(BF16)
    zd_ref[...] = _dot(hb, wd_ref[...]).astype(BF16)
    zgt_ref[...] = _dot_nt(wgt_ref[...], hb)


def _in_proj(x2, g, wa, wb, wc, wd, wgt):
    n = x2.shape[0]
    tm = TOKEN_TILE
    row = lambda w: pl.BlockSpec((tm, w), lambda i: (i, 0))
    full = lambda a: pl.BlockSpec(a.shape, lambda i: (0,) * a.ndim)
    widths = (wa.shape[1], wb.shape[1], wc.shape[1], wd.shape[1])
    return pl.pallas_call(
        _in_proj_kernel,
        grid=(n // tm,),
        in_specs=[row(D_MODEL), full(g), full(wa), full(wb), full(wc), full(wd), full(wgt)],
        out_specs=[row(w) for w in widths] + [pl.BlockSpec((wgt.shape[0], tm), lambda i: (0, i))],
        out_shape=[jax.ShapeDtypeStruct((n, w), BF16) for w in widths]
        + [jax.ShapeDtypeStruct((wgt.shape[0], n), F32)],
        compiler_params=_cparams("parallel"),
        name="in_proj",
    )(x2, g, wa, wb, wc, wd, wgt)


def _attn_kernel(sink_ref, cur_ref, prev_ref, o_ref):
    first = pl.program_id(1) == 0
    nblk = cur_ref.shape[0] // WINDOW
    row = lax.broadcasted_iota(I32, (WINDOW, 2 * WINDOW), 0)
    col = lax.broadcasted_iota(I32, (WINDOW, 2 * WINDOW), 1)
    band = (col <= row + WINDOW) & (col > row)
    lane = lax.broadcasted_iota(I32, (1, 128), 1)
    kv_lanes = [(lane // HEAD_DIM) == g for g in range(KV_HEADS)]
    ones = jnp.ones((2 * WINDOW, 128), BF16)
    ko = 2 * 128
    vo = 3 * 128
    for j in range(nblk):
        cur = cur_ref[j * WINDOW:(j + 1) * WINDOW, :]
        if j == 0:
            prev = prev_ref[...]
            mask = band & ((col >= WINDOW) | jnp.logical_not(first))
        else:
            prev = cur_ref[(j - 1) * WINDOW:j * WINDOW, :]
            mask = band
        kk = jnp.concatenate([prev[:, ko:ko + 128], cur[:, ko:ko + 128]], axis=0)
        vaug = jnp.concatenate([jnp.concatenate([prev[:, vo:vo + 128], cur[:, vo:vo + 128]], axis=0), ones], axis=1)
        outs = []
        for r in range(ATTN_REP):
            qg = cur[:, r * 128:(r + 1) * 128]
            og = []
            for g in range(KV_HEADS):
                s = _dot_nt(jnp.where(kv_lanes[g], qg, jnp.zeros_like(qg)), kk)
                s = jnp.where(mask, s, NEG)
                sink = sink_ref[g * ATTN_REP + r]
                m = jnp.maximum(jnp.max(s, axis=-1, keepdims=True), sink)
                oa = _dot(jnp.exp(s - m).astype(BF16), vaug)
                og.append(oa[:, 0:128] / (oa[:, 128:256] + jnp.exp(sink - m)))
            outs.append(jnp.where(kv_lanes[0], og[0], og[1]))
        o_ref[j * WINDOW:(j + 1) * WINDOW, :] = jnp.concatenate(outs, axis=-1).astype(o_ref.dtype)


def _attention(za3, sinks):
    b, s, w = za3.shape
    ts = SEQ_TILE
    per = ts // WINDOW
    return pl.pallas_call(
        _attn_kernel,
        grid=(b, s // ts),
        in_specs=[pl.BlockSpec(memory_space=pltpu.SMEM),
                  pl.BlockSpec((None, ts, w), lambda bi, i: (bi, i, 0)),
                  pl.BlockSpec((None, WINDOW, w), lambda bi, i: (bi, jnp.maximum(i * per - 1, 0), 0))],
        out_specs=pl.BlockSpec((None, ts, GROUP_W), lambda bi, i: (bi, i, 0)),
        out_shape=jax.ShapeDtypeStruct((b, s, GROUP_W), BF16),
        compiler_params=_cparams("parallel", "parallel"),
        name="attn",
    )(sinks, za3, za3)


def _ssm_kernel(zb_ref, bre_ref, bim_ref, t_ref, cre_ref, cim_ref, are_ref, aim_ref, o_ref,
                x_ref, xs_ref, u_ref, vre_ref, vim_ref, sre_ref, sim_ref, y_ref, st_ref):
    ts = zb_ref.shape[0]
    L, G, H = SSM_CHUNK, SSM_GROUPS, SSM_GROUP
    nch = ts // L
    half_g = 128 // H
    n_half = G // half_g

    @pl.when(pl.program_id(1) == 0)
    def _():
        st_ref[...] = jnp.zeros_like(st_ref)

    blk = lax.broadcasted_iota(I32, (1, 128), 1) // H

    def block_transpose(arrs):
        a = list(arrs)
        s = half_g // 2
        while s >= 1:
            keep = (blk & s) == 0
            for i in range(half_g):
                if i & s == 0:
                    ai, aj = a[i], a[i + s]
                    a[i] = jnp.where(keep, ai, pltpu.roll(aj, s * H, axis=1))
                    a[i + s] = jnp.where(keep, pltpu.roll(ai, 128 - s * H, axis=1), aj)
            s //= 2
        return a

    for hf in range(n_half):
        x_ref[hf] = zb_ref[:, hf * 128:(hf + 1) * 128].astype(F32)
    for sg in range(L):
        for hf in range(n_half):
            xs_ref[sg * n_half + hf] = x_ref[hf, pl.ds(sg, nch, stride=L), :]
    for hf in range(n_half):
        for oc in range(L // half_g):
            for r0 in range(0, nch, SSM_SLAB):
                rows = slice(r0, r0 + SSM_SLAB)
                t = block_transpose([xs_ref[(oc * half_g + k) * n_half + hf, rows, :] for k in range(half_g)])
                for gl in range(half_g):
                    u_ref[hf * half_g + gl, rows, oc * 128:(oc + 1) * 128] = t[gl].astype(BF16)
    for g in range(G):
        ug = u_ref[g]
        vre_ref[pl.ds(g, nch, stride=G), :] = _dot(ug, bre_ref[g])
        vim_ref[pl.ds(g, nch, stride=G), :] = _dot(ug, bim_ref[g])

    are = are_ref[...]
    aim = aim_ref[...]

    def step(c, carry):
        sre, sim = carry
        r0 = pl.multiple_of(c * G, G)
        sre_ref[pl.ds(r0, G), :] = sre
        sim_ref[pl.ds(r0, G), :] = sim
        vre = vre_ref[pl.ds(r0, G), :]
        vim = vim_ref[pl.ds(r0, G), :]
        return (are * sre - aim * sim + vre, are * sim + aim * sre + vim)

    sre, sim = lax.fori_loop(0, nch, step, (st_ref[0:G, :], st_ref[G:2 * G, :]), unroll=4)
    st_ref[0:G, :] = sre
    st_ref[G:2 * G, :] = sim

    for g in range(G):
        y = _dot(u_ref[g], t_ref[g])
        y = y + _dot(sre_ref[pl.ds(g, nch, stride=G), :].astype(BF16), cre_ref[g])
        y = y + _dot(sim_ref[pl.ds(g, nch, stride=G), :].astype(BF16), cim_ref[g])
        y_ref[g] = y
    for hf in range(n_half):
        for oc in range(L // half_g):
            for r0 in range(0, nch, SSM_SLAB):
                t = block_transpose([y_ref[hf * half_g + gl, r0:r0 + SSM_SLAB, oc * 128:(oc + 1) * 128]
                                     for gl in range(half_g)])
                for k in range(half_g):
                    x_ref[hf, pl.ds(oc * half_g + k + L * r0, SSM_SLAB, stride=L), :] = t[k]
    for hf in range(n_half):
        o_ref[:, hf * 128:(hf + 1) * 128] = x_ref[hf].astype(o_ref.dtype)


def _ssm_matrices(a_re, a_im, b_re, b_im, c_re, c_im, d_skip, log_dt):
    L = SSM_CHUNK
    a = lax.complex(a_re.astype(F32), a_im.astype(F32))
    dt = jnp.exp(log_dt.astype(F32))[:, None]
    adt = a * dt
    a_bar = jnp.exp(adt)
    b_bar = ((a_bar - 1.0) / a)[..., None] * lax.complex(b_re.astype(F32), b_im.astype(F32))
    c_mat = lax.complex(c_re.astype(F32), c_im.astype(F32))
    lag = jnp.arange(L + 1, dtype=F32)
    pw = jnp.exp(adt[None] * lag[:, None, None])
    kern = jnp.einsum('gop,dgp,gpi->dgoi', c_mat, pw[:L], b_bar).real
    sig = jnp.arange(L)[:, None]
    tau = jnp.arange(L)[None, :]
    d = tau - sig
    kt = kern[jnp.clip(d, 0, L - 1)]
    kt = jnp.where((d >= 0)[:, :, None, None, None], kt, 0.0)
    eye = jnp.eye(SSM_GROUP, dtype=F32)
    dsk = d_skip.astype(F32).reshape(SSM_GROUPS, SSM_GROUP)
    kt = kt + (d == 0)[:, :, None, None, None] * (dsk[:, :, None] * eye)[None, None]
    t_mat = kt.transpose(2, 0, 4, 1, 3).reshape(SSM_GROUPS, L * SSM_GROUP, L * SSM_GROUP)
    bm = pw[:L][::-1][:, :, :, None] * b_bar[None]
    bm = bm.transpose(1, 0, 3, 2).reshape(SSM_GROUPS, L * SSM_GROUP, SSM_STATE)
    cm = c_mat[None] * pw[1:][:, :, None, :]
    cm = cm.transpose(1, 3, 0, 2).reshape(SSM_GROUPS, SSM_STATE, L * SSM_GROUP)
    a_chunk = pw[L]
    pad = 128 - SSM_STATE
    pc = lambda m: jnp.pad(m, ((0, 0), (0, 0), (0, pad))).astype(BF16)
    pr = lambda m: jnp.pad(m, ((0, 0), (0, pad), (0, 0))).astype(BF16)
    pa = lambda m: jnp.pad(m, ((0, 0), (0, pad)))
    return (pc(bm.real), pc(bm.imag), t_mat.astype(BF16), pr(cm.real), pr(-cm.imag),
            pa(a_chunk.real), pa(a_chunk.imag))


def _ssm(zb3, mats):
    b, s, w = zb3.shape
    L, G = SSM_CHUNK, SSM_GROUPS
    ts = min(SSM_TILE, s)
    nch = ts // L
    full = lambda a: pl.BlockSpec(a.shape, lambda bi, i: (0,) * a.ndim)
    return pl.pallas_call(
        _ssm_kernel,
        grid=(b, s // ts),
        in_specs=[pl.BlockSpec((None, ts, w), lambda bi, i: (bi, i, 0))] + [full(m) for m in mats],
        out_specs=pl.BlockSpec((None, ts, w), lambda bi, i: (bi, i, 0)),
        out_shape=jax.ShapeDtypeStruct((b, s, w), BF16),
        scratch_shapes=[pltpu.VMEM((w // 128, ts, 128), F32),
                        pltpu.VMEM((L * (w // 128), nch, 128), F32),
                        pltpu.VMEM((G, nch, L * SSM_GROUP), BF16),
                        pltpu.VMEM((nch * G, 128), F32), pltpu.VMEM((nch * G, 128), F32),
                        pltpu.VMEM((nch * G, 128), F32), pltpu.VMEM((nch * G, 128), F32),
                        pltpu.VMEM((G, nch, L * SSM_GROUP), F32),
                        pltpu.VMEM((2 * G, 128), F32)],
        compiler_params=_cparams("parallel", "arbitrary"),
        name="ssm",
    )(zb3, *mats)


def _mlstm_kernel(zc_ref, zgt_ref, convw_ref, gb_ref, ng_ref, triu_ref, bones_ref, o_ref,
                  st_ref, m_ref, tail_ref, qk_ref,
                  rt_ref, bl_ref, rm_ref, cr128_ref, cr64_ref, b64_ref, r64_ref, s_ref, hh_ref):
    ts = zc_ref.shape[0]
    hd, nh, w = HEAD_DIM, MLSTM_HEADS, GROUP_W
    nchunk = ts // CHUNK
    nr = nchunk * 8
    lane1 = lax.broadcasted_iota(I32, (1, 128), 1)
    half = [(lane1 // hd) == (h % 2) for h in range(nh)]
    grp = [slice((h * hd) // 128 * 128, (h * hd) // 128 * 128 + 128) for h in range(nh)]

    @pl.when(pl.program_id(1) == 0)
    def _():
        st_ref[...] = jnp.zeros_like(st_ref)
        m_ref[...] = jnp.zeros_like(m_ref)
        tail_ref[...] = jnp.zeros_like(tail_ref)

    r_i = lax.broadcasted_iota(I32, (CHUNK, CHUNK), 0)
    c_i = lax.broadcasted_iota(I32, (CHUNK, CHUNK), 1)
    causal = c_i <= r_i
    k64 = lax.broadcasted_iota(I32, (8, w), 0)
    j64 = lax.broadcasted_iota(I32, (8, w), 1)
    sel64 = (j64 // hd == k64 % nh).astype(BF16)
    k128 = lax.broadcasted_iota(I32, (8, nh * 128), 0)
    j128 = lax.broadcasted_iota(I32, (8, nh * 128), 1)
    sel128 = (j128 // 128 == k128 % nh).astype(BF16)

    def lanes(t):
        lo = jnp.where(lane1 < hd, t[0:1, :], t[1:2, :])
        hi = jnp.where(lane1 < hd, t[2:3, :], t[3:4, :])
        return jnp.concatenate([lo, hi], axis=1)

    rowm = lax.broadcasted_iota(I32, (nr, 128), 0) % 8
    lanem = lax.broadcasted_iota(I32, (nr, 128), 1)
    graw = jnp.concatenate([zgt_ref[:, c * CHUNK:(c + 1) * CHUNK] + gb_ref[...] for c in range(nchunk)], axis=0)
    g2 = jnp.where(rowm < nh, graw, _log_sigmoid(graw))
    ghi, glo = _split_bf16(g2)
    cum = _dot(ghi, triu_ref[...]) + _dot(glo, triu_ref[...])
    b_t = pltpu.roll(cum, nr - 4, axis=0)
    r_t = g2 - b_t
    cr = r_t
    sh = 1
    while sh < CHUNK:
        cr = jnp.maximum(cr, jnp.where(lanem >= sh, pltpu.roll(cr, sh, axis=1), NEG))
        sh *= 2
    rt_ref[...] = r_t
    bl_ref[...] = jnp.broadcast_to(b_t[:, CHUNK - 1:CHUNK], (nr, 128))
    rm_ref[...] = jnp.broadcast_to(cr[:, CHUNK - 1:CHUNK], (nr, 128))

    x = zc_ref[:, 0:2 * w].astype(F32)
    xe = jnp.concatenate([tail_ref[...], x], axis=0)
    tail_ref[...] = x[ts - 8:, :]
    cw = convw_ref[...]
    acc = x * cw[CONV_K - 1:CONV_K, :]
    for sft in range(1, CONV_K):
        acc = acc + xe[8 - sft:8 - sft + ts, :] * cw[CONV_K - 1 - sft:CONV_K - sft, :]
    lane_qk = lax.broadcasted_iota(I32, (1, 2 * w), 1)
    qk_ref[...] = acc * _sigmoid(acc) * jnp.where(lane_qk < w, 1.0, hd ** -0.5)

    def hi_lo_rows(v):
        hi = v.astype(BF16).astype(F32)
        return jnp.where(rowm < nh, hi, pltpu.roll(v - hi, 4, axis=0))

    a_cr = hi_lo_rows(cr)
    a_b = hi_lo_rows(b_t)
    a_r = hi_lo_rows(r_t)
    for c in range(nchunk):
        rows = slice(c * CHUNK, (c + 1) * CHUNK)
        t8 = slice(c * 8, (c + 1) * 8)
        a_cr_c = a_cr[t8].astype(BF16)
        cr128_ref[rows, :] = _dot_tn(a_cr_c, sel128)
        cr64_ref[rows, :] = _dot_tn(a_cr_c, sel64)
        b64_ref[rows, :] = _dot_tn(a_b[t8].astype(BF16), sel64)
        r64_ref[rows, :] = _dot_tn(a_r[t8].astype(BF16), sel64)
        for h in range(nh):
            qg = qk_ref[rows, grp[h]].astype(BF16)
            kg = qk_ref[rows, w + grp[h].start:w + grp[h].stop].astype(BF16)
            s_ref[rows, h * 128:(h + 1) * 128] = _dot_nt(qg, jnp.where(half[h], kg, jnp.zeros_like(kg)))

    lane_w = lax.broadcasted_iota(I32, (1, w), 1)
    hmask = [(lane_w // hd) == h for h in range(nh)]
    ones_blk = [m.astype(BF16) * jnp.ones((CHUNK, 1), BF16) for m in hmask]
    ones_cols = jnp.ones((CHUNK, w), BF16)
    zblk = jnp.zeros((hd, 128), F32)
    ngrp = w // 128
    m_prev = m_ref[...]
    cblk = [st_ref[h * hd:(h + 1) * hd, grp[h]] for h in range(nh)]
    nblk = [st_ref[h * hd:(h + 1) * hd, w + grp[h].start:w + grp[h].stop] for h in range(nh)]
    for c in range(nchunk):
        rows = slice(c * CHUNK, (c + 1) * CHUNK)
        t8 = slice(c * 8, (c + 1) * 8)
        r_c = rt_ref[t8, :]
        g_last = jnp.maximum(rm_ref[t8, :], m_prev)
        decay = jnp.exp(m_prev - g_last)
        mprev_l = lanes(m_prev)
        glast_l = lanes(g_last)
        gb64 = jnp.maximum(cr64_ref[rows, :], mprev_l)
        w_inter = jnp.exp(mprev_l - gb64)
        e_negm = jnp.exp(-(b64_ref[rows, :] + gb64))
        v_all = zc_ref[rows, 2 * w:3 * w]
        wcat = []
        vblocks = []
        srows = []
        for h in range(nh):
            gb128 = jnp.maximum(cr128_ref[rows, h * 128:(h + 1) * 128], m_prev[h:h + 1, :])
            d = jnp.exp(jnp.where(causal, r_c[h:h + 1, :] - gb128, NEG))
            wcat.append((d * s_ref[rows, h * 128:(h + 1) * 128]).astype(BF16))
            vblocks.append(jnp.concatenate([jnp.where(hmask[h], v_all, jnp.zeros_like(v_all)), ones_blk[h]], axis=1))
            g = (h * hd) // 128
            srows.append(jnp.concatenate([cblk[h] if j == g else zblk for j in range(ngrp)]
                                         + [nblk[h] if j == g else zblk for j in range(ngrp)], axis=1))
        s_bf = jnp.concatenate(srows, axis=0).astype(BF16)
        out_aug = _dot(jnp.concatenate(wcat, axis=1), jnp.concatenate(vblocks, axis=0))
        out_aug = out_aug + _dot((qk_ref[rows, 0:w] * w_inter).astype(BF16), s_bf)
        hh_ref[rows, :] = out_aug[:, 0:w] / jnp.maximum(jnp.abs(out_aug[:, w:2 * w]), e_negm)
        kw = (qk_ref[rows, w:2 * w] * jnp.exp(r64_ref[rows, :] - glast_l)).astype(BF16)
        upd = _dot_tn(kw, jnp.concatenate([v_all, ones_cols], axis=1))
        for h in range(nh):
            rs = slice(h * hd, (h + 1) * hd)
            cblk[h] = cblk[h] * decay[h:h + 1, :] + jnp.where(half[h], upd[rs, grp[h]], 0.0)
            nblk[h] = (nblk[h] * decay[h:h + 1, :]
                       + jnp.where(half[h], upd[rs, w + grp[h].start:w + grp[h].stop], 0.0))
        m_prev = bl_ref[t8, :] + g_last
    m_ref[...] = m_prev
    for h in range(nh):
        st_ref[h * hd:(h + 1) * hd, grp[h]] = cblk[h]
        st_ref[h * hd:(h + 1) * hd, w + grp[h].start:w + grp[h].stop] = nblk[h]

    hh = hh_ref[...]
    ms = _dot((hh * hh).astype(BF16), bones_ref[...])
    og = zc_ref[:, 3 * w:4 * w].astype(F32)
    o_ref[...] = (_sigmoid(og) * hh * lax.rsqrt(ms + EPS) * ng_ref[...]).astype(o_ref.dtype)


def _mlstm(zc3, zgt, conv_w, gate_b_rows, norm_g):
    b, s, w = zc3.shape
    ts = SEQ_TILE
    nt = s // ts
    gw = GROUP_W
    t = jnp.arange(CHUNK)
    triu = (t[:, None] <= t[None, :]).astype(BF16)
    hid = jnp.arange(gw) // HEAD_DIM
    bones = jnp.where(hid[:, None] == hid[None, :], 1.0 / HEAD_DIM, 0.0).astype(BF16)
    full = lambda a: pl.BlockSpec(a.shape, lambda bi, i: (0,) * a.ndim)
    return pl.pallas_call(
        _mlstm_kernel,
        grid=(b, nt),
        in_specs=[pl.BlockSpec((None, ts, w), lambda bi, i: (bi, i, 0)),
                  pl.BlockSpec((8, ts), lambda bi, i: (0, bi * nt + i)),
                  full(conv_w), full(gate_b_rows), full(norm_g), full(triu), full(bones)],
        out_specs=pl.BlockSpec((None, ts, gw), lambda bi, i: (bi, i, 0)),
        out_shape=jax.ShapeDtypeStruct((b, s, gw), BF16),
        scratch_shapes=[pltpu.VMEM((gw, 2 * gw), F32),
                        pltpu.VMEM((8, 128), F32),
                        pltpu.VMEM((8, 2 * gw), F32),
                        pltpu.VMEM((ts, 2 * gw), F32),
                        pltpu.VMEM((ts // CHUNK * 8, 128), F32),
                        pltpu.VMEM((ts // CHUNK * 8, 128), F32),
                        pltpu.VMEM((ts // CHUNK * 8, 128), F32),
                        pltpu.VMEM((ts, MLSTM_HEADS * 128), F32),
                        pltpu.VMEM((ts, gw), F32),
                        pltpu.VMEM((ts, gw), F32),
                        pltpu.VMEM((ts, gw), F32),
                        pltpu.VMEM((ts, MLSTM_HEADS * 128), F32),
                        pltpu.VMEM((ts, gw), F32)],
        compiler_params=_cparams("parallel", "arbitrary"),
        name="mlstm",
    )(zc3, zgt, conv_w, gate_b_rows, norm_g, triu, bones)


def _sgu_kernel(zd_ref, lng_ref, lnb_ref, w_ref, bias_ref, o_ref):
    tm = zd_ref.shape[0]
    gw = GROUP_W // SGU_GROUPS
    u = _gelu(zd_ref[:, 0:GROUP_W].astype(F32))
    v = _gelu(zd_ref[:, GROUP_W:2 * GROUP_W].astype(F32))
    mu = jnp.mean(v, axis=-1, keepdims=True)
    vc = v - mu
    var = jnp.mean(vc * vc, axis=-1, keepdims=True)
    vn = (vc * lax.rsqrt(var + EPS) * lng_ref[...] + lnb_ref[...]).astype(BF16)
    for c in range(tm // CHUNK):
        rows = slice(c * CHUNK, (c + 1) * CHUNK)
        mixed = jnp.concatenate([_dot(w_ref[g], vn[rows, g * gw:(g + 1) * gw]) for g in range(SGU_GROUPS)], axis=-1)
        o_ref[rows, :] = (u[rows, :] * (mixed + bias_ref[...])).astype(o_ref.dtype)


def _sgu(zd, ln_g, ln_b, w_tril, bias):
    n = zd.shape[0]
    tm = TOKEN_TILE
    full = lambda a: pl.BlockSpec(a.shape, lambda i: (0,) * a.ndim)
    return pl.pallas_call(
        _sgu_kernel,
        grid=(n // tm,),
        in_specs=[pl.BlockSpec((tm, 2 * GROUP_W), lambda i: (i, 0)), full(ln_g), full(ln_b), full(w_tril), full(bias)],
        out_specs=pl.BlockSpec((tm, GROUP_W), lambda i: (i, 0)),
        out_shape=jax.ShapeDtypeStruct((n, GROUP_W), BF16),
        compiler_params=_cparams("parallel"),
        name="sgu",
    )(zd, ln_g, ln_b, w_tril, bias)


def _out_proj_kernel(x_ref, ya_ref, yb_ref, yc_ref, yd_ref, gluw_ref, glub_ref, wo_ref, ng_ref,
                     rwh_ref, rwl_ref, rb_ref, triu_ref,
                     x1_ref, hx_ref, meta_ref, cnt_ref, carry_ref):
    tm = x_ref.shape[0]

    @pl.when(pl.program_id(0) == 0)
    def _():
        carry_ref[...] = jnp.zeros_like(carry_ref)

    yb = _gelu(yb_ref[...].astype(F32))
    yb = yb * _sigmoid(_dot(yb.astype(BF16), gluw_ref[...]) + glub_ref[...])
    mix = _dot(ya_ref[...], wo_ref[0:GROUP_W, :])
    mix = mix + _dot(yb.astype(BF16), wo_ref[GROUP_W:2 * GROUP_W, :])
    mix = mix + _dot(yc_ref[...], wo_ref[2 * GROUP_W:3 * GROUP_W, :])
    mix = mix + _dot(yd_ref[...], wo_ref[3 * GROUP_W:4 * GROUP_W, :])
    x1 = x_ref[...] + mix
    x1_ref[...] = x1
    h = _rms(x1, ng_ref[...])
    hx_ref[:, 0:PACK_W] = _pack_rows(h)

    hh, hl = _split_bf16(h)
    wh = rwh_ref[...]
    wl = rwl_ref[...]
    logits = _dot_nt(wh, hh) + _dot_nt(wh, hl) + _dot_nt(wl, hh) + rb_ref[...]
    gl = [logits[j:j + 1, :] for j in range(N_GROUPS)]
    gmax = functools.reduce(jnp.maximum, gl)
    gsel = jnp.full((1, tm), N_GROUPS - 1, I32)
    for j in range(N_GROUPS - 2, -1, -1):
        gsel = jnp.where(gl[j] == gmax, j, gsel)
    p_g = 1.0 / functools.reduce(jnp.add, [jnp.exp(v - gmax) for v in gl])
    e_in = []
    for i in range(EPG):
        v = logits[N_GROUPS + i:N_GROUPS + i + 1, :]
        for j in range(1, N_GROUPS):
            r = N_GROUPS + j * EPG + i
            v = jnp.where(gsel == j, logits[r:r + 1, :], v)
        e_in.append(v)
    v1 = functools.reduce(jnp.maximum, e_in)
    i1 = jnp.full((1, tm), EPG - 1, I32)
    for i in range(EPG - 2, -1, -1):
        i1 = jnp.where(e_in[i] == v1, i, i1)
    rest = [jnp.where(i1 == i, NEG, e_in[i]) for i in range(EPG)]
    v2 = functools.reduce(jnp.maximum, rest)
    i2 = jnp.full((1, tm), EPG - 1, I32)
    for i in range(EPG - 2, -1, -1):
        i2 = jnp.where((rest[i] == v2) & (i1 != i), i, i2)
    e2 = jnp.exp(v2 - v1)
    w1 = p_g / (1.0 + e2)
    w2 = p_g * e2 / (1.0 + e2)
    lo = jnp.minimum(i1, i2)
    hi = jnp.maximum(i1, i2)
    w_lo = jnp.where(i1 < i2, w1, w2)
    w_hi = jnp.where(i1 < i2, w2, w1)
    bucket = gsel * PAIRS + ((lo * (2 * EPG - 1 - lo)) >> 1) + (hi - lo - 1)

    kid = lax.broadcasted_iota(I32, (BUCKET_PAD, tm), 0)
    onehot = (kid == bucket).astype(F32)
    prefix = _dot(onehot.astype(BF16), triu_ref[...])
    carry = carry_ref[...]
    rank = jnp.sum(onehot * (prefix - 1.0 + carry[:, 0:1]), axis=0, keepdims=True)
    carry = carry + prefix[:, tm - 1:tm]
    carry_ref[...] = carry
    cnt_ref[...] = carry
    meta_ref[...] = jnp.concatenate([bucket, rank.astype(I32), jnp.zeros((6, tm), I32)], axis=0)
    wrows = jnp.concatenate([w_lo, w_hi, jnp.zeros((126, tm), F32)], axis=0)
    for c in range(tm // 128):
        hx_ref[c * 128:(c + 1) * 128, PACK_W:ROW_W] = pltpu.bitcast(wrows[:, c * 128:(c + 1) * 128].T, U32)


def _out_proj(x2, ya, yb, yc, yd, glu_w, glu_b, w_out, norm_g, rw_hi, rw_lo, rb):
    n = x2.shape[0]
    tm = TOKEN_TILE
    row = lambda w: pl.BlockSpec((tm, w), lambda i: (i, 0))
    full = lambda a: pl.BlockSpec(a.shape, lambda i: (0,) * a.ndim)
    t = jnp.arange(tm)
    triu = (t[:, None] <= t[None, :]).astype(BF16)
    return pl.pallas_call(
        _out_proj_kernel,
        grid=(n // tm,),
        in_specs=[row(D_MODEL), row(GROUP_W), row(GROUP_W), row(GROUP_W), row(GROUP_W),
                  full(glu_w), full(glu_b), full(w_out), full(norm_g), full(rw_hi), full(rw_lo), full(rb),
                  full(triu)],
        out_specs=[row(D_MODEL), row(ROW_W), pl.BlockSpec((8, tm), lambda i: (0, i)),
                   pl.BlockSpec((BUCKET_PAD, 128), lambda i: (0, 0))],
        out_shape=[jax.ShapeDtypeStruct((n, D_MODEL), F32), jax.ShapeDtypeStruct((n, ROW_W), U32),
                   jax.ShapeDtypeStruct((8, n), I32), jax.ShapeDtypeStruct((BUCKET_PAD, 128), F32)],
        scratch_shapes=[pltpu.VMEM((BUCKET_PAD, 128), F32)],
        compiler_params=_cparams("arbitrary"),
        name="out_proj",
    )(x2, ya, yb, yc, yd, glu_w, glu_b, w_out, norm_g, rw_hi, rw_lo, rb, triu)


def _dispatch_kernel(dest_ref, hx_ref, xs_in_ref, xs_ref, sem):
    del xs_in_ref
    nq = hx_ref.shape[0]

    def start(q, c):
        for r in range(SUBLANES):
            d = dest_ref[0, 0, q * SUBLANES + r]
            pltpu.make_async_copy(hx_ref.at[q, pl.ds(r, 1)], xs_ref.at[pl.ds(d, 1)], sem).start()
        return c

    lax.fori_loop(0, nq, start, 0)
    pltpu.make_async_copy(hx_ref, hx_ref, sem).wait()


def _dispatch(dest3, hx, p_tot):
    n = hx.shape[0]
    tm = MOE_TILE
    xs0 = jnp.zeros((p_tot, ROW_W), U32)
    hx = hx.reshape(n // SUBLANES, SUBLANES, ROW_W)
    return pl.pallas_call(
        _dispatch_kernel,
        grid=(n // tm,),
        in_specs=[pl.BlockSpec((1, 1, tm), lambda i: (i, 0, 0), memory_space=pltpu.SMEM),
                  pl.BlockSpec((tm // SUBLANES, SUBLANES, ROW_W), lambda i: (i, 0, 0)),
                  pl.BlockSpec(memory_space=pl.ANY)],
        out_specs=pl.BlockSpec(memory_space=pl.ANY),
        out_shape=jax.ShapeDtypeStruct((p_tot, ROW_W), U32),
        input_output_aliases={2: 0},
        scratch_shapes=[pltpu.SemaphoreType.DMA],
        compiler_params=_cparams("arbitrary"),
        name="dispatch",
    )(dest3, hx, xs0)


def _ffn_kernel(valid_ref, xblk_ref, run_ref, exprun_ref, nrun_ref, xs_ref, wg_hbm, wu_hbm, wd_hbm, ys_ref,
                stg_g, stg_u, stg_d, wgu_a, wdn_a, wgu_b, wdn_b, sem, *, layer):
    del xblk_ref
    j = pl.program_id(0)
    valid = valid_ref[j]
    jp = jnp.maximum(j - 1, 0)

    def expert_copies(s, r, slot):
        e = exprun_ref[s, r]
        return [pltpu.make_async_copy(src.at[layer, e], dst.at[s, slot], sem.at[s, slot])
                for src, dst in ((wg_hbm, stg_g), (wu_hbm, stg_u), (wd_hbm, stg_d))]

    for s, (wgu, wdn) in enumerate(((wgu_a, wdn_a), (wgu_b, wdn_b))):
        r = run_ref[s, j]

        @pl.when((j == 0) | (r != run_ref[s, jp]))
        def _():
            slot = r % 2

            @pl.when(j == 0)
            def _():
                for c in expert_copies(s, r, slot):
                    c.start()

            for c in expert_copies(s, r, slot):
                c.wait()

            @pl.when(r + 1 < nrun_ref[s])
            def _():
                for c in expert_copies(s, r + 1, 1 - slot):
                    c.start()

            wgu[:, 0:D_EXPERT] = stg_g[s, slot].astype(BF16)
            wgu[:, D_EXPERT:2 * D_EXPERT] = stg_u[s, slot].astype(BF16)
            wdn[...] = stg_d[s, slot].astype(BF16)

    def ffn_rows(rows):
        x_lo, x_hi = _unpack_rows(xs_ref[rows, 0:PACK_W])
        xb = jnp.concatenate([x_lo.astype(BF16), x_hi.astype(BF16)], axis=1)
        wts = pltpu.bitcast(xs_ref[rows, PACK_W:ROW_W], F32)

        def expert(wgu, wdn):
            gu = _dot(xb, wgu[...])
            g = gu[:, 0:D_EXPERT]
            act = (g * _sigmoid(g) * gu[:, D_EXPERT:2 * D_EXPERT]).astype(BF16)
            return _dot(act, wdn[...])

        ys_ref[rows, :] = _pack_rows(expert(wgu_a, wdn_a) * wts[:, 0:1] + expert(wgu_b, wdn_b) * wts[:, 1:2])

    half = FFN_SUB // 2
    for sb in range(FFN_BLOCK // FFN_SUB):
        r0 = sb * FFN_SUB
        left = valid - r0

        @pl.when(left <= 0)
        def _():
            ys_ref[r0:r0 + FFN_SUB, :] = jnp.zeros((FFN_SUB, PACK_W), U32)

        @pl.when((left > 0) & (left <= half))
        def _():
            ffn_rows(slice(r0, r0 + half))
            ys_ref[r0 + half:r0 + FFN_SUB, :] = jnp.zeros((half, PACK_W), U32)

        @pl.when(left > half)
        def _():
            ffn_rows(slice(r0, r0 + FFN_SUB))


def _ffn(valid, xblk, run, exprun, nrun, xs, layer, w_gate, w_up, w_down):
    p_tot = xs.shape[0]
    bm = FFN_BLOCK
    nblk = p_tot // bm
    up_shape = (D_MODEL, D_EXPERT)
    dn_shape = (D_EXPERT, D_MODEL)
    hbm = pl.BlockSpec(memory_space=pl.ANY)
    return pl.pallas_call(
        functools.partial(_ffn_kernel, layer=layer),
        grid_spec=pltpu.PrefetchScalarGridSpec(
            num_scalar_prefetch=5,
            grid=(nblk,),
            in_specs=[pl.BlockSpec((bm, ROW_W), lambda j, va, xb, ru, er, nr: (xb[j], 0)), hbm, hbm, hbm],
            out_specs=pl.BlockSpec((bm, PACK_W), lambda j, va, xb, ru, er, nr: (j, 0)),
            scratch_shapes=[pltpu.VMEM((2, 2) + up_shape, F32), pltpu.VMEM((2, 2) + up_shape, F32),
                            pltpu.VMEM((2, 2) + dn_shape, F32),
                            pltpu.VMEM((D_MODEL, 2 * D_EXPERT), BF16), pltpu.VMEM(dn_shape, BF16),
                            pltpu.VMEM((D_MODEL, 2 * D_EXPERT), BF16), pltpu.VMEM(dn_shape, BF16),
                            pltpu.SemaphoreType.DMA((2, 2))],
        ),
        out_shape=jax.ShapeDtypeStruct((p_tot, PACK_W), U32),
        compiler_params=_cparams("arbitrary"),
        name="ffn",
    )(valid, xblk, run, exprun, nrun, xs, w_gate, w_up, w_down)


def _combine_kernel(dest_ref, dnext_ref, x1_ref, ys_ref, ng_ref, o_ref, buf_ref, sem, *, final_norm):
    i = pl.program_id(0)
    tm = x1_ref.shape[0]
    nq = tm // SUBLANES
    slot = i % 2

    def gather(d_ref, sl):
        def start(q, c):
            for r in range(SUBLANES):
                d = d_ref[0, 0, q * SUBLANES + r]
                pltpu.make_async_copy(ys_ref.at[pl.ds(d, 1)], buf_ref.at[sl, q, pl.ds(r, 1)], sem.at[sl]).start()
            return c

        lax.fori_loop(0, nq, start, 0)

    @pl.when(i == 0)
    def _():
        gather(dest_ref, 0)

    pltpu.make_async_copy(buf_ref.at[slot], buf_ref.at[slot], sem.at[slot]).wait()

    @pl.when(i + 1 < pl.num_programs(0))
    def _():
        gather(dnext_ref, 1 - slot)

    y_lo, y_hi = _unpack_rows(buf_ref[slot].reshape(tm, PACK_W))
    x2 = x1_ref[...] + jnp.concatenate([y_lo, y_hi], axis=1)
    o_ref[...] = _rms(x2, ng_ref[...]) if final_norm else x2


def _combine(dest3, x1, ys, norm_g, final_norm):
    n = x1.shape[0]
    tm = MOE_TILE
    last = n // tm - 1
    return pl.pallas_call(
        functools.partial(_combine_kernel, final_norm=final_norm),
        grid=(n // tm,),
        in_specs=[pl.BlockSpec((1, 1, tm), lambda i: (i, 0, 0), memory_space=pltpu.SMEM),
                  pl.BlockSpec((1, 1, tm), lambda i: (jnp.minimum(i + 1, last), 0, 0), memory_space=pltpu.SMEM),
                  pl.BlockSpec((tm, D_MODEL), lambda i: (i, 0)),
                  pl.BlockSpec(memory_space=pl.ANY),
                  pl.BlockSpec((1, D_MODEL), lambda i: (0, 0))],
        out_specs=pl.BlockSpec((tm, D_MODEL), lambda i: (i, 0)),
        out_shape=jax.ShapeDtypeStruct((n, D_MODEL), F32),
        scratch_shapes=[pltpu.VMEM((2, tm // SUBLANES, SUBLANES, PACK_W), U32), pltpu.SemaphoreType.DMA((2,))],
        compiler_params=_cparams("arbitrary"),
        name="combine",
    )(dest3, dest3, x1, ys, norm_g)


def _routing_tables(meta, counts, n):
    bm = FFN_BLOCK
    nblk = n // bm + N_BUCKETS
    bucket = meta[0]
    rank = meta[1]
    cnt = counts[:N_BUCKETS, 0].astype(I32)
    padded = ((cnt + bm - 1) // bm) * bm
    pad_end = jnp.cumsum(padded)
    pad_start = pad_end - padded
    onehot = bucket[:, None] == jnp.arange(N_BUCKETS, dtype=I32)[None, :]
    dest = rank + jnp.sum(jnp.where(onehot, pad_start[None, :], 0), axis=1)
    nact = (pad_end[-1] // bm).astype(I32)
    j = jnp.arange(nblk, dtype=I32)
    blk = jnp.minimum(j, jnp.maximum(nact - 1, 0))
    pos = blk * bm
    bkt = jnp.minimum(jnp.sum((pad_end[None, :] <= pos[:, None]).astype(I32), axis=1), N_BUCKETS - 1)
    in_bkt = bkt[:, None] == jnp.arange(N_BUCKETS, dtype=I32)[None, :]
    sel = lambda tab: jnp.sum(jnp.where(in_bkt, tab[None, :], 0), axis=1)
    valid = jnp.where(j < nact, jnp.clip(sel(cnt) - (pos - sel(pad_start)), 0, bm), 0)
    lo_tab, hi_tab = [], []
    for g in range(N_GROUPS):
        for a in range(EPG):
            for b in range(a + 1, EPG):
                lo_tab.append(g * EPG + a)
                hi_tab.append(g * EPG + b)
    experts = jnp.stack([sel(jnp.asarray(lo_tab, I32)), sel(jnp.asarray(hi_tab, I32))]).astype(I32)
    change = jnp.concatenate([jnp.zeros((2, 1), I32), (experts[:, 1:] != experts[:, :-1]).astype(I32)], axis=1)
    run = jnp.cumsum(change, axis=1).astype(I32)
    is_run = run[:, :, None] == j[None, None, :]
    exprun = jnp.max(jnp.where(is_run, experts[:, :, None], 0), axis=1).astype(I32)
    nrun = run[:, -1] + 1
    return dest.astype(I32), valid.astype(I32), blk.astype(I32), run, exprun, nrun.astype(I32), nblk * bm


def _layer(x2, b, s, p, final_g):
    n = b * s
    za, zb, zc, zd, zgt = _in_proj(x2, p['norm_mix_g'], p['wa'], p['wb'], p['wc'], p['wd'], p['wgt'])
    ya = _attention(za.reshape(b, s, -1), p['sinks']).reshape(n, GROUP_W)
    yb = _ssm(zb.reshape(b, s, GROUP_W), p['ssm']).reshape(n, GROUP_W)
    yc = _mlstm(zc.reshape(b, s, -1), zgt, p['conv_w'], p['gate_b'], p['mlstm_norm_g']).reshape(n, GROUP_W)
    yd = _sgu(zd, p['sgu_ln_g'], p['sgu_ln_b'], p['sgu_w'], p['sgu_bias'])
    x1, hx, meta, counts = _out_proj(x2, ya, yb, yc, yd, p['glu_w'], p['glu_b'], p['w_out'], p['norm_ffn_g'],
                                     p['rw_hi'], p['rw_lo'], p['rb'])
    dest, valid, xblk, run, exprun, nrun, p_tot = _routing_tables(meta, counts, n)
    dest3 = dest.reshape(n // MOE_TILE, 1, MOE_TILE)
    xs = _dispatch(dest3, hx, p_tot)
    ys = _ffn(valid, xblk, run, exprun, nrun, xs, p['layer'], p['w_gate'], p['w_up'], p['w_down'])
    if final_g is None:
        return _combine(dest3, x1, ys, p['norm_ffn_g'], False)
    return _combine(dest3, x1, ys, final_g, True)


def _prep_layer(l, norm_mix_g, w_in, attn_sinks, ssm_a_re, ssm_a_im, ssm_b_re, ssm_b_im, ssm_c_re, ssm_c_im,
                ssm_d, ssm_log_dt, ssm_glu_w, ssm_glu_b, mlstm_conv_w, mlstm_gate_b, mlstm_norm_g,
                sgu_ln_g, sgu_ln_b, sgu_w, sgu_b, w_out, norm_ffn_g, router_group_w, router_group_b,
                router_expert_w, router_expert_b, expert_w_gate, expert_w_up, expert_w_down):
    w = w_in[l]
    o_su = 2 * GROUP_W
    o_c = o_su + GROUP_W
    o_ci = o_c + 3 * GROUP_W
    o_co = o_ci + 2 * MLSTM_HEADS
    o_d = o_co + GROUP_W
    wc = jnp.concatenate([w[:, o_c:o_ci], w[:, o_co:o_d]], axis=1)
    wgt = w[:, o_ci:o_co].T
    head_cols = lambda m, hq: m[..., hq * HEAD_DIM:(hq + 1) * HEAD_DIM]
    order = [g * ATTN_REP + r for r in range(ATTN_REP) for g in range(KV_HEADS)]
    wa = jnp.concatenate([head_cols(w, hq) * (HEAD_DIM ** -0.5) for hq in order] + [w[:, GROUP_W:o_su]], axis=1)
    wo = jnp.concatenate([w_out[l][hq * HEAD_DIM:(hq + 1) * HEAD_DIM] for hq in order] + [w_out[l][GROUP_W:]], axis=0)
    rw = jnp.concatenate([router_group_w[l], router_expert_w[l]], axis=1).T.astype(F32)
    rw = jnp.pad(rw, ((0, 4), (0, 0)))
    rw_hi = rw.astype(BF16)
    rw_lo = (rw - rw_hi.astype(F32)).astype(BF16)
    rb = jnp.pad(jnp.concatenate([router_group_b[l], router_expert_b[l]]).astype(F32), (0, 4))[:, None]
    tril = jnp.tril(jnp.ones((CHUNK, CHUNK), F32))
    gw = GROUP_W // SGU_GROUPS
    return dict(
        norm_mix_g=norm_mix_g[l][None, :].astype(F32),
        wa=wa.astype(BF16), wb=w[:, o_su:o_c].astype(BF16), wc=wc.astype(BF16),
        wd=w[:, o_d:].astype(BF16), wgt=wgt.astype(BF16),
        sinks=attn_sinks[l].astype(F32),
        ssm=_ssm_matrices(ssm_a_re[l], ssm_a_im[l], ssm_b_re[l], ssm_b_im[l], ssm_c_re[l], ssm_c_im[l],
                          ssm_d[l], ssm_log_dt[l]),
        glu_w=ssm_glu_w[l].astype(BF16), glu_b=ssm_glu_b[l][None, :].astype(F32),
        conv_w=mlstm_conv_w[l].astype(F32),
        gate_b=jnp.broadcast_to(mlstm_gate_b[l].astype(F32)[:, None], (2 * MLSTM_HEADS, 128)),
        mlstm_norm_g=mlstm_norm_g[l][None, :].astype(F32),
        sgu_ln_g=sgu_ln_g[l][None, :].astype(F32), sgu_ln_b=sgu_ln_b[l][None, :].astype(F32),
        sgu_w=(sgu_w[l].astype(F32) * tril).astype(BF16),
        sgu_bias=jnp.repeat(sgu_b[l].astype(F32).T, gw, axis=1),
        w_out=wo.astype(BF16), norm_ffn_g=norm_ffn_g[l][None, :].astype(F32),
        rw_hi=rw_hi, rw_lo=rw_lo, rb=rb,
        layer=l, w_gate=expert_w_gate, w_up=expert_w_up, w_down=expert_w_down,
    )


def kernel(x, norm_mix_g, w_in, attn_sinks, ssm_a_re, ssm_a_im, ssm_b_re, ssm_b_im, ssm_c_re, ssm_c_im, ssm_d, ssm_log_dt, ssm_glu_w, ssm_glu_b, mlstm_conv_w, mlstm_gate_b, mlstm_norm_g, sgu_ln_g, sgu_ln_b, sgu_w, sgu_b, w_out, norm_ffn_g, router_group_w, router_group_b, router_expert_w, router_expert_b, expert_w_gate, expert_w_up, expert_w_down, norm_final_g):
    b, s, d = x.shape
    depth = w_in.shape[0]
    x2 = x.reshape(b * s, d).astype(F32)
    for l in range(depth):
        p = _prep_layer(l, norm_mix_g, w_in, attn_sinks, ssm_a_re, ssm_a_im, ssm_b_re, ssm_b_im, ssm_c_re,
                        ssm_c_im, ssm_d, ssm_log_dt, ssm_glu_w, ssm_glu_b, mlstm_conv_w, mlstm_gate_b,
                        mlstm_norm_g, sgu_ln_g, sgu_ln_b, sgu_w, sgu_b, w_out, norm_ffn_g, router_group_w,
                        router_group_b, router_expert_w, router_expert_b, expert_w_gate, expert_w_up,
                        expert_w_down)
        final_g = norm_final_g[None, :].astype(F32) if l == depth - 1 else None
        x2 = _layer(x2, b, s, p, final_g)
    return x2.reshape(b, s, d).astype(x.dtype)
```

```python
import functools
import math

import jax
import jax.numpy as jnp
from jax import lax
from jax.experimental import pallas as pl
from jax.experimental.pallas import tpu as pltpu

F32 = jnp.float32
BF16 = jnp.bfloat16
I32 = jnp.int32
U32 = jnp.uint32

D_MODEL = 1024
GROUP_W = 256
HEAD_DIM = 64
EPS = 1e-6
NEG = -1e30
WINDOW = 128
KV_HEADS = 2
ATTN_REP = 2
SSM_GROUP = 16
SSM_GROUPS = 16
SSM_STATE = 64
SSM_CHUNK = 16
MLSTM_HEADS = 4
CHUNK = 128
CONV_K = 4
SGU_GROUPS = 4
N_GROUPS = 4
EPG = 8
N_EXPERTS = 32
D_EXPERT = 512
PAIRS = EPG * (EPG - 1) // 2
N_BUCKETS = N_GROUPS * PAIRS
BUCKET_PAD = 128
PACK_W = D_MODEL // 2
ROW_W = PACK_W + 128

TOKEN_TILE = 512
DISPATCH_TILE = 2048
COMBINE_TILE = 1024
SEQ_TILE = 512
SSM_TILE = 4096
SSM_SLAB = 64
FFN_BLOCK = 512
FFN_SUB = 256
SUBLANES = 8
VMEM_LIMIT = 56 * 1024 * 1024


def _cparams(*sem):
    return pltpu.CompilerParams(dimension_semantics=sem, vmem_limit_bytes=VMEM_LIMIT)


def _rms(x, g):
    return x * lax.rsqrt(jnp.mean(x * x, axis=-1, keepdims=True) + EPS) * g


def _gelu(x):
    return 0.5 * x * (1.0 + jnp.tanh(math.sqrt(2.0 / math.pi) * (x + 0.044715 * (x * x * x))))


def _sigmoid(x):
    return 1.0 / (1.0 + jnp.exp(-x))


def _log_sigmoid(x):
    return jnp.minimum(x, 0.0) - jnp.log(1.0 + jnp.exp(-jnp.abs(x)))


def _dot(a, b):
    return jnp.dot(a, b, preferred_element_type=F32)


def _dot_nt(a, b):
    return lax.dot_general(a, b, (((1,), (1,)), ((), ())), preferred_element_type=F32)


def _dot_tn(a, b):
    return lax.dot_general(a, b, (((0,), (0,)), ((), ())), preferred_element_type=F32)


def _split_bf16(x):
    hi = x.astype(BF16)
    lo = (x - hi.astype(F32)).astype(BF16)
    return hi, lo


def _pack_rows(x):
    w = x.shape[1] // 2
    bits = lambda v: pltpu.bitcast(v.astype(BF16).astype(F32), U32)
    return (bits(x[:, 0:w]) >> 16) | (bits(x[:, w:2 * w]) & jnp.uint32(0xFFFF0000))


def _unpack_rows(p):
    return pltpu.bitcast(p << 16, F32), pltpu.bitcast(p & jnp.uint32(0xFFFF0000), F32)


def _in_proj_kernel(x_ref, g_ref, wa_ref, wb_ref, wc_ref, wd_ref, wgt_ref,
                    za_ref, zb_ref, zc_ref, zd_ref, zgt_ref):
    hb = _rms(x_ref[...], g_ref[...]).astype(BF16)
    za_ref[...] = _dot(hb, wa_ref[...]).astype(BF16)
    zb_ref[...] = _dot(hb, wb_ref[...]).astype(BF16)
    zc_ref[...] = _dot(hb, wc_ref[...]).astype(BF16)
    zd_ref[...] = _dot(hb, wd_ref[...]).astype(BF16)
    zgt_ref[...] = _dot_nt(wgt_ref[...], hb)


def _in_proj(x2, g, wa, wb, wc, wd, wgt):
    n = x2.shape[0]
    tm = TOKEN_TILE
    row = lambda w: pl.BlockSpec((tm, w), lambda i: (i, 0))
    full = lambda a: pl.BlockSpec(a.shape, lambda i: (0,) * a.ndim)
    widths = (wa.shape[1], wb.shape[1], wc.shape[1], wd.shape[1])
    return pl.pallas_call(
        _in_proj_kernel,
        grid=(n // tm,),
        in_specs=[row(D_MODEL), full(g), full(wa), full(wb), full(wc), full(wd), full(wgt)],
        out_specs=[row(w) for w in widths] + [pl.BlockSpec((wgt.shape[0], tm), lambda i: (0, i))],
        out_shape=[jax.ShapeDtypeStruct((n, w), BF16) for w in widths]
        + [jax.ShapeDtypeStruct((wgt.shape[0], n), F32)],
        compiler_params=_cparams("parallel"),
        name="in_proj",
    )(x2, g, wa, wb, wc, wd, wgt)


def _attn_kernel(sink_ref, cur_ref, prev_ref, o_ref):
    first = pl.program_id(1) == 0
    nblk = cur_ref.shape[0] // WINDOW
    row = lax.broadcasted_iota(I32, (WINDOW, 2 * WINDOW), 0)
    col = lax.broadcasted_iota(I32, (WINDOW, 2 * WINDOW), 1)
    band = (col <= row + WINDOW) & (col > row)
    lane = lax.broadcasted_iota(I32, (1, 128), 1)
    kv_lanes = [(lane // HEAD_DIM) == g for g in range(KV_HEADS)]
    ones = jnp.ones((2 * WINDOW, 128), BF16)
    ko = 2 * 128
    vo = 3 * 128
    for j in range(nblk):
        cur = cur_ref[j * WINDOW:(j + 1) * WINDOW, :]
        if j == 0:
            prev = prev_ref[...]
            mask = band & ((col >= WINDOW) | jnp.logical_not(first))
        else:
            prev = cur_ref[(j - 1) * WINDOW:j * WINDOW, :]
            mask = band
        kk = jnp.concatenate([prev[:, ko:ko + 128], cur[:, ko:ko + 128]], axis=0)
        vaug = jnp.concatenate([jnp.concatenate([prev[:, vo:vo + 128], cur[:, vo:vo + 128]], axis=0), ones], axis=1)
        outs = []
        for r in range(ATTN_REP):
            qg = cur[:, r * 128:(r + 1) * 128]
            og = []
            for g in range(KV_HEADS):
                s = _dot_nt(jnp.where(kv_lanes[g], qg, jnp.zeros_like(qg)), kk)
                s = jnp.where(mask, s, NEG)
                sink = sink_ref[g * ATTN_REP + r]
                m = jnp.maximum(jnp.max(s, axis=-1, keepdims=True), sink)
                oa = _dot(jnp.exp(s - m).astype(BF16), vaug)
                og.append(oa[:, 0:128] / (oa[:, 128:256] + jnp.exp(sink - m)))
            outs.append(jnp.where(kv_lanes[0], og[0], og[1]))
        o_ref[j * WINDOW:(j + 1) * WINDOW, :] = jnp.concatenate(outs, axis=-1).astype(o_ref.dtype)


def _attention(za3, sinks):
    b, s, w = za3.shape
    ts = SEQ_TILE
    per = ts // WINDOW
    return pl.pallas_call(
        _attn_kernel,
        grid=(b, s // ts),
        in_specs=[pl.BlockSpec(memory_space=pltpu.SMEM),
                  pl.BlockSpec((None, ts, w), lambda bi, i: (bi, i, 0)),
                  pl.BlockSpec((None, WINDOW, w), lambda bi, i: (bi, jnp.maximum(i * per - 1, 0), 0))],
        out_specs=pl.BlockSpec((None, ts, GROUP_W), lambda bi, i: (bi, i, 0)),
        out_shape=jax.ShapeDtypeStruct((b, s, GROUP_W), BF16),
        compiler_params=_cparams("parallel", "parallel"),
        name="attn",
    )(sinks, za3, za3)


def _ssm_kernel(zb_ref, bre_ref, bim_ref, t_ref, cre_ref, cim_ref, are_ref, aim_ref, o_ref,
                x_ref, xs_ref, u_ref, vre_ref, vim_ref, sre_ref, sim_ref, y_ref, st_ref):
    ts = zb_ref.shape[0]
    L, G, H = SSM_CHUNK, SSM_GROUPS, SSM_GROUP
    nch = ts // L
    half_g = 128 // H
    n_half = G // half_g

    @pl.when(pl.program_id(1) == 0)
    def _():
        st_ref[...] = jnp.zeros_like(st_ref)

    blk = lax.broadcasted_iota(I32, (1, 128), 1) // H

    def block_transpose(arrs):
        a = list(arrs)
        s = half_g // 2
        while s >= 1:
            keep = (blk & s) == 0
            for i in range(half_g):
                if i & s == 0:
                    ai, aj = a[i], a[i + s]
                    a[i] = jnp.where(keep, ai, pltpu.roll(aj, s * H, axis=1))
                    a[i + s] = jnp.where(keep, pltpu.roll(ai, 128 - s * H, axis=1), aj)
            s //= 2
        return a

    for hf in range(n_half):
        x_ref[hf] = zb_ref[:, hf * 128:(hf + 1) * 128].astype(F32)
    for sg in range(L):
        for hf in range(n_half):
            xs_ref[sg * n_half + hf] = x_ref[hf, pl.ds(sg, nch, stride=L), :]
    for hf in range(n_half):
        for oc in range(L // half_g):
            for r0 in range(0, nch, SSM_SLAB):
                rows = slice(r0, r0 + SSM_SLAB)
                t = block_transpose([xs_ref[(oc * half_g + k) * n_half + hf, rows, :] for k in range(half_g)])
                for gl in range(half_g):
                    u_ref[hf * half_g + gl, rows, oc * 128:(oc + 1) * 128] = t[gl].astype(BF16)
    for g in range(G):
        ug = u_ref[g]
        vre_ref[pl.ds(g, nch, stride=G), :] = _dot(ug, bre_ref[g])
        vim_ref[pl.ds(g, nch, stride=G), :] = _dot(ug, bim_ref[g])

    are = are_ref[...]
    aim = aim_ref[...]

    def step(c, carry):
        sre, sim = carry
        r0 = pl.multiple_of(c * G, G)
        sre_ref[pl.ds(r0, G), :] = sre
        sim_ref[pl.ds(r0, G), :] = sim
        vre = vre_ref[pl.ds(r0, G), :]
        vim = vim_ref[pl.ds(r0, G), :]
        return (are * sre - aim * sim + vre, are * sim + aim * sre + vim)

    sre, sim = lax.fori_loop(0, nch, step, (st_ref[0:G, :], st_ref[G:2 * G, :]), unroll=4)
    st_ref[0:G, :] = sre
    st_ref[G:2 * G, :] = sim

    for g in range(G):
        y = _dot(u_ref[g], t_ref[g])
        y = y + _dot(sre_ref[pl.ds(g, nch, stride=G), :].astype(BF16), cre_ref[g])
        y = y + _dot(sim_ref[pl.ds(g, nch, stride=G), :].astype(BF16), cim_ref[g])
        y_ref[g] = y
    for hf in range(n_half):
        for oc in range(L // half_g):
            for r0 in range(0, nch, SSM_SLAB):
                t = block_transpose([y_ref[hf * half_g + gl, r0:r0 + SSM_SLAB, oc * 128:(oc + 1) * 128]
                                     for gl in range(half_g)])
                for k in range(half_g):
                    x_ref[hf, pl.ds(oc * half_g + k + L * r0, SSM_SLAB, stride=L), :] = t[k]
    for hf in range(n_half):
        o_ref[:, hf * 128:(hf + 1) * 128] = x_ref[hf].astype(o_ref.dtype)


def _ssm_matrices(a_re, a_im, b_re, b_im, c_re, c_im, d_skip, log_dt):
    L = SSM_CHUNK
    a = lax.complex(a_re.astype(F32), a_im.astype(F32))
    dt = jnp.exp(log_dt.astype(F32))[:, None]
    adt = a * dt
    a_bar = jnp.exp(adt)
    b_bar = ((a_bar - 1.0) / a)[..., None] * lax.complex(b_re.astype(F32), b_im.astype(F32))
    c_mat = lax.complex(c_re.astype(F32), c_im.astype(F32))
    lag = jnp.arange(L + 1, dtype=F32)
    pw = jnp.exp(adt[None] * lag[:, None, None])
    kern = jnp.einsum('gop,dgp,gpi->dgoi', c_mat, pw[:L], b_bar).real
    sig = jnp.arange(L)[:, None]
    tau = jnp.arange(L)[None, :]
    d = tau - sig
    kt = kern[jnp.clip(d, 0, L - 1)]
    kt = jnp.where((d >= 0)[:, :, None, None, None], kt, 0.0)
    eye = jnp.eye(SSM_GROUP, dtype=F32)
    dsk = d_skip.astype(F32).reshape(SSM_GROUPS, SSM_GROUP)
    kt = kt + (d == 0)[:, :, None, None, None] * (dsk[:, :, None] * eye)[None, None]
    t_mat = kt.transpose(2, 0, 4, 1, 3).reshape(SSM_GROUPS, L * SSM_GROUP, L * SSM_GROUP)
    bm = pw[:L][::-1][:, :, :, None] * b_bar[None]
    bm = bm.transpose(1, 0, 3, 2).reshape(SSM_GROUPS, L * SSM_GROUP, SSM_STATE)
    cm = c_mat[None] * pw[1:][:, :, None, :]
    cm = cm.transpose(1, 3, 0, 2).reshape(SSM_GROUPS, SSM_STATE, L * SSM_GROUP)
    a_chunk = pw[L]
    pad = 128 - SSM_STATE
    pc = lambda m: jnp.pad(m, ((0, 0), (0, 0), (0, pad))).astype(BF16)
    pr = lambda m: jnp.pad(m, ((0, 0), (0, pad), (0, 0))).astype(BF16)
    pa = lambda m: jnp.pad(m, ((0, 0), (0, pad)))
    return (pc(bm.real), pc(bm.imag), t_mat.astype(BF16), pr(cm.real), pr(-cm.imag),
            pa(a_chunk.real), pa(a_chunk.imag))


def _ssm(zb3, mats):
    b, s, w = zb3.shape
    L, G = SSM_CHUNK, SSM_GROUPS
    ts = min(SSM_TILE, s)
    nch = ts // L
    full = lambda a: pl.BlockSpec(a.shape, lambda bi, i: (0,) * a.ndim)
    return pl.pallas_call(
        _ssm_kernel,
        grid=(b, s // ts),
        in_specs=[pl.BlockSpec((None, ts, w), lambda bi, i: (bi, i, 0))] + [full(m) for m in mats],
        out_specs=pl.BlockSpec((None, ts, w), lambda bi, i: (bi, i, 0)),
        out_shape=jax.ShapeDtypeStruct((b, s, w), BF16),
        scratch_shapes=[pltpu.VMEM((w // 128, ts, 128), F32),
                        pltpu.VMEM((L * (w // 128), nch, 128), F32),
                        pltpu.VMEM((G, nch, L * SSM_GROUP), BF16),
                        pltpu.VMEM((nch * G, 128), F32), pltpu.VMEM((nch * G, 128), F32),
                        pltpu.VMEM((nch * G, 128), F32), pltpu.VMEM((nch * G, 128), F32),
                        pltpu.VMEM((G, nch, L * SSM_GROUP), F32),
                        pltpu.VMEM((2 * G, 128), F32)],
        compiler_params=_cparams("parallel", "arbitrary"),
        name="ssm",
    )(zb3, *mats)


def _mlstm_kernel(zc_ref, zgt_ref, convw_ref, gb_ref, ng_ref, triu_ref, bones_ref, o_ref,
                  st_ref, m_ref, tail_ref, qk_ref,
                  rt_ref, bl_ref, rm_ref, cr128_ref, cr64_ref, b64_ref, r64_ref, s_ref, hh_ref):
    ts = zc_ref.shape[0]
    hd, nh, w = HEAD_DIM, MLSTM_HEADS, GROUP_W
    nchunk = ts // CHUNK
    nr = nchunk * 8
    lane1 = lax.broadcasted_iota(I32, (1, 128), 1)
    half = [(lane1 // hd) == (h % 2) for h in range(nh)]
    grp = [slice((h * hd) // 128 * 128, (h * hd) // 128 * 128 + 128) for h in range(nh)]

    @pl.when(pl.program_id(1) == 0)
    def _():
        st_ref[...] = jnp.zeros_like(st_ref)
        m_ref[...] = jnp.zeros_like(m_ref)
        tail_ref[...] = jnp.zeros_like(tail_ref)

    r_i = lax.broadcasted_iota(I32, (CHUNK, CHUNK), 0)
    c_i = lax.broadcasted_iota(I32, (CHUNK, CHUNK), 1)
    causal = c_i <= r_i
    k64 = lax.broadcasted_iota(I32, (8, w), 0)
    j64 = lax.broadcasted_iota(I32, (8, w), 1)
    sel64 = (j64 // hd == k64 % nh).astype(BF16)
    k128 = lax.broadcasted_iota(I32, (8, nh * 128), 0)
    j128 = lax.broadcasted_iota(I32, (8, nh * 128), 1)
    sel128 = (j128 // 128 == k128 % nh).astype(BF16)

    def lanes(t):
        lo = jnp.where(lane1 < hd, t[0:1, :], t[1:2, :])
        hi = jnp.where(lane1 < hd, t[2:3, :], t[3:4, :])
        return jnp.concatenate([lo, hi], axis=1)

    rowm = lax.broadcasted_iota(I32, (nr, 128), 0) % 8
    lanem = lax.broadcasted_iota(I32, (nr, 128), 1)
    graw = jnp.concatenate([zgt_ref[:, c * CHUNK:(c + 1) * CHUNK] + gb_ref[...] for c in range(nchunk)], axis=0)
    g2 = jnp.where(rowm < nh, graw, _log_sigmoid(graw))
    ghi, glo = _split_bf16(g2)
    cum = _dot(ghi, triu_ref[...]) + _dot(glo, triu_ref[...])
    b_t = pltpu.roll(cum, nr - 4, axis=0)
    r_t = g2 - b_t
    cr = r_t
    sh = 1
    while sh < CHUNK:
        cr = jnp.maximum(cr, jnp.where(lanem >= sh, pltpu.roll(cr, sh, axis=1), NEG))
        sh *= 2
    rt_ref[...] = r_t
    bl_ref[...] = jnp.broadcast_to(b_t[:, CHUNK - 1:CHUNK], (nr, 128))
    rm_ref[...] = jnp.broadcast_to(cr[:, CHUNK - 1:CHUNK], (nr, 128))

    x = zc_ref[:, 0:2 * w].astype(F32)
    xe = jnp.concatenate([tail_ref[...], x], axis=0)
    tail_ref[...] = x[ts - 8:, :]
    cw = convw_ref[...]
    acc = x * cw[CONV_K - 1:CONV_K, :]
    for sft in range(1, CONV_K):
        acc = acc + xe[8 - sft:8 - sft + ts, :] * cw[CONV_K - 1 - sft:CONV_K - sft, :]
    lane_qk = lax.broadcasted_iota(I32, (1, 2 * w), 1)
    qk_ref[...] = acc * _sigmoid(acc) * jnp.where(lane_qk < w, 1.0, hd ** -0.5)

    def hi_lo_rows(v):
        hi = v.astype(BF16).astype(F32)
        return jnp.where(rowm < nh, hi, pltpu.roll(v - hi, 4, axis=0))

    a_cr = hi_lo_rows(cr)
    a_b = hi_lo_rows(b_t)
    a_r = hi_lo_rows(r_t)
    for c in range(nchunk):
        rows = slice(c * CHUNK, (c + 1) * CHUNK)
        t8 = slice(c * 8, (c + 1) * 8)
        a_cr_c = a_cr[t8].astype(BF16)
        cr128_ref[rows, :] = _dot_tn(a_cr_c, sel128)
        cr64_ref[rows, :] = _dot_tn(a_cr_c, sel64)
        b64_ref[rows, :] = _dot_tn(a_b[t8].astype(BF16), sel64)
        r64_ref[rows, :] = _dot_tn(a_r[t8].astype(BF16), sel64)
        for h in range(nh):
            qg = qk_ref[rows, grp[h]].astype(BF16)
            kg = qk_ref[rows, w + grp[h].start:w + grp[h].stop].astype(BF16)
            s_ref[rows, h * 128:(h + 1) * 128] = _dot_nt(qg, jnp.where(half[h], kg, jnp.zeros_like(kg)))

    lane_w = lax.broadcasted_iota(I32, (1, w), 1)
    hmask = [(lane_w // hd) == h for h in range(nh)]
    ones_blk = [m.astype(BF16) * jnp.ones((CHUNK, 1), BF16) for m in hmask]
    ones_cols = jnp.ones((CHUNK, w), BF16)
    zblk = jnp.zeros((hd, 128), F32)
    ngrp = w // 128
    m_prev = m_ref[...]
    cblk = [st_ref[h * hd:(h + 1) * hd, grp[h]] for h in range(nh)]
    nblk = [st_ref[h * hd:(h + 1) * hd, w + grp[h].start:w + grp[h].stop] for h in range(nh)]
    for c in range(nchunk):
        rows = slice(c * CHUNK, (c + 1) * CHUNK)
        t8 = slice(c * 8, (c + 1) * 8)
        r_c = rt_ref[t8, :]
        g_last = jnp.maximum(rm_ref[t8, :], m_prev)
        decay = jnp.exp(m_prev - g_last)
        mprev_l = lanes(m_prev)
        glast_l = lanes(g_last)
        gb64 = jnp.maximum(cr64_ref[rows, :], mprev_l)
        w_inter = jnp.exp(mprev_l - gb64)
        e_negm = jnp.exp(-(b64_ref[rows, :] + gb64))
        v_all = zc_ref[rows, 2 * w:3 * w]
        wcat = []
        vblocks = []
        srows = []
        for h in range(nh):
            gb128 = jnp.maximum(cr128_ref[rows, h * 128:(h + 1) * 128], m_prev[h:h + 1, :])
            d = jnp.exp(jnp.where(causal, r_c[h:h + 1, :] - gb128, NEG))
            wcat.append((d * s_ref[rows, h * 128:(h + 1) * 128]).astype(BF16))
            vblocks.append(jnp.concatenate([jnp.where(hmask[h], v_all, jnp.zeros_like(v_all)), ones_blk[h]], axis=1))
            g = (h * hd) // 128
            srows.append(jnp.concatenate([cblk[h] if j == g else zblk for j in range(ngrp)]
                                         + [nblk[h] if j == g else zblk for j in range(ngrp)], axis=1))
        s_bf = jnp.concatenate(srows, axis=0).astype(BF16)
        out_aug = _dot(jnp.concatenate(wcat, axis=1), jnp.concatenate(vblocks, axis=0))
        out_aug = out_aug + _dot((qk_ref[rows, 0:w] * w_inter).astype(BF16), s_bf)
        hh_ref[rows, :] = out_aug[:, 0:w] / jnp.maximum(jnp.abs(out_aug[:, w:2 * w]), e_negm)
        kw = (qk_ref[rows, w:2 * w] * jnp.exp(r64_ref[rows, :] - glast_l)).astype(BF16)
        upd = _dot_tn(kw, jnp.concatenate([v_all, ones_cols], axis=1))
        for h in range(nh):
            rs = slice(h * hd, (h + 1) * hd)
            cblk[h] = cblk[h] * decay[h:h + 1, :] + jnp.where(half[h], upd[rs, grp[h]], 0.0)
            nblk[h] = (nblk[h] * decay[h:h + 1, :]
                       + jnp.where(half[h], upd[rs, w + grp[h].start:w + grp[h].stop], 0.0))
        m_prev = bl_ref[t8, :] + g_last
    m_ref[...] = m_prev
    for h in range(nh):
        st_ref[h * hd:(h + 1) * hd, grp[h]] = cblk[h]
        st_ref[h * hd:(h + 1) * hd, w + grp[h].start:w + grp[h].stop] = nblk[h]

    hh = hh_ref[...]
    ms = _dot((hh * hh).astype(BF16), bones_ref[...])
    og = zc_ref[:, 3 * w:4 * w].astype(F32)
    o_ref[...] = (_sigmoid(og) * hh * lax.rsqrt(ms + EPS) * ng_ref[...]).astype(o_ref.dtype)


def _mlstm(zc3, zgt, conv_w, gate_b_rows, norm_g):
    b, s, w = zc3.shape
    ts = SEQ_TILE
    nt = s // ts
    gw = GROUP_W
    t = jnp.arange(CHUNK)
    triu = (t[:, None] <= t[None, :]).astype(BF16)
    hid = jnp.arange(gw) // HEAD_DIM
    bones = jnp.where(hid[:, None] == hid[None, :], 1.0 / HEAD_DIM, 0.0).astype(BF16)
    full = lambda a: pl.BlockSpec(a.shape, lambda bi, i: (0,) * a.ndim)
    return pl.pallas_call(
        _mlstm_kernel,
        grid=(b, nt),
        in_specs=[pl.BlockSpec((None, ts, w), lambda bi, i: (bi, i, 0)),
                  pl.BlockSpec((8, ts), lambda bi, i: (0, bi * nt + i)),
                  full(conv_w), full(gate_b_rows), full(norm_g), full(triu), full(bones)],
        out_specs=pl.BlockSpec((None, ts, gw), lambda bi, i: (bi, i, 0)),
        out_shape=jax.ShapeDtypeStruct((b, s, gw), BF16),
        scratch_shapes=[pltpu.VMEM((gw, 2 * gw), F32),
                        pltpu.VMEM((8, 128), F32),
                        pltpu.VMEM((8, 2 * gw), F32),
                        pltpu.VMEM((ts, 2 * gw), F32),
                        pltpu.VMEM((ts // CHUNK * 8, 128), F32),
                        pltpu.VMEM((ts // CHUNK * 8, 128), F32),
                        pltpu.VMEM((ts // CHUNK * 8, 128), F32),
                        pltpu.VMEM((ts, MLSTM_HEADS * 128), F32),
                        pltpu.VMEM((ts, gw), F32),
                        pltpu.VMEM((ts, gw), F32),
                        pltpu.VMEM((ts, gw), F32),
                        pltpu.VMEM((ts, MLSTM_HEADS * 128), F32),
                        pltpu.VMEM((ts, gw), F32)],
        compiler_params=_cparams("parallel", "arbitrary"),
        name="mlstm",
    )(zc3, zgt, conv_w, gate_b_rows, norm_g, triu, bones)


def _sgu_kernel(zd_ref, lng_ref, lnb_ref, w_ref, bias_ref, o_ref):
    tm = zd_ref.shape[0]
    gw = GROUP_W // SGU_GROUPS
    u = _gelu(zd_ref[:, 0:GROUP_W].astype(F32))
    v = _gelu(zd_ref[:, GROUP_W:2 * GROUP_W].astype(F32))
    mu = jnp.mean(v, axis=-1, keepdims=True)
    vc = v - mu
    var = jnp.mean(vc * vc, axis=-1, keepdims=True)
    vn = (vc * lax.rsqrt(var + EPS) * lng_ref[...] + lnb_ref[...]).astype(BF16)
    for c in range(tm // CHUNK):
        rows = slice(c * CHUNK, (c + 1) * CHUNK)
        mixed = jnp.concatenate([_dot(w_ref[g], vn[rows, g * gw:(g + 1) * gw]) for g in range(SGU_GROUPS)], axis=-1)
        o_ref[rows, :] = (u[rows, :] * (mixed + bias_ref[...])).astype(o_ref.dtype)


def _sgu(zd, ln_g, ln_b, w_tril, bias):
    n = zd.shape[0]
    tm = TOKEN_TILE
    full = lambda a: pl.BlockSpec(a.shape, lambda i: (0,) * a.ndim)
    return pl.pallas_call(
        _sgu_kernel,
        grid=(n // tm,),
        in_specs=[pl.BlockSpec((tm, 2 * GROUP_W), lambda i: (i, 0)), full(ln_g), full(ln_b), full(w_tril), full(bias)],
        out_specs=pl.BlockSpec((tm, GROUP_W), lambda i: (i, 0)),
        out_shape=jax.ShapeDtypeStruct((n, GROUP_W), BF16),
        compiler_params=_cparams("parallel"),
        name="sgu",
    )(zd, ln_g, ln_b, w_tril, bias)


def _out_proj_kernel(x_ref, ya_ref, yb_ref, yc_ref, yd_ref, gluw_ref, glub_ref, wo_ref, ng_ref,
                     rwh_ref, rwl_ref, rb_ref, triu_ref,
                     x1_ref, hx_ref, meta_ref, cnt_ref, carry_ref):
    tm = x_ref.shape[0]

    @pl.when(pl.program_id(0) == 0)
    def _():
        carry_ref[...] = jnp.zeros_like(carry_ref)

    yb = _gelu(yb_ref[...].astype(F32))
    yb = yb * _sigmoid(_dot(yb.astype(BF16), gluw_ref[...]) + glub_ref[...])
    mix = _dot(ya_ref[...], wo_ref[0:GROUP_W, :])
    mix = mix + _dot(yb.astype(BF16), wo_ref[GROUP_W:2 * GROUP_W, :])
    mix = mix + _dot(yc_ref[...], wo_ref[2 * GROUP_W:3 * GROUP_W, :])
    mix = mix + _dot(yd_ref[...], wo_ref[3 * GROUP_W:4 * GROUP_W, :])
    x1 = x_ref[...] + mix
    x1_ref[...] = x1
    h = _rms(x1, ng_ref[...])
    hx_ref[:, 0:PACK_W] = _pack_rows(h)

    hh, hl = _split_bf16(h)
    wh = rwh_ref[...]
    wl = rwl_ref[...]
    logits = _dot_nt(wh, hh) + _dot_nt(wh, hl) + _dot_nt(wl, hh) + rb_ref[...]
    gl = [logits[j:j + 1, :] for j in range(N_GROUPS)]
    gmax = functools.reduce(jnp.maximum, gl)
    gsel = jnp.full((1, tm), N_GROUPS - 1, I32)
    for j in range(N_GROUPS - 2, -1, -1):
        gsel = jnp.where(gl[j] == gmax, j, gsel)
    p_g = 1.0 / functools.reduce(jnp.add, [jnp.exp(v - gmax) for v in gl])
    e_in = []
    for i in range(EPG):
        v = logits[N_GROUPS + i:N_GROUPS + i + 1, :]
        for j in range(1, N_GROUPS):
            r = N_GROUPS + j * EPG + i
            v = jnp.where(gsel == j, logits[r:r + 1, :], v)
        e_in.append(v)
    v1 = functools.reduce(jnp.maximum, e_in)
    i1 = jnp.full((1, tm), EPG - 1, I32)
    for i in range(EPG - 2, -1, -1):
        i1 = jnp.where(e_in[i] == v1, i, i1)
    rest = [jnp.where(i1 == i, NEG, e_in[i]) for i in range(EPG)]
    v2 = functools.reduce(jnp.maximum, rest)
    i2 = jnp.full((1, tm), EPG - 1, I32)
    for i in range(EPG - 2, -1, -1):
        i2 = jnp.where((rest[i] == v2) & (i1 != i), i, i2)
    e2 = jnp.exp(v2 - v1)
    w1 = p_g / (1.0 + e2)
    w2 = p_g * e2 / (1.0 + e2)
    lo = jnp.minimum(i1, i2)
    hi = jnp.maximum(i1, i2)
    w_lo = jnp.where(i1 < i2, w1, w2)
    w_hi = jnp.where(i1 < i2, w2, w1)
    bucket = gsel * PAIRS + ((lo * (2 * EPG - 1 - lo)) >> 1) + (hi - lo - 1)

    kid = lax.broadcasted_iota(I32, (BUCKET_PAD, tm), 0)
    onehot = (kid == bucket).astype(F32)
    prefix = _dot(onehot.astype(BF16), triu_ref[...])
    carry = carry_ref[...]
    rank = jnp.sum(onehot * (prefix - 1.0 + carry[:, 0:1]), axis=0, keepdims=True)
    carry = carry + prefix[:, tm - 1:tm]
    carry_ref[...] = carry
    cnt_ref[...] = carry
    meta_ref[...] = jnp.concatenate([bucket, rank.astype(I32), jnp.zeros((6, tm), I32)], axis=0)
    wrows = jnp.concatenate([w_lo, w_hi, jnp.zeros((126, tm), F32)], axis=0)
    for c in range(tm // 128):
        hx_ref[c * 128:(c + 1) * 128, PACK_W:ROW_W] = pltpu.bitcast(wrows[:, c * 128:(c + 1) * 128].T, U32)


def _out_proj(x2, ya, yb, yc, yd, glu_w, glu_b, w_out, norm_g, rw_hi, rw_lo, rb):
    n = x2.shape[0]
    tm = TOKEN_TILE
    row = lambda w: pl.BlockSpec((tm, w), lambda i: (i, 0))
    full = lambda a: pl.BlockSpec(a.shape, lambda i: (0,) * a.ndim)
    t = jnp.arange(tm)
    triu = (t[:, None] <= t[None, :]).astype(BF16)
    return pl.pallas_call(
        _out_proj_kernel,
        grid=(n // tm,),
        in_specs=[row(D_MODEL), row(GROUP_W), row(GROUP_W), row(GROUP_W), row(GROUP_W),
                  full(glu_w), full(glu_b), full(w_out), full(norm_g), full(rw_hi), full(rw_lo), full(rb),
                  full(triu)],
        out_specs=[row(D_MODEL), row(ROW_W), pl.BlockSpec((8, tm), lambda i: (0, i)),
                   pl.BlockSpec((BUCKET_PAD, 128), lambda i: (0, 0))],
        out_shape=[jax.ShapeDtypeStruct((n, D_MODEL), F32), jax.ShapeDtypeStruct((n, ROW_W), U32),
                   jax.ShapeDtypeStruct((8, n), I32), jax.ShapeDtypeStruct((BUCKET_PAD, 128), F32)],
        scratch_shapes=[pltpu.VMEM((BUCKET_PAD, 128), F32)],
        compiler_params=_cparams("arbitrary"),
        name="out_proj",
    )(x2, ya, yb, yc, yd, glu_w, glu_b, w_out, norm_g, rw_hi, rw_lo, rb, triu)


def _dispatch_kernel(dest_ref, hx_ref, xs_in_ref, xs_ref, sem):
    del xs_in_ref
    nq = hx_ref.shape[0]

    def start(q, c):
        for r in range(SUBLANES):
            d = dest_ref[0, 0, q * SUBLANES + r]
            pltpu.make_async_copy(hx_ref.at[q, pl.ds(r, 1)], xs_ref.at[pl.ds(d, 1)], sem).start()
        return c

    lax.fori_loop(0, nq, start, 0)
    pltpu.make_async_copy(hx_ref, hx_ref, sem).wait()


def _dispatch(dest3, hx, p_tot):
    n = hx.shape[0]
    tm = DISPATCH_TILE
    xs0 = jnp.zeros((p_tot, ROW_W), U32)
    hx = hx.reshape(n // SUBLANES, SUBLANES, ROW_W)
    return pl.pallas_call(
        _dispatch_kernel,
        grid=(n // tm,),
        in_specs=[pl.BlockSpec((1, 1, tm), lambda i: (i, 0, 0), memory_space=pltpu.SMEM),
                  pl.BlockSpec((tm // SUBLANES, SUBLANES, ROW_W), lambda i: (i, 0, 0)),
                  pl.BlockSpec(memory_space=pl.ANY)],
        out_specs=pl.BlockSpec(memory_space=pl.ANY),
        out_shape=jax.ShapeDtypeStruct((p_tot, ROW_W), U32),
        input_output_aliases={2: 0},
        scratch_shapes=[pltpu.SemaphoreType.DMA],
        compiler_params=_cparams("arbitrary"),
        name="dispatch",
    )(dest3, hx, xs0)


def _ffn_kernel(valid_ref, xblk_ref, run_ref, exprun_ref, nrun_ref, xs_ref, wg_hbm, wu_hbm, wd_hbm, ys_ref,
                stg_g, stg_u, stg_d, wgu_a, wdn_a, wgu_b, wdn_b, sem, *, layer):
    del xblk_ref
    j = pl.program_id(0)
    valid = valid_ref[j]
    jp = jnp.maximum(j - 1, 0)

    def expert_copies(s, r, slot):
        e = exprun_ref[s, r]
        return [pltpu.make_async_copy(src.at[layer, e], dst.at[s, slot], sem.at[s, slot])
                for src, dst in ((wg_hbm, stg_g), (wu_hbm, stg_u), (wd_hbm, stg_d))]

    for s, (wgu, wdn) in enumerate(((wgu_a, wdn_a), (wgu_b, wdn_b))):
        r = run_ref[s, j]

        @pl.when((j == 0) | (r != run_ref[s, jp]))
        def _():
            slot = r % 2

            @pl.when(j == 0)
            def _():
                for c in expert_copies(s, r, slot):
                    c.start()

            for c in expert_copies(s, r, slot):
                c.wait()

            @pl.when(r + 1 < nrun_ref[s])
            def _():
                for c in expert_copies(s, r + 1, 1 - slot):
                    c.start()

            wgu[:, 0:D_EXPERT] = stg_g[s, slot].astype(BF16)
            wgu[:, D_EXPERT:2 * D_EXPERT] = stg_u[s, slot].astype(BF16)
            wdn[...] = stg_d[s, slot].astype(BF16)

    def ffn_rows(rows):
        x_lo, x_hi = _unpack_rows(xs_ref[rows, 0:PACK_W])
        xb = jnp.concatenate([x_lo.astype(BF16), x_hi.astype(BF16)], axis=1)
        wts = pltpu.bitcast(xs_ref[rows, PACK_W:ROW_W], F32)

        def expert(wgu, wdn):
            gu = _dot(xb, wgu[...])
            g = gu[:, 0:D_EXPERT]
            act = (g * _sigmoid(g) * gu[:, D_EXPERT:2 * D_EXPERT]).astype(BF16)
            return _dot(act, wdn[...])

        ys_ref[rows, :] = _pack_rows(expert(wgu_a, wdn_a) * wts[:, 0:1] + expert(wgu_b, wdn_b) * wts[:, 1:2])

    half = FFN_SUB // 2
    for sb in range(FFN_BLOCK // FFN_SUB):
        r0 = sb * FFN_SUB
        left = valid - r0

        @pl.when(left <= 0)
        def _():
            ys_ref[r0:r0 + FFN_SUB, :] = jnp.zeros((FFN_SUB, PACK_W), U32)

        @pl.when((left > 0) & (left <= half))
        def _():
            ffn_rows(slice(r0, r0 + half))
            ys_ref[r0 + half:r0 + FFN_SUB, :] = jnp.zeros((half, PACK_W), U32)

        @pl.when(left > half)
        def _():
            ffn_rows(slice(r0, r0 + FFN_SUB))


def _ffn(valid, xblk, run, exprun, nrun, xs, layer, w_gate, w_up, w_down):
    p_tot = xs.shape[0]
    bm = FFN_BLOCK
    nblk = p_tot // bm
    up_shape = (D_MODEL, D_EXPERT)
    dn_shape = (D_EXPERT, D_MODEL)
    hbm = pl.BlockSpec(memory_space=pl.ANY)
    return pl.pallas_call(
        functools.partial(_ffn_kernel, layer=layer),
        grid_spec=pltpu.PrefetchScalarGridSpec(
            num_scalar_prefetch=5,
            grid=(nblk,),
            in_specs=[pl.BlockSpec((bm, ROW_W), lambda j, va, xb, ru, er, nr: (xb[j], 0)), hbm, hbm, hbm],
            out_specs=pl.BlockSpec((bm, PACK_W), lambda j, va, xb, ru, er, nr: (j, 0)),
            scratch_shapes=[pltpu.VMEM((2, 2) + up_shape, F32), pltpu.VMEM((2, 2) + up_shape, F32),
                            pltpu.VMEM((2, 2) + dn_shape, F32),
                            pltpu.VMEM((D_MODEL, 2 * D_EXPERT), BF16), pltpu.VMEM(dn_shape, BF16),
                            pltpu.VMEM((D_MODEL, 2 * D_EXPERT), BF16), pltpu.VMEM(dn_shape, BF16),
                            pltpu.SemaphoreType.DMA((2, 2))],
        ),
        out_shape=jax.ShapeDtypeStruct((p_tot, PACK_W), U32),
        compiler_params=_cparams("arbitrary"),
        name="ffn",
    )(valid, xblk, run, exprun, nrun, xs, w_gate, w_up, w_down)


def _combine_kernel(dest_ref, dnext_ref, x1_ref, ys_ref, ng_ref, o_ref, buf_ref, sem, *, final_norm):
    i = pl.program_id(0)
    tm = x1_ref.shape[0]
    nq = tm // SUBLANES
    slot = i % 2

    def gather(d_ref, sl):
        def start(q, c):
            for r in range(SUBLANES):
                d = d_ref[0, 0, q * SUBLANES + r]
                pltpu.make_async_copy(ys_ref.at[pl.ds(d, 1)], buf_ref.at[sl, q, pl.ds(r, 1)], sem.at[sl]).start()
            return c

        lax.fori_loop(0, nq, start, 0)

    @pl.when(i == 0)
    def _():
        gather(dest_ref, 0)

    pltpu.make_async_copy(buf_ref.at[slot], buf_ref.at[slot], sem.at[slot]).wait()

    @pl.when(i + 1 < pl.num_programs(0))
    def _():
        gather(dnext_ref, 1 - slot)

    y_lo, y_hi = _unpack_rows(buf_ref[slot].reshape(tm, PACK_W))
    x2 = x1_ref[...] + jnp.concatenate([y_lo, y_hi], axis=1)
    o_ref[...] = _rms(x2, ng_ref[...]) if final_norm else x2


def _combine(dest3, x1, ys, norm_g, final_norm):
    n = x1.shape[0]
    tm = COMBINE_TILE
    last = n // tm - 1
    return pl.pallas_call(
        functools.partial(_combine_kernel, final_norm=final_norm),
        grid=(n // tm,),
        in_specs=[pl.BlockSpec((1, 1, tm), lambda i: (i, 0, 0), memory_space=pltpu.SMEM),
                  pl.BlockSpec((1, 1, tm), lambda i: (jnp.minimum(i + 1, last), 0, 0), memory_space=pltpu.SMEM),
                  pl.BlockSpec((tm, D_MODEL), lambda i: (i, 0)),
                  pl.BlockSpec(memory_space=pl.ANY),
                  pl.BlockSpec((1, D_MODEL), lambda i: (0, 0))],
        out_specs=pl.BlockSpec((tm, D_MODEL), lambda i: (i, 0)),
        out_shape=jax.ShapeDtypeStruct((n, D_MODEL), F32),
        scratch_shapes=[pltpu.VMEM((2, tm // SUBLANES, SUBLANES, PACK_W), U32), pltpu.SemaphoreType.DMA((2,))],
        compiler_params=_cparams("arbitrary"),
        name="combine",
    )(dest3, dest3, x1, ys, norm_g)


def _routing_tables(meta, counts, n):
    bm = FFN_BLOCK
    nblk = n // bm + N_BUCKETS
    bucket = meta[0]
    rank = meta[1]
    cnt = counts[:N_BUCKETS, 0].astype(I32)
    padded = ((cnt + bm - 1) // bm) * bm
    pad_end = jnp.cumsum(padded)
    pad_start = pad_end - padded
    onehot = bucket[:, None] == jnp.arange(N_BUCKETS, dtype=I32)[None, :]
    dest = rank + jnp.sum(jnp.where(onehot, pad_start[None, :], 0), axis=1)
    nact = (pad_end[-1] // bm).astype(I32)
    j = jnp.arange(nblk, dtype=I32)
    blk = jnp.minimum(j, jnp.maximum(nact - 1, 0))
    pos = blk * bm
    bkt = jnp.minimum(jnp.sum((pad_end[None, :] <= pos[:, None]).astype(I32), axis=1), N_BUCKETS - 1)
    in_bkt = bkt[:, None] == jnp.arange(N_BUCKETS, dtype=I32)[None, :]
    sel = lambda tab: jnp.sum(jnp.where(in_bkt, tab[None, :], 0), axis=1)
    valid = jnp.where(j < nact, jnp.clip(sel(cnt) - (pos - sel(pad_start)), 0, bm), 0)
    lo_tab, hi_tab = [], []
    for g in range(N_GROUPS):
        for a in range(EPG):
            for b in range(a + 1, EPG):
                lo_tab.append(g * EPG + a)
                hi_tab.append(g * EPG + b)
    experts = jnp.stack([sel(jnp.asarray(lo_tab, I32)), sel(jnp.asarray(hi_tab, I32))]).astype(I32)
    change = jnp.concatenate([jnp.zeros((2, 1), I32), (experts[:, 1:] != experts[:, :-1]).astype(I32)], axis=1)
    run = jnp.cumsum(change, axis=1).astype(I32)
    is_run = run[:, :, None] == j[None, None, :]
    exprun = jnp.max(jnp.where(is_run, experts[:, :, None], 0), axis=1).astype(I32)
    nrun = run[:, -1] + 1
    return dest.astype(I32), valid.astype(I32), blk.astype(I32), run, exprun, nrun.astype(I32), nblk * bm


def _layer(x2, b, s, p, final_g):
    n = b * s
    za, zb, zc, zd, zgt = _in_proj(x2, p['norm_mix_g'], p['wa'], p['wb'], p['wc'], p['wd'], p['wgt'])
    ya = _attention(za.reshape(b, s, -1), p['sinks']).reshape(n, GROUP_W)
    yb = _ssm(zb.reshape(b, s, GROUP_W), p['ssm']).reshape(n, GROUP_W)
    yc = _mlstm(zc.reshape(b, s, -1), zgt, p['conv_w'], p['gate_b'], p['mlstm_norm_g']).reshape(n, GROUP_W)
    yd = _sgu(zd, p['sgu_ln_g'], p['sgu_ln_b'], p['sgu_w'], p['sgu_bias'])
    x1, hx, meta, counts = _out_proj(x2, ya, yb, yc, yd, p['glu_w'], p['glu_b'], p['w_out'], p['norm_ffn_g'],
                                     p['rw_hi'], p['rw_lo'], p['rb'])
    dest, valid, xblk, run, exprun, nrun, p_tot = _routing_tables(meta, counts, n)
    xs = _dispatch(dest.reshape(n // DISPATCH_TILE, 1, DISPATCH_TILE), hx, p_tot)
    ys = _ffn(valid, xblk, run, exprun, nrun, xs, p['layer'], p['w_gate'], p['w_up'], p['w_down'])
    dest3 = dest.reshape(n // COMBINE_TILE, 1, COMBINE_TILE)
    if final_g is None:
        return _combine(dest3, x1, ys, p['norm_ffn_g'], False)
    return _combine(dest3, x1, ys, final_g, True)


def _prep_layer(l, norm_mix_g, w_in, attn_sinks, ssm_a_re, ssm_a_im, ssm_b_re, ssm_b_im, ssm_c_re, ssm_c_im,
                ssm_d, ssm_log_dt, ssm_glu_w, ssm_glu_b, mlstm_conv_w, mlstm_gate_b, mlstm_norm_g,
                sgu_ln_g, sgu_ln_b, sgu_w, sgu_b, w_out, norm_ffn_g, router_group_w, router_group_b,
                router_expert_w, router_expert_b, expert_w_gate, expert_w_up, expert_w_down):
    w = w_in[l]
    o_su = 2 * GROUP_W
    o_c = o_su + GROUP_W
    o_ci = o_c + 3 * GROUP_W
    o_co = o_ci + 2 * MLSTM_HEADS
    o_d = o_co + GROUP_W
    wc = jnp.concatenate([w[:, o_c:o_ci], w[:, o_co:o_d]], axis=1)
    wgt = w[:, o_ci:o_co].T
    head_cols = lambda m, hq: m[..., hq * HEAD_DIM:(hq + 1) * HEAD_DIM]
    order = [g * ATTN_REP + r for r in range(ATTN_REP) for g in range(KV_HEADS)]
    wa = jnp.concatenate([head_cols(w, hq) * (HEAD_DIM ** -0.5) for hq in order] + [w[:, GROUP_W:o_su]], axis=1)
    wo = jnp.concatenate([w_out[l][hq * HEAD_DIM:(hq + 1) * HEAD_DIM] for hq in order] + [w_out[l][GROUP_W:]], axis=0)
    rw = jnp.concatenate([router_group_w[l], router_expert_w[l]], axis=1).T.astype(F32)
    rw = jnp.pad(rw, ((0, 4), (0, 0)))
    rw_hi = rw.astype(BF16)
    rw_lo = (rw - rw_hi.astype(F32)).astype(BF16)
    rb = jnp.pad(jnp.concatenate([router_group_b[l], router_expert_b[l]]).astype(F32), (0, 4))[:, None]
    tril = jnp.tril(jnp.ones((CHUNK, CHUNK), F32))
    gw = GROUP_W // SGU_GROUPS
    return dict(
        norm_mix_g=norm_mix_g[l][None, :].astype(F32),
        wa=wa.astype(BF16), wb=w[:, o_su:o_c].astype(BF16), wc=wc.astype(BF16),
        wd=w[:, o_d:].astype(BF16), wgt=wgt.astype(BF16),
        sinks=attn_sinks[l].astype(F32),
        ssm=_ssm_matrices(ssm_a_re[l], ssm_a_im[l], ssm_b_re[l], ssm_b_im[l], ssm_c_re[l], ssm_c_im[l],
                          ssm_d[l], ssm_log_dt[l]),
        glu_w=ssm_glu_w[l].astype(BF16), glu_b=ssm_glu_b[l][None, :].astype(F32),
        conv_w=mlstm_conv_w[l].astype(F32),
        gate_b=jnp.broadcast_to(mlstm_gate_b[l].astype(F32)[:, None], (2 * MLSTM_HEADS, 128)),
        mlstm_norm_g=mlstm_norm_g[l][None, :].astype(F32),
        sgu_ln_g=sgu_ln_g[l][None, :].astype(F32), sgu_ln_b=sgu_ln_b[l][None, :].astype(F32),
        sgu_w=(sgu_w[l].astype(F32) * tril).astype(BF16),
        sgu_bias=jnp.repeat(sgu_b[l].astype(F32).T, gw, axis=1),
        w_out=wo.astype(BF16), norm_ffn_g=norm_ffn_g[l][None, :].astype(F32),
        rw_hi=rw_hi, rw_lo=rw_lo, rb=rb,
        layer=l, w_gate=expert_w_gate, w_up=expert_w_up, w_down=expert_w_down,
    )


def kernel(x, norm_mix_g, w_in, attn_sinks, ssm_a_re, ssm_a_im, ssm_b_re, ssm_b_im, ssm_c_re, ssm_c_im, ssm_d, ssm_log_dt, ssm_glu_w, ssm_glu_b, mlstm_conv_w, mlstm_gate_b, mlstm_norm_g, sgu_ln_g, sgu_ln_b, sgu_w, sgu_b, w_out, norm_ffn_g, router_group_w, router_group_b, router_expert_w, router_expert_b, expert_w_gate, expert_w_up, expert_w_down, norm_final_g):
    b, s, d = x.shape
    depth = w_in.shape[0]
    x2 = x.reshape(b * s, d).astype(F32)
    for l in range(depth):
        p = _prep_layer(l, norm_mix_g, w_in, attn_sinks, ssm_a_re, ssm_a_im, ssm_b_re, ssm_b_im, ssm_c_re,
                        ssm_c_im, ssm_d, ssm_log_dt, ssm_glu_w, ssm_glu_b, mlstm_conv_w, mlstm_gate_b,
                        mlstm_norm_g, sgu_ln_g, sgu_ln_b, sgu_w, sgu_b, w_out, norm_ffn_g, router_group_w,
                        router_group_b, router_expert_w, router_expert_b, expert_w_gate, expert_w_up,
                        expert_w_down)
        final_g = norm_final_g[None, :].astype(F32) if l == depth - 1 else None
        x2 = _layer(x2, b, s, p, final_g)
    return x2.reshape(b, s, d).astype(x.dtype)
```
